```python
import math
import jax, jax.numpy as jnp
from jax import lax
import numpy as np

D_MODEL = 1024
BATCH = 1
SEQ = 16384
DEPTH = 2
DEC_BATCH = 128
DEC_SEQ = 4
PAST_LEN = 16384
PAGE_SIZE = 128

N_META = 16
N_EVEN = (DEPTH + 1) // 2
N_ODD = DEPTH // 2
EPS = 1e-6
N_Q_HEADS = 16
N_KV_HEADS = 2
GQA = N_Q_HEADS // N_KV_HEADS
HEAD_DIM = 64
WINDOW = 128
ATTN_BLOCK = WINDOW
D_ATTN = N_Q_HEADS * HEAD_DIM
D_KV = N_KV_HEADS * HEAD_DIM
INVALID_POS = 2 ** 30
POOL_WINDOWS = (2, 4, 8, 16)
POOL_GROUP_DIM = 96
N_POOL_GROUPS = len(POOL_WINDOWS)
D_POOL = N_POOL_GROUPS * POOL_GROUP_DIM
POOL_BUF = max(POOL_WINDOWS) - 1
D_IN_EVEN = D_ATTN + 2 * D_KV + D_POOL
D_MIX_EVEN = D_ATTN + D_POOL
D_LRU = 512
LRU_HEADS = 8
LRU_BLOCK = D_LRU // LRU_HEADS
LRU_C = 8.0
CONV_WIDTH = 4
SSD_HEADS = 8
SSD_HEAD_DIM = 64
D_SSD = SSD_HEADS * SSD_HEAD_DIM
SSD_GROUPS = 2
SSD_STATE = 128
SSD_CHUNK = 128
D_XBC = D_SSD + 2 * SSD_GROUPS * SSD_STATE
D_IN_ODD = 2 * D_LRU + D_SSD + D_XBC + SSD_HEADS
D_MIX_ODD = D_LRU + D_SSD
MOE_GROUPS = 4
EXPERTS_PER_GROUP = 8
N_EXPERTS = MOE_GROUPS * EXPERTS_PER_GROUP
MOE_TOP_K = 2
D_EXPERT = 256
MOE_BLOCK = 128

kernel_name = 'hybrid_swa_pool_rglru_ssd_hmoe_step'

F32 = jnp.float32


def rmsnorm(x, g):
    xf = x.astype(F32)
    y = xf * lax.rsqrt(jnp.mean(xf * xf, axis=-1, keepdims=True) + EPS)
    return (y * g.astype(F32)).astype(x.dtype)


def alibi_slopes():
    h = jnp.arange(1, N_Q_HEADS + 1, dtype=F32)
    return jnp.exp2(-8.0 * h / N_Q_HEADS)


def window_attention(q, k, v, q_pos, k_pos, sinks):
    b, n, tq = q.shape[:3]
    tk = k.shape[2]
    qg = q.reshape(b, n, tq, N_KV_HEADS, GQA, HEAD_DIM)
    s = jnp.einsum('bnqkgd,bnskd->bnkgqs', qg, k).astype(F32) * (HEAD_DIM ** -0.5)
    dist = q_pos[:, :, None] - k_pos[:, None, :]
    slot_is_meta = (jnp.arange(tk) < N_META)[None, None, :]
    allowed = (dist >= 0) & (slot_is_meta | ((dist < WINDOW) & (k_pos[:, None, :] >= N_META)))
    slopes = alibi_slopes().reshape(N_KV_HEADS, GQA)[None, None, :, :, None, None]
    s = jnp.where(allowed[None, :, None, None], s - slopes * dist.astype(F32)[None, :, None, None], -jnp.inf)
    sink = sinks.astype(F32).reshape(N_KV_HEADS, GQA)[None, None, :, :, None, None]
    m = jnp.maximum(jnp.max(s, axis=-1, keepdims=True), sink)
    pr = jnp.exp(s - m)
    pr = pr / (jnp.sum(pr, axis=-1, keepdims=True) + jnp.exp(sink - m))
    o = jnp.einsum('bnkgqs,bnskd->bnqkgd', pr.astype(v.dtype), v)
    return o.reshape(b, n, tq, D_ATTN)


def attn_prompt(q, k, v, sinks):
    b, T = q.shape[:2]
    L = T - N_META
    nb = L // ATTN_BLOCK
    meta_pos = jnp.arange(N_META, dtype=jnp.int32)
    km, vm = k[:, :N_META], v[:, :N_META]
    o_meta = window_attention(q[:, None, :N_META], km[:, None], vm[:, None], meta_pos[None], meta_pos[None], sinks)[:, 0]
    qr = q[:, N_META:].reshape(b, nb, ATTN_BLOCK, N_Q_HEADS, HEAD_DIM)

    def band(t):
        tr = t[:, N_META:].reshape(b, nb, ATTN_BLOCK, N_KV_HEADS, HEAD_DIM)
        prev = jnp.concatenate([jnp.zeros_like(tr[:, :1]), tr[:, :-1]], axis=1)
        meta = jnp.broadcast_to(t[:, None, :N_META], (b, nb, N_META, N_KV_HEADS, HEAD_DIM))
        return jnp.concatenate([meta, prev, tr], axis=2)

    q_pos = N_META + jnp.arange(L, dtype=jnp.int32).reshape(nb, ATTN_BLOCK)
    prev_pos = jnp.where(jnp.arange(nb)[:, None] > 0, q_pos - ATTN_BLOCK, INVALID_POS)
    k_pos = jnp.concatenate([jnp.broadcast_to(meta_pos[None], (nb, N_META)), prev_pos, q_pos], axis=1)
    o_real = window_attention(qr, band(k), band(v), q_pos, k_pos, sinks).reshape(b, L, D_ATTN)
    o = jnp.concatenate([o_meta, o_real], axis=1)
    new_k = jnp.concatenate([km, k[:, -WINDOW:]], axis=1)
    new_v = jnp.concatenate([vm, v[:, -WINDOW:]], axis=1)
    return o, new_k, new_v


def attn_sample(q, k, v, k_buf, v_buf, sinks):
    S = q.shape[1]
    new_pos = PAST_LEN + jnp.arange(S, dtype=jnp.int32)
    k_pos = jnp.concatenate([jnp.arange(N_META, dtype=jnp.int32),
                             PAST_LEN - WINDOW + jnp.arange(WINDOW, dtype=jnp.int32), new_pos])
    kk = jnp.concatenate([k_buf.astype(k.dtype), k], axis=1)
    vv = jnp.concatenate([v_buf.astype(v.dtype), v], axis=1)
    o = window_attention(q[:, None], kk[:, None], vv[:, None], new_pos[None], k_pos[None], sinks)[:, 0]
    new_k = jnp.concatenate([kk[:, :N_META], kk[:, -WINDOW:]], axis=1)
    new_v = jnp.concatenate([vv[:, :N_META], vv[:, -WINDOW:]], axis=1)
    return o, new_k, new_v


def pool_mix(p, p_buf, pos0, w_pool, scale):
    b, L, _ = p.shape
    seq = jnp.concatenate([p_buf.astype(p.dtype), p], axis=1)
    sf = seq.astype(F32)
    cs = jnp.concatenate([jnp.zeros((b, 1, D_POOL), F32), jnp.cumsum(sf, axis=1)], axis=1)
    cur = sf[:, POOL_BUF:]
    pos = pos0 + jnp.arange(L, dtype=jnp.int32)
    diffs = []
    for g, w in enumerate(POOL_WINDOWS):
        c0, c1 = g * POOL_GROUP_DIM, (g + 1) * POOL_GROUP_DIM
        hi = cs[:, POOL_BUF + 1:, c0:c1]
        lo = cs[:, POOL_BUF + 1 - w:POOL_BUF + 1 - w + L, c0:c1]
        cnt = jnp.minimum(pos + 1, w).astype(F32)[None, :, None]
        diffs.append((hi - lo) / cnt - cur[:, :, c0:c1])
    d = jnp.stack(diffs, axis=2)
    y = jnp.einsum('blgc,gcd->blgd', d, w_pool.astype(F32)).reshape(b, L, D_POOL) * scale.astype(F32)
    return y.astype(p.dtype), seq[:, -POOL_BUF:]


def causal_conv(u, buf, w, bias):
    L = u.shape[1]
    seq = jnp.concatenate([buf.astype(u.dtype), u], axis=1)
    out = bias
    for tap in range(CONV_WIDTH):
        out = out + seq[:, tap:tap + L] * w[tap]
    return out, seq[:, -(CONV_WIDTH - 1):]


def rglru(xc, h0, w_rg, b_rg, w_ig, b_ig, lam):
    b, L, _ = xc.shape
    xb = xc.reshape(b, L, LRU_HEADS, LRU_BLOCK)
    r = jax.nn.sigmoid(jnp.einsum('blhi,hij->blhj', xb, w_rg).reshape(b, L, D_LRU) + b_rg)
    i = jax.nn.sigmoid(jnp.einsum('blhi,hij->blhj', xb, w_ig).reshape(b, L, D_LRU) + b_ig)
    log_a = -LRU_C * r.astype(F32) * jax.nn.softplus(-lam.astype(F32))
    a = jnp.exp(log_a)
    bx = jnp.sqrt(-jnp.expm1(2.0 * log_a)) * (i * xc).astype(F32)

    def combine(c1, c2):
        a1, b1 = c1
        a2, b2 = c2
        return a1 * a2, a2 * b1 + b2

    a_cum, b_cum = lax.associative_scan(combine, (a, bx), axis=1)
    h = a_cum * h0.astype(F32)[:, None] + b_cum
    return h.astype(xc.dtype), h[:, -1].astype(h0.dtype)


def ssd_scan(x, dt, a, bm, cm, h0, chunk):
    b, L, H, P = x.shape
    nc = L // chunk
    rep = H // bm.shape[2]
    bh = jnp.repeat(bm.astype(F32), rep, axis=2).reshape(b, nc, chunk, H, SSD_STATE)
    ch = jnp.repeat(cm.astype(F32), rep, axis=2).reshape(b, nc, chunk, H, SSD_STATE)
    xc = x.astype(F32).reshape(b, nc, chunk, H, P)
    dtc = dt.reshape(b, nc, chunk, H)
    acum = jnp.cumsum(dtc * a, axis=2)
    causal = jnp.tril(jnp.ones((chunk, chunk), dtype=bool))[None, None, :, :, None]
    seg = acum[:, :, :, None, :] - acum[:, :, None, :, :]
    decay = jnp.exp(jnp.where(causal, seg, -jnp.inf))
    scores = jnp.einsum('bcthn,bcshn->bctsh', ch, bh) * decay * dtc[:, :, None, :, :]
    y_intra = jnp.einsum('bctsh,bcshp->bcthp', scores, xc)
    w_end = jnp.exp(acum[:, :, -1:, :] - acum) * dtc
    states = jnp.einsum('bcsh,bcshn,bcshp->bchpn', w_end, bh, xc)
    chunk_decay = jnp.exp(acum[:, :, -1, :])

    def step(h, inp):
        st, dec = inp
        return dec[:, :, None, None] * h + st, h

    h_last, h_in = lax.scan(step, h0.astype(F32), (jnp.moveaxis(states, 1, 0), jnp.moveaxis(chunk_decay, 1, 0)))
    h_in = jnp.moveaxis(h_in, 0, 1)
    y_inter = jnp.einsum('bcthn,bchpn->bcthp', ch * jnp.exp(acum)[..., None], h_in)
    return (y_intra + y_inter).reshape(b, L, H, P), h_last


def even_mixer(hn, w_in, w_out, sinks, w_pool, pool_scale, k_buf, v_buf, pool_buf, is_prompt):
    b, L, _ = hn.shape
    proj = hn @ w_in
    q, k, v, pin = jnp.split(proj, [D_ATTN, D_ATTN + D_KV, D_ATTN + 2 * D_KV], axis=-1)
    q = q.reshape(b, L, N_Q_HEADS, HEAD_DIM)
    k = k.reshape(b, L, N_KV_HEADS, HEAD_DIM)
    v = v.reshape(b, L, N_KV_HEADS, HEAD_DIM)
    if is_prompt:
        o, new_k, new_v = attn_prompt(q, k, v, sinks)
        pos0 = 0
    else:
        o, new_k, new_v = attn_sample(q, k, v, k_buf, v_buf, sinks)
        pos0 = PAST_LEN
    y_pool, new_pool = pool_mix(pin, pool_buf, pos0, w_pool, pool_scale)
    out = jnp.concatenate([o, y_pool], axis=-1) @ w_out
    return out, (new_k, new_v, new_pool)


def odd_mixer(hn, w_in, w_out, conv_w_c, conv_b_c, w_rg, b_rg, w_ig, b_ig, lam,
              conv_w_d, conv_b_d, dt_bias, a_log, d_skip, norm_w_d,
              lru_conv_buf, lru_h0, ssd_conv_buf, ssd_h0, is_prompt):
    b, L, _ = hn.shape
    proj = hn @ w_in
    cx, cg, z, xbc, dt_raw = jnp.split(
        proj, [D_LRU, 2 * D_LRU, 2 * D_LRU + D_SSD, 2 * D_LRU + D_SSD + D_XBC], axis=-1)
    xc, new_lru_conv = causal_conv(cx, lru_conv_buf, conv_w_c, conv_b_c)
    h, new_lru_h = rglru(xc, lru_h0, w_rg, b_rg, w_ig, b_ig, lam)
    y_c = h * jax.nn.gelu(cg)
    xbc_c, new_ssd_conv = causal_conv(xbc, ssd_conv_buf, conv_w_d, conv_b_d)
    xbc_c = jax.nn.silu(xbc_c)
    xs, bm, cm = jnp.split(xbc_c, [D_SSD, D_SSD + SSD_GROUPS * SSD_STATE], axis=-1)
    xs = xs.reshape(b, L, SSD_HEADS, SSD_HEAD_DIM)
    bm = bm.reshape(b, L, SSD_GROUPS, SSD_STATE)
    cm = cm.reshape(b, L, SSD_GROUPS, SSD_STATE)
    dt = jax.nn.softplus((dt_raw + dt_bias).astype(F32))
    a = -jnp.exp(a_log.astype(F32))
    if is_prompt:
        y_m, h_m = ssd_scan(xs[:, :N_META], dt[:, :N_META], a, bm[:, :N_META], cm[:, :N_META], ssd_h0, N_META)
        y_r, h_T = ssd_scan(xs[:, N_META:], dt[:, N_META:], a, bm[:, N_META:], cm[:, N_META:], h_m, SSD_CHUNK)
        y = jnp.concatenate([y_m, y_r], axis=1)
    else:
        y, h_T = ssd_scan(xs, dt, a, bm, cm, ssd_h0, L)
    y = y + d_skip.astype(F32)[:, None] * xs.astype(F32)
    y_d = rmsnorm(y.reshape(b, L, D_SSD) * jax.nn.silu(z.astype(F32)), norm_w_d).astype(hn.dtype)
    out = jnp.concatenate([y_c, y_d], axis=-1) @ w_out
    return out, (new_lru_conv, new_lru_h, new_ssd_conv, h_T.astype(ssd_h0.dtype))


def hier_moe_block(xb, w_rg, b_rg, w_re, b_re, w_gate, w_up, w_down):
    lg = (xb @ w_rg + b_rg).astype(F32)
    g_star = jnp.argmax(lg, axis=-1)
    g_hot = jax.nn.one_hot(g_star, MOE_GROUPS, dtype=F32)
    p_group = jnp.sum(jax.nn.softmax(lg, axis=-1) * g_hot, axis=-1, keepdims=True)
    le = (xb @ w_re + b_re).astype(F32).reshape(-1, MOE_GROUPS, EXPERTS_PER_GROUP)
    le_g = jnp.einsum('nge,ng->ne', le, g_hot)
    top_v, top_i = lax.top_k(le_g, MOE_TOP_K)
    gate = jax.nn.softmax(top_v, axis=-1) * p_group
    expert = g_star[:, None] * EXPERTS_PER_GROUP + top_i
    combine = jnp.einsum('nk,nke->ne', gate, jax.nn.one_hot(expert, N_EXPERTS, dtype=F32))
    hg = jnp.einsum('nd,edf->nef', xb, w_gate)
    hu = jnp.einsum('nd,edf->nef', xb, w_up)
    h = jax.nn.silu(hg) * hu * combine[:, :, None].astype(xb.dtype)
    return jnp.einsum('nef,efd->nd', h, w_down)


def channel_mixer(x, w_rg, b_rg, w_re, b_re, w_gate, w_up, w_down):
    shp = x.shape
    xf = x.reshape(-1, D_MODEL)
    n = xf.shape[0]
    xf = jnp.pad(xf, ((0, (-n) % MOE_BLOCK), (0, 0)))
    y = lax.map(lambda xb: hier_moe_block(xb, w_rg, b_rg, w_re, b_re, w_gate, w_up, w_down),
                xf.reshape(-1, MOE_BLOCK, D_MODEL))
    return y.reshape(-1, D_MODEL)[:n].reshape(shp)


def trunk(x, states, p, is_prompt):
    swa_k, swa_v, pool_buf, lru_conv, lru_h, ssd_conv, ssd_h = states
    even_out, odd_out = [], []
    for li in range(DEPTH):
        hn = rmsnorm(x, p['norm_mix'][li])
        if li % 2 == 0:
            e = li // 2
            kb = None if swa_k is None else swa_k[e]
            vb = None if swa_v is None else swa_v[e]
            mix, st = even_mixer(hn, p['attn_w_in'][e], p['attn_w_out'][e], p['attn_sinks'][e],
                                 p['pool_w'][e], p['pool_scale'][e], kb, vb, pool_buf[e], is_prompt)
            even_out.append(st)
        else:
            o = li // 2
            mix, st = odd_mixer(hn, p['rec_w_in'][o], p['rec_w_out'][o], p['lru_conv_w'][o], p['lru_conv_b'][o],
                                p['lru_w_rg'][o], p['lru_b_rg'][o], p['lru_w_ig'][o], p['lru_b_ig'][o],
                                p['lru_lambda'][o], p['ssd_conv_w'][o], p['ssd_conv_b'][o], p['ssd_dt_bias'][o],
                                p['ssd_a_log'][o], p['ssd_d'][o], p['ssd_norm_w'][o],
                                lru_conv[o], lru_h[o], ssd_conv[o], ssd_h[o], is_prompt)
            odd_out.append(st)
        x = x + mix
        x = x + channel_mixer(rmsnorm(x, p['norm_ffn'][li]), p['moe_w_rg'][li], p['moe_b_rg'][li],
                              p['moe_w_re'][li], p['moe_b_re'][li], p['moe_w_gate'][li],
                              p['moe_w_up'][li], p['moe_w_down'][li])
    y = rmsnorm(x, p['norm_final'])
    new_even = [jnp.stack(list(s), axis=0) for s in zip(*even_out)]
    new_odd = [jnp.stack(list(s), axis=0) for s in zip(*odd_out)]
    return y, new_even + new_odd


def setup_inputs(seed: int = 0) -> dict:
    key = jax.random.key(seed)
    ks = iter(jax.random.split(key, 64))

    def nrm(shape, scale=1.0):
        return scale * jax.random.normal(next(ks), shape, F32)

    def gain(shape, s=0.02):
        return 1.0 + s * jax.random.normal(next(ks), shape, F32)

    a_init = jax.random.uniform(next(ks), (N_ODD, D_LRU), F32, 0.9, 0.999)
    sig = a_init ** (1.0 / LRU_C)
    lru_lambda = jnp.log(sig) - jnp.log1p(-sig)
    dt0 = jnp.exp(jax.random.uniform(next(ks), (N_ODD, SSD_HEADS), F32, math.log(1e-3), math.log(1e-1)))
    ssd_dt_bias = dt0 + jnp.log(-jnp.expm1(-dt0))
    ssd_a_log = jnp.log(jax.random.uniform(next(ks), (N_ODD, SSD_HEADS), F32, 1.0, 16.0))
    return {
        'x_prompt': nrm((BATCH, SEQ, D_MODEL)),
        'x_sample': nrm((DEC_BATCH, DEC_SEQ, D_MODEL)),
        'cache_swa_k': nrm((N_EVEN, DEC_BATCH, N_META + WINDOW, N_KV_HEADS, HEAD_DIM)),
        'cache_swa_v': nrm((N_EVEN, DEC_BATCH, N_META + WINDOW, N_KV_HEADS, HEAD_DIM)),
        'cache_pool': nrm((N_EVEN, DEC_BATCH, POOL_BUF, D_POOL)),
        'state_lru_conv': nrm((N_ODD, DEC_BATCH, CONV_WIDTH - 1, D_LRU)),
        'state_lru_h': nrm((N_ODD, DEC_BATCH, D_LRU), 0.5),
        'state_ssd_conv': nrm((N_ODD, DEC_BATCH, CONV_WIDTH - 1, D_XBC)),
        'state_ssd_h': nrm((N_ODD, DEC_BATCH, SSD_HEADS, SSD_HEAD_DIM, SSD_STATE), 0.1),
        'meta_tokens': nrm((N_META, D_MODEL)),
        'norm_mix': gain((DEPTH, D_MODEL)),
        'norm_ffn': gain((DEPTH, D_MODEL)),
        'norm_final': gain((D_MODEL,)),
        'attn_w_in': nrm((N_EVEN, D_MODEL, D_IN_EVEN), D_MODEL ** -0.5),
        'attn_w_out': nrm((N_EVEN, D_MIX_EVEN, D_MODEL), D_MIX_EVEN ** -0.5),
        'attn_sinks': nrm((N_EVEN, N_Q_HEADS)),
        'pool_w': nrm((N_EVEN, N_POOL_GROUPS, POOL_GROUP_DIM, POOL_GROUP_DIM), POOL_GROUP_DIM ** -0.5),
        'pool_scale': gain((N_EVEN, D_POOL), 0.1),
        'rec_w_in': nrm((N_ODD, D_MODEL, D_IN_ODD), D_MODEL ** -0.5),
        'rec_w_out': nrm((N_ODD, D_MIX_ODD, D_MODEL), D_MIX_ODD ** -0.5),
        'lru_conv_w': nrm((N_ODD, CONV_WIDTH, D_LRU), CONV_WIDTH ** -0.5),
        'lru_conv_b': nrm((N_ODD, D_LRU), 0.01),
        'lru_w_rg': nrm((N_ODD, LRU_HEADS, LRU_BLOCK, LRU_BLOCK), LRU_BLOCK ** -0.5),
        'lru_b_rg': nrm((N_ODD, D_LRU), 0.01),
        'lru_w_ig': nrm((N_ODD, LRU_HEADS, LRU_BLOCK, LRU_BLOCK), LRU_BLOCK ** -0.5),
        'lru_b_ig': nrm((N_ODD, D_LRU), 0.01),
        'lru_lambda': lru_lambda,
        'ssd_conv_w': nrm((N_ODD, CONV_WIDTH, D_XBC), CONV_WIDTH ** -0.5),
        'ssd_conv_b': nrm((N_ODD, D_XBC), 0.01),
        'ssd_dt_bias': ssd_dt_bias,
        'ssd_a_log': ssd_a_log,
        'ssd_d': gain((N_ODD, SSD_HEADS), 0.1),
        'ssd_norm_w': gain((N_ODD, D_SSD)),
        'moe_w_rg': nrm((DEPTH, D_MODEL, MOE_GROUPS), D_MODEL ** -0.5),
        'moe_b_rg': nrm((DEPTH, MOE_GROUPS), 0.01),
        'moe_w_re': nrm((DEPTH, D_MODEL, N_EXPERTS), D_MODEL ** -0.5),
        'moe_b_re': nrm((DEPTH, N_EXPERTS), 0.01),
        'moe_w_gate': nrm((DEPTH, N_EXPERTS, D_MODEL, D_EXPERT), D_MODEL ** -0.5),
        'moe_w_up': nrm((DEPTH, N_EXPERTS, D_MODEL, D_EXPERT), D_MODEL ** -0.5),
        'moe_w_down': nrm((DEPTH, N_EXPERTS, D_EXPERT, D_MODEL), D_EXPERT ** -0.5),
    }


def reference(x_prompt, x_sample, cache_swa_k, cache_swa_v, cache_pool, state_lru_conv, state_lru_h,
              state_ssd_conv, state_ssd_h, meta_tokens, norm_mix, norm_ffn, norm_final,
              attn_w_in, attn_w_out, attn_sinks, pool_w, pool_scale, rec_w_in, rec_w_out,
              lru_conv_w, lru_conv_b, lru_w_rg, lru_b_rg, lru_w_ig, lru_b_ig, lru_lambda,
              ssd_conv_w, ssd_conv_b, ssd_dt_bias, ssd_a_log, ssd_d, ssd_norm_w,
              moe_w_rg, moe_b_rg, moe_w_re, moe_b_re, moe_w_gate, moe_w_up, moe_w_down):
    p = {
        'norm_mix': norm_mix, 'norm_ffn': norm_ffn, 'norm_final': norm_final,
        'attn_w_in': attn_w_in, 'attn_w_out': attn_w_out, 'attn_sinks': attn_sinks,
        'pool_w': pool_w, 'pool_scale': pool_scale,
        'rec_w_in': rec_w_in, 'rec_w_out': rec_w_out,
        'lru_conv_w': lru_conv_w, 'lru_conv_b': lru_conv_b, 'lru_w_rg': lru_w_rg, 'lru_b_rg': lru_b_rg,
        'lru_w_ig': lru_w_ig, 'lru_b_ig': lru_b_ig, 'lru_lambda': lru_lambda,
        'ssd_conv_w': ssd_conv_w, 'ssd_conv_b': ssd_conv_b, 'ssd_dt_bias': ssd_dt_bias,
        'ssd_a_log': ssd_a_log, 'ssd_d': ssd_d, 'ssd_norm_w': ssd_norm_w,
        'moe_w_rg': moe_w_rg, 'moe_b_rg': moe_b_rg, 'moe_w_re': moe_w_re, 'moe_b_re': moe_b_re,
        'moe_w_gate': moe_w_gate, 'moe_w_up': moe_w_up, 'moe_w_down': moe_w_down,
    }
    b = x_prompt.shape[0]
    dt_ = x_prompt.dtype
    meta = jnp.broadcast_to(meta_tokens.astype(dt_)[None], (b, N_META, D_MODEL))
    xp = jnp.concatenate([meta, x_prompt], axis=1)
    prompt_states = (None, None,
                     jnp.zeros((N_EVEN, b, POOL_BUF, D_POOL), dt_),
                     jnp.zeros((N_ODD, b, CONV_WIDTH - 1, D_LRU), dt_),
                     jnp.zeros((N_ODD, b, D_LRU), dt_),
                     jnp.zeros((N_ODD, b, CONV_WIDTH - 1, D_XBC), dt_),
                     jnp.zeros((N_ODD, b, SSD_HEADS, SSD_HEAD_DIM, SSD_STATE), dt_))
    yp, sp = trunk(xp, prompt_states, p, True)
    y_prompt = yp[:, N_META:]
    swa_k_p, swa_v_p, pool_p, lru_conv_p, lru_h_p, ssd_conv_p, ssd_h_p = sp
    sample_states = (cache_swa_k, cache_swa_v, cache_pool, state_lru_conv, state_lru_h,
                     state_ssd_conv, state_ssd_h)
    y_sample, ss = trunk(x_sample, sample_states, p, False)
    swa_k_s, swa_v_s, pool_s, lru_conv_s, lru_h_s, ssd_conv_s, ssd_h_s = ss
    return (y_prompt, y_sample, swa_k_p, swa_v_p, pool_p, lru_conv_p, lru_h_p, ssd_conv_p, ssd_h_p,
            swa_k_s, swa_v_s, pool_s, lru_conv_s, lru_h_s, ssd_conv_s, ssd_h_s)
```

```python
import functools

import jax
import jax.numpy as jnp
from jax import lax
from jax.experimental import pallas as pl
from jax.experimental.pallas import tpu as pltpu

F32 = jnp.float32
BF16 = jnp.bfloat16
I32 = jnp.int32

D_MODEL = 1024
N_META = 16
EPS = 1e-6
PAST_LEN = 16384
N_Q_HEADS = 16
N_KV_HEADS = 2
GQA = N_Q_HEADS // N_KV_HEADS
HEAD_DIM = 64
WINDOW = 128
D_ATTN = N_Q_HEADS * HEAD_DIM
D_KV = N_KV_HEADS * HEAD_DIM
POOL_WINDOWS = (2, 4, 8, 16)
POOL_GROUP_DIM = 96
D_POOL = len(POOL_WINDOWS) * POOL_GROUP_DIM
POOL_BUF = max(POOL_WINDOWS) - 1
D_MIX_EVEN = D_ATTN + D_POOL
D_LRU = 512
LRU_HEADS = 8
LRU_BLOCK = D_LRU // LRU_HEADS
LRU_C = 8.0
CONV_WIDTH = 4
SSD_HEADS = 8
SSD_HEAD_DIM = 64
D_SSD = SSD_HEADS * SSD_HEAD_DIM
SSD_GROUPS = 2
SSD_STATE = 128
D_XBC = D_SSD + 2 * SSD_GROUPS * SSD_STATE
D_MIX_ODD = D_LRU + D_SSD
MOE_GROUPS = 4
EXPERTS_PER_GROUP = 8
N_EXPERTS = MOE_GROUPS * EXPERTS_PER_GROUP
D_EXPERT = 256

BLK = 128
META_PAD = BLK - N_META
LANES = 128
MOE_TM = 256
NEG_INF = float("-inf")
VMEM_LIMIT = 56 * 1024 * 1024


def _cparams(*sem):
    return pltpu.CompilerParams(dimension_semantics=sem, vmem_limit_bytes=VMEM_LIMIT)


def _token_tile(n_rows):
    nb = n_rows // BLK
    for k in (8, 7, 6, 5, 4, 3, 2, 1):
        if nb % k == 0:
            return k * BLK
    return BLK


def _sigmoid(x):
    return 1.0 / (1.0 + jnp.exp(-x))


def _silu(x):
    return x * _sigmoid(x)


def _softplus(x):
    return jnp.maximum(x, 0.0) + jnp.log1p(jnp.exp(-jnp.abs(x)))


def _gelu_tanh(x):
    return 0.5 * x * (1.0 + jnp.tanh(0.7978845608028654 * (x + 0.044715 * x * x * x)))


def _dot(a, b):
    return jnp.dot(a.astype(BF16), b.astype(BF16), preferred_element_type=F32)


def _dot_nt(a, b):
    return lax.dot_general(a.astype(BF16), b.astype(BF16), (((1,), (1,)), ((), ())),
                           preferred_element_type=F32)


def _rms(x, g):
    ms = jnp.mean(x * x, axis=-1, keepdims=True)
    return x * lax.rsqrt(ms + EPS) * g


def _norm_proj_kernel(x_ref, g_ref, w_ref, *out_refs, splits):
    y = _rms(x_ref[...], g_ref[...]).astype(BF16)
    off = 0
    for o_ref, n in zip(out_refs, splits):
        o_ref[...] = jnp.dot(y, w_ref[:, off:off + n], preferred_element_type=F32)
        off += n


def _norm_proj(x, g, w_bf16, splits):
    n = x.shape[0]
    tt = _token_tile(n)
    return pl.pallas_call(
        functools.partial(_norm_proj_kernel, splits=splits),
        grid=(n // tt,),
        in_specs=[pl.BlockSpec((tt, D_MODEL), lambda i: (i, 0)),
                  pl.BlockSpec((1, D_MODEL), lambda i: (0, 0)),
                  pl.BlockSpec(w_bf16.shape, lambda i: (0, 0))],
        out_specs=[pl.BlockSpec((tt, s), lambda i: (i, 0)) for s in splits],
        out_shape=[jax.ShapeDtypeStruct((n, s), F32) for s in splits],
        compiler_params=_cparams("parallel"),
        name="norm_proj",
    )(x, g.reshape(1, D_MODEL), w_bf16)


def _softmax_sink_av(s, sink, v):
    m = jnp.maximum(jnp.max(s, axis=-1, keepdims=True), sink)
    pr = jnp.exp(s - m)
    den = jnp.sum(pr, axis=-1, keepdims=True) + jnp.exp(sink - m)
    return jnp.dot(pr.astype(BF16), v, preferred_element_type=F32) / den


def _mix0_prompt_kernel(*refs, nblk):
    j = pl.program_id(0)
    o_ref = refs[8]

    @pl.when(j <= nblk)
    def _():
        _mix0_prompt_step(*refs)

    @pl.when(j > nblk)
    def _():
        o_ref[...] = jnp.zeros(o_ref.shape, F32)


def _mix0_prompt_step(sink_ref, q_ref, kvc_ref, kvp_ref, kvm_ref, p_ref, wp_ref, ps_ref,
                      o_ref, seq_ref):
    j = pl.program_id(0)
    nk = 2 * BLK + N_META
    kv_all = jnp.concatenate([kvp_ref[...], kvc_ref[...], kvm_ref[...]], axis=0)
    k_all = kv_all[:, :D_KV].astype(BF16)
    v_all = kv_all[:, D_KV:].astype(BF16)
    ql = lax.broadcasted_iota(I32, (BLK, nk), 0)
    kl = lax.broadcasted_iota(I32, (BLK, nk), 1)
    is_meta_slot = kl >= 2 * BLK
    dist_local = ql - kl + BLK
    dist_meta = N_META + (j - 1) * BLK + ql - (kl - 2 * BLK)
    dist = jnp.where(is_meta_slot, dist_meta, dist_local)
    kl_min = jnp.where(j >= 2, 0, jnp.where(j == 1, BLK, BLK + META_PAD))
    meta_hi = jnp.where(j >= 1, 2 * BLK + N_META, 2 * BLK)
    ok_local = (dist_local >= 0) & (dist_local < WINDOW) & (kl >= kl_min) & (kl < 2 * BLK)
    allowed = ok_local | (is_meta_slot & (kl < meta_hi))
    distf = dist.astype(F32)
    for h in range(N_Q_HEADS):
        g = h // GQA
        slope = 2.0 ** (-8.0 * (h + 1) / N_Q_HEADS)
        s = _dot_nt(q_ref[:, h * HEAD_DIM:(h + 1) * HEAD_DIM],
                    k_all[:, g * HEAD_DIM:(g + 1) * HEAD_DIM]) * (HEAD_DIM ** -0.5)
        s = jnp.where(allowed, s - slope * distf, NEG_INF)
        o_ref[:, h * HEAD_DIM:(h + 1) * HEAD_DIM] = _softmax_sink_av(
            s, sink_ref[h], v_all[:, g * HEAD_DIM:(g + 1) * HEAD_DIM])

    @pl.when(j == 0)
    def _():
        seq_ref[0:N_META, :] = jnp.zeros((N_META, D_POOL), F32)

    seq_ref[N_META:N_META + BLK, :] = p_ref[...]
    s0 = seq_ref[...]
    s1 = s0 + pltpu.roll(s0, 1, 0)
    s2 = s1 + pltpu.roll(s1, 2, 0)
    s3 = s2 + pltpu.roll(s2, 4, 0)
    s4 = s3 + pltpu.roll(s3, 8, 0)
    rows = N_META + BLK
    col = lax.broadcasted_iota(I32, (rows, D_POOL), 1)
    row = lax.broadcasted_iota(I32, (rows, D_POOL), 0) - N_META
    wsum = jnp.where(col < POOL_GROUP_DIM, s1,
                     jnp.where(col < 2 * POOL_GROUP_DIM, s2,
                               jnp.where(col < 3 * POOL_GROUP_DIM, s3, s4)))
    wlen = jnp.where(col < POOL_GROUP_DIM, 2,
                     jnp.where(col < 2 * POOL_GROUP_DIM, 4,
                               jnp.where(col < 3 * POOL_GROUP_DIM, 8, 16)))
    pos = row + jnp.where(j == 0, -META_PAD, N_META + (j - 1) * BLK)
    cnt = jnp.clip(jnp.minimum(pos + 1, wlen), 1, 16).astype(F32)
    d = (wsum / cnt - s0)[N_META:, :]
    o_ref[:, D_ATTN:] = _dot(d, wp_ref[...]) * ps_ref[...]
    seq_ref[0:N_META, :] = p_ref[BLK - N_META:, :]


def _mix0_prompt(q, kv, p, sinks, wpool_bd, pool_scale, t_rows, meta_blk):
    n = q.shape[0]
    nblk = t_rows // BLK

    def cur(j):
        return jnp.where(j == 0, meta_blk, j - 1)

    meta16 = (meta_blk * BLK + META_PAD) // N_META
    return pl.pallas_call(
        functools.partial(_mix0_prompt_kernel, nblk=nblk),
        grid=(meta_blk + 1,),
        in_specs=[pl.BlockSpec(memory_space=pltpu.SMEM),
                  pl.BlockSpec((BLK, D_ATTN), lambda j: (cur(j), 0)),
                  pl.BlockSpec((BLK, 2 * D_KV), lambda j: (cur(j), 0)),
                  pl.BlockSpec((BLK, 2 * D_KV), lambda j: (jnp.maximum(j - 2, 0), 0)),
                  pl.BlockSpec((N_META, 2 * D_KV), lambda j: (meta16, 0)),
                  pl.BlockSpec((BLK, D_POOL), lambda j: (cur(j), 0)),
                  pl.BlockSpec((D_POOL, D_POOL), lambda j: (0, 0)),
                  pl.BlockSpec((1, D_POOL), lambda j: (0, 0))],
        out_specs=pl.BlockSpec((BLK, D_MIX_EVEN), lambda j: (cur(j), 0)),
        out_shape=jax.ShapeDtypeStruct((n, D_MIX_EVEN), F32),
        scratch_shapes=[pltpu.VMEM((N_META + BLK, D_POOL), F32)],
        compiler_params=_cparams("arbitrary"),
        name="mix0_prompt",
    )(sinks, q, kv, kv, kv, p, wpool_bd, pool_scale.reshape(1, D_POOL))


def _attn_sample_kernel(sink_ref, q_ref, ck_ref, cv_ref, kn_ref, vn_ref, o_ref, *, n_new):
    nk = N_META + WINDOW + n_new
    nq = n_new * GQA
    k_all = jnp.concatenate([ck_ref[...], kn_ref[...]], axis=1).astype(BF16)
    v_all = jnp.concatenate([cv_ref[...], vn_ref[...]], axis=1).astype(BF16)
    r = lax.broadcasted_iota(I32, (nq, nk), 0)
    slot = lax.broadcasted_iota(I32, (nq, nk), 1)
    t = r // GQA
    hh = r - t * GQA
    kpos = jnp.where(slot < N_META, slot,
                     jnp.where(slot < N_META + WINDOW, PAST_LEN - WINDOW - N_META + slot,
                               PAST_LEN - N_META - WINDOW + slot))
    dist = PAST_LEN + t - kpos
    allowed = (dist >= 0) & ((slot < N_META) | (dist < WINDOW))
    distf = dist.astype(F32)
    hh1 = hh[:, :1]
    for g in range(N_KV_HEADS):
        slope = jnp.exp2(-8.0 * (g * GQA + hh + 1).astype(F32) / N_Q_HEADS)
        sink = jnp.zeros((nq, 1), F32)
        for i in range(GQA):
            sink = jnp.where(hh1 == i, sink_ref[g * GQA + i], sink)
        kg = k_all[:, :, g * HEAD_DIM:(g + 1) * HEAD_DIM]
        vg = v_all[:, :, g * HEAD_DIM:(g + 1) * HEAD_DIM]
        s = jnp.einsum("bqd,bkd->bqk", q_ref[:, g].astype(BF16), kg,
                       preferred_element_type=F32) * (HEAD_DIM ** -0.5)
        s = jnp.where(allowed[None], s - (slope * distf)[None], NEG_INF)
        sink = sink[None]
        m = jnp.maximum(jnp.max(s, axis=-1, keepdims=True), sink)
        pr = jnp.exp(s - m)
        den = jnp.sum(pr, axis=-1, keepdims=True) + jnp.exp(sink - m)
        o = jnp.einsum("bqk,bkd->bqd", pr.astype(BF16), vg, preferred_element_type=F32)
        o_ref[:, g] = o / den


def _attn_sample(qs, ck, cv, kn, vn, sinks):
    b, _, nq, _ = qs.shape
    n_new = kn.shape[1]
    bs = 16 if b % 16 == 0 else b
    nc = N_META + WINDOW
    return pl.pallas_call(
        functools.partial(_attn_sample_kernel, n_new=n_new),
        grid=(b // bs,),
        in_specs=[pl.BlockSpec(memory_space=pltpu.SMEM),
                  pl.BlockSpec((bs, N_KV_HEADS, nq, HEAD_DIM), lambda i: (i, 0, 0, 0)),
                  pl.BlockSpec((bs, nc, D_KV), lambda i: (i, 0, 0)),
                  pl.BlockSpec((bs, nc, D_KV), lambda i: (i, 0, 0)),
                  pl.BlockSpec((bs, n_new, D_KV), lambda i: (i, 0, 0)),
                  pl.BlockSpec((bs, n_new, D_KV), lambda i: (i, 0, 0))],
        out_specs=pl.BlockSpec((bs, N_KV_HEADS, nq, HEAD_DIM), lambda i: (i, 0, 0, 0)),
        out_shape=jax.ShapeDtypeStruct(qs.shape, F32),
        compiler_params=_cparams("parallel"),
        name="attn_sample",
    )(sinks, qs, ck, cv, kn, vn)


def _pool_sample_kernel(buf_ref, p_ref, wp_ref, ps_ref, o_ref, *, n_new):
    seq = [buf_ref[i] for i in range(POOL_BUF)] + [p_ref[i] for i in range(n_new)]
    b = p_ref.shape[1]
    col = lax.broadcasted_iota(I32, (b, D_POOL), 1)
    for t in range(n_new):
        cur = seq[POOL_BUF + t]
        acc = cur
        sums = {}
        for back in range(1, max(POOL_WINDOWS)):
            acc = acc + seq[POOL_BUF + t - back]
            if back + 1 in POOL_WINDOWS:
                sums[back + 1] = acc
        mean = jnp.where(col < POOL_GROUP_DIM, sums[2] / 2.0,
                         jnp.where(col < 2 * POOL_GROUP_DIM, sums[4] / 4.0,
                                   jnp.where(col < 3 * POOL_GROUP_DIM, sums[8] / 8.0,
                                             sums[16] / 16.0)))
        o_ref[t] = _dot(mean - cur, wp_ref[...]) * ps_ref[...]


def _pool_sample(buf_tm, p_tm, wpool_bd, pool_scale):
    n_new = p_tm.shape[0]
    return pl.pallas_call(
        functools.partial(_pool_sample_kernel, n_new=n_new),
        out_shape=jax.ShapeDtypeStruct(p_tm.shape, F32),
        compiler_params=pltpu.CompilerParams(vmem_limit_bytes=VMEM_LIMIT),
        name="pool_sample",
    )(buf_tm, p_tm, wpool_bd, pool_scale.reshape(1, D_POOL))


def _outproj_route_kernel(x_ref, mix_ref, w_ref, g_ref, wrh_ref, wrl_ref, br_ref,
                          x1_ref, xn_ref, info_ref, cnt_ref, acc_ref, *, valid_lo, valid_mid, tt):
    i = pl.program_id(0)
    x1 = x_ref[...] + _dot(mix_ref[...], w_ref[...])
    x1_ref[...] = x1
    xn = _rms(x1, g_ref[...])
    xn_ref[...] = xn
    hi = xn.astype(BF16)
    lo = (xn - hi.astype(F32)).astype(BF16)
    logits = (jnp.dot(hi, wrh_ref[...], preferred_element_type=F32)
              + jnp.dot(lo, wrh_ref[...], preferred_element_type=F32)
              + jnp.dot(hi, wrl_ref[...], preferred_element_type=F32)) + br_ref[...]
    lane = lax.broadcasted_iota(I32, (tt, LANES), 1)
    lanef = lane.astype(F32)
    lg = jnp.where(lane < MOE_GROUPS, logits, NEG_INF)
    mg = jnp.max(lg, axis=-1, keepdims=True)
    gstar = jnp.min(jnp.where(lg == mg, lanef, 1e9), axis=-1, keepdims=True)
    pgroup = 1.0 / jnp.sum(jnp.exp(lg - mg), axis=-1, keepdims=True)
    lo_lane = MOE_GROUPS + gstar * EXPERTS_PER_GROUP
    le = jnp.where((lanef >= lo_lane) & (lanef < lo_lane + EXPERTS_PER_GROUP), logits, NEG_INF)
    v1 = jnp.max(le, axis=-1, keepdims=True)
    i1 = jnp.min(jnp.where(le == v1, lanef, 1e9), axis=-1, keepdims=True)
    le2 = jnp.where(lanef == i1, NEG_INF, le)
    v2 = jnp.max(le2, axis=-1, keepdims=True)
    i2 = jnp.min(jnp.where(le2 == v2, lanef, 1e9), axis=-1, keepdims=True)
    tq = jnp.exp(v2 - v1)
    w1 = pgroup / (1.0 + tq)
    w2 = pgroup * tq / (1.0 + tq)
    e1 = i1 - MOE_GROUPS
    e2 = i2 - MOE_GROUPS
    rowg = i * tt + lax.broadcasted_iota(I32, (tt, LANES), 0)
    valid = (rowg < valid_lo) | (rowg >= valid_mid)
    oh1 = jnp.where(valid & (lanef == e1), 1.0, 0.0)
    oh2 = jnp.where(valid & (lanef == e2), 1.0, 0.0)
    both = oh1 + oh2
    rr = lax.broadcasted_iota(I32, (tt, tt), 0)
    cc = lax.broadcasted_iota(I32, (tt, tt), 1)
    tril = jnp.where(cc < rr, 1.0, 0.0).astype(BF16)

    @pl.when(i == 0)
    def _():
        acc_ref[...] = jnp.zeros((1, LANES), F32)

    base = acc_ref[...] + jnp.dot(tril, both.astype(BF16), preferred_element_type=F32)
    rank1 = jnp.sum(oh1 * base, axis=-1, keepdims=True)
    rank2 = jnp.sum(oh2 * base, axis=-1, keepdims=True)
    total = acc_ref[...] + jnp.sum(both, axis=0, keepdims=True)
    acc_ref[...] = total
    cnt_ref[...] = jnp.broadcast_to(total, (8, LANES))
    info = jnp.where(lane == 0, e1, jnp.where(lane == 1, e2, jnp.where(lane == 2, w1, jnp.where(
        lane == 3, w2, jnp.where(lane == 4, rank1, jnp.where(lane == 5, rank2, 0.0))))))
    info_ref[...] = info


def _outproj_route(x, mix, w_out_bf16, g_ffn, wr_hi, wr_lo, br, valid_lo, valid_mid):
    n = x.shape[0]
    tt = _token_tile(n)
    dm = mix.shape[1]
    return pl.pallas_call(
        functools.partial(_outproj_route_kernel, valid_lo=valid_lo, valid_mid=valid_mid, tt=tt),
        grid=(n // tt,),
        in_specs=[pl.BlockSpec((tt, D_MODEL), lambda i: (i, 0)),
                  pl.BlockSpec((tt, dm), lambda i: (i, 0)),
                  pl.BlockSpec((dm, D_MODEL), lambda i: (0, 0)),
                  pl.BlockSpec((1, D_MODEL), lambda i: (0, 0)),
                  pl.BlockSpec((D_MODEL, LANES), lambda i: (0, 0)),
                  pl.BlockSpec((D_MODEL, LANES), lambda i: (0, 0)),
                  pl.BlockSpec((1, LANES), lambda i: (0, 0))],
        out_specs=[pl.BlockSpec((tt, D_MODEL), lambda i: (i, 0)),
                   pl.BlockSpec((tt, D_MODEL), lambda i: (i, 0)),
                   pl.BlockSpec((tt, LANES), lambda i: (i, 0)),
                   pl.BlockSpec((8, LANES), lambda i: (0, 0))],
        out_shape=[jax.ShapeDtypeStruct((n, D_MODEL), F32),
                   jax.ShapeDtypeStruct((n, D_MODEL), F32),
                   jax.ShapeDtypeStruct((n, LANES), F32),
                   jax.ShapeDtypeStruct((8, LANES), F32)],
        scratch_shapes=[pltpu.VMEM((1, LANES), F32)],
        compiler_params=_cparams("arbitrary"),
        name="outproj_route",
    )(x, mix, w_out_bf16, g_ffn.reshape(1, D_MODEL), wr_hi, wr_lo, br)


def _row_ranges(i, tt, valid_lo, valid_mid):
    lo = i * tt
    a_end = jnp.clip(valid_lo - lo, 0, tt)
    b_start = jnp.clip(valid_mid - lo, 0, tt)
    return a_end, b_start


def _dispatch_kernel(pos_ref, zt_ref, xn_ref, xs_ref, zbuf, sem, zsem, *, tt, valid_lo, valid_mid,
                     n_tiles):
    i = pl.program_id(0)

    @pl.when(i == 0)
    def _():
        zbuf[...] = jnp.zeros((MOE_TM, D_MODEL), F32)

        def zero_copy(t):
            return pltpu.make_async_copy(zbuf, xs_ref.at[pl.ds(t * MOE_TM, MOE_TM)], zsem)

        def zstart(t, c):
            @pl.when(zt_ref[t] != 0)
            def _():
                zero_copy(t).start()
            return c

        def zwait(t, c):
            @pl.when(zt_ref[t] != 0)
            def _():
                zero_copy(t).wait()
            return c

        lax.fori_loop(0, n_tiles, zstart, 0)
        lax.fori_loop(0, n_tiles, zwait, 0)

    a_end, b_start = _row_ranges(i, tt, valid_lo, valid_mid)

    def row_copy(r, k):
        return pltpu.make_async_copy(xn_ref.at[pl.ds(r, 1)],
                                     xs_ref.at[pl.ds(pos_ref[0, 0, k * tt + r], 1)], sem)

    def start(r, c):
        row_copy(r, 0).start()
        row_copy(r, 1).start()
        return c

    def wait(r, c):
        row_copy(r, 0).wait()
        row_copy(r, 1).wait()
        return c

    lax.fori_loop(0, a_end, start, 0)
    lax.fori_loop(b_start, tt, start, 0)
    lax.fori_loop(0, a_end, wait, 0)
    lax.fori_loop(b_start, tt, wait, 0)


def _dispatch(xn, pos, zero_tiles, n_sorted, valid_lo, valid_mid):
    n = xn.shape[0]
    tt = _token_tile(n)
    return pl.pallas_call(
        functools.partial(_dispatch_kernel, tt=tt, valid_lo=valid_lo, valid_mid=valid_mid,
                          n_tiles=n_sorted // MOE_TM),
        grid=(n // tt,),
        in_specs=[pl.BlockSpec((1, 1, 2 * tt), lambda i: (i, 0, 0), memory_space=pltpu.SMEM),
                  pl.BlockSpec(memory_space=pltpu.SMEM),
                  pl.BlockSpec((tt, D_MODEL), lambda i: (i, 0))],
        out_specs=pl.BlockSpec(memory_space=pl.ANY),
        out_shape=jax.ShapeDtypeStruct((n_sorted, D_MODEL), F32),
        scratch_shapes=[pltpu.VMEM((MOE_TM, D_MODEL), F32),
                        pltpu.SemaphoreType.DMA(()),
                        pltpu.SemaphoreType.DMA(())],
        compiler_params=_cparams("arbitrary"),
        name="moe_dispatch",
    )(pos, zero_tiles, xn)


def _expert_ffn_kernel(te_ref, nu_ref, xs_ref, wg_ref, wu_ref, wd_ref, ys_ref, wgb, wub, wdb):
    j = pl.program_id(0)
    changed = (j == 0) | (te_ref[j] != te_ref[jnp.maximum(j - 1, 0)])

    @pl.when((j < nu_ref[0]) & changed)
    def _():
        wgb[...] = wg_ref[0].astype(BF16)
        wub[...] = wu_ref[0].astype(BF16)
        wdb[...] = wd_ref[0].astype(BF16)

    @pl.when(j < nu_ref[0])
    def _():
        x = xs_ref[...].astype(BF16)
        hg = jnp.dot(x, wgb[...], preferred_element_type=F32)
        hu = jnp.dot(x, wub[...], preferred_element_type=F32)
        h = (_silu(hg) * hu).astype(BF16)
        ys_ref[...] = jnp.dot(h, wdb[...], preferred_element_type=F32)

    @pl.when(j >= nu_ref[0])
    def _():
        ys_ref[...] = jnp.zeros(ys_ref.shape, F32)


def _expert_ffn(xs, tile_expert, n_used, w_gate, w_up, w_down):
    n_sorted = xs.shape[0]
    nt = n_sorted // MOE_TM

    def row_map(j, te, nu):
        return (jnp.minimum(j, nu[0] - 1), 0)

    def w_map(j, te, nu):
        return (te[j], 0, 0)

    return pl.pallas_call(
        _expert_ffn_kernel,
        grid_spec=pltpu.PrefetchScalarGridSpec(
            num_scalar_prefetch=2,
            grid=(nt,),
            in_specs=[pl.BlockSpec((MOE_TM, D_MODEL), row_map),
                      pl.BlockSpec((1, D_MODEL, D_EXPERT), w_map),
                      pl.BlockSpec((1, D_MODEL, D_EXPERT), w_map),
                      pl.BlockSpec((1, D_EXPERT, D_MODEL), w_map)],
            out_specs=pl.BlockSpec((MOE_TM, D_MODEL), lambda j, te, nu: (j, 0)),
            scratch_shapes=[pltpu.VMEM((D_MODEL, D_EXPERT), BF16),
                            pltpu.VMEM((D_MODEL, D_EXPERT), BF16),
                            pltpu.VMEM((D_EXPERT, D_MODEL), BF16)]),
        out_shape=jax.ShapeDtypeStruct((n_sorted, D_MODEL), F32),
        compiler_params=_cparams("arbitrary"),
        name="expert_ffn",
    )(tile_expert, n_used, xs, w_gate, w_up, w_down)


def _combine_kernel(pos_ref, x1_ref, info_ref, g_ref, ys_ref, o_ref, ybuf, sem, *,
                    tt, valid_lo, valid_mid, final_norm):
    i = pl.program_id(0)
    a_end, b_start = _row_ranges(i, tt, valid_lo, valid_mid)

    @pl.when(i == 0)
    def _():
        ybuf[...] = jnp.zeros(ybuf.shape, F32)

    def row_copy(r, k):
        return pltpu.make_async_copy(ys_ref.at[pl.ds(pos_ref[0, 0, k * tt + r], 1)],
                                     ybuf.at[k, pl.ds(r, 1)], sem)

    def start(r, c):
        row_copy(r, 0).start()
        row_copy(r, 1).start()
        return c

    def wait(r, c):
        row_copy(r, 0).wait()
        row_copy(r, 1).wait()
        return c

    lax.fori_loop(0, a_end, start, 0)
    lax.fori_loop(b_start, tt, start, 0)
    lax.fori_loop(0, a_end, wait, 0)
    lax.fori_loop(b_start, tt, wait, 0)
    info = info_ref[...]
    w1 = info[:, 2:3]
    w2 = info[:, 3:4]
    row = lax.broadcasted_iota(I32, (tt, 1), 0)
    valid = (row < a_end) | (row >= b_start)
    y = jnp.where(valid, w1 * ybuf[0] + w2 * ybuf[1], 0.0)
    out = x1_ref[...] + y
    if final_norm:
        out = _rms(out, g_ref[...])
    o_ref[...] = out


def _combine(x1, info, pos, ys, g_final, valid_lo, valid_mid, final_norm):
    n = x1.shape[0]
    tt = _token_tile(n)
    return pl.pallas_call(
        functools.partial(_combine_kernel, tt=tt, valid_lo=valid_lo, valid_mid=valid_mid,
                          final_norm=final_norm),
        grid=(n // tt,),
        in_specs=[pl.BlockSpec((1, 1, 2 * tt), lambda i: (i, 0, 0), memory_space=pltpu.SMEM),
                  pl.BlockSpec((tt, D_MODEL), lambda i: (i, 0)),
                  pl.BlockSpec((tt, LANES), lambda i: (i, 0)),
                  pl.BlockSpec((1, D_MODEL), lambda i: (0, 0)),
                  pl.BlockSpec(memory_space=pl.ANY)],
        out_specs=pl.BlockSpec((tt, D_MODEL), lambda i: (i, 0)),
        out_shape=jax.ShapeDtypeStruct((n, D_MODEL), F32),
        scratch_shapes=[pltpu.VMEM((2, tt, D_MODEL), F32),
                        pltpu.SemaphoreType.DMA(())],
        compiler_params=_cparams("arbitrary"),
        name="moe_combine",
    )(pos, x1, info, g_final.reshape(1, D_MODEL), ys)


def _moe(x1, xn, info, cnt, w_gate, w_up, w_down, g_final, valid_lo, valid_mid, final_norm):
    n = x1.shape[0]
    tt = _token_tile(n)
    n_valid = valid_lo + (n - valid_mid)
    nt = (2 * n_valid + N_EXPERTS * (MOE_TM - 1)) // MOE_TM
    counts = cnt[0, :N_EXPERTS].astype(I32)
    tiles = (counts + MOE_TM - 1) // MOE_TM
    tile_end = jnp.cumsum(tiles)
    tile_start = tile_end - tiles
    n_used = tile_end[-1:]
    e1 = info[:, 0].astype(I32)
    e2 = info[:, 1].astype(I32)
    rows = jnp.arange(n)
    valid = (rows < valid_lo) | (rows >= valid_mid)
    pos1 = jnp.where(valid, tile_start[e1] * MOE_TM + info[:, 4].astype(I32), 0)
    pos2 = jnp.where(valid, tile_start[e2] * MOE_TM + info[:, 5].astype(I32), 0)
    pos = jnp.concatenate([pos1.reshape(n // tt, 1, tt), pos2.reshape(n // tt, 1, tt)], axis=2)
    tile_expert = jnp.searchsorted(tile_end, jnp.minimum(jnp.arange(nt), n_used[0] - 1),
                                   side="right").astype(I32)
    tile_ids = jnp.arange(nt)
    is_last = jnp.any((tile_ids[:, None] == tile_end[None, :] - 1) & (tiles[None, :] > 0), axis=1)
    zero_tiles = (is_last | (tile_ids >= n_used[0])).astype(I32)
    xs = _dispatch(xn, pos, zero_tiles, nt * MOE_TM, valid_lo, valid_mid)
    ys = _expert_ffn(xs, tile_expert, n_used.astype(I32), w_gate, w_up, w_down)
    return _combine(x1, info, pos, ys, g_final, valid_lo, valid_mid, final_norm)


def _shift_rows(x, k, fill):
    row = lax.broadcasted_iota(I32, x.shape, 0)
    return jnp.where(row >= k, pltpu.roll(x, k, 0), fill)


def _lru_gates(xc, wgate_ref, bgate_ref, lam_sp):
    gates = _dot(xc, wgate_ref[...]) + bgate_ref[...]
    r = _sigmoid(gates[:, :D_LRU])
    ig = _sigmoid(gates[:, D_LRU:])
    log_a = -LRU_C * r * lam_sp
    a = jnp.exp(log_a)
    bx = jnp.sqrt(-jnp.tanh(log_a) * (a * a + 1.0)) * (ig * xc)
    return a, bx


def _conv_block(seq_ref, u_ref, w_ref, b_ref, j):
    @pl.when(j == 0)
    def _():
        seq_ref[0:8, :] = jnp.zeros((8, seq_ref.shape[1]), F32)

    seq_ref[8:8 + BLK, :] = u_ref[...]
    out = b_ref[...]
    for tap in range(CONV_WIDTH):
        out = out + seq_ref[pl.ds(8 - (CONV_WIDTH - 1) + tap, BLK), :] * w_ref[tap:tap + 1, :]
    seq_ref[0:8, :] = seq_ref[BLK:BLK + 8, :]
    return out


def _mix1_prompt_kernel(*refs, nblk):
    j = pl.program_id(0)
    o_ref = refs[16]

    @pl.when(j <= nblk)
    def _():
        _mix1_prompt_step(*refs)

    @pl.when(j > nblk)
    def _():
        o_ref[...] = jnp.zeros(o_ref.shape, F32)


def _mix1_prompt_step(cx_ref, cg_ref, z_ref, xbc_ref, dt_ref,
                      cwc_ref, cbc_ref, wgate_ref, bgate_ref, lam_ref,
                      cwd_ref, cbd_ref, dtb_ref, alog_ref, dskip_ref, nw_ref,
                      o_ref, lruh_ref, ssdh_ref,
                      seqc_ref, seqd_ref, hl_ref, hs_ref):
    j = pl.program_id(0)
    rowc = lax.broadcasted_iota(I32, (BLK, 1), 0)
    live = rowc >= jnp.where(j >= 1, 0, META_PAD)

    @pl.when(j == 0)
    def _():
        hl_ref[...] = jnp.zeros(hl_ref.shape, F32)
        hs_ref[...] = jnp.zeros(hs_ref.shape, F32)

    xc = _conv_block(seqc_ref, cx_ref, cwc_ref, cbc_ref, j)
    a, bx = _lru_gates(xc, wgate_ref, bgate_ref, _softplus(-lam_ref[...]))
    a = jnp.where(live, a, 1.0)
    bx = jnp.where(live, bx, 0.0)
    k = 1
    while k < BLK:
        bx = a * _shift_rows(bx, k, 0.0) + bx
        a = a * _shift_rows(a, k, 1.0)
        k *= 2
    h = a * hl_ref[...] + bx
    hl_ref[...] = h[BLK - 1:BLK, :]
    lruh_ref[...] = h[BLK - 1:BLK, :]
    o_ref[:, :D_LRU] = h * _gelu_tanh(cg_ref[...])

    xbc = _silu(_conv_block(seqd_ref, xbc_ref, cwd_ref, cbd_ref, j))
    xs = xbc[:, :D_SSD]
    dt = jnp.where(live, _softplus(dt_ref[...] + dtb_ref[...]), 0.0)
    a_neg = -jnp.exp(alog_ref[...])
    acum = dt * a_neg
    k = 1
    while k < BLK:
        acum = acum + _shift_rows(acum, k, 0.0)
        k *= 2
    acum_t = acum.T
    dt_t = dt.T
    xs_t = xs.T
    tri = (lax.broadcasted_iota(I32, (BLK, BLK), 0) >= lax.broadcasted_iota(I32, (BLK, BLK), 1))
    rep = SSD_HEADS // SSD_GROUPS
    ys = []
    for g in range(SSD_GROUPS):
        bg = xbc[:, D_SSD + g * SSD_STATE:D_SSD + (g + 1) * SSD_STATE]
        cgm = xbc[:, D_SSD + (SSD_GROUPS + g) * SSD_STATE:D_SSD + (SSD_GROUPS + g + 1) * SSD_STATE]
        cb = _dot_nt(cgm, bg)
        for hh in range(rep):
            hd = g * rep + hh
            col = acum[:, hd:hd + 1]
            rowv = acum_t[hd:hd + 1, :]
            decay = jnp.exp(jnp.where(tri, col - rowv, NEG_INF))
            scores = cb * decay * dt_t[hd:hd + 1, :]
            xh = xs[:, hd * SSD_HEAD_DIM:(hd + 1) * SSD_HEAD_DIM]
            y = _dot(scores, xh)
            hprev = hs_ref[hd]
            y = y + jnp.exp(col) * _dot_nt(cgm, hprev)
            last = acum_t[hd:hd + 1, BLK - 1:BLK]
            w_end = jnp.exp(last - rowv) * dt_t[hd:hd + 1, :]
            xw_t = xs_t[hd * SSD_HEAD_DIM:(hd + 1) * SSD_HEAD_DIM, :] * w_end
            hnew = jnp.exp(last) * hprev + _dot(xw_t, bg)
            hs_ref[hd] = hnew
            ssdh_ref[hd] = hnew
            ys.append(y + dskip_ref[:, hd * SSD_HEAD_DIM:(hd + 1) * SSD_HEAD_DIM] * xh)
    y = jnp.concatenate(ys, axis=1) * _silu(z_ref[...])
    o_ref[:, D_LRU:] = _rms(y, nw_ref[...])


def _mix1_prompt(cx, cg, z, xbc, dt, wts, t_rows, meta_blk):
    n = cx.shape[0]
    nblk = t_rows // BLK

    def cur(j):
        return (jnp.where(j == 0, meta_blk, j - 1), 0)

    def const2(j):
        return (0, 0)

    in_specs = [pl.BlockSpec((BLK, D_LRU), cur), pl.BlockSpec((BLK, D_LRU), cur),
                pl.BlockSpec((BLK, D_SSD), cur), pl.BlockSpec((BLK, D_XBC), cur),
                pl.BlockSpec((BLK, LANES), cur)]
    in_specs += [pl.BlockSpec(w.shape, const2) for w in wts]
    return pl.pallas_call(
        functools.partial(_mix1_prompt_kernel, nblk=nblk),
        grid=(meta_blk + 1,),
        in_specs=in_specs,
        out_specs=[pl.BlockSpec((BLK, D_MIX_ODD), cur),
                   pl.BlockSpec((1, D_LRU), const2),
                   pl.BlockSpec((SSD_HEADS, SSD_HEAD_DIM, SSD_STATE), lambda j: (0, 0, 0))],
        out_shape=[jax.ShapeDtypeStruct((n, D_MIX_ODD), F32),
                   jax.ShapeDtypeStruct((1, D_LRU), F32),
                   jax.ShapeDtypeStruct((SSD_HEADS, SSD_HEAD_DIM, SSD_STATE), F32)],
        scratch_shapes=[pltpu.VMEM((8 + BLK, D_LRU), F32),
                        pltpu.VMEM((8 + BLK, D_XBC), F32),
                        pltpu.VMEM((1, D_LRU), F32),
                        pltpu.VMEM((SSD_HEADS, SSD_HEAD_DIM, SSD_STATE), F32)],
        compiler_params=_cparams("arbitrary"),
        name="mix1_prompt",
    )(cx, cg, z, xbc, dt, *wts)


def _conv_step(seq, t, w_ref, b_ref):
    out = b_ref[...]
    for tap in range(CONV_WIDTH):
        out = out + seq[t + tap] * w_ref[tap:tap + 1, :]
    return out


def _lru_sample_kernel(cx_ref, cg_ref, buf_ref, h0_ref, cwc_ref, cbc_ref, wgate_ref, bgate_ref,
                       lam_ref, o_ref, hout_ref, *, n_new):
    seq = [buf_ref[i] for i in range(CONV_WIDTH - 1)] + [cx_ref[i] for i in range(n_new)]
    lam_sp = _softplus(-lam_ref[...])
    h = h0_ref[...]
    for t in range(n_new):
        xc = _conv_step(seq, t, cwc_ref, cbc_ref)
        a, bx = _lru_gates(xc, wgate_ref, bgate_ref, lam_sp)
        h = a * h + bx
        o_ref[t] = h * _gelu_tanh(cg_ref[t])
    hout_ref[...] = h


def _lru_sample(cx_tm, cg_tm, buf_tm, h0, wts):
    n_new = cx_tm.shape[0]
    return pl.pallas_call(
        functools.partial(_lru_sample_kernel, n_new=n_new),
        out_shape=[jax.ShapeDtypeStruct(cx_tm.shape, F32), jax.ShapeDtypeStruct(h0.shape, F32)],
        compiler_params=pltpu.CompilerParams(vmem_limit_bytes=VMEM_LIMIT),
        name="lru_sample",
    )(cx_tm, cg_tm, buf_tm, h0, *wts)


def _split3(x):
    hi = x.astype(BF16)
    r = x - hi.astype(F32)
    mid = r.astype(BF16)
    lo = (r - mid.astype(F32)).astype(BF16)
    return hi, mid, lo


def _ssd_sample_kernel(xbc_ref, dt_ref, z_ref, buf_ref, h0_ref, cwd_ref, cbd_ref, dtb_ref, alog_ref,
                       dskip_ref, nw_ref, o_ref, hout_ref, xc_ref, dec_ref, dtx_ref, y_ref, *, n_new, bs):
    seq = [buf_ref[i] for i in range(CONV_WIDTH - 1)] + [xbc_ref[i] for i in range(n_new)]
    a_neg = -jnp.exp(alog_ref[...])
    hcol = lax.broadcasted_iota(I32, (LANES, D_SSD), 1) // SSD_HEAD_DIM
    expand = jnp.where(lax.broadcasted_iota(I32, (LANES, D_SSD), 0) == hcol, 1.0, 0.0).astype(BF16)

    def widen(v):
        parts = _split3(v)
        return sum(jnp.dot(p, expand, preferred_element_type=F32) for p in parts)

    for t in range(n_new):
        xc = _silu(_conv_step(seq, t, cwd_ref, cbd_ref))
        xc_ref[t] = xc
        dt = _softplus(dt_ref[t] + dtb_ref[...])
        dtw = widen(dt)
        dec_ref[t] = jnp.exp(widen(dt * a_neg))
        dtx_ref[t] = dtw * xc[:, :D_SSD]

    zero_rows = jnp.zeros((LANES - 2 * n_new * SSD_GROUPS, SSD_STATE), F32)
    half = D_SSD // SSD_GROUPS

    def per_seq(b, c):
        rows = [xc_ref[t, pl.ds(b, 1), :] for t in range(n_new)]
        bc = [r[:, D_SSD + k * SSD_STATE:D_SSD + (k + 1) * SSD_STATE]
              for r in rows for k in range(2 * SSD_GROUPS)]
        bc_t = jnp.concatenate(bc + [zero_rows], axis=0).T
        ht = h0_ref[b].T
        for t in range(n_new):
            bcol = [jnp.broadcast_to(bc_t[:, t * 4 + g:t * 4 + g + 1], (SSD_STATE, half))
                    for g in range(SSD_GROUPS)]
            ccol = [jnp.broadcast_to(bc_t[:, t * 4 + SSD_GROUPS + g:t * 4 + SSD_GROUPS + g + 1],
                                     (SSD_STATE, half)) for g in range(SSD_GROUPS)]
            bw = jnp.concatenate(bcol, axis=1)
            cw = jnp.concatenate(ccol, axis=1)
            ht = ht * dec_ref[t, pl.ds(b, 1), :] + bw * dtx_ref[t, pl.ds(b, 1), :]
            y = jnp.sum(cw * ht, axis=0, keepdims=True)
            y_ref[t, pl.ds(b, 1), :] = y + dskip_ref[...] * rows[t][:, :D_SSD]
        hout_ref[b] = ht.T
        return c

    lax.fori_loop(0, bs, per_seq, 0)
    for t in range(n_new):
        o_ref[t] = _rms(y_ref[t] * _silu(z_ref[t]), nw_ref[...])


def _ssd_sample(xbc_tm, dt_tm, z_tm, buf_tm, h0, wts):
    n_new, b, _ = xbc_tm.shape
    bs = 16 if b % 16 == 0 else b
    hp = SSD_HEADS * SSD_HEAD_DIM

    def tm(c):
        return pl.BlockSpec((n_new, bs, c), lambda i: (0, i, 0))

    in_specs = [tm(D_XBC), tm(LANES), tm(D_SSD),
                pl.BlockSpec((CONV_WIDTH - 1, bs, D_XBC), lambda i: (0, i, 0)),
                pl.BlockSpec((bs, hp, SSD_STATE), lambda i: (i, 0, 0))]
    in_specs += [pl.BlockSpec(w.shape, lambda i: (0, 0)) for w in wts]
    return pl.pallas_call(
        functools.partial(_ssd_sample_kernel, n_new=n_new, bs=bs),
        grid=(b // bs,),
        in_specs=in_specs,
        out_specs=[tm(D_SSD), pl.BlockSpec((bs, hp, SSD_STATE), lambda i: (i, 0, 0))],
        out_shape=[jax.ShapeDtypeStruct((n_new, b, D_SSD), F32),
                   jax.ShapeDtypeStruct((b, hp, SSD_STATE), F32)],
        scratch_shapes=[pltpu.VMEM((n_new, bs, D_XBC), F32),
                        pltpu.VMEM((n_new, bs, D_SSD), F32),
                        pltpu.VMEM((n_new, bs, D_SSD), F32),
                        pltpu.VMEM((n_new, bs, D_SSD), F32)],
        compiler_params=_cparams("parallel"),
        name="ssd_sample",
    )(xbc_tm, dt_tm, z_tm, buf_tm, h0, *wts)


def _block_diag(w):
    g, c, _ = w.shape
    eye = jnp.eye(g, dtype=w.dtype)
    return (eye[:, None, :, None] * w[:, :, None, :]).reshape(g * c, g * c)


def _router_weights(w_rg, b_rg, w_re, b_re):
    w = jnp.concatenate([w_rg, w_re], axis=1)
    w = jnp.pad(w, ((0, 0), (0, LANES - w.shape[1])))
    hi = w.astype(BF16)
    lo = (w - hi.astype(F32)).astype(BF16)
    b = jnp.pad(jnp.concatenate([b_rg, b_re]), (0, LANES - MOE_GROUPS - N_EXPERTS)).reshape(1, LANES)
    return hi, lo, b


def kernel(x_prompt, x_sample, cache_swa_k, cache_swa_v, cache_pool, state_lru_conv, state_lru_h, state_ssd_conv, state_ssd_h, meta_tokens, norm_mix, norm_ffn, norm_final, attn_w_in, attn_w_out, attn_sinks, pool_w, pool_scale, rec_w_in, rec_w_out, lru_conv_w, lru_conv_b, lru_w_rg, lru_b_rg, lru_w_ig, lru_b_ig, lru_lambda, ssd_conv_w, ssd_conv_b, ssd_dt_bias, ssd_a_log, ssd_d, ssd_norm_w, moe_w_rg, moe_b_rg, moe_w_re, moe_b_re, moe_w_gate, moe_w_up, moe_w_down):
    t_rows = x_prompt.shape[1]
    b, s_new = x_sample.shape[:2]
    ns = b * s_new
    nsp = -(-ns // BLK) * BLK
    n = t_rows + nsp + BLK
    meta_blk = (t_rows + nsp) // BLK
    valid_lo = t_rows + ns
    valid_mid = t_rows + nsp + META_PAD
    assert x_prompt.shape[0] == 1 and t_rows % BLK == 0 and b % 8 == 0 and ns % BLK == 0

    x = jnp.concatenate([
        x_prompt[0],
        x_sample.transpose(1, 0, 2).reshape(ns, D_MODEL),
        jnp.zeros((nsp - ns + META_PAD, D_MODEL), F32),
        meta_tokens.astype(F32)], axis=0)

    def sample_tm(a):
        return a[t_rows:t_rows + ns].reshape(s_new, b, a.shape[1])

    def put_sample(full, upd_tm):
        return lax.dynamic_update_slice(full, upd_tm.reshape(ns, upd_tm.shape[-1]), (t_rows, 0))

    q, kv, p = _norm_proj(x, norm_mix[0], attn_w_in[0].astype(BF16), (D_ATTN, 2 * D_KV, D_POOL))
    wpool_bd = _block_diag(pool_w[0]).astype(BF16)
    mix = _mix0_prompt(q, kv, p, attn_sinks[0], wpool_bd, pool_scale[0], t_rows, meta_blk)
    nq = s_new * GQA
    qs = sample_tm(q).reshape(s_new, b, N_KV_HEADS, GQA, HEAD_DIM).transpose(1, 2, 0, 3, 4)
    qs = qs.reshape(b, N_KV_HEADS, nq, HEAD_DIM)
    kv_s = sample_tm(kv).transpose(1, 0, 2)
    kn, vn = kv_s[:, :, :D_KV], kv_s[:, :, D_KV:]
    ck = cache_swa_k[0].reshape(b, N_META + WINDOW, D_KV)
    cv = cache_swa_v[0].reshape(b, N_META + WINDOW, D_KV)
    o_s = _attn_sample(qs, ck, cv, kn, vn, attn_sinks[0])
    o_s = o_s.reshape(b, N_KV_HEADS, s_new, GQA, HEAD_DIM).transpose(2, 0, 1, 3, 4).reshape(s_new, b, D_ATTN)
    p_s = sample_tm(p)
    pool_buf_tm = cache_pool[0].transpose(1, 0, 2)
    yp_s = _pool_sample(pool_buf_tm, p_s, wpool_bd, pool_scale[0])
    mix = put_sample(mix, jnp.concatenate([o_s, yp_s], axis=-1))
    wr_hi, wr_lo, br = _router_weights(moe_w_rg[0], moe_b_rg[0], moe_w_re[0], moe_b_re[0])
    x1, xn, info, cnt = _outproj_route(x, mix, attn_w_out[0].astype(BF16), norm_ffn[0],
                                       wr_hi, wr_lo, br, valid_lo, valid_mid)
    x = _moe(x1, xn, info, cnt, moe_w_gate[0], moe_w_up[0], moe_w_down[0], norm_final,
             valid_lo, valid_mid, False)

    meta_rows = slice(n - N_META, n)
    last_rows = slice(t_rows - WINDOW, t_rows)
    swa_k_p = jnp.concatenate([kv[meta_rows, :D_KV], kv[last_rows, :D_KV]], axis=0)
    swa_v_p = jnp.concatenate([kv[meta_rows, D_KV:], kv[last_rows, D_KV:]], axis=0)
    swa_k_p = swa_k_p.reshape(1, 1, N_META + WINDOW, N_KV_HEADS, HEAD_DIM)
    swa_v_p = swa_v_p.reshape(1, 1, N_META + WINDOW, N_KV_HEADS, HEAD_DIM)
    pool_p = p[t_rows - POOL_BUF:t_rows].reshape(1, 1, POOL_BUF, D_POOL)
    swa_k_s = jnp.concatenate([ck[:, :N_META], ck[:, N_META + s_new:], kn], axis=1)
    swa_v_s = jnp.concatenate([cv[:, :N_META], cv[:, N_META + s_new:], vn], axis=1)
    swa_k_s = swa_k_s.reshape(1, b, N_META + WINDOW, N_KV_HEADS, HEAD_DIM)
    swa_v_s = swa_v_s.reshape(1, b, N_META + WINDOW, N_KV_HEADS, HEAD_DIM)
    pool_s = jnp.concatenate([cache_pool[0], p_s.transpose(1, 0, 2)], axis=1)[:, -POOL_BUF:][None]

    d_in_odd = rec_w_in.shape[2]
    w_in1 = jnp.pad(rec_w_in[0], ((0, 0), (0, LANES - SSD_HEADS))).astype(BF16)
    assert d_in_odd == 2 * D_LRU + D_SSD + D_XBC + SSD_HEADS
    cx, cg, z, xbc, dt = _norm_proj(x, norm_mix[1], w_in1, (D_LRU, D_LRU, D_SSD, D_XBC, LANES))
    wgate = jnp.concatenate([_block_diag(lru_w_rg[0]), _block_diag(lru_w_ig[0])], axis=1).astype(BF16)
    bgate = jnp.concatenate([lru_b_rg[0], lru_b_ig[0]]).reshape(1, 2 * D_LRU)
    pad8 = (0, LANES - SSD_HEADS)
    dtb = jnp.pad(ssd_dt_bias[0], pad8).reshape(1, LANES)
    alog = jnp.pad(ssd_a_log[0], pad8).reshape(1, LANES)
    dskip = jnp.repeat(ssd_d[0], SSD_HEAD_DIM).reshape(1, D_SSD)
    lru_w = (lru_conv_w[0], lru_conv_b[0].reshape(1, D_LRU), wgate, bgate, lru_lambda[0].reshape(1, D_LRU))
    ssd_w = (ssd_conv_w[0], ssd_conv_b[0].reshape(1, D_XBC), dtb, alog, dskip, ssd_norm_w[0].reshape(1, D_SSD))
    mix, lru_h_p, ssd_h_p = _mix1_prompt(cx, cg, z, xbc, dt, lru_w + ssd_w, t_rows, meta_blk)
    cx_s, cg_s, z_s, xbc_s, dt_s = (sample_tm(a) for a in (cx, cg, z, xbc, dt))
    yc_s, lru_h_s = _lru_sample(cx_s, cg_s, state_lru_conv[0].transpose(1, 0, 2), state_lru_h[0], lru_w)
    h0 = state_ssd_h[0].reshape(b, SSD_HEADS * SSD_HEAD_DIM, SSD_STATE)
    yd_s, ssd_h_s = _ssd_sample(xbc_s, dt_s, z_s, state_ssd_conv[0].transpose(1, 0, 2), h0, ssd_w)
    mix = put_sample(mix, jnp.concatenate([yc_s, yd_s], axis=-1))
    wr_hi, wr_lo, br = _router_weights(moe_w_rg[1], moe_b_rg[1], moe_w_re[1], moe_b_re[1])
    x1, xn, info, cnt = _outproj_route(x, mix, rec_w_out[0].astype(BF16), norm_ffn[1],
                                       wr_hi, wr_lo, br, valid_lo, valid_mid)
    y = _moe(x1, xn, info, cnt, moe_w_gate[1], moe_w_up[1], moe_w_down[1], norm_final,
             valid_lo, valid_mid, True)

    tail = CONV_WIDTH - 1
    lru_conv_p = cx[t_rows - tail:t_rows].reshape(1, 1, tail, D_LRU)
    ssd_conv_p = xbc[t_rows - tail:t_rows].reshape(1, 1, tail, D_XBC)
    lru_conv_s = jnp.concatenate([state_lru_conv[0], cx_s.transpose(1, 0, 2)], axis=1)[:, -tail:][None]
    ssd_conv_s = jnp.concatenate([state_ssd_conv[0], xbc_s.transpose(1, 0, 2)], axis=1)[:, -tail:][None]

    y_prompt = y[:t_rows][None]
    y_sample = y[t_rows:t_rows + ns].reshape(s_new, b, D_MODEL).transpose(1, 0, 2)
    return (y_prompt, y_sample, swa_k_p, swa_v_p, pool_p, lru_conv_p,
            lru_h_p.reshape(1, 1, D_LRU), ssd_conv_p,
            ssd_h_p.reshape(1, 1, SSD_HEADS, SSD_HEAD_DIM, SSD_STATE),
            swa_k_s, swa_v_s, pool_s, lru_conv_s, lru_h_s[None], ssd_conv_s,
            ssd_h_s.reshape(1, b, SSD_HEADS, SSD_HEAD_DIM, SSD_STATE))
```

```python
import functools

import jax
import jax.numpy as jnp
from jax import lax
from jax.experimental import pallas as pl
from jax.experimental.pallas import tpu as pltpu

F32 = jnp.float32
BF16 = jnp.bfloat16
I32 = jnp.int32

D_MODEL = 1024
N_META = 16
EPS = 1e-6
PAST_LEN = 16384
N_Q_HEADS = 16
N_KV_HEADS = 2
GQA = N_Q_HEADS // N_KV_HEADS
HEAD_DIM = 64
WINDOW = 128
D_ATTN = N_Q_HEADS * HEAD_DIM
D_KV = N_KV_HEADS * HEAD_DIM
POOL_WINDOWS = (2, 4, 8, 16)
POOL_GROUP_DIM = 96
D_POOL = len(POOL_WINDOWS) * POOL_GROUP_DIM
POOL_BUF = max(POOL_WINDOWS) - 1
D_MIX_EVEN = D_ATTN + D_POOL
D_LRU = 512
LRU_HEADS = 8
LRU_BLOCK = D_LRU // LRU_HEADS
LRU_C = 8.0
CONV_WIDTH = 4
SSD_HEADS = 8
SSD_HEAD_DIM = 64
D_SSD = SSD_HEADS * SSD_HEAD_DIM
SSD_GROUPS = 2
SSD_STATE = 128
D_XBC = D_SSD + 2 * SSD_GROUPS * SSD_STATE
D_MIX_ODD = D_LRU + D_SSD
MOE_GROUPS = 4
EXPERTS_PER_GROUP = 8
N_EXPERTS = MOE_GROUPS * EXPERTS_PER_GROUP
D_EXPERT = 256

BLK = 128
META_PAD = BLK - N_META
LANES = 128
MOE_TM = 256
NEG_INF = float("-inf")
VMEM_LIMIT = 56 * 1024 * 1024


def _cparams(*sem):
    return pltpu.CompilerParams(dimension_semantics=sem, vmem_limit_bytes=VMEM_LIMIT)


TT = 512
SUB = 8


def _small_tile_ranges(t_rows, ns, n):
    out = {}
    for k in range(t_rows // TT, n // TT):
        lo = k * TT
        ranges = []
        a0, a1 = max(lo, t_rows), min(lo + TT, t_rows + ns)
        if a1 > a0:
            ranges.append((a0 - lo, a1 - lo))
        b0 = max(lo, n - N_META)
        if lo + TT > b0:
            ranges.append((b0 - lo, TT))
        out[k] = tuple(ranges)
    return out


def _for_tile_rows(i, n_full, ranges, fn):
    @pl.when(i < n_full)
    def _():
        fn(0, TT)

    for k, rs in ranges.items():
        if rs:
            @pl.when(i == k)
            def _(rs=rs):
                for lo, hi in rs:
                    fn(lo, hi)


def _slab(c, rows, base=0):
    return pl.ds(base + c, rows, stride=SUB)


def _tile_of(r):
    return pl.ds(pl.multiple_of(r * SUB, SUB), SUB)


def _sigmoid(x):
    return 1.0 / (1.0 + jnp.exp(-x))


def _silu(x):
    return x * _sigmoid(x)


def _softplus(x):
    return jnp.maximum(x, 0.0) + jnp.log1p(jnp.exp(-jnp.abs(x)))


def _gelu_tanh(x):
    return 0.5 * x * (1.0 + jnp.tanh(0.7978845608028654 * (x + 0.044715 * x * x * x)))


def _dot(a, b):
    return jnp.dot(a.astype(BF16), b.astype(BF16), preferred_element_type=F32)


def _dot_nt(a, b):
    return lax.dot_general(a.astype(BF16), b.astype(BF16), (((1,), (1,)), ((), ())),
                           preferred_element_type=F32)


def _rms(x, g):
    ms = jnp.mean(x * x, axis=-1, keepdims=True)
    return x * lax.rsqrt(ms + EPS) * g


def _row_specs(xs, width):
    if len(xs) == 1:
        return [pl.BlockSpec((TT, width), lambda i: (i, 0))]
    na = xs[0].shape[0] // TT
    return [pl.BlockSpec((TT, width), lambda i: (jnp.minimum(i, na - 1), 0)),
            pl.BlockSpec((TT, width), lambda i: (jnp.maximum(i - na, 0), 0))]


def _pick_rows(refs, n_first):
    if len(refs) == 1:
        return refs[0][...]
    return jnp.where(pl.program_id(0) < n_first, refs[0][...], refs[1][...])


def _norm_proj_kernel(*refs, splits, n_x, n_first):
    x_refs, (g_ref, w_ref), out_refs = refs[:n_x], refs[n_x:n_x + 2], refs[n_x + 2:]
    y = _rms(_pick_rows(x_refs, n_first), g_ref[...]).astype(BF16)
    off = 0
    for o_ref, n in zip(out_refs, splits):
        o_ref[...] = jnp.dot(y, w_ref[:, off:off + n], preferred_element_type=F32)
        off += n


def _norm_proj(xs, g, w_bf16, splits):
    n = sum(x.shape[0] for x in xs)
    return pl.pallas_call(
        functools.partial(_norm_proj_kernel, splits=splits, n_x=len(xs), n_first=xs[0].shape[0] // TT),
        grid=(n // TT,),
        in_specs=_row_specs(xs, D_MODEL) + [pl.BlockSpec((1, D_MODEL), lambda i: (0, 0)),
                                             pl.BlockSpec(w_bf16.shape, lambda i: (0, 0))],
        out_specs=[pl.BlockSpec((TT, s), lambda i: (i, 0)) for s in splits],
        out_shape=[jax.ShapeDtypeStruct((n, s), F32) for s in splits],
        compiler_params=_cparams("parallel"),
        name="norm_proj",
    )(*xs, g.reshape(1, D_MODEL), w_bf16)


def _mix0_prompt_kernel(*refs, nblk):
    j = pl.program_id(0)
    o_ref = refs[8]

    @pl.when(j <= nblk)
    def _():
        _mix0_prompt_step(*refs)

    @pl.when(j > nblk)
    def _():
        o_ref[...] = jnp.zeros(o_ref.shape, F32)


def _mix0_prompt_step(sink_ref, q_ref, kvc_ref, kvp_ref, kvm_ref, p_ref, wp_ref, ps_ref,
                      o_ref, seq_ref):
    j = pl.program_id(0)
    kv_all = jnp.concatenate([kvp_ref[...], kvc_ref[...], kvm_ref[...],
                              jnp.zeros((META_PAD, 2 * D_KV), F32)], axis=0)
    k_all = kv_all[:, :D_KV]
    v_all = kv_all[:, D_KV:]
    k_rot = pltpu.roll(k_all, HEAD_DIM, 1)
    v_rot = pltpu.roll(v_all, HEAD_DIM, 1)
    lo_half = lax.broadcasted_iota(I32, k_all.shape, 1) < HEAD_DIM

    def halves(own_lanes, other_lanes, g):
        src_lo, src_hi = (own_lanes, other_lanes) if g == 0 else (other_lanes, own_lanes)
        return (jnp.where(lo_half, src_lo, 0.0).astype(BF16), jnp.where(lo_half, 0.0, src_hi).astype(BF16))

    r = lax.broadcasted_iota(I32, (BLK, BLK), 0)
    c = lax.broadcasted_iota(I32, (BLK, BLK), 1)
    own = r >= c
    dist_loc = jnp.where(own, r - c, r - c + BLK).astype(F32)
    d_min = jnp.where(j >= 2, -BLK, 0)
    c_min = jnp.where(j == 0, META_PAD, 0)
    mask_loc = jnp.where((r - c >= d_min) & (c >= c_min), 0.0, NEG_INF)
    meta_hi = jnp.where(j >= 1, N_META, 0)
    dist_meta = jnp.where(c < meta_hi, (N_META + (j - 1) * BLK + r - c).astype(F32), jnp.inf)
    slabs = GQA // 2
    nkeys = 3 * BLK
    for g in range(N_KV_HEADS):
        k_even, k_odd = halves(k_all, k_rot, g)
        v_even, v_odd = halves(v_all, v_rot, g)
        qg = jnp.concatenate([q_ref[:, (g * slabs + i) * LANES:(g * slabs + i + 1) * LANES]
                              for i in range(slabs)], axis=0) * (HEAD_DIM ** -0.5)
        s_all = _dot_nt(qg, jnp.concatenate([k_even, k_odd], axis=0))
        probs, rdens = ([], []), ([], [])
        for i in range(slabs):
            for par in range(2):
                h = g * GQA + 2 * i + par
                slope = 2.0 ** (-8.0 * (h + 1) / N_Q_HEADS)
                sink = sink_ref[h]
                s = s_all[i * BLK:(i + 1) * BLK, par * nkeys:(par + 1) * nkeys]
                s_loc = jnp.where(own, s[:, BLK:2 * BLK], s[:, :BLK]) - slope * dist_loc + mask_loc
                s_met = s[:, 2 * BLK:] - slope * dist_meta
                m = jnp.maximum(jnp.max(jnp.maximum(s_loc, s_met), axis=-1, keepdims=True), sink)
                p_loc = jnp.exp(s_loc - m)
                p_met = jnp.exp(s_met - m)
                den = jnp.sum(p_loc + p_met, axis=-1, keepdims=True) + jnp.exp(sink - m)
                probs[par].append(jnp.concatenate([jnp.where(own, 0.0, p_loc).astype(BF16),
                                                   jnp.where(own, p_loc, 0.0).astype(BF16),
                                                   p_met.astype(BF16)], axis=1))
                rdens[par].append(1.0 / den)
        o = (jnp.dot(jnp.concatenate(probs[0], axis=0), v_even, preferred_element_type=F32)
             * jnp.concatenate(rdens[0], axis=0)
             + jnp.dot(jnp.concatenate(probs[1], axis=0), v_odd, preferred_element_type=F32)
             * jnp.concatenate(rdens[1], axis=0))
        for i in range(slabs):
            o_ref[:, (g * slabs + i) * LANES:(g * slabs + i + 1) * LANES] = o[i * BLK:(i + 1) * BLK, :]

    @pl.when(j == 0)
    def _():
        seq_ref[0:N_META, :] = jnp.zeros((N_META, D_POOL), F32)

    seq_ref[N_META:N_META + BLK, :] = p_ref[...]
    s0 = seq_ref[...]
    s1 = s0 + pltpu.roll(s0, 1, 0)
    s2 = s1 + pltpu.roll(s1, 2, 0)
    s3 = s2 + pltpu.roll(s2, 4, 0)
    s4 = s3 + pltpu.roll(s3, 8, 0)
    rows = N_META + BLK
    col = lax.broadcasted_iota(I32, (rows, D_POOL), 1)
    row = lax.broadcasted_iota(I32, (rows, D_POOL), 0) - N_META
    wsum = jnp.where(col < POOL_GROUP_DIM, s1,
                     jnp.where(col < 2 * POOL_GROUP_DIM, s2,
                               jnp.where(col < 3 * POOL_GROUP_DIM, s3, s4)))
    wlen = jnp.where(col < POOL_GROUP_DIM, 2,
                     jnp.where(col < 2 * POOL_GROUP_DIM, 4,
                               jnp.where(col < 3 * POOL_GROUP_DIM, 8, 16)))
    pos = row + jnp.where(j == 0, -META_PAD, N_META + (j - 1) * BLK)
    cnt = jnp.clip(jnp.minimum(pos + 1, wlen), 1, 16).astype(F32)
    d = (wsum / cnt - s0)[N_META:, :]
    o_ref[:, D_ATTN:] = _dot(d, wp_ref[...]) * ps_ref[...]
    seq_ref[0:N_META, :] = p_ref[BLK - N_META:, :]


def _mix0_prompt(q, kv, p, sinks, wpool_bd, pool_scale, t_rows, meta_blk):
    n = q.shape[0]
    nblk = t_rows // BLK

    def cur(j):
        return jnp.where(j == 0, meta_blk, j - 1)

    meta16 = (meta_blk * BLK + META_PAD) // N_META
    return pl.pallas_call(
        functools.partial(_mix0_prompt_kernel, nblk=nblk),
        grid=(meta_blk + 1,),
        in_specs=[pl.BlockSpec(memory_space=pltpu.SMEM),
                  pl.BlockSpec((BLK, D_ATTN), lambda j: (cur(j), 0)),
                  pl.BlockSpec((BLK, 2 * D_KV), lambda j: (cur(j), 0)),
                  pl.BlockSpec((BLK, 2 * D_KV), lambda j: (jnp.maximum(j - 2, 0), 0)),
                  pl.BlockSpec((N_META, 2 * D_KV), lambda j: (meta16, 0)),
                  pl.BlockSpec((BLK, D_POOL), lambda j: (cur(j), 0)),
                  pl.BlockSpec((D_POOL, D_POOL), lambda j: (0, 0)),
                  pl.BlockSpec((1, D_POOL), lambda j: (0, 0))],
        out_specs=pl.BlockSpec((BLK, D_MIX_EVEN), lambda j: (cur(j), 0)),
        out_shape=jax.ShapeDtypeStruct((n, D_MIX_EVEN), F32),
        scratch_shapes=[pltpu.VMEM((N_META + BLK, D_POOL), F32)],
        compiler_params=_cparams("arbitrary"),
        name="mix0_prompt",
    )(sinks, q, kv, kv, kv, p, wpool_bd, pool_scale.reshape(1, D_POOL))


def _attn_sample_kernel(sink_ref, q_ref, ck_ref, cv_ref, kn_ref, vn_ref, o_ref, *, n_new):
    nk = N_META + WINDOW + n_new
    nq = n_new * GQA
    k_all = jnp.concatenate([ck_ref[...], kn_ref[...]], axis=1).astype(BF16)
    v_all = jnp.concatenate([cv_ref[...], vn_ref[...]], axis=1).astype(BF16)
    r = lax.broadcasted_iota(I32, (nq, nk), 0)
    slot = lax.broadcasted_iota(I32, (nq, nk), 1)
    t = r // GQA
    hh = r - t * GQA
    kpos = jnp.where(slot < N_META, slot,
                     jnp.where(slot < N_META + WINDOW, PAST_LEN - WINDOW - N_META + slot,
                               PAST_LEN - N_META - WINDOW + slot))
    dist = PAST_LEN + t - kpos
    allowed = (dist >= 0) & ((slot < N_META) | (dist < WINDOW))
    distf = dist.astype(F32)
    hh1 = hh[:, :1]
    for g in range(N_KV_HEADS):
        slope = jnp.exp2(-8.0 * (g * GQA + hh + 1).astype(F32) / N_Q_HEADS)
        sink = jnp.zeros((nq, 1), F32)
        for i in range(GQA):
            sink = jnp.where(hh1 == i, sink_ref[g * GQA + i], sink)
        kg = k_all[:, :, g * HEAD_DIM:(g + 1) * HEAD_DIM]
        vg = v_all[:, :, g * HEAD_DIM:(g + 1) * HEAD_DIM]
        s = jnp.einsum("bqd,bkd->bqk", q_ref[:, g].astype(BF16), kg,
                       preferred_element_type=F32) * (HEAD_DIM ** -0.5)
        s = jnp.where(allowed[None], s - (slope * distf)[None], NEG_INF)
        sink = sink[None]
        m = jnp.maximum(jnp.max(s, axis=-1, keepdims=True), sink)
        pr = jnp.exp(s - m)
        den = jnp.sum(pr, axis=-1, keepdims=True) + jnp.exp(sink - m)
        o = jnp.einsum("bqk,bkd->bqd", pr.astype(BF16), vg, preferred_element_type=F32)
        o_ref[:, g] = o / den


def _attn_sample(qs, ck, cv, kn, vn, sinks):
    b, _, nq, _ = qs.shape
    n_new = kn.shape[1]
    bs = 16 if b % 16 == 0 else b
    nc = N_META + WINDOW
    return pl.pallas_call(
        functools.partial(_attn_sample_kernel, n_new=n_new),
        grid=(b // bs,),
        in_specs=[pl.BlockSpec(memory_space=pltpu.SMEM),
                  pl.BlockSpec((bs, N_KV_HEADS, nq, HEAD_DIM), lambda i: (i, 0, 0, 0)),
                  pl.BlockSpec((bs, nc, D_KV), lambda i: (i, 0, 0)),
                  pl.BlockSpec((bs, nc, D_KV), lambda i: (i, 0, 0)),
                  pl.BlockSpec((bs, n_new, D_KV), lambda i: (i, 0, 0)),
                  pl.BlockSpec((bs, n_new, D_KV), lambda i: (i, 0, 0))],
        out_specs=pl.BlockSpec((bs, N_KV_HEADS, nq, HEAD_DIM), lambda i: (i, 0, 0, 0)),
        out_shape=jax.ShapeDtypeStruct(qs.shape, F32),
        compiler_params=_cparams("parallel"),
        name="attn_sample",
    )(sinks, qs, ck, cv, kn, vn)


def _pool_sample_kernel(buf_ref, p_ref, wp_ref, ps_ref, o_ref, *, n_new):
    seq = [buf_ref[i] for i in range(POOL_BUF)] + [p_ref[i] for i in range(n_new)]
    b = p_ref.shape[1]
    col = lax.broadcasted_iota(I32, (b, D_POOL), 1)
    for t in range(n_new):
        cur = seq[POOL_BUF + t]
        acc = cur
        sums = {}
        for back in range(1, max(POOL_WINDOWS)):
            acc = acc + seq[POOL_BUF + t - back]
            if back + 1 in POOL_WINDOWS:
                sums[back + 1] = acc
        mean = jnp.where(col < POOL_GROUP_DIM, sums[2] / 2.0,
                         jnp.where(col < 2 * POOL_GROUP_DIM, sums[4] / 4.0,
                                   jnp.where(col < 3 * POOL_GROUP_DIM, sums[8] / 8.0,
                                             sums[16] / 16.0)))
        o_ref[t] = _dot(mean - cur, wp_ref[...]) * ps_ref[...]


def _pool_sample(buf_tm, p_tm, wpool_bd, pool_scale):
    n_new = p_tm.shape[0]
    return pl.pallas_call(
        functools.partial(_pool_sample_kernel, n_new=n_new),
        out_shape=jax.ShapeDtypeStruct(p_tm.shape, F32),
        compiler_params=pltpu.CompilerParams(vmem_limit_bytes=VMEM_LIMIT),
        name="pool_sample",
    )(buf_tm, p_tm, wpool_bd, pool_scale.reshape(1, D_POOL))


def _outproj_route_kernel(*refs, valid_lo, valid_mid, n_x, n_first, ns):
    tt = TT
    x_refs = refs[:n_x]
    (mix_ref, mixs_ref, w_ref, g_ref, wrh_ref, wrl_ref, br_ref,
     x1_ref, xn_ref, info_ref, infot_ref, cnt_ref, acc_ref) = refs[n_x:]
    i = pl.program_id(0)
    x = _pick_rows(x_refs, n_first)

    @pl.when(i != n_first)
    def _():
        x1_ref[...] = x + _dot(mix_ref[...], w_ref[...])

    @pl.when(i == n_first)
    def _():
        x1_ref[:ns, :] = x[:ns] + _dot(mixs_ref[...], w_ref[...])
        if ns < tt:
            x1_ref[ns:, :] = x[ns:] + _dot(mix_ref[ns:, :], w_ref[...])

    x1 = x1_ref[...]
    xn = _rms(x1, g_ref[...])
    for cc_ in range(D_MODEL // LANES):
        xn_ref[_slab(cc_, tt), :] = xn[:, cc_ * LANES:(cc_ + 1) * LANES]
    hi = xn.astype(BF16)
    lo = (xn - hi.astype(F32)).astype(BF16)
    logits = (jnp.dot(hi, wrh_ref[...], preferred_element_type=F32)
              + jnp.dot(lo, wrh_ref[...], preferred_element_type=F32)
              + jnp.dot(hi, wrl_ref[...], preferred_element_type=F32)) + br_ref[...]
    lane = lax.broadcasted_iota(I32, (tt, LANES), 1)
    lanef = lane.astype(F32)
    lg = jnp.where(lane < MOE_GROUPS, logits, NEG_INF)
    mg = jnp.max(lg, axis=-1, keepdims=True)
    gstar = jnp.min(jnp.where(lg == mg, lanef, 1e9), axis=-1, keepdims=True)
    pgroup = 1.0 / jnp.sum(jnp.exp(lg - mg), axis=-1, keepdims=True)
    lo_lane = MOE_GROUPS + gstar * EXPERTS_PER_GROUP
    le = jnp.where((lanef >= lo_lane) & (lanef < lo_lane + EXPERTS_PER_GROUP), logits, NEG_INF)
    v1 = jnp.max(le, axis=-1, keepdims=True)
    i1 = jnp.min(jnp.where(le == v1, lanef, 1e9), axis=-1, keepdims=True)
    le2 = jnp.where(lanef == i1, NEG_INF, le)
    v2 = jnp.max(le2, axis=-1, keepdims=True)
    i2 = jnp.min(jnp.where(le2 == v2, lanef, 1e9), axis=-1, keepdims=True)
    tq = jnp.exp(v2 - v1)
    w1 = pgroup / (1.0 + tq)
    w2 = pgroup * tq / (1.0 + tq)
    e1 = i1 - MOE_GROUPS
    e2 = i2 - MOE_GROUPS
    rowg = i * tt + lax.broadcasted_iota(I32, (tt, LANES), 0)
    valid = (rowg < valid_lo) | (rowg >= valid_mid)
    oh1 = jnp.where(valid & (lanef == e1), 1.0, 0.0)
    oh2 = jnp.where(valid & (lanef == e2), 1.0, 0.0)
    both = oh1 + oh2
    rr = lax.broadcasted_iota(I32, (tt, tt), 0)
    cc = lax.broadcasted_iota(I32, (tt, tt), 1)
    tril = jnp.where(cc < rr, 1.0, 0.0).astype(BF16)

    @pl.when(i == 0)
    def _():
        acc_ref[...] = jnp.zeros((1, LANES), F32)

    base = acc_ref[...] + jnp.dot(tril, both.astype(BF16), preferred_element_type=F32)
    rank1 = jnp.sum(oh1 * base, axis=-1, keepdims=True)
    rank2 = jnp.sum(oh2 * base, axis=-1, keepdims=True)
    total = acc_ref[...] + jnp.sum(both, axis=0, keepdims=True)
    acc_ref[...] = total
    cnt_ref[...] = jnp.broadcast_to(total, (8, LANES))
    info = jnp.where(lane == 0, e1, jnp.where(lane == 1, e2, jnp.where(lane == 2, w1, jnp.where(
        lane == 3, w2, jnp.where(lane == 4, rank1, jnp.where(lane == 5, rank2, 0.0))))))
    info_ref[...] = info
    infot_ref[0] = info.T[:SUB, :]


def _outproj_route(xs, mix, mix_s, w_out_bf16, g_ffn, wr_hi, wr_lo, br, t_rows, valid_lo, valid_mid):
    n = mix.shape[0]
    ns, dm = mix_s.shape
    nt = n // TT

    def const2(i):
        return (0, 0)

    return pl.pallas_call(
        functools.partial(_outproj_route_kernel, valid_lo=valid_lo, valid_mid=valid_mid,
                          n_x=len(xs), n_first=t_rows // TT, ns=ns),
        grid=(nt,),
        in_specs=_row_specs(xs, D_MODEL) + [
            pl.BlockSpec((TT, dm), lambda i: (i, 0)),
            pl.BlockSpec((ns, dm), const2),
            pl.BlockSpec((dm, D_MODEL), const2),
            pl.BlockSpec((1, D_MODEL), const2),
            pl.BlockSpec((D_MODEL, LANES), const2),
            pl.BlockSpec((D_MODEL, LANES), const2),
            pl.BlockSpec((1, LANES), const2)],
        out_specs=[pl.BlockSpec((TT, D_MODEL), lambda i: (i, 0)),
                   pl.BlockSpec((TT * SUB, LANES), lambda i: (i, 0)),
                   pl.BlockSpec((TT, LANES), lambda i: (i, 0)),
                   pl.BlockSpec((1, SUB, TT), lambda i: (i, 0, 0)),
                   pl.BlockSpec((8, LANES), const2)],
        out_shape=[jax.ShapeDtypeStruct((n, D_MODEL), F32),
                   jax.ShapeDtypeStruct((n * SUB, LANES), F32),
                   jax.ShapeDtypeStruct((n, LANES), F32),
                   jax.ShapeDtypeStruct((nt, SUB, TT), F32),
                   jax.ShapeDtypeStruct((8, LANES), F32)],
        scratch_shapes=[pltpu.VMEM((1, LANES), F32)],
        compiler_params=_cparams("arbitrary"),
        name="outproj_route",
    )(*xs, mix, mix_s, w_out_bf16, g_ffn.reshape(1, D_MODEL), wr_hi, wr_lo, br)


ROWS_PER_ISSUE = 8


def _issue_rows(lo, hi, row_copy):
    assert (hi - lo) % ROWS_PER_ISSUE == 0

    def body(it, c):
        for u in range(ROWS_PER_ISSUE):
            r = lo + it * ROWS_PER_ISSUE + u
            row_copy(r, 0).start(priority=0)
            row_copy(r, 1).start(priority=1)
        return c

    lax.fori_loop(0, (hi - lo) // ROWS_PER_ISSUE, body, 0)


def _dispatch_kernel(pos_ref, zt_ref, xn_ref, xs_ref, zbuf, sem, zsem, *, n_full, ranges, n_tiles):
    i = pl.program_id(0)

    @pl.when(i == 0)
    def _():
        zbuf[...] = jnp.zeros(zbuf.shape, F32)

        def zero_copy(t):
            rows = MOE_TM * SUB
            return pltpu.make_async_copy(zbuf, xs_ref.at[pl.ds(pl.multiple_of(t * rows, rows), rows)], zsem)

        def zstart(t, c):
            @pl.when(zt_ref[t] != 0)
            def _():
                zero_copy(t).start()
            return c

        def zwait(t, c):
            @pl.when(zt_ref[t] != 0)
            def _():
                zero_copy(t).wait()
            return c

        lax.fori_loop(0, n_tiles, zstart, 0)
        lax.fori_loop(0, n_tiles, zwait, 0)

    def row_copy(r, k):
        return pltpu.make_async_copy(xn_ref.at[_tile_of(r)], xs_ref.at[_tile_of(pos_ref[0, 0, k * TT + r])], sem)

    def scatter(lo, hi):
        _issue_rows(lo, hi, row_copy)
        nrow = (hi - lo) * SUB
        for _ in range(2):
            pltpu.make_async_copy(xn_ref.at[pl.ds(lo * SUB, nrow)], xs_ref.at[pl.ds(0, nrow)], sem).wait()

    _for_tile_rows(i, n_full, ranges, scatter)


def _dispatch(xn, pos, zero_tiles, n_sorted, t_rows, ranges):
    n = xn.shape[0] // SUB
    return pl.pallas_call(
        functools.partial(_dispatch_kernel, n_full=t_rows // TT, ranges=ranges,
                          n_tiles=n_sorted // MOE_TM),
        grid=(n // TT,),
        in_specs=[pl.BlockSpec((1, 1, 2 * TT), lambda i: (i, 0, 0), memory_space=pltpu.SMEM),
                  pl.BlockSpec(memory_space=pltpu.SMEM),
                  pl.BlockSpec((TT * SUB, LANES), lambda i: (i, 0))],
        out_specs=pl.BlockSpec(memory_space=pl.ANY),
        out_shape=jax.ShapeDtypeStruct((n_sorted * SUB, LANES), F32),
        scratch_shapes=[pltpu.VMEM((MOE_TM * SUB, LANES), F32),
                        pltpu.SemaphoreType.DMA(()),
                        pltpu.SemaphoreType.DMA(())],
        compiler_params=_cparams("arbitrary"),
        name="moe_dispatch",
    )(pos, zero_tiles, xn)


def _expert_ffn_kernel(te_ref, nu_ref, xs_ref, wg_ref, wu_ref, wd_ref, ys_ref, wgb, wub, wdb):
    j = pl.program_id(0)
    changed = (j == 0) | (te_ref[j] != te_ref[jnp.maximum(j - 1, 0)])

    @pl.when((j < nu_ref[0]) & changed)
    def _():
        wgb[...] = wg_ref[0].astype(BF16)
        wub[...] = wu_ref[0].astype(BF16)
        wdb[...] = wd_ref[0].astype(BF16)

    n_slab = D_MODEL // LANES

    @pl.when(j < nu_ref[0])
    def _():
        x = jnp.concatenate([xs_ref[_slab(c, MOE_TM), :] for c in range(n_slab)], axis=1).astype(BF16)
        hg = jnp.dot(x, wgb[...], preferred_element_type=F32)
        hu = jnp.dot(x, wub[...], preferred_element_type=F32)
        h = (_silu(hg) * hu).astype(BF16)
        y = jnp.dot(h, wdb[...], preferred_element_type=F32)
        for c in range(n_slab):
            ys_ref[_slab(c, MOE_TM), :] = y[:, c * LANES:(c + 1) * LANES]

    @pl.when(j >= nu_ref[0])
    def _():
        ys_ref[...] = jnp.zeros(ys_ref.shape, F32)


def _expert_ffn(xs, tile_expert, n_used, w_gate, w_up, w_down):
    n_sorted = xs.shape[0] // SUB
    nt = n_sorted // MOE_TM

    def row_map(j, te, nu):
        return (jnp.minimum(j, nu[0] - 1), 0)

    def w_map(j, te, nu):
        return (te[j], 0, 0)

    return pl.pallas_call(
        _expert_ffn_kernel,
        grid_spec=pltpu.PrefetchScalarGridSpec(
            num_scalar_prefetch=2,
            grid=(nt,),
            in_specs=[pl.BlockSpec((MOE_TM * SUB, LANES), row_map),
                      pl.BlockSpec((1, D_MODEL, D_EXPERT), w_map),
                      pl.BlockSpec((1, D_MODEL, D_EXPERT), w_map),
                      pl.BlockSpec((1, D_EXPERT, D_MODEL), w_map)],
            out_specs=pl.BlockSpec((MOE_TM * SUB, LANES), lambda j, te, nu: (j, 0)),
            scratch_shapes=[pltpu.VMEM((D_MODEL, D_EXPERT), BF16),
                            pltpu.VMEM((D_MODEL, D_EXPERT), BF16),
                            pltpu.VMEM((D_EXPERT, D_MODEL), BF16)]),
        out_shape=jax.ShapeDtypeStruct((n_sorted * SUB, LANES), F32),
        compiler_params=_cparams("arbitrary"),
        name="expert_ffn",
    )(tile_expert, n_used, xs, w_gate, w_up, w_down)


def _combine_kernel(posc_ref, posn_ref, x1_ref, info_ref, g_ref, ys_ref, *rest,
                    n_full, n_tiles, ranges, valid_lo, valid_mid, final_norm):
    out_refs, (ybuf, sem) = rest[:-2], rest[-2:]
    i = pl.program_id(0)
    slot = lax.rem(i, 2)

    def buf_base(s, k):
        return (s * 2 + k) * (TT * SUB)

    def gather(pos_ref, dst_slot):
        def row_copy(r, k):
            dst = pl.ds(pl.multiple_of(buf_base(dst_slot, k) + r * SUB, SUB), SUB)
            return pltpu.make_async_copy(ys_ref.at[_tile_of(pos_ref[0, 0, k * TT + r])],
                                         ybuf.at[dst], sem.at[dst_slot])
        return lambda lo, hi: _issue_rows(lo, hi, row_copy)

    @pl.when(i == 0)
    def _():
        ybuf[...] = jnp.zeros(ybuf.shape, F32)
        _for_tile_rows(i, n_full, ranges, gather(posc_ref, 0))

    @pl.when(i + 1 < n_tiles)
    def _():
        _for_tile_rows(i + 1, n_full, ranges, gather(posn_ref, 1 - slot))

    def drain(lo, hi):
        nrow = (hi - lo) * SUB
        for k in range(2):
            dst = pl.ds(pl.multiple_of(buf_base(slot, k) + lo * SUB, SUB), nrow)
            pltpu.make_async_copy(ys_ref.at[pl.ds(0, nrow)], ybuf.at[dst], sem.at[slot]).wait()

    _for_tile_rows(i, n_full, ranges, drain)

    info = info_ref[...]
    w1 = info[:, 2:3]
    w2 = info[:, 3:4]
    rowg = i * TT + lax.broadcasted_iota(I32, (TT, 1), 0)
    valid = (rowg < valid_lo) | (rowg >= valid_mid)
    outs = []
    for c in range(D_MODEL // LANES):
        y1 = ybuf[_slab(c, TT, buf_base(slot, 0)), :]
        y2 = ybuf[_slab(c, TT, buf_base(slot, 1)), :]
        outs.append(x1_ref[:, c * LANES:(c + 1) * LANES] + jnp.where(valid, w1 * y1 + w2 * y2, 0.0))
    out = jnp.concatenate(outs, axis=1)
    if not final_norm:
        out_refs[0][...] = out
    else:
        out = _rms(out, g_ref[...])

        @pl.when(i < n_full)
        def _():
            out_refs[0][...] = out

        @pl.when(i >= n_full)
        def _():
            out_refs[1][...] = out


def _combine(x1, info, pos, ys, g_final, t_rows, ranges, valid_lo, valid_mid, final_norm):
    n = x1.shape[0]
    nt = n // TT
    n_full = t_rows // TT
    if final_norm:
        out_specs = [pl.BlockSpec((TT, D_MODEL), lambda i: (jnp.minimum(i, n_full - 1), 0)),
                     pl.BlockSpec((TT, D_MODEL), lambda i: (jnp.maximum(i - n_full, 0), 0))]
        out_shape = [jax.ShapeDtypeStruct((t_rows, D_MODEL), F32),
                     jax.ShapeDtypeStruct((n - t_rows, D_MODEL), F32)]
    else:
        out_specs = [pl.BlockSpec((TT, D_MODEL), lambda i: (i, 0))]
        out_shape = [jax.ShapeDtypeStruct((n, D_MODEL), F32)]
    return pl.pallas_call(
        functools.partial(_combine_kernel, n_full=n_full, n_tiles=nt, ranges=ranges,
                          valid_lo=valid_lo, valid_mid=valid_mid, final_norm=final_norm),
        grid=(nt,),
        in_specs=[pl.BlockSpec((1, 1, 2 * TT), lambda i: (i, 0, 0), memory_space=pltpu.SMEM),
                  pl.BlockSpec((1, 1, 2 * TT), lambda i: (jnp.minimum(i + 1, nt - 1), 0, 0),
                               memory_space=pltpu.SMEM),
                  pl.BlockSpec((TT, D_MODEL), lambda i: (i, 0)),
                  pl.BlockSpec((TT, LANES), lambda i: (i, 0)),
                  pl.BlockSpec((1, D_MODEL), lambda i: (0, 0)),
                  pl.BlockSpec(memory_space=pl.ANY)],
        out_specs=out_specs,
        out_shape=out_shape,
        scratch_shapes=[pltpu.VMEM((2 * 2 * TT * SUB, LANES), F32),
                        pltpu.SemaphoreType.DMA((2,))],
        compiler_params=_cparams("arbitrary"),
        name="moe_combine",
    )(pos, pos, x1, info, g_final.reshape(1, D_MODEL), ys)


def _moe(x1, xn, info, infot, cnt, w_gate, w_up, w_down, g_final, t_rows, ns, final_norm):
    n = x1.shape[0]
    valid_lo, valid_mid = t_rows + ns, n - N_META
    ranges = _small_tile_ranges(t_rows, ns, n)
    n_valid = valid_lo + N_META
    nt = (2 * n_valid + N_EXPERTS * (MOE_TM - 1)) // MOE_TM
    counts = cnt[0, :N_EXPERTS].astype(I32)
    tiles = (counts + MOE_TM - 1) // MOE_TM
    tile_end = jnp.cumsum(tiles)
    row_start = (tile_end - tiles) * MOE_TM
    n_used = tile_end[-1:]
    experts = jnp.arange(N_EXPERTS, dtype=I32)

    def slot_rows(e, rank):
        start = jnp.sum(jnp.where(e[..., None] == experts, row_start, 0), axis=-1)
        return start + rank

    pos = jnp.concatenate([slot_rows(infot[:, 0].astype(I32), infot[:, 4].astype(I32)),
                           slot_rows(infot[:, 1].astype(I32), infot[:, 5].astype(I32))], axis=-1)
    rows = jnp.arange(n).reshape(n // TT, TT)
    valid = (rows < valid_lo) | (rows >= valid_mid)
    pos = jnp.where(jnp.concatenate([valid, valid], axis=-1), pos, 0)[:, None, :]
    tile_ids = jnp.arange(nt, dtype=I32)
    tile_expert = jnp.sum(tile_end[None, :] <= jnp.minimum(tile_ids, n_used[0] - 1)[:, None],
                          axis=1).astype(I32)
    is_last = jnp.any((tile_ids[:, None] == tile_end[None, :] - 1) & (tiles[None, :] > 0), axis=1)
    zero_tiles = (is_last | (tile_ids >= n_used[0])).astype(I32)
    xs = _dispatch(xn, pos, zero_tiles, nt * MOE_TM, t_rows, ranges)
    ys = _expert_ffn(xs, tile_expert, n_used.astype(I32), w_gate, w_up, w_down)
    return _combine(x1, info, pos, ys, g_final, t_rows, ranges, valid_lo, valid_mid, final_norm)


def _shift_rows(x, k, fill):
    row = lax.broadcasted_iota(I32, x.shape, 0)
    return jnp.where(row >= k, pltpu.roll(x, k, 0), fill)


def _lru_gates(xc, wgate_ref, bgate_ref, lam_sp):
    gates = _dot(xc, wgate_ref[...]) + bgate_ref[...]
    r = _sigmoid(gates[:, :D_LRU])
    ig = _sigmoid(gates[:, D_LRU:])
    log_a = -LRU_C * r * lam_sp
    a = jnp.exp(log_a)
    bx = jnp.sqrt(-jnp.tanh(log_a) * (a * a + 1.0)) * (ig * xc)
    return a, bx


def _conv_block(seq_ref, u_ref, w_ref, b_ref, j):
    @pl.when(j == 0)
    def _():
        seq_ref[0:8, :] = jnp.zeros((8, seq_ref.shape[1]), F32)

    seq_ref[8:8 + BLK, :] = u_ref[...]
    out = b_ref[...]
    for tap in range(CONV_WIDTH):
        out = out + seq_ref[pl.ds(8 - (CONV_WIDTH - 1) + tap, BLK), :] * w_ref[tap:tap + 1, :]
    seq_ref[0:8, :] = seq_ref[BLK:BLK + 8, :]
    return out


def _mix1_prompt_kernel(*refs, nblk):
    j = pl.program_id(0)
    o_ref = refs[16]

    @pl.when(j <= nblk)
    def _():
        _mix1_prompt_step(*refs)

    @pl.when(j > nblk)
    def _():
        o_ref[...] = jnp.zeros(o_ref.shape, F32)


def _mix1_prompt_step(cx_ref, cg_ref, z_ref, xbc_ref, dt_ref,
                      cwc_ref, cbc_ref, wgate_ref, bgate_ref, lam_ref,
                      cwd_ref, cbd_ref, dtb_ref, alog_ref, dskip_ref, nw_ref,
                      o_ref, lruh_ref, ssdh_ref,
                      seqc_ref, seqd_ref, hl_ref, hs_ref):
    j = pl.program_id(0)
    rowc = lax.broadcasted_iota(I32, (BLK, 1), 0)
    live = rowc >= jnp.where(j >= 1, 0, META_PAD)

    @pl.when(j == 0)
    def _():
        hl_ref[...] = jnp.zeros(hl_ref.shape, F32)
        hs_ref[...] = jnp.zeros(hs_ref.shape, F32)

    xc = _conv_block(seqc_ref, cx_ref, cwc_ref, cbc_ref, j)
    a, bx = _lru_gates(xc, wgate_ref, bgate_ref, _softplus(-lam_ref[...]))
    a = jnp.where(live, a, 1.0)
    bx = jnp.where(live, bx, 0.0)
    k = 1
    while k < BLK:
        bx = a * _shift_rows(bx, k, 0.0) + bx
        a = a * _shift_rows(a, k, 1.0)
        k *= 2
    h = a * hl_ref[...] + bx
    hl_ref[...] = h[BLK - 1:BLK, :]
    lruh_ref[...] = h[BLK - 1:BLK, :]
    o_ref[:, :D_LRU] = h * _gelu_tanh(cg_ref[...])

    xbc = _silu(_conv_block(seqd_ref, xbc_ref, cwd_ref, cbd_ref, j))
    xs = xbc[:, :D_SSD]
    dt = jnp.where(live, _softplus(dt_ref[...] + dtb_ref[...]), 0.0)
    a_neg = -jnp.exp(alog_ref[...])
    acum = dt * a_neg
    k = 1
    while k < BLK:
        acum = acum + _shift_rows(acum, k, 0.0)
        k *= 2
    acum_t = acum.T
    dt_t = dt.T
    xs_t = xs.T
    tri = (lax.broadcasted_iota(I32, (BLK, BLK), 0) >= lax.broadcasted_iota(I32, (BLK, BLK), 1))
    rep = SSD_HEADS // SSD_GROUPS
    ys = []
    for g in range(SSD_GROUPS):
        bg = xbc[:, D_SSD + g * SSD_STATE:D_SSD + (g + 1) * SSD_STATE]
        cgm = xbc[:, D_SSD + (SSD_GROUPS + g) * SSD_STATE:D_SSD + (SSD_GROUPS + g + 1) * SSD_STATE]
        cb = _dot_nt(cgm, bg)
        for hh in range(rep):
            hd = g * rep + hh
            col = acum[:, hd:hd + 1]
            rowv = acum_t[hd:hd + 1, :]
            decay = jnp.exp(jnp.where(tri, col - rowv, NEG_INF))
            scores = cb * decay * dt_t[hd:hd + 1, :]
            xh = xs[:, hd * SSD_HEAD_DIM:(hd + 1) * SSD_HEAD_DIM]
            y = _dot(scores, xh)
            hprev = hs_ref[hd]
            y = y + jnp.exp(col) * _dot_nt(cgm, hprev)
            last = acum_t[hd:hd + 1, BLK - 1:BLK]
            w_end = jnp.exp(last - rowv) * dt_t[hd:hd + 1, :]
            xw_t = xs_t[hd * SSD_HEAD_DIM:(hd + 1) * SSD_HEAD_DIM, :] * w_end
            hnew = jnp.exp(last) * hprev + _dot(xw_t, bg)
            hs_ref[hd] = hnew
            ssdh_ref[hd] = hnew
            ys.append(y + dskip_ref[:, hd * SSD_HEAD_DIM:(hd + 1) * SSD_HEAD_DIM] * xh)
    y = jnp.concatenate(ys, axis=1) * _silu(z_ref[...])
    o_ref[:, D_LRU:] = _rms(y, nw_ref[...])


def _mix1_prompt(cx, cg, z, xbc, dt, wts, t_rows, meta_blk):
    n = cx.shape[0]
    nblk = t_rows // BLK

    def cur(j):
        return (jnp.where(j == 0, meta_blk, j - 1), 0)

    def const2(j):
        return (0, 0)

    in_specs = [pl.BlockSpec((BLK, D_LRU), cur), pl.BlockSpec((BLK, D_LRU), cur),
                pl.BlockSpec((BLK, D_SSD), cur), pl.BlockSpec((BLK, D_XBC), cur),
                pl.BlockSpec((BLK, LANES), cur)]
    in_specs += [pl.BlockSpec(w.shape, const2) for w in wts]
    return pl.pallas_call(
        functools.partial(_mix1_prompt_kernel, nblk=nblk),
        grid=(meta_blk + 1,),
        in_specs=in_specs,
        out_specs=[pl.BlockSpec((BLK, D_MIX_ODD), cur),
                   pl.BlockSpec((1, D_LRU), const2),
                   pl.BlockSpec((SSD_HEADS, SSD_HEAD_DIM, SSD_STATE), lambda j: (0, 0, 0))],
        out_shape=[jax.ShapeDtypeStruct((n, D_MIX_ODD), F32),
                   jax.ShapeDtypeStruct((1, D_LRU), F32),
                   jax.ShapeDtypeStruct((SSD_HEADS, SSD_HEAD_DIM, SSD_STATE), F32)],
        scratch_shapes=[pltpu.VMEM((8 + BLK, D_LRU), F32),
                        pltpu.VMEM((8 + BLK, D_XBC), F32),
                        pltpu.VMEM((1, D_LRU), F32),
                        pltpu.VMEM((SSD_HEADS, SSD_HEAD_DIM, SSD_STATE), F32)],
        compiler_params=_cparams("arbitrary"),
        name="mix1_prompt",
    )(cx, cg, z, xbc, dt, *wts)


def _conv_step(seq, t, w_ref, b_ref):
    out = b_ref[...]
    for tap in range(CONV_WIDTH):
        out = out + seq[t + tap] * w_ref[tap:tap + 1, :]
    return out


def _lru_sample_kernel(cx_ref, cg_ref, buf_ref, h0_ref, cwc_ref, cbc_ref, wgate_ref, bgate_ref,
                       lam_ref, o_ref, hout_ref, *, n_new):
    seq = [buf_ref[i] for i in range(CONV_WIDTH - 1)] + [cx_ref[i] for i in range(n_new)]
    lam_sp = _softplus(-lam_ref[...])
    h = h0_ref[...]
    for t in range(n_new):
        xc = _conv_step(seq, t, cwc_ref, cbc_ref)
        a, bx = _lru_gates(xc, wgate_ref, bgate_ref, lam_sp)
        h = a * h + bx
        o_ref[t] = h * _gelu_tanh(cg_ref[t])
    hout_ref[...] = h


def _lru_sample(cx_tm, cg_tm, buf_tm, h0, wts):
    n_new = cx_tm.shape[0]
    return pl.pallas_call(
        functools.partial(_lru_sample_kernel, n_new=n_new),
        out_shape=[jax.ShapeDtypeStruct(cx_tm.shape, F32), jax.ShapeDtypeStruct(h0.shape, F32)],
        compiler_params=pltpu.CompilerParams(vmem_limit_bytes=VMEM_LIMIT),
        name="lru_sample",
    )(cx_tm, cg_tm, buf_tm, h0, *wts)


def _split3(x):
    hi = x.astype(BF16)
    r = x - hi.astype(F32)
    mid = r.astype(BF16)
    lo = (r - mid.astype(F32)).astype(BF16)
    return hi, mid, lo


def _ssd_sample_kernel(xbc_ref, dt_ref, z_ref, buf_ref, h0_ref, cwd_ref, cbd_ref, dtb_ref, alog_ref,
                       dskip_ref, nw_ref, o_ref, hout_ref, xc_ref, dec_ref, dtx_ref, y_ref, *, n_new, bs):
    seq = [buf_ref[i] for i in range(CONV_WIDTH - 1)] + [xbc_ref[i] for i in range(n_new)]
    a_neg = -jnp.exp(alog_ref[...])
    hcol = lax.broadcasted_iota(I32, (LANES, D_SSD), 1) // SSD_HEAD_DIM
    expand = jnp.where(lax.broadcasted_iota(I32, (LANES, D_SSD), 0) == hcol, 1.0, 0.0).astype(BF16)

    def widen(v):
        parts = _split3(v)
        return sum(jnp.dot(p, expand, preferred_element_type=F32) for p in parts)

    for t in range(n_new):
        xc = _silu(_conv_step(seq, t, cwd_ref, cbd_ref))
        xc_ref[t] = xc
        dt = _softplus(dt_ref[t] + dtb_ref[...])
        dtw = widen(dt)
        dec_ref[t] = jnp.exp(widen(dt * a_neg))
        dtx_ref[t] = dtw * xc[:, :D_SSD]

    zero_rows = jnp.zeros((LANES - 2 * n_new * SSD_GROUPS, SSD_STATE), F32)
    half = D_SSD // SSD_GROUPS

    def per_seq(b, c):
        rows = [xc_ref[t, pl.ds(b, 1), :] for t in range(n_new)]
        bc = [r[:, D_SSD + k * SSD_STATE:D_SSD + (k + 1) * SSD_STATE]
              for r in rows for k in range(2 * SSD_GROUPS)]
        bc_t = jnp.concatenate(bc + [zero_rows], axis=0).T
        ht = h0_ref[b].T
        for t in range(n_new):
            bcol = [jnp.broadcast_to(bc_t[:, t * 4 + g:t * 4 + g + 1], (SSD_STATE, half))
                    for g in range(SSD_GROUPS)]
            ccol = [jnp.broadcast_to(bc_t[:, t * 4 + SSD_GROUPS + g:t * 4 + SSD_GROUPS + g + 1],
                                     (SSD_STATE, half)) for g in range(SSD_GROUPS)]
            bw = jnp.concatenate(bcol, axis=1)
            cw = jnp.concatenate(ccol, axis=1)
            ht = ht * dec_ref[t, pl.ds(b, 1), :] + bw * dtx_ref[t, pl.ds(b, 1), :]
            y = jnp.sum(cw * ht, axis=0, keepdims=True)
            y_ref[t, pl.ds(b, 1), :] = y + dskip_ref[...] * rows[t][:, :D_SSD]
        hout_ref[b] = ht.T
        return c

    lax.fori_loop(0, bs, per_seq, 0)
    for t in range(n_new):
        o_ref[t] = _rms(y_ref[t] * _silu(z_ref[t]), nw_ref[...])


def _ssd_sample(xbc_tm, dt_tm, z_tm, buf_tm, h0, wts):
    n_new, b, _ = xbc_tm.shape
    bs = 16 if b % 16 == 0 else b
    hp = SSD_HEADS * SSD_HEAD_DIM

    def tm(c):
        return pl.BlockSpec((n_new, bs, c), lambda i: (0, i, 0))

    in_specs = [tm(D_XBC), tm(LANES), tm(D_SSD),
                pl.BlockSpec((CONV_WIDTH - 1, bs, D_XBC), lambda i: (0, i, 0)),
                pl.BlockSpec((bs, hp, SSD_STATE), lambda i: (i, 0, 0))]
    in_specs += [pl.BlockSpec(w.shape, lambda i: (0, 0)) for w in wts]
    return pl.pallas_call(
        functools.partial(_ssd_sample_kernel, n_new=n_new, bs=bs),
        grid=(b // bs,),
        in_specs=in_specs,
        out_specs=[tm(D_SSD), pl.BlockSpec((bs, hp, SSD_STATE), lambda i: (i, 0, 0))],
        out_shape=[jax.ShapeDtypeStruct((n_new, b, D_SSD), F32),
                   jax.ShapeDtypeStruct((b, hp, SSD_STATE), F32)],
        scratch_shapes=[pltpu.VMEM((n_new, bs, D_XBC), F32),
                        pltpu.VMEM((n_new, bs, D_SSD), F32),
                        pltpu.VMEM((n_new, bs, D_SSD), F32),
                        pltpu.VMEM((n_new, bs, D_SSD), F32)],
        compiler_params=_cparams("parallel"),
        name="ssd_sample",
    )(xbc_tm, dt_tm, z_tm, buf_tm, h0, *wts)


def _block_diag(w):
    g, c, _ = w.shape
    eye = jnp.eye(g, dtype=w.dtype)
    return (eye[:, None, :, None] * w[:, :, None, :]).reshape(g * c, g * c)


def _router_weights(w_rg, b_rg, w_re, b_re):
    w = jnp.concatenate([w_rg, w_re], axis=1)
    w = jnp.pad(w, ((0, 0), (0, LANES - w.shape[1])))
    hi = w.astype(BF16)
    lo = (w - hi.astype(F32)).astype(BF16)
    b = jnp.pad(jnp.concatenate([b_rg, b_re]), (0, LANES - MOE_GROUPS - N_EXPERTS)).reshape(1, LANES)
    return hi, lo, b


def kernel(x_prompt, x_sample, cache_swa_k, cache_swa_v, cache_pool, state_lru_conv, state_lru_h, state_ssd_conv, state_ssd_h, meta_tokens, norm_mix, norm_ffn, norm_final, attn_w_in, attn_w_out, attn_sinks, pool_w, pool_scale, rec_w_in, rec_w_out, lru_conv_w, lru_conv_b, lru_w_rg, lru_b_rg, lru_w_ig, lru_b_ig, lru_lambda, ssd_conv_w, ssd_conv_b, ssd_dt_bias, ssd_a_log, ssd_d, ssd_norm_w, moe_w_rg, moe_b_rg, moe_w_re, moe_b_re, moe_w_gate, moe_w_up, moe_w_down):
    t_rows = x_prompt.shape[1]
    b, s_new = x_sample.shape[:2]
    ns = b * s_new
    small = -(-(ns + BLK) // TT) * TT
    n = t_rows + small
    meta_blk = n // BLK - 1
    valid_lo = t_rows + ns
    valid_mid = n - N_META
    assert x_prompt.shape[0] == 1 and t_rows % TT == 0 and ns % BLK == 0 and ns <= TT

    x_small = jnp.concatenate([
        x_sample.transpose(1, 0, 2).reshape(ns, D_MODEL),
        jnp.zeros((small - ns - N_META, D_MODEL), F32),
        meta_tokens.astype(F32)], axis=0)
    xs0 = (x_prompt[0], x_small)

    def sample_tm(a):
        return a[t_rows:t_rows + ns].reshape(s_new, b, a.shape[1])

    q, kv, p = _norm_proj(xs0, norm_mix[0], attn_w_in[0].astype(BF16), (D_ATTN, 2 * D_KV, D_POOL))
    wpool_bd = _block_diag(pool_w[0]).astype(BF16)
    mix = _mix0_prompt(q, kv, p, attn_sinks[0], wpool_bd, pool_scale[0], t_rows, meta_blk)
    nq = s_new * GQA
    qs = sample_tm(q).reshape(s_new, b, N_KV_HEADS, GQA, HEAD_DIM).transpose(1, 2, 0, 3, 4)
    qs = qs.reshape(b, N_KV_HEADS, nq, HEAD_DIM)
    kv_s = sample_tm(kv).transpose(1, 0, 2)
    kn, vn = kv_s[:, :, :D_KV], kv_s[:, :, D_KV:]
    ck = cache_swa_k[0].reshape(b, N_META + WINDOW, D_KV)
    cv = cache_swa_v[0].reshape(b, N_META + WINDOW, D_KV)
    o_s = _attn_sample(qs, ck, cv, kn, vn, attn_sinks[0])
    o_s = o_s.reshape(b, N_KV_HEADS, s_new, GQA, HEAD_DIM).transpose(2, 0, 1, 3, 4).reshape(s_new, b, D_ATTN)
    p_s = sample_tm(p)
    pool_buf_tm = cache_pool[0].transpose(1, 0, 2)
    yp_s = _pool_sample(pool_buf_tm, p_s, wpool_bd, pool_scale[0])
    mix_s = jnp.concatenate([o_s, yp_s], axis=-1).reshape(ns, D_MIX_EVEN)
    wr_hi, wr_lo, br = _router_weights(moe_w_rg[0], moe_b_rg[0], moe_w_re[0], moe_b_re[0])
    x1, xn, info, infot, cnt = _outproj_route(xs0, mix, mix_s, attn_w_out[0].astype(BF16), norm_ffn[0],
                                              wr_hi, wr_lo, br, t_rows, valid_lo, valid_mid)
    x, = _moe(x1, xn, info, infot, cnt, moe_w_gate[0], moe_w_up[0], moe_w_down[0], norm_final,
              t_rows, ns, False)

    meta_rows = slice(n - N_META, n)
    last_rows = slice(t_rows - WINDOW, t_rows)
    swa_k_p = jnp.concatenate([kv[meta_rows, :D_KV], kv[last_rows, :D_KV]], axis=0)
    swa_v_p = jnp.concatenate([kv[meta_rows, D_KV:], kv[last_rows, D_KV:]], axis=0)
    swa_k_p = swa_k_p.reshape(1, 1, N_META + WINDOW, N_KV_HEADS, HEAD_DIM)
    swa_v_p = swa_v_p.reshape(1, 1, N_META + WINDOW, N_KV_HEADS, HEAD_DIM)
    pool_p = p[t_rows - POOL_BUF:t_rows].reshape(1, 1, POOL_BUF, D_POOL)
    swa_k_s = jnp.concatenate([ck[:, :N_META], ck[:, N_META + s_new:], kn], axis=1)
    swa_v_s = jnp.concatenate([cv[:, :N_META], cv[:, N_META + s_new:], vn], axis=1)
    swa_k_s = swa_k_s.reshape(1, b, N_META + WINDOW, N_KV_HEADS, HEAD_DIM)
    swa_v_s = swa_v_s.reshape(1, b, N_META + WINDOW, N_KV_HEADS, HEAD_DIM)
    pool_s = jnp.concatenate([cache_pool[0], p_s.transpose(1, 0, 2)], axis=1)[:, -POOL_BUF:][None]

    d_in_odd = rec_w_in.shape[2]
    w_in1 = jnp.pad(rec_w_in[0], ((0, 0), (0, LANES - SSD_HEADS))).astype(BF16)
    assert d_in_odd == 2 * D_LRU + D_SSD + D_XBC + SSD_HEADS
    cx, cg, z, xbc, dt = _norm_proj((x,), norm_mix[1], w_in1, (D_LRU, D_LRU, D_SSD, D_XBC, LANES))
    wgate = jnp.concatenate([_block_diag(lru_w_rg[0]), _block_diag(lru_w_ig[0])], axis=1).astype(BF16)
    bgate = jnp.concatenate([lru_b_rg[0], lru_b_ig[0]]).reshape(1, 2 * D_LRU)
    pad8 = (0, LANES - SSD_HEADS)
    dtb = jnp.pad(ssd_dt_bias[0], pad8).reshape(1, LANES)
    alog = jnp.pad(ssd_a_log[0], pad8).reshape(1, LANES)
    dskip = jnp.repeat(ssd_d[0], SSD_HEAD_DIM).reshape(1, D_SSD)
    lru_w = (lru_conv_w[0], lru_conv_b[0].reshape(1, D_LRU), wgate, bgate, lru_lambda[0].reshape(1, D_LRU))
    ssd_w = (ssd_conv_w[0], ssd_conv_b[0].reshape(1, D_XBC), dtb, alog, dskip, ssd_norm_w[0].reshape(1, D_SSD))
    mix, lru_h_p, ssd_h_p = _mix1_prompt(cx, cg, z, xbc, dt, lru_w + ssd_w, t_rows, meta_blk)
    cx_s, cg_s, z_s, xbc_s, dt_s = (sample_tm(a) for a in (cx, cg, z, xbc, dt))
    yc_s, lru_h_s = _lru_sample(cx_s, cg_s, state_lru_conv[0].transpose(1, 0, 2), state_lru_h[0], lru_w)
    h0 = state_ssd_h[0].reshape(b, SSD_HEADS * SSD_HEAD_DIM, SSD_STATE)
    yd_s, ssd_h_s = _ssd_sample(xbc_s, dt_s, z_s, state_ssd_conv[0].transpose(1, 0, 2), h0, ssd_w)
    mix_s = jnp.concatenate([yc_s, yd_s], axis=-1).reshape(ns, D_MIX_ODD)
    wr_hi, wr_lo, br = _router_weights(moe_w_rg[1], moe_b_rg[1], moe_w_re[1], moe_b_re[1])
    x1, xn, info, infot, cnt = _outproj_route((x,), mix, mix_s, rec_w_out[0].astype(BF16), norm_ffn[1],
                                              wr_hi, wr_lo, br, t_rows, valid_lo, valid_mid)
    y_p, y_small = _moe(x1, xn, info, infot, cnt, moe_w_gate[1], moe_w_up[1], moe_w_down[1], norm_final,
                        t_rows, ns, True)

    tail = CONV_WIDTH - 1
    lru_conv_p = cx[t_rows - tail:t_rows].reshape(1, 1, tail, D_LRU)
    ssd_conv_p = xbc[t_rows - tail:t_rows].reshape(1, 1, tail, D_XBC)
    lru_conv_s = jnp.concatenate([state_lru_conv[0], cx_s.transpose(1, 0, 2)], axis=1)[:, -tail:][None]
    ssd_conv_s = jnp.concatenate([state_ssd_conv[0], xbc_s.transpose(1, 0, 2)], axis=1)[:, -tail:][None]

    y_prompt = y_p[None]
    y_sample = y_small[:ns].reshape(s_new, b, D_MODEL).transpose(1, 0, 2)
    return (y_prompt, y_sample, swa_k_p, swa_v_p, pool_p, lru_conv_p,
            lru_h_p.reshape(1, 1, D_LRU), ssd_conv_p,
            ssd_h_p.reshape(1, 1, SSD_HEADS, SSD_HEAD_DIM, SSD_STATE),
            swa_k_s, swa_v_s, pool_s, lru_conv_s, lru_h_s[None], ssd_conv_s,
            ssd_h_s.reshape(1, b, SSD_HEADS, SSD_HEAD_DIM, SSD_STATE))
```

```python
import functools

import jax
import jax.numpy as jnp
from jax import lax
from jax.experimental import pallas as pl
from jax.experimental.pallas import tpu as pltpu

F32 = jnp.float32
BF16 = jnp.bfloat16
I32 = jnp.int32

D_MODEL = 1024
N_META = 16
EPS = 1e-6
PAST_LEN = 16384
N_Q_HEADS = 16
N_KV_HEADS = 2
GQA = N_Q_HEADS // N_KV_HEADS
HEAD_DIM = 64
WINDOW = 128
D_ATTN = N_Q_HEADS * HEAD_DIM
D_KV = N_KV_HEADS * HEAD_DIM
POOL_WINDOWS = (2, 4, 8, 16)
POOL_GROUP_DIM = 96
D_POOL = len(POOL_WINDOWS) * POOL_GROUP_DIM
POOL_BUF = max(POOL_WINDOWS) - 1
D_MIX_EVEN = D_ATTN + D_POOL
D_LRU = 512
LRU_HEADS = 8
LRU_BLOCK = D_LRU // LRU_HEADS
LRU_C = 8.0
CONV_WIDTH = 4
SSD_HEADS = 8
SSD_HEAD_DIM = 64
D_SSD = SSD_HEADS * SSD_HEAD_DIM
SSD_GROUPS = 2
SSD_STATE = 128
D_XBC = D_SSD + 2 * SSD_GROUPS * SSD_STATE
D_MIX_ODD = D_LRU + D_SSD
MOE_GROUPS = 4
EXPERTS_PER_GROUP = 8
N_EXPERTS = MOE_GROUPS * EXPERTS_PER_GROUP
D_EXPERT = 256

BLK = 128
META_PAD = BLK - N_META
LANES = 128
MOE_TM = 256
NEG_INF = float("-inf")
VMEM_LIMIT = 56 * 1024 * 1024


def _cparams(*sem):
    return pltpu.CompilerParams(dimension_semantics=sem, vmem_limit_bytes=VMEM_LIMIT)


TT = 512
SUB = 8


def _small_tile_ranges(t_rows, ns, n):
    out = {}
    for k in range(t_rows // TT, n // TT):
        lo = k * TT
        ranges = []
        a0, a1 = max(lo, t_rows), min(lo + TT, t_rows + ns)
        if a1 > a0:
            ranges.append((a0 - lo, a1 - lo))
        b0 = max(lo, n - N_META)
        if lo + TT > b0:
            ranges.append((b0 - lo, TT))
        out[k] = tuple(ranges)
    return out


def _for_tile_rows(i, n_full, ranges, fn):
    @pl.when(i < n_full)
    def _():
        fn(0, TT)

    for k, rs in ranges.items():
        if rs:
            @pl.when(i == k)
            def _(rs=rs):
                for lo, hi in rs:
                    fn(lo, hi)


def _slab(c, rows, base=0):
    return pl.ds(base + c, rows, stride=SUB)


def _tile_of(r):
    return pl.ds(pl.multiple_of(r * SUB, SUB), SUB)


def _sigmoid(x):
    return 1.0 / (1.0 + jnp.exp(-x))


def _silu(x):
    return x * _sigmoid(x)


def _softplus(x):
    return jnp.maximum(x, 0.0) + jnp.log1p(jnp.exp(-jnp.abs(x)))


def _gelu_tanh(x):
    return 0.5 * x * (1.0 + jnp.tanh(0.7978845608028654 * (x + 0.044715 * x * x * x)))


def _dot(a, b):
    return jnp.dot(a.astype(BF16), b.astype(BF16), preferred_element_type=F32)


def _dot_nt(a, b):
    return lax.dot_general(a.astype(BF16), b.astype(BF16), (((1,), (1,)), ((), ())),
                           preferred_element_type=F32)


def _rms(x, g):
    ms = jnp.mean(x * x, axis=-1, keepdims=True)
    return x * lax.rsqrt(ms + EPS) * g


def _row_specs(xs, width):
    if len(xs) == 1:
        return [pl.BlockSpec((TT, width), lambda i: (i, 0))]
    na = xs[0].shape[0] // TT
    return [pl.BlockSpec((TT, width), lambda i: (jnp.minimum(i, na - 1), 0)),
            pl.BlockSpec((TT, width), lambda i: (jnp.maximum(i - na, 0), 0))]


def _pick_rows(refs, n_first):
    if len(refs) == 1:
        return refs[0][...]
    return jnp.where(pl.program_id(0) < n_first, refs[0][...], refs[1][...])


def _norm_proj_kernel(*refs, splits, n_x, n_first):
    x_refs, (g_ref, w_ref), out_refs = refs[:n_x], refs[n_x:n_x + 2], refs[n_x + 2:]
    y = _rms(_pick_rows(x_refs, n_first), g_ref[...]).astype(BF16)
    off = 0
    for o_ref, n in zip(out_refs, splits):
        o_ref[...] = jnp.dot(y, w_ref[:, off:off + n], preferred_element_type=F32)
        off += n


def _norm_proj(xs, g, w_bf16, splits):
    n = sum(x.shape[0] for x in xs)
    return pl.pallas_call(
        functools.partial(_norm_proj_kernel, splits=splits, n_x=len(xs), n_first=xs[0].shape[0] // TT),
        grid=(n // TT,),
        in_specs=_row_specs(xs, D_MODEL) + [pl.BlockSpec((1, D_MODEL), lambda i: (0, 0)),
                                             pl.BlockSpec(w_bf16.shape, lambda i: (0, 0))],
        out_specs=[pl.BlockSpec((TT, s), lambda i: (i, 0)) for s in splits],
        out_shape=[jax.ShapeDtypeStruct((n, s), F32) for s in splits],
        compiler_params=_cparams("parallel"),
        name="norm_proj",
    )(*xs, g.reshape(1, D_MODEL), w_bf16)


def _mix0_prompt_kernel(*refs, nblk):
    j = pl.program_id(0)
    o_ref = refs[8]

    @pl.when(j <= nblk)
    def _():
        _mix0_prompt_step(*refs)

    @pl.when(j > nblk)
    def _():
        o_ref[...] = jnp.zeros(o_ref.shape, F32)


def _mix0_prompt_step(sink_ref, q_ref, kvc_ref, kvp_ref, kvm_ref, p_ref, wp_ref, ps_ref,
                      o_ref, seq_ref, s_ref, pr_ref, rd_ref):
    j = pl.program_id(0)
    kv_all = jnp.concatenate([kvp_ref[...], kvc_ref[...], kvm_ref[...],
                              jnp.zeros((META_PAD, 2 * D_KV), F32)], axis=0)
    k_all = kv_all[:, :D_KV]
    v_all = kv_all[:, D_KV:]
    k_rot = pltpu.roll(k_all, HEAD_DIM, 1)
    v_rot = pltpu.roll(v_all, HEAD_DIM, 1)
    lo_half = lax.broadcasted_iota(I32, k_all.shape, 1) < HEAD_DIM

    def halves(own_lanes, other_lanes, g):
        src_lo, src_hi = (own_lanes, other_lanes) if g == 0 else (other_lanes, own_lanes)
        return (jnp.where(lo_half, src_lo, 0.0).astype(BF16), jnp.where(lo_half, 0.0, src_hi).astype(BF16))

    r = lax.broadcasted_iota(I32, (BLK, BLK), 0)
    c = lax.broadcasted_iota(I32, (BLK, BLK), 1)
    own = r >= c
    dist_loc = jnp.where(own, r - c, r - c + BLK).astype(F32)
    d_min = jnp.where(j >= 2, -BLK, 0)
    c_min = jnp.where(j == 0, META_PAD, 0)
    mask_loc = jnp.where((r - c >= d_min) & (c >= c_min), 0.0, NEG_INF)
    meta_hi = jnp.where(j >= 1, N_META, 0)
    dist_meta = jnp.where(c < meta_hi, (N_META + (j - 1) * BLK + r - c).astype(F32), jnp.inf)
    slabs = GQA // 2
    nkeys = 3 * BLK
    for g in range(N_KV_HEADS):
        k_even, k_odd = halves(k_all, k_rot, g)
        v_even, v_odd = halves(v_all, v_rot, g)
        qg = jnp.concatenate([q_ref[:, (g * slabs + i) * LANES:(g * slabs + i + 1) * LANES]
                              for i in range(slabs)], axis=0) * (HEAD_DIM ** -0.5)
        s_ref[...] = _dot_nt(qg, jnp.concatenate([k_even, k_odd], axis=0))
        for i in range(slabs):
            rows = slice(i * BLK, (i + 1) * BLK)
            for par in range(2):
                h = g * GQA + 2 * i + par
                slope = 2.0 ** (-8.0 * (h + 1) / N_Q_HEADS)
                sink = sink_ref[h]
                k0 = par * nkeys
                s_loc = (jnp.where(own, s_ref[rows, k0 + BLK:k0 + 2 * BLK], s_ref[rows, k0:k0 + BLK])
                         - slope * dist_loc + mask_loc)
                s_met = s_ref[rows, k0 + 2 * BLK:k0 + 3 * BLK] - slope * dist_meta
                m = jnp.maximum(jnp.max(jnp.maximum(s_loc, s_met), axis=-1, keepdims=True), sink)
                p_loc = jnp.exp(s_loc - m)
                p_met = jnp.exp(s_met - m)
                den = jnp.sum(p_loc + p_met, axis=-1, keepdims=True) + jnp.exp(sink - m)
                pr_ref[par, rows, 0:BLK] = jnp.where(own, 0.0, p_loc).astype(BF16)
                pr_ref[par, rows, BLK:2 * BLK] = jnp.where(own, p_loc, 0.0).astype(BF16)
                pr_ref[par, rows, 2 * BLK:] = p_met.astype(BF16)
                rd_ref[par, rows, :] = jnp.broadcast_to(1.0 / den, (BLK, LANES))
        o = (jnp.dot(pr_ref[0], v_even, preferred_element_type=F32) * rd_ref[0]
             + jnp.dot(pr_ref[1], v_odd, preferred_element_type=F32) * rd_ref[1])
        for i in range(slabs):
            o_ref[:, (g * slabs + i) * LANES:(g * slabs + i + 1) * LANES] = o[i * BLK:(i + 1) * BLK, :]

    @pl.when(j == 0)
    def _():
        seq_ref[0:N_META, :] = jnp.zeros((N_META, D_POOL), F32)

    seq_ref[N_META:N_META + BLK, :] = p_ref[...]
    s0 = seq_ref[...]
    s1 = s0 + pltpu.roll(s0, 1, 0)
    s2 = s1 + pltpu.roll(s1, 2, 0)
    s3 = s2 + pltpu.roll(s2, 4, 0)
    s4 = s3 + pltpu.roll(s3, 8, 0)
    rows = N_META + BLK
    col = lax.broadcasted_iota(I32, (rows, D_POOL), 1)
    row = lax.broadcasted_iota(I32, (rows, D_POOL), 0) - N_META
    wsum = jnp.where(col < POOL_GROUP_DIM, s1,
                     jnp.where(col < 2 * POOL_GROUP_DIM, s2,
                               jnp.where(col < 3 * POOL_GROUP_DIM, s3, s4)))
    wlen = jnp.where(col < POOL_GROUP_DIM, 2,
                     jnp.where(col < 2 * POOL_GROUP_DIM, 4,
                               jnp.where(col < 3 * POOL_GROUP_DIM, 8, 16)))
    pos = row + jnp.where(j == 0, -META_PAD, N_META + (j - 1) * BLK)
    cnt = jnp.clip(jnp.minimum(pos + 1, wlen), 1, 16).astype(F32)
    d = (wsum / cnt - s0)[N_META:, :]
    o_ref[:, D_ATTN:] = _dot(d, wp_ref[...]) * ps_ref[...]
    seq_ref[0:N_META, :] = p_ref[BLK - N_META:, :]


def _mix0_prompt(q, kv, p, sinks, wpool_bd, pool_scale, t_rows, meta_blk):
    n = q.shape[0]
    nblk = t_rows // BLK

    def cur(j):
        return jnp.where(j == 0, meta_blk, j - 1)

    meta16 = (meta_blk * BLK + META_PAD) // N_META
    return pl.pallas_call(
        functools.partial(_mix0_prompt_kernel, nblk=nblk),
        grid=(meta_blk + 1,),
        in_specs=[pl.BlockSpec(memory_space=pltpu.SMEM),
                  pl.BlockSpec((BLK, D_ATTN), lambda j: (cur(j), 0)),
                  pl.BlockSpec((BLK, 2 * D_KV), lambda j: (cur(j), 0)),
                  pl.BlockSpec((BLK, 2 * D_KV), lambda j: (jnp.maximum(j - 2, 0), 0)),
                  pl.BlockSpec((N_META, 2 * D_KV), lambda j: (meta16, 0)),
                  pl.BlockSpec((BLK, D_POOL), lambda j: (cur(j), 0)),
                  pl.BlockSpec((D_POOL, D_POOL), lambda j: (0, 0)),
                  pl.BlockSpec((1, D_POOL), lambda j: (0, 0))],
        out_specs=pl.BlockSpec((BLK, D_MIX_EVEN), lambda j: (cur(j), 0)),
        out_shape=jax.ShapeDtypeStruct((n, D_MIX_EVEN), F32),
        scratch_shapes=[pltpu.VMEM((N_META + BLK, D_POOL), F32),
                        pltpu.VMEM((GQA // 2 * BLK, 2 * 3 * BLK), F32),
                        pltpu.VMEM((2, GQA // 2 * BLK, 3 * BLK), BF16),
                        pltpu.VMEM((2, GQA // 2 * BLK, LANES), F32)],
        compiler_params=_cparams("arbitrary"),
        name="mix0_prompt",
    )(sinks, q, kv, kv, kv, p, wpool_bd, pool_scale.reshape(1, D_POOL))


def _attn_sample_kernel(sink_ref, q_ref, ck_ref, cv_ref, kn_ref, vn_ref, o_ref, *, n_new):
    nk = N_META + WINDOW + n_new
    nq = n_new * GQA
    k_all = jnp.concatenate([ck_ref[...], kn_ref[...]], axis=1).astype(BF16)
    v_all = jnp.concatenate([cv_ref[...], vn_ref[...]], axis=1).astype(BF16)
    r = lax.broadcasted_iota(I32, (nq, nk), 0)
    slot = lax.broadcasted_iota(I32, (nq, nk), 1)
    t = r // GQA
    hh = r - t * GQA
    kpos = jnp.where(slot < N_META, slot,
                     jnp.where(slot < N_META + WINDOW, PAST_LEN - WINDOW - N_META + slot,
                               PAST_LEN - N_META - WINDOW + slot))
    dist = PAST_LEN + t - kpos
    allowed = (dist >= 0) & ((slot < N_META) | (dist < WINDOW))
    distf = dist.astype(F32)
    hh1 = hh[:, :1]
    for g in range(N_KV_HEADS):
        slope = jnp.exp2(-8.0 * (g * GQA + hh + 1).astype(F32) / N_Q_HEADS)
        sink = jnp.zeros((nq, 1), F32)
        for i in range(GQA):
            sink = jnp.where(hh1 == i, sink_ref[g * GQA + i], sink)
        kg = k_all[:, :, g * HEAD_DIM:(g + 1) * HEAD_DIM]
        vg = v_all[:, :, g * HEAD_DIM:(g + 1) * HEAD_DIM]
        s = jnp.einsum("bqd,bkd->bqk", q_ref[:, g].astype(BF16), kg,
                       preferred_element_type=F32) * (HEAD_DIM ** -0.5)
        s = jnp.where(allowed[None], s - (slope * distf)[None], NEG_INF)
        sink = sink[None]
        m = jnp.maximum(jnp.max(s, axis=-1, keepdims=True), sink)
        pr = jnp.exp(s - m)
        den = jnp.sum(pr, axis=-1, keepdims=True) + jnp.exp(sink - m)
        o = jnp.einsum("bqk,bkd->bqd", pr.astype(BF16), vg, preferred_element_type=F32)
        o_ref[:, g] = o / den


def _attn_sample(qs, ck, cv, kn, vn, sinks):
    b, _, nq, _ = qs.shape
    n_new = kn.shape[1]
    bs = 16 if b % 16 == 0 else b
    nc = N_META + WINDOW
    return pl.pallas_call(
        functools.partial(_attn_sample_kernel, n_new=n_new),
        grid=(b // bs,),
        in_specs=[pl.BlockSpec(memory_space=pltpu.SMEM),
                  pl.BlockSpec((bs, N_KV_HEADS, nq, HEAD_DIM), lambda i: (i, 0, 0, 0)),
                  pl.BlockSpec((bs, nc, D_KV), lambda i: (i, 0, 0)),
                  pl.BlockSpec((bs, nc, D_KV), lambda i: (i, 0, 0)),
                  pl.BlockSpec((bs, n_new, D_KV), lambda i: (i, 0, 0)),
                  pl.BlockSpec((bs, n_new, D_KV), lambda i: (i, 0, 0))],
        out_specs=pl.BlockSpec((bs, N_KV_HEADS, nq, HEAD_DIM), lambda i: (i, 0, 0, 0)),
        out_shape=jax.ShapeDtypeStruct(qs.shape, F32),
        compiler_params=_cparams("parallel"),
        name="attn_sample",
    )(sinks, qs, ck, cv, kn, vn)


def _pool_sample_kernel(buf_ref, p_ref, wp_ref, ps_ref, o_ref, *, n_new):
    seq = [buf_ref[i] for i in range(POOL_BUF)] + [p_ref[i] for i in range(n_new)]
    b = p_ref.shape[1]
    col = lax.broadcasted_iota(I32, (b, D_POOL), 1)
    for t in range(n_new):
        cur = seq[POOL_BUF + t]
        acc = cur
        sums = {}
        for back in range(1, max(POOL_WINDOWS)):
            acc = acc + seq[POOL_BUF + t - back]
            if back + 1 in POOL_WINDOWS:
                sums[back + 1] = acc
        mean = jnp.where(col < POOL_GROUP_DIM, sums[2] / 2.0,
                         jnp.where(col < 2 * POOL_GROUP_DIM, sums[4] / 4.0,
                                   jnp.where(col < 3 * POOL_GROUP_DIM, sums[8] / 8.0,
                                             sums[16] / 16.0)))
        o_ref[t] = _dot(mean - cur, wp_ref[...]) * ps_ref[...]


def _pool_sample(buf_tm, p_tm, wpool_bd, pool_scale):
    n_new = p_tm.shape[0]
    return pl.pallas_call(
        functools.partial(_pool_sample_kernel, n_new=n_new),
        out_shape=jax.ShapeDtypeStruct(p_tm.shape, F32),
        compiler_params=pltpu.CompilerParams(vmem_limit_bytes=VMEM_LIMIT),
        name="pool_sample",
    )(buf_tm, p_tm, wpool_bd, pool_scale.reshape(1, D_POOL))


def _outproj_route_kernel(*refs, valid_lo, valid_mid, n_x, n_first, ns):
    tt = TT
    x_refs = refs[:n_x]
    (mix_ref, mixs_ref, w_ref, g_ref, wrh_ref, wrl_ref, br_ref,
     x1_ref, xn_ref, info_ref, infot_ref, cnt_ref, acc_ref) = refs[n_x:]
    i = pl.program_id(0)
    x = _pick_rows(x_refs, n_first)

    @pl.when(i != n_first)
    def _():
        x1_ref[...] = x + _dot(mix_ref[...], w_ref[...])

    @pl.when(i == n_first)
    def _():
        x1_ref[:ns, :] = x[:ns] + _dot(mixs_ref[...], w_ref[...])
        if ns < tt:
            x1_ref[ns:, :] = x[ns:] + _dot(mix_ref[ns:, :], w_ref[...])

    x1 = x1_ref[...]
    xn = _rms(x1, g_ref[...])
    for cc_ in range(D_MODEL // LANES):
        xn_ref[_slab(cc_, tt), :] = xn[:, cc_ * LANES:(cc_ + 1) * LANES]
    hi = xn.astype(BF16)
    lo = (xn - hi.astype(F32)).astype(BF16)
    hi_out = jnp.dot(hi, jnp.concatenate([wrh_ref[...], wrl_ref[...]], axis=1), preferred_element_type=F32)
    logits = (hi_out[:, :LANES] + jnp.dot(lo, wrh_ref[...], preferred_element_type=F32)
              + hi_out[:, LANES:]) + br_ref[...]
    lane = lax.broadcasted_iota(I32, (tt, LANES), 1)
    lanef = lane.astype(F32)
    lg = jnp.where(lane < MOE_GROUPS, logits, NEG_INF)
    mg = jnp.max(lg, axis=-1, keepdims=True)
    gstar = jnp.min(jnp.where(lg == mg, lanef, 1e9), axis=-1, keepdims=True)
    pgroup = 1.0 / jnp.sum(jnp.exp(lg - mg), axis=-1, keepdims=True)
    lo_lane = MOE_GROUPS + gstar * EXPERTS_PER_GROUP
    le = jnp.where((lanef >= lo_lane) & (lanef < lo_lane + EXPERTS_PER_GROUP), logits, NEG_INF)
    v1 = jnp.max(le, axis=-1, keepdims=True)
    i1 = jnp.min(jnp.where(le == v1, lanef, 1e9), axis=-1, keepdims=True)
    le2 = jnp.where(lanef == i1, NEG_INF, le)
    v2 = jnp.max(le2, axis=-1, keepdims=True)
    i2 = jnp.min(jnp.where(le2 == v2, lanef, 1e9), axis=-1, keepdims=True)
    tq = jnp.exp(v2 - v1)
    w1 = pgroup / (1.0 + tq)
    w2 = pgroup * tq / (1.0 + tq)
    e1 = i1 - MOE_GROUPS
    e2 = i2 - MOE_GROUPS
    rowg = i * tt + lax.broadcasted_iota(I32, (tt, LANES), 0)
    valid = (rowg < valid_lo) | (rowg >= valid_mid)
    oh1 = jnp.where(valid & (lanef == e1), 1.0, 0.0)
    oh2 = jnp.where(valid & (lanef == e2), 1.0, 0.0)
    both = oh1 + oh2
    rr = lax.broadcasted_iota(I32, (tt, tt), 0)
    cc = lax.broadcasted_iota(I32, (tt, tt), 1)
    tril = jnp.where(cc < rr, 1.0, 0.0).astype(BF16)

    @pl.when(i == 0)
    def _():
        acc_ref[...] = jnp.zeros((1, LANES), F32)

    base = acc_ref[...] + jnp.dot(tril, both.astype(BF16), preferred_element_type=F32)
    rank1 = jnp.sum(oh1 * base, axis=-1, keepdims=True)
    rank2 = jnp.sum(oh2 * base, axis=-1, keepdims=True)
    total = acc_ref[...] + jnp.sum(both, axis=0, keepdims=True)
    acc_ref[...] = total
    cnt_ref[...] = jnp.broadcast_to(total, (8, LANES))
    info = jnp.where(lane == 0, e1, jnp.where(lane == 1, e2, jnp.where(lane == 2, w1, jnp.where(
        lane == 3, w2, jnp.where(lane == 4, rank1, jnp.where(lane == 5, rank2, 0.0))))))
    info_ref[...] = info
    infot_ref[0] = info.T[:SUB, :]


def _outproj_route(xs, mix, mix_s, w_out_bf16, g_ffn, wr_hi, wr_lo, br, t_rows, valid_lo, valid_mid):
    n = mix.shape[0]
    ns, dm = mix_s.shape
    nt = n // TT

    def const2(i):
        return (0, 0)

    return pl.pallas_call(
        functools.partial(_outproj_route_kernel, valid_lo=valid_lo, valid_mid=valid_mid,
                          n_x=len(xs), n_first=t_rows // TT, ns=ns),
        grid=(nt,),
        in_specs=_row_specs(xs, D_MODEL) + [
            pl.BlockSpec((TT, dm), lambda i: (i, 0)),
            pl.BlockSpec((ns, dm), const2),
            pl.BlockSpec((dm, D_MODEL), const2),
            pl.BlockSpec((1, D_MODEL), const2),
            pl.BlockSpec((D_MODEL, LANES), const2),
            pl.BlockSpec((D_MODEL, LANES), const2),
            pl.BlockSpec((1, LANES), const2)],
        out_specs=[pl.BlockSpec((TT, D_MODEL), lambda i: (i, 0)),
                   pl.BlockSpec((TT * SUB, LANES), lambda i: (i, 0)),
                   pl.BlockSpec((TT, LANES), lambda i: (i, 0)),
                   pl.BlockSpec((1, SUB, TT), lambda i: (i, 0, 0)),
                   pl.BlockSpec((8, LANES), const2)],
        out_shape=[jax.ShapeDtypeStruct((n, D_MODEL), F32),
                   jax.ShapeDtypeStruct((n * SUB, LANES), F32),
                   jax.ShapeDtypeStruct((n, LANES), F32),
                   jax.ShapeDtypeStruct((nt, SUB, TT), F32),
                   jax.ShapeDtypeStruct((8, LANES), F32)],
        scratch_shapes=[pltpu.VMEM((1, LANES), F32)],
        compiler_params=_cparams("arbitrary"),
        name="outproj_route",
    )(*xs, mix, mix_s, w_out_bf16, g_ffn.reshape(1, D_MODEL), wr_hi, wr_lo, br)


ROWS_PER_ISSUE = 8


def _issue_rows(lo, hi, row_copy):
    assert (hi - lo) % ROWS_PER_ISSUE == 0

    def body(it, c):
        for u in range(ROWS_PER_ISSUE):
            r = lo + it * ROWS_PER_ISSUE + u
            row_copy(r, 0).start(priority=0)
            row_copy(r, 1).start(priority=1)
        return c

    lax.fori_loop(0, (hi - lo) // ROWS_PER_ISSUE, body, 0)


def _dispatch_kernel(pos_ref, zt_ref, xn_ref, xs_ref, zbuf, sem, zsem, *, n_full, ranges, n_tiles):
    i = pl.program_id(0)

    @pl.when(i == 0)
    def _():
        zbuf[...] = jnp.zeros(zbuf.shape, F32)

        def zero_copy(t):
            rows = MOE_TM * SUB
            return pltpu.make_async_copy(zbuf, xs_ref.at[pl.ds(pl.multiple_of(t * rows, rows), rows)], zsem)

        def zstart(t, c):
            @pl.when(zt_ref[t] != 0)
            def _():
                zero_copy(t).start()
            return c

        def zwait(t, c):
            @pl.when(zt_ref[t] != 0)
            def _():
                zero_copy(t).wait()
            return c

        lax.fori_loop(0, n_tiles, zstart, 0)
        lax.fori_loop(0, n_tiles, zwait, 0)

    def row_copy(r, k):
        return pltpu.make_async_copy(xn_ref.at[_tile_of(r)], xs_ref.at[_tile_of(pos_ref[0, 0, k * TT + r])], sem)

    def scatter(lo, hi):
        _issue_rows(lo, hi, row_copy)
        nrow = (hi - lo) * SUB
        for _ in range(2):
            pltpu.make_async_copy(xn_ref.at[pl.ds(lo * SUB, nrow)], xs_ref.at[pl.ds(0, nrow)], sem).wait()

    _for_tile_rows(i, n_full, ranges, scatter)


def _dispatch(xn, pos, zero_tiles, n_sorted, t_rows, ranges):
    n = xn.shape[0] // SUB
    return pl.pallas_call(
        functools.partial(_dispatch_kernel, n_full=t_rows // TT, ranges=ranges,
                          n_tiles=n_sorted // MOE_TM),
        grid=(n // TT,),
        in_specs=[pl.BlockSpec((1, 1, 2 * TT), lambda i: (i, 0, 0), memory_space=pltpu.SMEM),
                  pl.BlockSpec(memory_space=pltpu.SMEM),
                  pl.BlockSpec((TT * SUB, LANES), lambda i: (i, 0))],
        out_specs=pl.BlockSpec(memory_space=pl.ANY),
        out_shape=jax.ShapeDtypeStruct((n_sorted * SUB, LANES), F32),
        scratch_shapes=[pltpu.VMEM((MOE_TM * SUB, LANES), F32),
                        pltpu.SemaphoreType.DMA(()),
                        pltpu.SemaphoreType.DMA(())],
        compiler_params=_cparams("arbitrary"),
        name="moe_dispatch",
    )(pos, zero_tiles, xn)


def _expert_ffn_kernel(te_ref, nu_ref, xs_ref, wg_ref, wu_ref, wd_ref, ys_ref, wgb, wub, wdb):
    j = pl.program_id(0)
    changed = (j == 0) | (te_ref[j] != te_ref[jnp.maximum(j - 1, 0)])

    @pl.when((j < nu_ref[0]) & changed)
    def _():
        wgb[...] = wg_ref[0, 0].astype(BF16)
        wub[...] = wu_ref[0, 0].astype(BF16)
        wdb[...] = wd_ref[0, 0].astype(BF16)

    n_slab = D_MODEL // LANES

    @pl.when(j < nu_ref[0])
    def _():
        x = jnp.concatenate([xs_ref[_slab(c, MOE_TM), :] for c in range(n_slab)], axis=1).astype(BF16)
        hg = jnp.dot(x, wgb[...], preferred_element_type=F32)
        hu = jnp.dot(x, wub[...], preferred_element_type=F32)
        h = (_silu(hg) * hu).astype(BF16)
        y = jnp.dot(h, wdb[...], preferred_element_type=F32)
        for c in range(n_slab):
            ys_ref[_slab(c, MOE_TM), :] = y[:, c * LANES:(c + 1) * LANES]

    @pl.when(j >= nu_ref[0])
    def _():
        ys_ref[...] = jnp.zeros(ys_ref.shape, F32)


def _expert_ffn(xs, tile_expert, n_used, w_gate, w_up, w_down, layer):
    n_sorted = xs.shape[0] // SUB
    nt = n_sorted // MOE_TM

    def row_map(j, te, nu):
        return (jnp.minimum(j, nu[0] - 1), 0)

    def w_map(j, te, nu):
        return (layer, te[j], 0, 0)

    return pl.pallas_call(
        _expert_ffn_kernel,
        grid_spec=pltpu.PrefetchScalarGridSpec(
            num_scalar_prefetch=2,
            grid=(nt,),
            in_specs=[pl.BlockSpec((MOE_TM * SUB, LANES), row_map),
                      pl.BlockSpec((1, 1, D_MODEL, D_EXPERT), w_map),
                      pl.BlockSpec((1, 1, D_MODEL, D_EXPERT), w_map),
                      pl.BlockSpec((1, 1, D_EXPERT, D_MODEL), w_map)],
            out_specs=pl.BlockSpec((MOE_TM * SUB, LANES), lambda j, te, nu: (j, 0)),
            scratch_shapes=[pltpu.VMEM((D_MODEL, D_EXPERT), BF16),
                            pltpu.VMEM((D_MODEL, D_EXPERT), BF16),
                            pltpu.VMEM((D_EXPERT, D_MODEL), BF16)]),
        out_shape=jax.ShapeDtypeStruct((n_sorted * SUB, LANES), F32),
        compiler_params=_cparams("arbitrary"),
        name="expert_ffn",
    )(tile_expert, n_used, xs, w_gate, w_up, w_down)


def _combine_kernel(posc_ref, posn_ref, x1_ref, info_ref, g_ref, ys_ref, *rest,
                    n_full, n_tiles, ranges, valid_lo, valid_mid, final_norm):
    out_refs, (ybuf, sem) = rest[:-2], rest[-2:]
    i = pl.program_id(0)
    slot = lax.rem(i, 2)

    def buf_base(s, k):
        return (s * 2 + k) * (TT * SUB)

    def gather(pos_ref, dst_slot):
        def row_copy(r, k):
            dst = pl.ds(pl.multiple_of(buf_base(dst_slot, k) + r * SUB, SUB), SUB)
            return pltpu.make_async_copy(ys_ref.at[_tile_of(pos_ref[0, 0, k * TT + r])],
                                         ybuf.at[dst], sem.at[dst_slot])
        return lambda lo, hi: _issue_rows(lo, hi, row_copy)

    @pl.when(i == 0)
    def _():
        ybuf[...] = jnp.zeros(ybuf.shape, F32)
        _for_tile_rows(i, n_full, ranges, gather(posc_ref, 0))

    @pl.when(i + 1 < n_tiles)
    def _():
        _for_tile_rows(i + 1, n_full, ranges, gather(posn_ref, 1 - slot))

    def drain(lo, hi):
        nrow = (hi - lo) * SUB
        for k in range(2):
            dst = pl.ds(pl.multiple_of(buf_base(slot, k) + lo * SUB, SUB), nrow)
            pltpu.make_async_copy(ys_ref.at[pl.ds(0, nrow)], ybuf.at[dst], sem.at[slot]).wait()

    _for_tile_rows(i, n_full, ranges, drain)

    info = info_ref[...]
    w1 = info[:, 2:3]
    w2 = info[:, 3:4]
    rowg = i * TT + lax.broadcasted_iota(I32, (TT, 1), 0)
    valid = (rowg < valid_lo) | (rowg >= valid_mid)
    outs = []
    for c in range(D_MODEL // LANES):
        y1 = ybuf[_slab(c, TT, buf_base(slot, 0)), :]
        y2 = ybuf[_slab(c, TT, buf_base(slot, 1)), :]
        outs.append(x1_ref[:, c * LANES:(c + 1) * LANES] + jnp.where(valid, w1 * y1 + w2 * y2, 0.0))
    out = jnp.concatenate(outs, axis=1)
    if not final_norm:
        out_refs[0][...] = out
    else:
        out = _rms(out, g_ref[...])

        @pl.when(i < n_full)
        def _():
            out_refs[0][...] = out

        @pl.when(i >= n_full)
        def _():
            out_refs[1][...] = out


def _combine(x1, info, pos, ys, g_final, t_rows, ranges, valid_lo, valid_mid, final_norm):
    n = x1.shape[0]
    nt = n // TT
    n_full = t_rows // TT
    if final_norm:
        out_specs = [pl.BlockSpec((TT, D_MODEL), lambda i: (jnp.minimum(i, n_full - 1), 0)),
                     pl.BlockSpec((TT, D_MODEL), lambda i: (jnp.maximum(i - n_full, 0), 0))]
        out_shape = [jax.ShapeDtypeStruct((t_rows, D_MODEL), F32),
                     jax.ShapeDtypeStruct((n - t_rows, D_MODEL), F32)]
    else:
        out_specs = [pl.BlockSpec((TT, D_MODEL), lambda i: (i, 0))]
        out_shape = [jax.ShapeDtypeStruct((n, D_MODEL), F32)]
    return pl.pallas_call(
        functools.partial(_combine_kernel, n_full=n_full, n_tiles=nt, ranges=ranges,
                          valid_lo=valid_lo, valid_mid=valid_mid, final_norm=final_norm),
        grid=(nt,),
        in_specs=[pl.BlockSpec((1, 1, 2 * TT), lambda i: (i, 0, 0), memory_space=pltpu.SMEM),
                  pl.BlockSpec((1, 1, 2 * TT), lambda i: (jnp.minimum(i + 1, nt - 1), 0, 0),
                               memory_space=pltpu.SMEM),
                  pl.BlockSpec((TT, D_MODEL), lambda i: (i, 0)),
                  pl.BlockSpec((TT, LANES), lambda i: (i, 0)),
                  pl.BlockSpec((1, D_MODEL), lambda i: (0, 0)),
                  pl.BlockSpec(memory_space=pl.ANY)],
        out_specs=out_specs,
        out_shape=out_shape,
        scratch_shapes=[pltpu.VMEM((2 * 2 * TT * SUB, LANES), F32),
                        pltpu.SemaphoreType.DMA((2,))],
        compiler_params=_cparams("arbitrary"),
        name="moe_combine",
    )(pos, pos, x1, info, g_final.reshape(1, D_MODEL), ys)


def _moe(x1, xn, info, infot, cnt, w_gate, w_up, w_down, layer, g_final, t_rows, ns, final_norm):
    n = x1.shape[0]
    valid_lo, valid_mid = t_rows + ns, n - N_META
    ranges = _small_tile_ranges(t_rows, ns, n)
    n_valid = valid_lo + N_META
    nt = (2 * n_valid + N_EXPERTS * (MOE_TM - 1)) // MOE_TM
    counts = cnt[0, :N_EXPERTS].astype(I32)
    tiles = (counts + MOE_TM - 1) // MOE_TM
    tile_end = jnp.cumsum(tiles)
    row_start = (tile_end - tiles) * MOE_TM
    n_used = tile_end[-1:]
    experts = jnp.arange(N_EXPERTS, dtype=I32)

    def slot_rows(e, rank):
        start = jnp.sum(jnp.where(e[..., None] == experts, row_start, 0), axis=-1)
        return start + rank

    pos = jnp.concatenate([slot_rows(infot[:, 0].astype(I32), infot[:, 4].astype(I32)),
                           slot_rows(infot[:, 1].astype(I32), infot[:, 5].astype(I32))], axis=-1)
    rows = jnp.arange(n).reshape(n // TT, TT)
    valid = (rows < valid_lo) | (rows >= valid_mid)
    pos = jnp.where(jnp.concatenate([valid, valid], axis=-1), pos, 0)[:, None, :]
    tile_ids = jnp.arange(nt, dtype=I32)
    tile_expert = jnp.sum(tile_end[None, :] <= jnp.minimum(tile_ids, n_used[0] - 1)[:, None],
                          axis=1).astype(I32)
    is_last = jnp.any((tile_ids[:, None] == tile_end[None, :] - 1) & (tiles[None, :] > 0), axis=1)
    zero_tiles = (is_last | (tile_ids >= n_used[0])).astype(I32)
    xs = _dispatch(xn, pos, zero_tiles, nt * MOE_TM, t_rows, ranges)
    ys = _expert_ffn(xs, tile_expert, n_used.astype(I32), w_gate, w_up, w_down, layer)
    return _combine(x1, info, pos, ys, g_final, t_rows, ranges, valid_lo, valid_mid, final_norm)


def _shift_rows(x, k, fill):
    if k % SUB == 0:
        return jnp.concatenate([jnp.full((k, x.shape[1]), fill, x.dtype), x[:x.shape[0] - k]], axis=0)
    row = lax.broadcasted_iota(I32, x.shape, 0)
    return jnp.where(row >= k, pltpu.roll(x, k, 0), fill)


def _lru_gates(xc, wgate_ref, bgate_ref, lam_sp):
    gates = _dot(xc, wgate_ref[...]) + bgate_ref[...]
    r = _sigmoid(gates[:, :D_LRU])
    ig = _sigmoid(gates[:, D_LRU:])
    log_a = -LRU_C * r * lam_sp
    a = jnp.exp(log_a)
    bx = jnp.sqrt(-jnp.tanh(log_a) * (a * a + 1.0)) * (ig * xc)
    return a, bx


def _conv_block(seq_ref, u_ref, w_ref, b_ref, j):
    @pl.when(j == 0)
    def _():
        seq_ref[0:8, :] = jnp.zeros((8, seq_ref.shape[1]), F32)

    u = u_ref[...]
    seq_ref[8:8 + BLK, :] = u
    seq = seq_ref[...]
    out = b_ref[...] + u * w_ref[CONV_WIDTH - 1:CONV_WIDTH, :]
    for back in range(1, CONV_WIDTH):
        tap = CONV_WIDTH - 1 - back
        out = out + pltpu.roll(seq, back, 0)[8:, :] * w_ref[tap:tap + 1, :]
    seq_ref[0:8, :] = u[BLK - 8:, :]
    return out


def _mix1_prompt_kernel(*refs, nblk):
    j = pl.program_id(0)
    o_ref = refs[16]

    @pl.when(j <= nblk)
    def _():
        _mix1_prompt_step(*refs)

    @pl.when(j > nblk)
    def _():
        o_ref[...] = jnp.zeros(o_ref.shape, F32)


def _mix1_prompt_step(cx_ref, cg_ref, z_ref, xbc_ref, dt_ref,
                      cwc_ref, cbc_ref, wgate_ref, bgate_ref, lam_ref,
                      cwd_ref, cbd_ref, dtb_ref, alog_ref, dskip_ref, nw_ref,
                      o_ref, lruh_ref, ssdh_ref,
                      seqc_ref, seqd_ref, hl_ref, hs_ref):
    j = pl.program_id(0)
    rowc = lax.broadcasted_iota(I32, (BLK, 1), 0)
    live = rowc >= jnp.where(j >= 1, 0, META_PAD)

    @pl.when(j == 0)
    def _():
        hl_ref[...] = jnp.zeros(hl_ref.shape, F32)
        hs_ref[...] = jnp.zeros(hs_ref.shape, F32)

    xc = _conv_block(seqc_ref, cx_ref, cwc_ref, cbc_ref, j)
    a, bx = _lru_gates(xc, wgate_ref, bgate_ref, _softplus(-lam_ref[...]))
    a = jnp.where(live, a, 1.0)
    bx = jnp.where(live, bx, 0.0)
    k = 1
    while k < BLK:
        bx = a * _shift_rows(bx, k, 0.0) + bx
        a = a * _shift_rows(a, k, 1.0)
        k *= 2
    h = a * hl_ref[...] + bx
    hl_ref[...] = h[BLK - 1:BLK, :]
    lruh_ref[...] = h[BLK - 1:BLK, :]
    o_ref[:, :D_LRU] = h * _gelu_tanh(cg_ref[...])

    xbc = _silu(_conv_block(seqd_ref, xbc_ref, cwd_ref, cbd_ref, j))
    xs = xbc[:, :D_SSD]
    dt = jnp.where(live, _softplus(dt_ref[...] + dtb_ref[...]), 0.0)
    a_neg = -jnp.exp(alog_ref[...])
    acum = dt * a_neg
    k = 1
    while k < BLK:
        acum = acum + _shift_rows(acum, k, 0.0)
        k *= 2
    acum_t = acum.T
    dt_t = dt.T
    xs_t = xs.T
    tri = (lax.broadcasted_iota(I32, (BLK, BLK), 0) >= lax.broadcasted_iota(I32, (BLK, BLK), 1))
    rep = SSD_HEADS // SSD_GROUPS
    ys = []
    for g in range(SSD_GROUPS):
        bg = xbc[:, D_SSD + g * SSD_STATE:D_SSD + (g + 1) * SSD_STATE]
        cgm = xbc[:, D_SSD + (SSD_GROUPS + g) * SSD_STATE:D_SSD + (SSD_GROUPS + g + 1) * SSD_STATE]
        cb = _dot_nt(cgm, bg)
        for hh in range(rep):
            hd = g * rep + hh
            col = acum[:, hd:hd + 1]
            rowv = acum_t[hd:hd + 1, :]
            decay = jnp.exp(jnp.where(tri, col - rowv, NEG_INF))
            scores = cb * decay * dt_t[hd:hd + 1, :]
            xh = xs[:, hd * SSD_HEAD_DIM:(hd + 1) * SSD_HEAD_DIM]
            y = _dot(scores, xh)
            hprev = hs_ref[hd]
            y = y + jnp.exp(col) * _dot_nt(cgm, hprev)
            last = acum_t[hd:hd + 1, BLK - 1:BLK]
            w_end = jnp.exp(last - rowv) * dt_t[hd:hd + 1, :]
            xw_t = xs_t[hd * SSD_HEAD_DIM:(hd + 1) * SSD_HEAD_DIM, :] * w_end
            hnew = jnp.exp(last) * hprev + _dot(xw_t, bg)
            hs_ref[hd] = hnew
            ssdh_ref[hd] = hnew
            ys.append(y + dskip_ref[:, hd * SSD_HEAD_DIM:(hd + 1) * SSD_HEAD_DIM] * xh)
    y = jnp.concatenate(ys, axis=1) * _silu(z_ref[...])
    o_ref[:, D_LRU:] = _rms(y, nw_ref[...])


def _mix1_prompt(cx, cg, z, xbc, dt, wts, t_rows, meta_blk):
    n = cx.shape[0]
    nblk = t_rows // BLK

    def cur(j):
        return (jnp.where(j == 0, meta_blk, j - 1), 0)

    def const2(j):
        return (0, 0)

    in_specs = [pl.BlockSpec((BLK, D_LRU), cur), pl.BlockSpec((BLK, D_LRU), cur),
                pl.BlockSpec((BLK, D_SSD), cur), pl.BlockSpec((BLK, D_XBC), cur),
                pl.BlockSpec((BLK, LANES), cur)]
    in_specs += [pl.BlockSpec(w.shape, const2) for w in wts]
    return pl.pallas_call(
        functools.partial(_mix1_prompt_kernel, nblk=nblk),
        grid=(meta_blk + 1,),
        in_specs=in_specs,
        out_specs=[pl.BlockSpec((BLK, D_MIX_ODD), cur),
                   pl.BlockSpec((1, D_LRU), const2),
                   pl.BlockSpec((SSD_HEADS, SSD_HEAD_DIM, SSD_STATE), lambda j: (0, 0, 0))],
        out_shape=[jax.ShapeDtypeStruct((n, D_MIX_ODD), F32),
                   jax.ShapeDtypeStruct((1, D_LRU), F32),
                   jax.ShapeDtypeStruct((SSD_HEADS, SSD_HEAD_DIM, SSD_STATE), F32)],
        scratch_shapes=[pltpu.VMEM((8 + BLK, D_LRU), F32),
                        pltpu.VMEM((8 + BLK, D_XBC), F32),
                        pltpu.VMEM((1, D_LRU), F32),
                        pltpu.VMEM((SSD_HEADS, SSD_HEAD_DIM, SSD_STATE), F32)],
        compiler_params=_cparams("arbitrary"),
        name="mix1_prompt",
    )(cx, cg, z, xbc, dt, *wts)


def _conv_step(seq, t, w_ref, b_ref):
    out = b_ref[...]
    for tap in range(CONV_WIDTH):
        out = out + seq[t + tap] * w_ref[tap:tap + 1, :]
    return out


def _lru_sample_kernel(cx_ref, cg_ref, buf_ref, h0_ref, cwc_ref, cbc_ref, wgate_ref, bgate_ref,
                       lam_ref, o_ref, hout_ref, *, n_new):
    seq = [buf_ref[i] for i in range(CONV_WIDTH - 1)] + [cx_ref[i] for i in range(n_new)]
    lam_sp = _softplus(-lam_ref[...])
    h = h0_ref[...]
    for t in range(n_new):
        xc = _conv_step(seq, t, cwc_ref, cbc_ref)
        a, bx = _lru_gates(xc, wgate_ref, bgate_ref, lam_sp)
        h = a * h + bx
        o_ref[t] = h * _gelu_tanh(cg_ref[t])
    hout_ref[...] = h


def _lru_sample(cx_tm, cg_tm, buf_tm, h0, wts):
    n_new = cx_tm.shape[0]
    return pl.pallas_call(
        functools.partial(_lru_sample_kernel, n_new=n_new),
        out_shape=[jax.ShapeDtypeStruct(cx_tm.shape, F32), jax.ShapeDtypeStruct(h0.shape, F32)],
        compiler_params=pltpu.CompilerParams(vmem_limit_bytes=VMEM_LIMIT),
        name="lru_sample",
    )(cx_tm, cg_tm, buf_tm, h0, *wts)


def _split3(x):
    hi = x.astype(BF16)
    r = x - hi.astype(F32)
    mid = r.astype(BF16)
    lo = (r - mid.astype(F32)).astype(BF16)
    return hi, mid, lo


def _ssd_sample_kernel(xbc_ref, dt_ref, z_ref, buf_ref, h0_ref, cwd_ref, cbd_ref, dtb_ref, alog_ref,
                       dskip_ref, nw_ref, o_ref, hout_ref, xc_ref, dec_ref, dtx_ref, y_ref, *, n_new, bs):
    seq = [buf_ref[i] for i in range(CONV_WIDTH - 1)] + [xbc_ref[i] for i in range(n_new)]
    a_neg = -jnp.exp(alog_ref[...])
    hcol = lax.broadcasted_iota(I32, (LANES, D_SSD), 1) // SSD_HEAD_DIM
    expand = jnp.where(lax.broadcasted_iota(I32, (LANES, D_SSD), 0) == hcol, 1.0, 0.0).astype(BF16)

    def widen(v):
        parts = _split3(v)
        return sum(jnp.dot(p, expand, preferred_element_type=F32) for p in parts)

    for t in range(n_new):
        xc = _silu(_conv_step(seq, t, cwd_ref, cbd_ref))
        xc_ref[t] = xc
        dt = _softplus(dt_ref[t] + dtb_ref[...])
        dtw = widen(dt)
        dec_ref[t] = jnp.exp(widen(dt * a_neg))
        dtx_ref[t] = dtw * xc[:, :D_SSD]

    zero_rows = jnp.zeros((LANES - 2 * n_new * SSD_GROUPS, SSD_STATE), F32)
    half = D_SSD // SSD_GROUPS

    def per_seq(b, c):
        rows = [xc_ref[t, pl.ds(b, 1), :] for t in range(n_new)]
        bc = [r[:, D_SSD + k * SSD_STATE:D_SSD + (k + 1) * SSD_STATE]
              for r in rows for k in range(2 * SSD_GROUPS)]
        bc_t = jnp.concatenate(bc + [zero_rows], axis=0).T
        ht = h0_ref[b].T
        for t in range(n_new):
            bcol = [jnp.broadcast_to(bc_t[:, t * 4 + g:t * 4 + g + 1], (SSD_STATE, half))
                    for g in range(SSD_GROUPS)]
            ccol = [jnp.broadcast_to(bc_t[:, t * 4 + SSD_GROUPS + g:t * 4 + SSD_GROUPS + g + 1],
                                     (SSD_STATE, half)) for g in range(SSD_GROUPS)]
            bw = jnp.concatenate(bcol, axis=1)
            cw = jnp.concatenate(ccol, axis=1)
            ht = ht * dec_ref[t, pl.ds(b, 1), :] + bw * dtx_ref[t, pl.ds(b, 1), :]
            y = jnp.sum(cw * ht, axis=0, keepdims=True)
            y_ref[t, pl.ds(b, 1), :] = y + dskip_ref[...] * rows[t][:, :D_SSD]
        hout_ref[b] = ht.T
        return c

    lax.fori_loop(0, bs, per_seq, 0)
    for t in range(n_new):
        o_ref[t] = _rms(y_ref[t] * _silu(z_ref[t]), nw_ref[...])


def _ssd_sample(xbc_tm, dt_tm, z_tm, buf_tm, h0, wts):
    n_new, b, _ = xbc_tm.shape
    bs = 16 if b % 16 == 0 else b
    hp = SSD_HEADS * SSD_HEAD_DIM

    def tm(c):
        return pl.BlockSpec((n_new, bs, c), lambda i: (0, i, 0))

    in_specs = [tm(D_XBC), tm(LANES), tm(D_SSD),
                pl.BlockSpec((CONV_WIDTH - 1, bs, D_XBC), lambda i: (0, i, 0)),
                pl.BlockSpec((bs, hp, SSD_STATE), lambda i: (i, 0, 0))]
    in_specs += [pl.BlockSpec(w.shape, lambda i: (0, 0)) for w in wts]
    return pl.pallas_call(
        functools.partial(_ssd_sample_kernel, n_new=n_new, bs=bs),
        grid=(b // bs,),
        in_specs=in_specs,
        out_specs=[tm(D_SSD), pl.BlockSpec((bs, hp, SSD_STATE), lambda i: (i, 0, 0))],
        out_shape=[jax.ShapeDtypeStruct((n_new, b, D_SSD), F32),
                   jax.ShapeDtypeStruct((b, hp, SSD_STATE), F32)],
        scratch_shapes=[pltpu.VMEM((n_new, bs, D_XBC), F32),
                        pltpu.VMEM((n_new, bs, D_SSD), F32),
                        pltpu.VMEM((n_new, bs, D_SSD), F32),
                        pltpu.VMEM((n_new, bs, D_SSD), F32)],
        compiler_params=_cparams("parallel"),
        name="ssd_sample",
    )(xbc_tm, dt_tm, z_tm, buf_tm, h0, *wts)


def _block_diag(w):
    g, c, _ = w.shape
    eye = jnp.eye(g, dtype=w.dtype)
    return (eye[:, None, :, None] * w[:, :, None, :]).reshape(g * c, g * c)


def _router_weights(w_rg, b_rg, w_re, b_re):
    w = jnp.concatenate([w_rg, w_re], axis=1)
    w = jnp.pad(w, ((0, 0), (0, LANES - w.shape[1])))
    hi = w.astype(BF16)
    lo = (w - hi.astype(F32)).astype(BF16)
    b = jnp.pad(jnp.concatenate([b_rg, b_re]), (0, LANES - MOE_GROUPS - N_EXPERTS)).reshape(1, LANES)
    return hi, lo, b


def kernel(x_prompt, x_sample, cache_swa_k, cache_swa_v, cache_pool, state_lru_conv, state_lru_h, state_ssd_conv, state_ssd_h, meta_tokens, norm_mix, norm_ffn, norm_final, attn_w_in, attn_w_out, attn_sinks, pool_w, pool_scale, rec_w_in, rec_w_out, lru_conv_w, lru_conv_b, lru_w_rg, lru_b_rg, lru_w_ig, lru_b_ig, lru_lambda, ssd_conv_w, ssd_conv_b, ssd_dt_bias, ssd_a_log, ssd_d, ssd_norm_w, moe_w_rg, moe_b_rg, moe_w_re, moe_b_re, moe_w_gate, moe_w_up, moe_w_down):
    t_rows = x_prompt.shape[1]
    b, s_new = x_sample.shape[:2]
    ns = b * s_new
    small = -(-(ns + BLK) // TT) * TT
    n = t_rows + small
    meta_blk = n // BLK - 1
    valid_lo = t_rows + ns
    valid_mid = n - N_META
    assert x_prompt.shape[0] == 1 and t_rows % TT == 0 and ns % BLK == 0 and ns <= TT

    x_small = jnp.concatenate([
        x_sample.transpose(1, 0, 2).reshape(ns, D_MODEL),
        jnp.zeros((small - ns - N_META, D_MODEL), F32),
        meta_tokens.astype(F32)], axis=0)
    xs0 = (x_prompt[0], x_small)

    def sample_tm(a):
        return a[t_rows:t_rows + ns].reshape(s_new, b, a.shape[1])

    q, kv, p = _norm_proj(xs0, norm_mix[0], attn_w_in[0].astype(BF16), (D_ATTN, 2 * D_KV, D_POOL))
    wpool_bd = _block_diag(pool_w[0]).astype(BF16)
    mix = _mix0_prompt(q, kv, p, attn_sinks[0], wpool_bd, pool_scale[0], t_rows, meta_blk)
    nq = s_new * GQA
    qs = sample_tm(q).reshape(s_new, b, N_KV_HEADS, GQA, HEAD_DIM).transpose(1, 2, 0, 3, 4)
    qs = qs.reshape(b, N_KV_HEADS, nq, HEAD_DIM)
    kv_s = sample_tm(kv).transpose(1, 0, 2)
    kn, vn = kv_s[:, :, :D_KV], kv_s[:, :, D_KV:]
    ck = cache_swa_k[0].reshape(b, N_META + WINDOW, D_KV)
    cv = cache_swa_v[0].reshape(b, N_META + WINDOW, D_KV)
    o_s = _attn_sample(qs, ck, cv, kn, vn, attn_sinks[0])
    o_s = o_s.reshape(b, N_KV_HEADS, s_new, GQA, HEAD_DIM).transpose(2, 0, 1, 3, 4).reshape(s_new, b, D_ATTN)
    p_s = sample_tm(p)
    pool_buf_tm = cache_pool[0].transpose(1, 0, 2)
    yp_s = _pool_sample(pool_buf_tm, p_s, wpool_bd, pool_scale[0])
    mix_s = jnp.concatenate([o_s, yp_s], axis=-1).reshape(ns, D_MIX_EVEN)
    wr_hi, wr_lo, br = _router_weights(moe_w_rg[0], moe_b_rg[0], moe_w_re[0], moe_b_re[0])
    x1, xn, info, infot, cnt = _outproj_route(xs0, mix, mix_s, attn_w_out[0].astype(BF16), norm_ffn[0],
                                              wr_hi, wr_lo, br, t_rows, valid_lo, valid_mid)
    x, = _moe(x1, xn, info, infot, cnt, moe_w_gate, moe_w_up, moe_w_down, 0, norm_final,
              t_rows, ns, False)

    meta_rows = slice(n - N_META, n)
    last_rows = slice(t_rows - WINDOW, t_rows)
    swa_k_p = jnp.concatenate([kv[meta_rows, :D_KV], kv[last_rows, :D_KV]], axis=0)
    swa_v_p = jnp.concatenate([kv[meta_rows, D_KV:], kv[last_rows, D_KV:]], axis=0)
    swa_k_p = swa_k_p.reshape(1, 1, N_META + WINDOW, N_KV_HEADS, HEAD_DIM)
    swa_v_p = swa_v_p.reshape(1, 1, N_META + WINDOW, N_KV_HEADS, HEAD_DIM)
    pool_p = p[t_rows - POOL_BUF:t_rows].reshape(1, 1, POOL_BUF, D_POOL)
    swa_k_s = jnp.concatenate([ck[:, :N_META], ck[:, N_META + s_new:], kn], axis=1)
    swa_v_s = jnp.concatenate([cv[:, :N_META], cv[:, N_META + s_new:], vn], axis=1)
    swa_k_s = swa_k_s.reshape(1, b, N_META + WINDOW, N_KV_HEADS, HEAD_DIM)
    swa_v_s = swa_v_s.reshape(1, b, N_META + WINDOW, N_KV_HEADS, HEAD_DIM)
    pool_s = jnp.concatenate([cache_pool[0], p_s.transpose(1, 0, 2)], axis=1)[:, -POOL_BUF:][None]

    d_in_odd = rec_w_in.shape[2]
    w_in1 = jnp.pad(rec_w_in[0], ((0, 0), (0, LANES - SSD_HEADS))).astype(BF16)
    assert d_in_odd == 2 * D_LRU + D_SSD + D_XBC + SSD_HEADS
    cx, cg, z, xbc, dt = _norm_proj((x,), norm_mix[1], w_in1, (D_LRU, D_LRU, D_SSD, D_XBC, LANES))
    wgate = jnp.concatenate([_block_diag(lru_w_rg[0]), _block_diag(lru_w_ig[0])], axis=1).astype(BF16)
    bgate = jnp.concatenate([lru_b_rg[0], lru_b_ig[0]]).reshape(1, 2 * D_LRU)
    pad8 = (0, LANES - SSD_HEADS)
    dtb = jnp.pad(ssd_dt_bias[0], pad8).reshape(1, LANES)
    alog = jnp.pad(ssd_a_log[0], pad8).reshape(1, LANES)
    dskip = jnp.repeat(ssd_d[0], SSD_HEAD_DIM).reshape(1, D_SSD)
    lru_w = (lru_conv_w[0], lru_conv_b[0].reshape(1, D_LRU), wgate, bgate, lru_lambda[0].reshape(1, D_LRU))
    ssd_w = (ssd_conv_w[0], ssd_conv_b[0].reshape(1, D_XBC), dtb, alog, dskip, ssd_norm_w[0].reshape(1, D_SSD))
    mix, lru_h_p, ssd_h_p = _mix1_prompt(cx, cg, z, xbc, dt, lru_w + ssd_w, t_rows, meta_blk)
    cx_s, cg_s, z_s, xbc_s, dt_s = (sample_tm(a) for a in (cx, cg, z, xbc, dt))
    yc_s, lru_h_s = _lru_sample(cx_s, cg_s, state_lru_conv[0].transpose(1, 0, 2), state_lru_h[0], lru_w)
    h0 = state_ssd_h[0].reshape(b, SSD_HEADS * SSD_HEAD_DIM, SSD_STATE)
    yd_s, ssd_h_s = _ssd_sample(xbc_s, dt_s, z_s, state_ssd_conv[0].transpose(1, 0, 2), h0, ssd_w)
    mix_s = jnp.concatenate([yc_s, yd_s], axis=-1).reshape(ns, D_MIX_ODD)
    wr_hi, wr_lo, br = _router_weights(moe_w_rg[1], moe_b_rg[1], moe_w_re[1], moe_b_re[1])
    x1, xn, info, infot, cnt = _outproj_route((x,), mix, mix_s, rec_w_out[0].astype(BF16), norm_ffn[1],
                                              wr_hi, wr_lo, br, t_rows, valid_lo, valid_mid)
    y_p, y_small = _moe(x1, xn, info, infot, cnt, moe_w_gate, moe_w_up, moe_w_down, 1, norm_final,
                        t_rows, ns, True)

    tail = CONV_WIDTH - 1
    lru_conv_p = cx[t_rows - tail:t_rows].reshape(1, 1, tail, D_LRU)
    ssd_conv_p = xbc[t_rows - tail:t_rows].reshape(1, 1, tail, D_XBC)
    lru_conv_s = jnp.concatenate([state_lru_conv[0], cx_s.transpose(1, 0, 2)], axis=1)[:, -tail:][None]
    ssd_conv_s = jnp.concatenate([state_ssd_conv[0], xbc_s.transpose(1, 0, 2)], axis=1)[:, -tail:][None]

    y_prompt = y_p[None]
    y_sample = y_small[:ns].reshape(s_new, b, D_MODEL).transpose(1, 0, 2)
    return (y_prompt, y_sample, swa_k_p, swa_v_p, pool_p, lru_conv_p,
            lru_h_p.reshape(1, 1, D_LRU), ssd_conv_p,
            ssd_h_p.reshape(1, 1, SSD_HEADS, SSD_HEAD_DIM, SSD_STATE),
            swa_k_s, swa_v_s, pool_s, lru_conv_s, lru_h_s[None], ssd_conv_s,
            ssd_h_s.reshape(1, b, SSD_HEADS, SSD_HEAD_DIM, SSD_STATE))
```

```python
import functools

import jax
import jax.numpy as jnp
from jax import lax
from jax.experimental import pallas as pl
from jax.experimental.pallas import tpu as pltpu

F32 = jnp.float32
BF16 = jnp.bfloat16
I32 = jnp.int32

D_MODEL = 1024
N_META = 16
EPS = 1e-6
PAST_LEN = 16384
N_Q_HEADS = 16
N_KV_HEADS = 2
GQA = N_Q_HEADS // N_KV_HEADS
HEAD_DIM = 64
WINDOW = 128
D_ATTN = N_Q_HEADS * HEAD_DIM
D_KV = N_KV_HEADS * HEAD_DIM
POOL_WINDOWS = (2, 4, 8, 16)
POOL_GROUP_DIM = 96
D_POOL = len(POOL_WINDOWS) * POOL_GROUP_DIM
POOL_BUF = max(POOL_WINDOWS) - 1
D_MIX_EVEN = D_ATTN + D_POOL
D_LRU = 512
LRU_HEADS = 8
LRU_BLOCK = D_LRU // LRU_HEADS
LRU_C = 8.0
CONV_WIDTH = 4
SSD_HEADS = 8
SSD_HEAD_DIM = 64
D_SSD = SSD_HEADS * SSD_HEAD_DIM
SSD_GROUPS = 2
SSD_STATE = 128
D_XBC = D_SSD + 2 * SSD_GROUPS * SSD_STATE
D_MIX_ODD = D_LRU + D_SSD
MOE_GROUPS = 4
EXPERTS_PER_GROUP = 8
N_EXPERTS = MOE_GROUPS * EXPERTS_PER_GROUP
D_EXPERT = 256

BLK = 128
META_PAD = BLK - N_META
LANES = 128
MOE_TM = 256
NEG_INF = float("-inf")
VMEM_LIMIT = 56 * 1024 * 1024


def _cparams(*sem):
    return pltpu.CompilerParams(dimension_semantics=sem, vmem_limit_bytes=VMEM_LIMIT)


TT = 512
SUB = 8


def _small_tile_ranges(t_rows, ns, n):
    out = {}
    for k in range(t_rows // TT, n // TT):
        lo = k * TT
        ranges = []
        a0, a1 = max(lo, t_rows), min(lo + TT, t_rows + ns)
        if a1 > a0:
            ranges.append((a0 - lo, a1 - lo))
        b0 = max(lo, n - N_META)
        if lo + TT > b0:
            ranges.append((b0 - lo, TT))
        out[k] = tuple(ranges)
    return out


def _for_tile_rows(i, n_full, ranges, fn):
    @pl.when(i < n_full)
    def _():
        fn(0, TT)

    for k, rs in ranges.items():
        if rs:
            @pl.when(i == k)
            def _(rs=rs):
                for lo, hi in rs:
                    fn(lo, hi)


def _slab(c, rows, base=0):
    return pl.ds(base + c, rows, stride=SUB)


def _tile_of(r):
    return pl.ds(pl.multiple_of(r * SUB, SUB), SUB)


def _sigmoid(x):
    return 1.0 / (1.0 + jnp.exp(-x))


def _silu(x):
    return x * _sigmoid(x)


def _softplus(x):
    return jnp.maximum(x, 0.0) + jnp.log1p(jnp.exp(-jnp.abs(x)))


def _gelu_tanh(x):
    return 0.5 * x * (1.0 + jnp.tanh(0.7978845608028654 * (x + 0.044715 * x * x * x)))


def _dot(a, b):
    return jnp.dot(a.astype(BF16), b.astype(BF16), preferred_element_type=F32)


def _dot_nt(a, b):
    return lax.dot_general(a.astype(BF16), b.astype(BF16), (((1,), (1,)), ((), ())),
                           preferred_element_type=F32)


def _rms(x, g):
    ms = jnp.mean(x * x, axis=-1, keepdims=True)
    return x * lax.rsqrt(ms + EPS) * g


def _row_specs(xs, width):
    if len(xs) == 1:
        return [pl.BlockSpec((TT, width), lambda i: (i, 0))]
    na = xs[0].shape[0] // TT
    return [pl.BlockSpec((TT, width), lambda i: (jnp.minimum(i, na - 1), 0)),
            pl.BlockSpec((TT, width), lambda i: (jnp.maximum(i - na, 0), 0))]


def _pick_rows(refs, n_first):
    if len(refs) == 1:
        return refs[0][...]
    return jnp.where(pl.program_id(0) < n_first, refs[0][...], refs[1][...])


def _norm_proj_kernel(*refs, splits, n_x, n_first):
    x_refs, (g_ref, w_ref), out_refs = refs[:n_x], refs[n_x:n_x + 2], refs[n_x + 2:]
    y = _rms(_pick_rows(x_refs, n_first), g_ref[...]).astype(BF16)
    off = 0
    for o_ref, n in zip(out_refs, splits):
        o_ref[...] = jnp.dot(y, w_ref[:, off:off + n], preferred_element_type=F32).astype(o_ref.dtype)
        off += n


def _norm_proj(xs, g, w_bf16, splits, dtypes=None):
    n = sum(x.shape[0] for x in xs)
    dtypes = dtypes or (F32,) * len(splits)
    return pl.pallas_call(
        functools.partial(_norm_proj_kernel, splits=splits, n_x=len(xs), n_first=xs[0].shape[0] // TT),
        grid=(n // TT,),
        in_specs=_row_specs(xs, D_MODEL) + [pl.BlockSpec((1, D_MODEL), lambda i: (0, 0)),
                                             pl.BlockSpec(w_bf16.shape, lambda i: (0, 0))],
        out_specs=[pl.BlockSpec((TT, s), lambda i: (i, 0)) for s in splits],
        out_shape=[jax.ShapeDtypeStruct((n, s), dt) for s, dt in zip(splits, dtypes)],
        compiler_params=_cparams("parallel"),
        name="norm_proj",
    )(*xs, g.reshape(1, D_MODEL), w_bf16)


def _mix0_prompt_kernel(*refs, nblk):
    j = pl.program_id(0)
    o_ref = refs[8]

    @pl.when(j <= nblk)
    def _():
        _mix0_prompt_step(*refs)

    @pl.when(j > nblk)
    def _():
        o_ref[...] = jnp.zeros(o_ref.shape, o_ref.dtype)


def _mix0_prompt_step(sink_ref, q_ref, kvc_ref, kvp_ref, kvm_ref, p_ref, wp_ref, ps_ref,
                      o_ref, seq_ref, s_ref, pr_ref, rd_ref):
    j = pl.program_id(0)
    kv_all = jnp.concatenate([kvp_ref[...], kvc_ref[...], kvm_ref[...],
                              jnp.zeros((META_PAD, 2 * D_KV), F32)], axis=0)
    k_all = kv_all[:, :D_KV]
    v_all = kv_all[:, D_KV:]
    k_rot = pltpu.roll(k_all, HEAD_DIM, 1)
    v_rot = pltpu.roll(v_all, HEAD_DIM, 1)
    lo_half = lax.broadcasted_iota(I32, k_all.shape, 1) < HEAD_DIM

    def halves(own_lanes, other_lanes, g):
        src_lo, src_hi = (own_lanes, other_lanes) if g == 0 else (other_lanes, own_lanes)
        return (jnp.where(lo_half, src_lo, 0.0).astype(BF16), jnp.where(lo_half, 0.0, src_hi).astype(BF16))

    r = lax.broadcasted_iota(I32, (BLK, BLK), 0)
    c = lax.broadcasted_iota(I32, (BLK, BLK), 1)
    own = r >= c
    dist_loc = jnp.where(own, r - c, r - c + BLK).astype(F32)
    d_min = jnp.where(j >= 2, -BLK, 0)
    c_min = jnp.where(j == 0, META_PAD, 0)
    mask_loc = jnp.where((r - c >= d_min) & (c >= c_min), 0.0, NEG_INF)
    meta_hi = jnp.where(j >= 1, N_META, 0)
    dist_meta = jnp.where(c < meta_hi, (N_META + (j - 1) * BLK + r - c).astype(F32), jnp.inf)
    slabs = GQA // 2
    nkeys = 3 * BLK
    for g in range(N_KV_HEADS):
        k_even, k_odd = halves(k_all, k_rot, g)
        v_even, v_odd = halves(v_all, v_rot, g)
        qg = jnp.concatenate([q_ref[:, (g * slabs + i) * LANES:(g * slabs + i + 1) * LANES]
                              for i in range(slabs)], axis=0) * (HEAD_DIM ** -0.5)
        s_ref[...] = _dot_nt(qg, jnp.concatenate([k_even, k_odd], axis=0))
        for i in range(slabs):
            rows = slice(i * BLK, (i + 1) * BLK)
            for par in range(2):
                h = g * GQA + 2 * i + par
                slope = 2.0 ** (-8.0 * (h + 1) / N_Q_HEADS)
                sink = sink_ref[h]
                k0 = par * nkeys
                s_loc = (jnp.where(own, s_ref[rows, k0 + BLK:k0 + 2 * BLK], s_ref[rows, k0:k0 + BLK])
                         - slope * dist_loc + mask_loc)
                s_met = s_ref[rows, k0 + 2 * BLK:k0 + 3 * BLK] - slope * dist_meta
                m = jnp.maximum(jnp.max(jnp.maximum(s_loc, s_met), axis=-1, keepdims=True), sink)
                p_loc = jnp.exp(s_loc - m)
                p_met = jnp.exp(s_met - m)
                den = jnp.sum(p_loc + p_met, axis=-1, keepdims=True) + jnp.exp(sink - m)
                pr_ref[par, rows, 0:BLK] = jnp.where(own, 0.0, p_loc).astype(BF16)
                pr_ref[par, rows, BLK:2 * BLK] = jnp.where(own, p_loc, 0.0).astype(BF16)
                pr_ref[par, rows, 2 * BLK:] = p_met.astype(BF16)
                rd_ref[par, rows, :] = jnp.broadcast_to(1.0 / den, (BLK, LANES))
        o = (jnp.dot(pr_ref[0], v_even, preferred_element_type=F32) * rd_ref[0]
             + jnp.dot(pr_ref[1], v_odd, preferred_element_type=F32) * rd_ref[1])
        for i in range(slabs):
            o_ref[:, (g * slabs + i) * LANES:(g * slabs + i + 1) * LANES] = (
                o[i * BLK:(i + 1) * BLK, :].astype(o_ref.dtype))

    @pl.when(j == 0)
    def _():
        seq_ref[0:N_META, :] = jnp.zeros((N_META, D_POOL), F32)

    seq_ref[N_META:N_META + BLK, :] = p_ref[...]
    s0 = seq_ref[...]
    s1 = s0 + pltpu.roll(s0, 1, 0)
    s2 = s1 + pltpu.roll(s1, 2, 0)
    s3 = s2 + pltpu.roll(s2, 4, 0)
    s4 = s3 + pltpu.roll(s3, 8, 0)
    rows = N_META + BLK
    col = lax.broadcasted_iota(I32, (rows, D_POOL), 1)
    row = lax.broadcasted_iota(I32, (rows, D_POOL), 0) - N_META
    wsum = jnp.where(col < POOL_GROUP_DIM, s1,
                     jnp.where(col < 2 * POOL_GROUP_DIM, s2,
                               jnp.where(col < 3 * POOL_GROUP_DIM, s3, s4)))
    wlen = jnp.where(col < POOL_GROUP_DIM, 2,
                     jnp.where(col < 2 * POOL_GROUP_DIM, 4,
                               jnp.where(col < 3 * POOL_GROUP_DIM, 8, 16)))
    pos = row + jnp.where(j == 0, -META_PAD, N_META + (j - 1) * BLK)
    cnt = jnp.clip(jnp.minimum(pos + 1, wlen), 1, 16).astype(F32)
    d = (wsum / cnt - s0)[N_META:, :]
    o_ref[:, D_ATTN:] = (_dot(d, wp_ref[...]) * ps_ref[...]).astype(o_ref.dtype)
    seq_ref[0:N_META, :] = p_ref[BLK - N_META:, :]


def _mix0_prompt(q, kv, p, sinks, wpool_bd, pool_scale, t_rows, meta_blk):
    n = q.shape[0]
    nblk = t_rows // BLK

    def cur(j):
        return jnp.where(j == 0, meta_blk, j - 1)

    meta16 = (meta_blk * BLK + META_PAD) // N_META
    return pl.pallas_call(
        functools.partial(_mix0_prompt_kernel, nblk=nblk),
        grid=(meta_blk + 1,),
        in_specs=[pl.BlockSpec(memory_space=pltpu.SMEM),
                  pl.BlockSpec((BLK, D_ATTN), lambda j: (cur(j), 0)),
                  pl.BlockSpec((BLK, 2 * D_KV), lambda j: (cur(j), 0)),
                  pl.BlockSpec((BLK, 2 * D_KV), lambda j: (jnp.maximum(j - 2, 0), 0)),
                  pl.BlockSpec((N_META, 2 * D_KV), lambda j: (meta16, 0)),
                  pl.BlockSpec((BLK, D_POOL), lambda j: (cur(j), 0)),
                  pl.BlockSpec((D_POOL, D_POOL), lambda j: (0, 0)),
                  pl.BlockSpec((1, D_POOL), lambda j: (0, 0))],
        out_specs=pl.BlockSpec((BLK, D_MIX_EVEN), lambda j: (cur(j), 0)),
        out_shape=jax.ShapeDtypeStruct((n, D_MIX_EVEN), BF16),
        scratch_shapes=[pltpu.VMEM((N_META + BLK, D_POOL), F32),
                        pltpu.VMEM((GQA // 2 * BLK, 2 * 3 * BLK), F32),
                        pltpu.VMEM((2, GQA // 2 * BLK, 3 * BLK), BF16),
                        pltpu.VMEM((2, GQA // 2 * BLK, LANES), F32)],
        compiler_params=_cparams("arbitrary"),
        name="mix0_prompt",
    )(sinks, q, kv, kv, kv, p, wpool_bd, pool_scale.reshape(1, D_POOL))


def _attn_sample_kernel(sink_ref, q_ref, ck_ref, cv_ref, kn_ref, vn_ref, o_ref, *, n_new):
    nk = N_META + WINDOW + n_new
    nq = n_new * GQA
    k_all = jnp.concatenate([ck_ref[...], kn_ref[...]], axis=1).astype(BF16)
    v_all = jnp.concatenate([cv_ref[...], vn_ref[...]], axis=1).astype(BF16)
    r = lax.broadcasted_iota(I32, (nq, nk), 0)
    slot = lax.broadcasted_iota(I32, (nq, nk), 1)
    t = r // GQA
    hh = r - t * GQA
    kpos = jnp.where(slot < N_META, slot,
                     jnp.where(slot < N_META + WINDOW, PAST_LEN - WINDOW - N_META + slot,
                               PAST_LEN - N_META - WINDOW + slot))
    dist = PAST_LEN + t - kpos
    allowed = (dist >= 0) & ((slot < N_META) | (dist < WINDOW))
    distf = dist.astype(F32)
    hh1 = hh[:, :1]
    for g in range(N_KV_HEADS):
        slope = jnp.exp2(-8.0 * (g * GQA + hh + 1).astype(F32) / N_Q_HEADS)
        sink = jnp.zeros((nq, 1), F32)
        for i in range(GQA):
            sink = jnp.where(hh1 == i, sink_ref[g * GQA + i], sink)
        kg = k_all[:, :, g * HEAD_DIM:(g + 1) * HEAD_DIM]
        vg = v_all[:, :, g * HEAD_DIM:(g + 1) * HEAD_DIM]
        s = jnp.einsum("bqd,bkd->bqk", q_ref[:, g].astype(BF16), kg,
                       preferred_element_type=F32) * (HEAD_DIM ** -0.5)
        s = jnp.where(allowed[None], s - (slope * distf)[None], NEG_INF)
        sink = sink[None]
        m = jnp.maximum(jnp.max(s, axis=-1, keepdims=True), sink)
        pr = jnp.exp(s - m)
        den = jnp.sum(pr, axis=-1, keepdims=True) + jnp.exp(sink - m)
        o = jnp.einsum("bqk,bkd->bqd", pr.astype(BF16), vg, preferred_element_type=F32)
        o_ref[:, g] = o / den


def _attn_sample(qs, ck, cv, kn, vn, sinks):
    b, _, nq, _ = qs.shape
    n_new = kn.shape[1]
    bs = 16 if b % 16 == 0 else b
    nc = N_META + WINDOW
    return pl.pallas_call(
        functools.partial(_attn_sample_kernel, n_new=n_new),
        grid=(b // bs,),
        in_specs=[pl.BlockSpec(memory_space=pltpu.SMEM),
                  pl.BlockSpec((bs, N_KV_HEADS, nq, HEAD_DIM), lambda i: (i, 0, 0, 0)),
                  pl.BlockSpec((bs, nc, D_KV), lambda i: (i, 0, 0)),
                  pl.BlockSpec((bs, nc, D_KV), lambda i: (i, 0, 0)),
                  pl.BlockSpec((bs, n_new, D_KV), lambda i: (i, 0, 0)),
                  pl.BlockSpec((bs, n_new, D_KV), lambda i: (i, 0, 0))],
        out_specs=pl.BlockSpec((bs, N_KV_HEADS, nq, HEAD_DIM), lambda i: (i, 0, 0, 0)),
        out_shape=jax.ShapeDtypeStruct(qs.shape, F32),
        compiler_params=_cparams("parallel"),
        name="attn_sample",
    )(sinks, qs, ck, cv, kn, vn)


def _pool_sample_kernel(buf_ref, p_ref, wp_ref, ps_ref, o_ref, *, n_new):
    seq = [buf_ref[i] for i in range(POOL_BUF)] + [p_ref[i] for i in range(n_new)]
    b = p_ref.shape[1]
    col = lax.broadcasted_iota(I32, (b, D_POOL), 1)
    for t in range(n_new):
        cur = seq[POOL_BUF + t]
        acc = cur
        sums = {}
        for back in range(1, max(POOL_WINDOWS)):
            acc = acc + seq[POOL_BUF + t - back]
            if back + 1 in POOL_WINDOWS:
                sums[back + 1] = acc
        mean = jnp.where(col < POOL_GROUP_DIM, sums[2] / 2.0,
                         jnp.where(col < 2 * POOL_GROUP_DIM, sums[4] / 4.0,
                                   jnp.where(col < 3 * POOL_GROUP_DIM, sums[8] / 8.0,
                                             sums[16] / 16.0)))
        o_ref[t] = _dot(mean - cur, wp_ref[...]) * ps_ref[...]


def _pool_sample(buf_tm, p_tm, wpool_bd, pool_scale):
    n_new = p_tm.shape[0]
    return pl.pallas_call(
        functools.partial(_pool_sample_kernel, n_new=n_new),
        out_shape=jax.ShapeDtypeStruct(p_tm.shape, F32),
        compiler_params=pltpu.CompilerParams(vmem_limit_bytes=VMEM_LIMIT),
        name="pool_sample",
    )(buf_tm, p_tm, wpool_bd, pool_scale.reshape(1, D_POOL))


def _outproj_route_kernel(*refs, valid_lo, valid_mid, n_x, n_first, ns):
    tt = TT
    x_refs = refs[:n_x]
    (mix_ref, mixs_ref, w_ref, g_ref, wrh_ref, wrl_ref, br_ref,
     x1_ref, xn_ref, info_ref, infot_ref, cnt_ref, acc_ref) = refs[n_x:]
    i = pl.program_id(0)
    x = _pick_rows(x_refs, n_first)

    @pl.when(i != n_first)
    def _():
        x1_ref[...] = x + _dot(mix_ref[...], w_ref[...])

    @pl.when(i == n_first)
    def _():
        x1_ref[:ns, :] = x[:ns] + _dot(mixs_ref[...], w_ref[...])
        if ns < tt:
            x1_ref[ns:, :] = x[ns:] + _dot(mix_ref[ns:, :], w_ref[...])

    x1 = x1_ref[...]
    xn = _rms(x1, g_ref[...])
    for cc_ in range(D_MODEL // LANES):
        xn_ref[_slab(cc_, tt), :] = xn[:, cc_ * LANES:(cc_ + 1) * LANES]
    hi = xn.astype(BF16)
    lo = (xn - hi.astype(F32)).astype(BF16)
    hi_out = jnp.dot(hi, jnp.concatenate([wrh_ref[...], wrl_ref[...]], axis=1), preferred_element_type=F32)
    logits = (hi_out[:, :LANES] + jnp.dot(lo, wrh_ref[...], preferred_element_type=F32)
              + hi_out[:, LANES:]) + br_ref[...]
    lane = lax.broadcasted_iota(I32, (tt, LANES), 1)
    lanef = lane.astype(F32)
    lg = jnp.where(lane < MOE_GROUPS, logits, NEG_INF)
    mg = jnp.max(lg, axis=-1, keepdims=True)
    gstar = jnp.min(jnp.where(lg == mg, lanef, 1e9), axis=-1, keepdims=True)
    pgroup = 1.0 / jnp.sum(jnp.exp(lg - mg), axis=-1, keepdims=True)
    lo_lane = MOE_GROUPS + gstar * EXPERTS_PER_GROUP
    le = jnp.where((lanef >= lo_lane) & (lanef < lo_lane + EXPERTS_PER_GROUP), logits, NEG_INF)
    v1 = jnp.max(le, axis=-1, keepdims=True)
    i1 = jnp.min(jnp.where(le == v1, lanef, 1e9), axis=-1, keepdims=True)
    le2 = jnp.where(lanef == i1, NEG_INF, le)
    v2 = jnp.max(le2, axis=-1, keepdims=True)
    i2 = jnp.min(jnp.where(le2 == v2, lanef, 1e9), axis=-1, keepdims=True)
    tq = jnp.exp(v2 - v1)
    w1 = pgroup / (1.0 + tq)
    w2 = pgroup * tq / (1.0 + tq)
    e1 = i1 - MOE_GROUPS
    e2 = i2 - MOE_GROUPS
    rowg = i * tt + lax.broadcasted_iota(I32, (tt, LANES), 0)
    valid = (rowg < valid_lo) | (rowg >= valid_mid)
    oh1 = jnp.where(valid & (lanef == e1), 1.0, 0.0)
    oh2 = jnp.where(valid & (lanef == e2), 1.0, 0.0)
    both = oh1 + oh2
    rr = lax.broadcasted_iota(I32, (tt, tt), 0)
    cc = lax.broadcasted_iota(I32, (tt, tt), 1)
    tril = jnp.where(cc < rr, 1.0, 0.0).astype(BF16)

    @pl.when(i == 0)
    def _():
        acc_ref[...] = jnp.zeros((1, LANES), F32)

    base = acc_ref[...] + jnp.dot(tril, both.astype(BF16), preferred_element_type=F32)
    rank1 = jnp.sum(oh1 * base, axis=-1, keepdims=True)
    rank2 = jnp.sum(oh2 * base, axis=-1, keepdims=True)
    total = acc_ref[...] + jnp.sum(both, axis=0, keepdims=True)
    acc_ref[...] = total
    cnt_ref[...] = jnp.broadcast_to(total, (8, LANES))
    info = jnp.where(lane == 0, e1, jnp.where(lane == 1, e2, jnp.where(lane == 2, w1, jnp.where(
        lane == 3, w2, jnp.where(lane == 4, rank1, jnp.where(lane == 5, rank2, 0.0))))))
    info_ref[...] = info
    infot_ref[0] = info.T[:SUB, :]


def _outproj_route(xs, mix, mix_s, w_out_bf16, g_ffn, wr_hi, wr_lo, br, t_rows, valid_lo, valid_mid):
    n = mix.shape[0]
    ns, dm = mix_s.shape
    nt = n // TT

    def const2(i):
        return (0, 0)

    return pl.pallas_call(
        functools.partial(_outproj_route_kernel, valid_lo=valid_lo, valid_mid=valid_mid,
                          n_x=len(xs), n_first=t_rows // TT, ns=ns),
        grid=(nt,),
        in_specs=_row_specs(xs, D_MODEL) + [
            pl.BlockSpec((TT, dm), lambda i: (i, 0)),
            pl.BlockSpec((ns, dm), const2),
            pl.BlockSpec((dm, D_MODEL), const2),
            pl.BlockSpec((1, D_MODEL), const2),
            pl.BlockSpec((D_MODEL, LANES), const2),
            pl.BlockSpec((D_MODEL, LANES), const2),
            pl.BlockSpec((1, LANES), const2)],
        out_specs=[pl.BlockSpec((TT, D_MODEL), lambda i: (i, 0)),
                   pl.BlockSpec((TT * SUB, LANES), lambda i: (i, 0)),
                   pl.BlockSpec((TT, LANES), lambda i: (i, 0)),
                   pl.BlockSpec((1, SUB, TT), lambda i: (i, 0, 0)),
                   pl.BlockSpec((8, LANES), const2)],
        out_shape=[jax.ShapeDtypeStruct((n, D_MODEL), F32),
                   jax.ShapeDtypeStruct((n * SUB, LANES), F32),
                   jax.ShapeDtypeStruct((n, LANES), F32),
                   jax.ShapeDtypeStruct((nt, SUB, TT), F32),
                   jax.ShapeDtypeStruct((8, LANES), F32)],
        scratch_shapes=[pltpu.VMEM((1, LANES), F32)],
        compiler_params=_cparams("arbitrary"),
        name="outproj_route",
    )(*xs, mix, mix_s, w_out_bf16, g_ffn.reshape(1, D_MODEL), wr_hi, wr_lo, br)


ROWS_PER_ISSUE = 8


def _issue_rows(lo, hi, row_copy):
    assert (hi - lo) % ROWS_PER_ISSUE == 0

    def body(it, c):
        for u in range(ROWS_PER_ISSUE):
            r = lo + it * ROWS_PER_ISSUE + u
            row_copy(r, 0).start(priority=0)
            row_copy(r, 1).start(priority=1)
        return c

    lax.fori_loop(0, (hi - lo) // ROWS_PER_ISSUE, body, 0)


def _dispatch_kernel(pos_ref, zt_ref, xn_ref, xs_ref, zbuf, sem, zsem, *, n_full, ranges, n_tiles):
    i = pl.program_id(0)

    @pl.when(i == 0)
    def _():
        zbuf[...] = jnp.zeros(zbuf.shape, F32)

        def zero_copy(t):
            rows = MOE_TM * SUB
            return pltpu.make_async_copy(zbuf, xs_ref.at[pl.ds(pl.multiple_of(t * rows, rows), rows)], zsem)

        def zstart(t, c):
            @pl.when(zt_ref[t] != 0)
            def _():
                zero_copy(t).start()
            return c

        def zwait(t, c):
            @pl.when(zt_ref[t] != 0)
            def _():
                zero_copy(t).wait()
            return c

        lax.fori_loop(0, n_tiles, zstart, 0)
        lax.fori_loop(0, n_tiles, zwait, 0)

    def row_copy(r, k):
        return pltpu.make_async_copy(xn_ref.at[_tile_of(r)], xs_ref.at[_tile_of(pos_ref[0, 0, k * TT + r])], sem)

    def scatter(lo, hi):
        _issue_rows(lo, hi, row_copy)
        nrow = (hi - lo) * SUB
        for _ in range(2):
            pltpu.make_async_copy(xn_ref.at[pl.ds(lo * SUB, nrow)], xs_ref.at[pl.ds(0, nrow)], sem).wait()

    _for_tile_rows(i, n_full, ranges, scatter)


def _dispatch(xn, pos, zero_tiles, n_sorted, t_rows, ranges):
    n = xn.shape[0] // SUB
    return pl.pallas_call(
        functools.partial(_dispatch_kernel, n_full=t_rows // TT, ranges=ranges,
                          n_tiles=n_sorted // MOE_TM),
        grid=(n // TT,),
        in_specs=[pl.BlockSpec((1, 1, 2 * TT), lambda i: (i, 0, 0), memory_space=pltpu.SMEM),
                  pl.BlockSpec(memory_space=pltpu.SMEM),
                  pl.BlockSpec((TT * SUB, LANES), lambda i: (i, 0))],
        out_specs=pl.BlockSpec(memory_space=pl.ANY),
        out_shape=jax.ShapeDtypeStruct((n_sorted * SUB, LANES), F32),
        scratch_shapes=[pltpu.VMEM((MOE_TM * SUB, LANES), F32),
                        pltpu.SemaphoreType.DMA(()),
                        pltpu.SemaphoreType.DMA(())],
        compiler_params=_cparams("arbitrary"),
        name="moe_dispatch",
    )(pos, zero_tiles, xn)


def _expert_ffn_kernel(te_ref, nu_ref, xs_ref, wg_ref, wu_ref, wd_ref, ys_ref, wgb, wub, wdb):
    j = pl.program_id(0)
    changed = (j == 0) | (te_ref[j] != te_ref[jnp.maximum(j - 1, 0)])

    @pl.when((j < nu_ref[0]) & changed)
    def _():
        wgb[...] = wg_ref[0, 0].astype(BF16)
        wub[...] = wu_ref[0, 0].astype(BF16)
        wdb[...] = wd_ref[0, 0].astype(BF16)

    n_slab = D_MODEL // LANES

    @pl.when(j < nu_ref[0])
    def _():
        x = jnp.concatenate([xs_ref[_slab(c, MOE_TM), :] for c in range(n_slab)], axis=1).astype(BF16)
        hg = jnp.dot(x, wgb[...], preferred_element_type=F32)
        hu = jnp.dot(x, wub[...], preferred_element_type=F32)
        h = (_silu(hg) * hu).astype(BF16)
        y = jnp.dot(h, wdb[...], preferred_element_type=F32)
        for c in range(n_slab):
            ys_ref[_slab(c, MOE_TM), :] = y[:, c * LANES:(c + 1) * LANES]

    @pl.when(j >= nu_ref[0])
    def _():
        ys_ref[...] = jnp.zeros(ys_ref.shape, F32)


def _expert_ffn(xs, tile_expert, n_used, w_gate, w_up, w_down, layer):
    n_sorted = xs.shape[0] // SUB
    nt = n_sorted // MOE_TM

    def row_map(j, te, nu):
        return (jnp.minimum(j, nu[0] - 1), 0)

    def w_map(j, te, nu):
        return (layer, te[j], 0, 0)

    return pl.pallas_call(
        _expert_ffn_kernel,
        grid_spec=pltpu.PrefetchScalarGridSpec(
            num_scalar_prefetch=2,
            grid=(nt,),
            in_specs=[pl.BlockSpec((MOE_TM * SUB, LANES), row_map),
                      pl.BlockSpec((1, 1, D_MODEL, D_EXPERT), w_map),
                      pl.BlockSpec((1, 1, D_MODEL, D_EXPERT), w_map),
                      pl.BlockSpec((1, 1, D_EXPERT, D_MODEL), w_map)],
            out_specs=pl.BlockSpec((MOE_TM * SUB, LANES), lambda j, te, nu: (j, 0)),
            scratch_shapes=[pltpu.VMEM((D_MODEL, D_EXPERT), BF16),
                            pltpu.VMEM((D_MODEL, D_EXPERT), BF16),
                            pltpu.VMEM((D_EXPERT, D_MODEL), BF16)]),
        out_shape=jax.ShapeDtypeStruct((n_sorted * SUB, LANES), F32),
        compiler_params=_cparams("arbitrary"),
        name="expert_ffn",
    )(tile_expert, n_used, xs, w_gate, w_up, w_down)


def _combine_kernel(posc_ref, posn_ref, x1_ref, info_ref, g_ref, ys_ref, *rest,
                    n_full, n_tiles, ranges, valid_lo, valid_mid, final_norm):
    out_refs, (ybuf, sem) = rest[:-2], rest[-2:]
    i = pl.program_id(0)
    slot = lax.rem(i, 2)

    def buf_base(s, k):
        return (s * 2 + k) * (TT * SUB)

    def gather(pos_ref, dst_slot):
        def row_copy(r, k):
            dst = pl.ds(pl.multiple_of(buf_base(dst_slot, k) + r * SUB, SUB), SUB)
            return pltpu.make_async_copy(ys_ref.at[_tile_of(pos_ref[0, 0, k * TT + r])],
                                         ybuf.at[dst], sem.at[dst_slot])
        return lambda lo, hi: _issue_rows(lo, hi, row_copy)

    @pl.when(i == 0)
    def _():
        ybuf[...] = jnp.zeros(ybuf.shape, F32)
        _for_tile_rows(i, n_full, ranges, gather(posc_ref, 0))

    @pl.when(i + 1 < n_tiles)
    def _():
        _for_tile_rows(i + 1, n_full, ranges, gather(posn_ref, 1 - slot))

    def drain(lo, hi):
        nrow = (hi - lo) * SUB
        for k in range(2):
            dst = pl.ds(pl.multiple_of(buf_base(slot, k) + lo * SUB, SUB), nrow)
            pltpu.make_async_copy(ys_ref.at[pl.ds(0, nrow)], ybuf.at[dst], sem.at[slot]).wait()

    _for_tile_rows(i, n_full, ranges, drain)

    info = info_ref[...]
    w1 = info[:, 2:3]
    w2 = info[:, 3:4]
    rowg = i * TT + lax.broadcasted_iota(I32, (TT, 1), 0)
    valid = (rowg < valid_lo) | (rowg >= valid_mid)
    outs = []
    for c in range(D_MODEL // LANES):
        y1 = ybuf[_slab(c, TT, buf_base(slot, 0)), :]
        y2 = ybuf[_slab(c, TT, buf_base(slot, 1)), :]
        outs.append(x1_ref[:, c * LANES:(c + 1) * LANES] + jnp.where(valid, w1 * y1 + w2 * y2, 0.0))
    out = jnp.concatenate(outs, axis=1)
    if not final_norm:
        out_refs[0][...] = out
    else:
        out = _rms(out, g_ref[...])

        @pl.when(i < n_full)
        def _():
            out_refs[0][...] = out

        @pl.when(i >= n_full)
        def _():
            out_refs[1][...] = out


def _combine(x1, info, pos, ys, g_final, t_rows, ranges, valid_lo, valid_mid, final_norm):
    n = x1.shape[0]
    nt = n // TT
    n_full = t_rows // TT
    if final_norm:
        out_specs = [pl.BlockSpec((TT, D_MODEL), lambda i: (jnp.minimum(i, n_full - 1), 0)),
                     pl.BlockSpec((TT, D_MODEL), lambda i: (jnp.maximum(i - n_full, 0), 0))]
        out_shape = [jax.ShapeDtypeStruct((t_rows, D_MODEL), F32),
                     jax.ShapeDtypeStruct((n - t_rows, D_MODEL), F32)]
    else:
        out_specs = [pl.BlockSpec((TT, D_MODEL), lambda i: (i, 0))]
        out_shape = [jax.ShapeDtypeStruct((n, D_MODEL), F32)]
    return pl.pallas_call(
        functools.partial(_combine_kernel, n_full=n_full, n_tiles=nt, ranges=ranges,
                          valid_lo=valid_lo, valid_mid=valid_mid, final_norm=final_norm),
        grid=(nt,),
        in_specs=[pl.BlockSpec((1, 1, 2 * TT), lambda i: (i, 0, 0), memory_space=pltpu.SMEM),
                  pl.BlockSpec((1, 1, 2 * TT), lambda i: (jnp.minimum(i + 1, nt - 1), 0, 0),
                               memory_space=pltpu.SMEM),
                  pl.BlockSpec((TT, D_MODEL), lambda i: (i, 0)),
                  pl.BlockSpec((TT, LANES), lambda i: (i, 0)),
                  pl.BlockSpec((1, D_MODEL), lambda i: (0, 0)),
                  pl.BlockSpec(memory_space=pl.ANY)],
        out_specs=out_specs,
        out_shape=out_shape,
        scratch_shapes=[pltpu.VMEM((2 * 2 * TT * SUB, LANES), F32),
                        pltpu.SemaphoreType.DMA((2,))],
        compiler_params=_cparams("arbitrary"),
        name="moe_combine",
    )(pos, pos, x1, info, g_final.reshape(1, D_MODEL), ys)


def _moe(x1, xn, info, infot, cnt, w_gate, w_up, w_down, layer, g_final, t_rows, ns, final_norm):
    n = x1.shape[0]
    valid_lo, valid_mid = t_rows + ns, n - N_META
    ranges = _small_tile_ranges(t_rows, ns, n)
    n_valid = valid_lo + N_META
    nt = (2 * n_valid + N_EXPERTS * (MOE_TM - 1)) // MOE_TM
    counts = cnt[0, :N_EXPERTS].astype(I32)
    tiles = (counts + MOE_TM - 1) // MOE_TM
    tile_end = jnp.cumsum(tiles)
    row_start = (tile_end - tiles) * MOE_TM
    n_used = tile_end[-1:]
    experts = jnp.arange(N_EXPERTS, dtype=I32)

    def slot_rows(e, rank):
        start = jnp.sum(jnp.where(e[..., None] == experts, row_start, 0), axis=-1)
        return start + rank

    pos = jnp.concatenate([slot_rows(infot[:, 0].astype(I32), infot[:, 4].astype(I32)),
                           slot_rows(infot[:, 1].astype(I32), infot[:, 5].astype(I32))], axis=-1)
    rows = jnp.arange(n).reshape(n // TT, TT)
    valid = (rows < valid_lo) | (rows >= valid_mid)
    pos = jnp.where(jnp.concatenate([valid, valid], axis=-1), pos, 0)[:, None, :]
    tile_ids = jnp.arange(nt, dtype=I32)
    tile_expert = jnp.sum(tile_end[None, :] <= jnp.minimum(tile_ids, n_used[0] - 1)[:, None],
                          axis=1).astype(I32)
    is_last = jnp.any((tile_ids[:, None] == tile_end[None, :] - 1) & (tiles[None, :] > 0), axis=1)
    zero_tiles = (is_last | (tile_ids >= n_used[0])).astype(I32)
    xs = _dispatch(xn, pos, zero_tiles, nt * MOE_TM, t_rows, ranges)
    ys = _expert_ffn(xs, tile_expert, n_used.astype(I32), w_gate, w_up, w_down, layer)
    return _combine(x1, info, pos, ys, g_final, t_rows, ranges, valid_lo, valid_mid, final_norm)


def _shift_rows(x, k, fill):
    if k % SUB == 0:
        return jnp.concatenate([jnp.full((k, x.shape[1]), fill, x.dtype), x[:x.shape[0] - k]], axis=0)
    row = lax.broadcasted_iota(I32, x.shape, 0)
    return jnp.where(row >= k, pltpu.roll(x, k, 0), fill)


def _lru_gates(xc, wgate_ref, bgate_ref, lam_sp):
    gates = _dot(xc, wgate_ref[...]) + bgate_ref[...]
    r = _sigmoid(gates[:, :D_LRU])
    ig = _sigmoid(gates[:, D_LRU:])
    log_a = -LRU_C * r * lam_sp
    a = jnp.exp(log_a)
    bx = jnp.sqrt(-jnp.tanh(log_a) * (a * a + 1.0)) * (ig * xc)
    return a, bx


def _conv_block(seq_ref, u_ref, w_ref, b_ref, j):
    @pl.when(j == 0)
    def _():
        seq_ref[0:8, :] = jnp.zeros((8, seq_ref.shape[1]), F32)

    u = u_ref[...]
    seq_ref[8:8 + BLK, :] = u
    seq = seq_ref[...]
    out = b_ref[...] + u * w_ref[CONV_WIDTH - 1:CONV_WIDTH, :]
    for back in range(1, CONV_WIDTH):
        tap = CONV_WIDTH - 1 - back
        out = out + pltpu.roll(seq, back, 0)[8:, :] * w_ref[tap:tap + 1, :]
    seq_ref[0:8, :] = u[BLK - 8:, :]
    return out


def _mix1_prompt_kernel(*refs, nblk):
    j = pl.program_id(0)
    o_ref = refs[16]

    @pl.when(j <= nblk)
    def _():
        _mix1_prompt_step(*refs)

    @pl.when(j > nblk)
    def _():
        o_ref[...] = jnp.zeros(o_ref.shape, o_ref.dtype)


def _mix1_prompt_step(cx_ref, cg_ref, z_ref, xbc_ref, dt_ref,
                      cwc_ref, cbc_ref, wgate_ref, bgate_ref, lam_ref,
                      cwd_ref, cbd_ref, dtb_ref, alog_ref, dskip_ref, nw_ref,
                      o_ref, lruh_ref, ssdh_ref,
                      seqc_ref, seqd_ref, hl_ref, hs_ref):
    j = pl.program_id(0)
    rowc = lax.broadcasted_iota(I32, (BLK, 1), 0)
    live = rowc >= jnp.where(j >= 1, 0, META_PAD)

    @pl.when(j == 0)
    def _():
        hl_ref[...] = jnp.zeros(hl_ref.shape, F32)
        hs_ref[...] = jnp.zeros(hs_ref.shape, F32)

    xc = _conv_block(seqc_ref, cx_ref, cwc_ref, cbc_ref, j)
    a, bx = _lru_gates(xc, wgate_ref, bgate_ref, _softplus(-lam_ref[...]))
    a = jnp.where(live, a, 1.0)
    bx = jnp.where(live, bx, 0.0)
    k = 1
    while k < BLK:
        bx = a * _shift_rows(bx, k, 0.0) + bx
        a = a * _shift_rows(a, k, 1.0)
        k *= 2
    h = a * hl_ref[...] + bx
    hl_ref[...] = h[BLK - 1:BLK, :]
    lruh_ref[...] = h[BLK - 1:BLK, :]
    o_ref[:, :D_LRU] = (h * _gelu_tanh(cg_ref[...])).astype(o_ref.dtype)

    xbc = _silu(_conv_block(seqd_ref, xbc_ref, cwd_ref, cbd_ref, j))
    xs = xbc[:, :D_SSD]
    dt = jnp.where(live, _softplus(dt_ref[...] + dtb_ref[...]), 0.0)
    a_neg = -jnp.exp(alog_ref[...])
    acum = dt * a_neg
    k = 1
    while k < BLK:
        acum = acum + _shift_rows(acum, k, 0.0)
        k *= 2
    acum_t = acum.T
    dt_t = dt.T
    xs_t = xs.T
    tri = (lax.broadcasted_iota(I32, (BLK, BLK), 0) >= lax.broadcasted_iota(I32, (BLK, BLK), 1))
    rep = SSD_HEADS // SSD_GROUPS
    ys = []
    for g in range(SSD_GROUPS):
        bg = xbc[:, D_SSD + g * SSD_STATE:D_SSD + (g + 1) * SSD_STATE]
        cgm = xbc[:, D_SSD + (SSD_GROUPS + g) * SSD_STATE:D_SSD + (SSD_GROUPS + g + 1) * SSD_STATE]
        cb = _dot_nt(cgm, bg)
        for hh in range(rep):
            hd = g * rep + hh
            col = acum[:, hd:hd + 1]
            rowv = acum_t[hd:hd + 1, :]
            decay = jnp.exp(jnp.where(tri, col - rowv, NEG_INF))
            scores = cb * decay * dt_t[hd:hd + 1, :]
            xh = xs[:, hd * SSD_HEAD_DIM:(hd + 1) * SSD_HEAD_DIM]
            y = _dot(scores, xh)
            hprev = hs_ref[hd]
            y = y + jnp.exp(col) * _dot_nt(cgm, hprev)
            last = acum_t[hd:hd + 1, BLK - 1:BLK]
            w_end = jnp.exp(last - rowv) * dt_t[hd:hd + 1, :]
            xw_t = xs_t[hd * SSD_HEAD_DIM:(hd + 1) * SSD_HEAD_DIM, :] * w_end
            hnew = jnp.exp(last) * hprev + _dot(xw_t, bg)
            hs_ref[hd] = hnew
            ssdh_ref[hd] = hnew
            ys.append(y + dskip_ref[:, hd * SSD_HEAD_DIM:(hd + 1) * SSD_HEAD_DIM] * xh)
    y = jnp.concatenate(ys, axis=1) * _silu(z_ref[...])
    o_ref[:, D_LRU:] = _rms(y, nw_ref[...]).astype(o_ref.dtype)


def _mix1_prompt(cx, cg, z, xbc, dt, wts, t_rows, meta_blk):
    n = cx.shape[0]
    nblk = t_rows // BLK

    def cur(j):
        return (jnp.where(j == 0, meta_blk, j - 1), 0)

    def const2(j):
        return (0, 0)

    in_specs = [pl.BlockSpec((BLK, D_LRU), cur), pl.BlockSpec((BLK, D_LRU), cur),
                pl.BlockSpec((BLK, D_SSD), cur), pl.BlockSpec((BLK, D_XBC), cur),
                pl.BlockSpec((BLK, LANES), cur)]
    in_specs += [pl.BlockSpec(w.shape, const2) for w in wts]
    return pl.pallas_call(
        functools.partial(_mix1_prompt_kernel, nblk=nblk),
        grid=(meta_blk + 1,),
        in_specs=in_specs,
        out_specs=[pl.BlockSpec((BLK, D_MIX_ODD), cur),
                   pl.BlockSpec((1, D_LRU), const2),
                   pl.BlockSpec((SSD_HEADS, SSD_HEAD_DIM, SSD_STATE), lambda j: (0, 0, 0))],
        out_shape=[jax.ShapeDtypeStruct((n, D_MIX_ODD), BF16),
                   jax.ShapeDtypeStruct((1, D_LRU), F32),
                   jax.ShapeDtypeStruct((SSD_HEADS, SSD_HEAD_DIM, SSD_STATE), F32)],
        scratch_shapes=[pltpu.VMEM((8 + BLK, D_LRU), F32),
                        pltpu.VMEM((8 + BLK, D_XBC), F32),
                        pltpu.VMEM((1, D_LRU), F32),
                        pltpu.VMEM((SSD_HEADS, SSD_HEAD_DIM, SSD_STATE), F32)],
        compiler_params=_cparams("arbitrary"),
        name="mix1_prompt",
    )(cx, cg, z, xbc, dt, *wts)


def _conv_step(seq, t, w_ref, b_ref):
    out = b_ref[...]
    for tap in range(CONV_WIDTH):
        out = out + seq[t + tap] * w_ref[tap:tap + 1, :]
    return out


def _lru_sample_kernel(cx_ref, cg_ref, buf_ref, h0_ref, cwc_ref, cbc_ref, wgate_ref, bgate_ref,
                       lam_ref, o_ref, hout_ref, *, n_new):
    seq = [buf_ref[i] for i in range(CONV_WIDTH - 1)] + [cx_ref[i] for i in range(n_new)]
    lam_sp = _softplus(-lam_ref[...])
    h = h0_ref[...]
    for t in range(n_new):
        xc = _conv_step(seq, t, cwc_ref, cbc_ref)
        a, bx = _lru_gates(xc, wgate_ref, bgate_ref, lam_sp)
        h = a * h + bx
        o_ref[t] = h * _gelu_tanh(cg_ref[t])
    hout_ref[...] = h


def _lru_sample(cx_tm, cg_tm, buf_tm, h0, wts):
    n_new = cx_tm.shape[0]
    return pl.pallas_call(
        functools.partial(_lru_sample_kernel, n_new=n_new),
        out_shape=[jax.ShapeDtypeStruct(cx_tm.shape, F32), jax.ShapeDtypeStruct(h0.shape, F32)],
        compiler_params=pltpu.CompilerParams(vmem_limit_bytes=VMEM_LIMIT),
        name="lru_sample",
    )(cx_tm, cg_tm, buf_tm, h0, *wts)


def _split3(x):
    hi = x.astype(BF16)
    r = x - hi.astype(F32)
    mid = r.astype(BF16)
    lo = (r - mid.astype(F32)).astype(BF16)
    return hi, mid, lo


def _ssd_sample_kernel(xbc_ref, dt_ref, z_ref, buf_ref, h0_ref, cwd_ref, cbd_ref, dtb_ref, alog_ref,
                       dskip_ref, nw_ref, o_ref, hout_ref, xc_ref, dec_ref, dtx_ref, y_ref, *, n_new, bs):
    seq = [buf_ref[i] for i in range(CONV_WIDTH - 1)] + [xbc_ref[i] for i in range(n_new)]
    a_neg = -jnp.exp(alog_ref[...])
    hcol = lax.broadcasted_iota(I32, (LANES, D_SSD), 1) // SSD_HEAD_DIM
    expand = jnp.where(lax.broadcasted_iota(I32, (LANES, D_SSD), 0) == hcol, 1.0, 0.0).astype(BF16)

    def widen(v):
        parts = _split3(v)
        return sum(jnp.dot(p, expand, preferred_element_type=F32) for p in parts)

    for t in range(n_new):
        xc = _silu(_conv_step(seq, t, cwd_ref, cbd_ref))
        xc_ref[t] = xc
        dt = _softplus(dt_ref[t] + dtb_ref[...])
        dtw = widen(dt)
        dec_ref[t] = jnp.exp(widen(dt * a_neg))
        dtx_ref[t] = dtw * xc[:, :D_SSD]

    zero_rows = jnp.zeros((LANES - 2 * n_new * SSD_GROUPS, SSD_STATE), F32)
    half = D_SSD // SSD_GROUPS

    def per_seq(b, c):
        rows = [xc_ref[t, pl.ds(b, 1), :] for t in range(n_new)]
        bc = [r[:, D_SSD + k * SSD_STATE:D_SSD + (k + 1) * SSD_STATE]
              for r in rows for k in range(2 * SSD_GROUPS)]
        bc_t = jnp.concatenate(bc + [zero_rows], axis=0).T
        ht = h0_ref[b].T
        for t in range(n_new):
            bcol = [jnp.broadcast_to(bc_t[:, t * 4 + g:t * 4 + g + 1], (SSD_STATE, half))
                    for g in range(SSD_GROUPS)]
            ccol = [jnp.broadcast_to(bc_t[:, t * 4 + SSD_GROUPS + g:t * 4 + SSD_GROUPS + g + 1],
                                     (SSD_STATE, half)) for g in range(SSD_GROUPS)]
            bw = jnp.concatenate(bcol, axis=1)
            cw = jnp.concatenate(ccol, axis=1)
            ht = ht * dec_ref[t, pl.ds(b, 1), :] + bw * dtx_ref[t, pl.ds(b, 1), :]
            y = jnp.sum(cw * ht, axis=0, keepdims=True)
            y_ref[t, pl.ds(b, 1), :] = y + dskip_ref[...] * rows[t][:, :D_SSD]
        hout_ref[b] = ht.T
        return c

    lax.fori_loop(0, bs, per_seq, 0)
    for t in range(n_new):
        o_ref[t] = _rms(y_ref[t] * _silu(z_ref[t]), nw_ref[...])


def _ssd_sample(xbc_tm, dt_tm, z_tm, buf_tm, h0, wts):
    n_new, b, _ = xbc_tm.shape
    bs = 16 if b % 16 == 0 else b
    hp = SSD_HEADS * SSD_HEAD_DIM

    def tm(c):
        return pl.BlockSpec((n_new, bs, c), lambda i: (0, i, 0))

    in_specs = [tm(D_XBC), tm(LANES), tm(D_SSD),
                pl.BlockSpec((CONV_WIDTH - 1, bs, D_XBC), lambda i: (0, i, 0)),
                pl.BlockSpec((bs, hp, SSD_STATE), lambda i: (i, 0, 0))]
    in_specs += [pl.BlockSpec(w.shape, lambda i: (0, 0)) for w in wts]
    return pl.pallas_call(
        functools.partial(_ssd_sample_kernel, n_new=n_new, bs=bs),
        grid=(b // bs,),
        in_specs=in_specs,
        out_specs=[tm(D_SSD), pl.BlockSpec((bs, hp, SSD_STATE), lambda i: (i, 0, 0))],
        out_shape=[jax.ShapeDtypeStruct((n_new, b, D_SSD), F32),
                   jax.ShapeDtypeStruct((b, hp, SSD_STATE), F32)],
        scratch_shapes=[pltpu.VMEM((n_new, bs, D_XBC), F32),
                        pltpu.VMEM((n_new, bs, D_SSD), F32),
                        pltpu.VMEM((n_new, bs, D_SSD), F32),
                        pltpu.VMEM((n_new, bs, D_SSD), F32)],
        compiler_params=_cparams("parallel"),
        name="ssd_sample",
    )(xbc_tm, dt_tm, z_tm, buf_tm, h0, *wts)


def _block_diag(w):
    g, c, _ = w.shape
    eye = jnp.eye(g, dtype=w.dtype)
    return (eye[:, None, :, None] * w[:, :, None, :]).reshape(g * c, g * c)


def _router_weights(w_rg, b_rg, w_re, b_re):
    w = jnp.concatenate([w_rg, w_re], axis=1)
    w = jnp.pad(w, ((0, 0), (0, LANES - w.shape[1])))
    hi = w.astype(BF16)
    lo = (w - hi.astype(F32)).astype(BF16)
    b = jnp.pad(jnp.concatenate([b_rg, b_re]), (0, LANES - MOE_GROUPS - N_EXPERTS)).reshape(1, LANES)
    return hi, lo, b


def kernel(x_prompt, x_sample, cache_swa_k, cache_swa_v, cache_pool, state_lru_conv, state_lru_h, state_ssd_conv, state_ssd_h, meta_tokens, norm_mix, norm_ffn, norm_final, attn_w_in, attn_w_out, attn_sinks, pool_w, pool_scale, rec_w_in, rec_w_out, lru_conv_w, lru_conv_b, lru_w_rg, lru_b_rg, lru_w_ig, lru_b_ig, lru_lambda, ssd_conv_w, ssd_conv_b, ssd_dt_bias, ssd_a_log, ssd_d, ssd_norm_w, moe_w_rg, moe_b_rg, moe_w_re, moe_b_re, moe_w_gate, moe_w_up, moe_w_down):
    t_rows = x_prompt.shape[1]
    b, s_new = x_sample.shape[:2]
    ns = b * s_new
    small = -(-(ns + BLK) // TT) * TT
    n = t_rows + small
    meta_blk = n // BLK - 1
    valid_lo = t_rows + ns
    valid_mid = n - N_META
    assert x_prompt.shape[0] == 1 and t_rows % TT == 0 and ns % BLK == 0 and ns <= TT

    x_small = jnp.concatenate([
        x_sample.transpose(1, 0, 2).reshape(ns, D_MODEL),
        jnp.zeros((small - ns - N_META, D_MODEL), F32),
        meta_tokens.astype(F32)], axis=0)
    xs0 = (x_prompt[0], x_small)

    def sample_tm(a):
        return a[t_rows:t_rows + ns].reshape(s_new, b, a.shape[1])

    q, kv, p = _norm_proj(xs0, norm_mix[0], attn_w_in[0].astype(BF16), (D_ATTN, 2 * D_KV, D_POOL),
                          (BF16, F32, F32))
    wpool_bd = _block_diag(pool_w[0]).astype(BF16)
    mix = _mix0_prompt(q, kv, p, attn_sinks[0], wpool_bd, pool_scale[0], t_rows, meta_blk)
    nq = s_new * GQA
    qs = sample_tm(q).reshape(s_new, b, N_KV_HEADS, GQA, HEAD_DIM).transpose(1, 2, 0, 3, 4)
    qs = qs.reshape(b, N_KV_HEADS, nq, HEAD_DIM)
    kv_s = sample_tm(kv).transpose(1, 0, 2)
    kn, vn = kv_s[:, :, :D_KV], kv_s[:, :, D_KV:]
    ck = cache_swa_k[0].reshape(b, N_META + WINDOW, D_KV)
    cv = cache_swa_v[0].reshape(b, N_META + WINDOW, D_KV)
    o_s = _attn_sample(qs, ck, cv, kn, vn, attn_sinks[0])
    o_s = o_s.reshape(b, N_KV_HEADS, s_new, GQA, HEAD_DIM).transpose(2, 0, 1, 3, 4).reshape(s_new, b, D_ATTN)
    p_s = sample_tm(p)
    pool_buf_tm = cache_pool[0].transpose(1, 0, 2)
    yp_s = _pool_sample(pool_buf_tm, p_s, wpool_bd, pool_scale[0])
    mix_s = jnp.concatenate([o_s, yp_s], axis=-1).reshape(ns, D_MIX_EVEN)
    wr_hi, wr_lo, br = _router_weights(moe_w_rg[0], moe_b_rg[0], moe_w_re[0], moe_b_re[0])
    x1, xn, info, infot, cnt = _outproj_route(xs0, mix, mix_s, attn_w_out[0].astype(BF16), norm_ffn[0],
                                              wr_hi, wr_lo, br, t_rows, valid_lo, valid_mid)
    x, = _moe(x1, xn, info, infot, cnt, moe_w_gate, moe_w_up, moe_w_down, 0, norm_final,
              t_rows, ns, False)

    meta_rows = slice(n - N_META, n)
    last_rows = slice(t_rows - WINDOW, t_rows)
    swa_k_p = jnp.concatenate([kv[meta_rows, :D_KV], kv[last_rows, :D_KV]], axis=0)
    swa_v_p = jnp.concatenate([kv[meta_rows, D_KV:], kv[last_rows, D_KV:]], axis=0)
    swa_k_p = swa_k_p.reshape(1, 1, N_META + WINDOW, N_KV_HEADS, HEAD_DIM)
    swa_v_p = swa_v_p.reshape(1, 1, N_META + WINDOW, N_KV_HEADS, HEAD_DIM)
    pool_p = p[t_rows - POOL_BUF:t_rows].reshape(1, 1, POOL_BUF, D_POOL)
    swa_k_s = jnp.concatenate([ck[:, :N_META], ck[:, N_META + s_new:], kn], axis=1)
    swa_v_s = jnp.concatenate([cv[:, :N_META], cv[:, N_META + s_new:], vn], axis=1)
    swa_k_s = swa_k_s.reshape(1, b, N_META + WINDOW, N_KV_HEADS, HEAD_DIM)
    swa_v_s = swa_v_s.reshape(1, b, N_META + WINDOW, N_KV_HEADS, HEAD_DIM)
    pool_s = jnp.concatenate([cache_pool[0], p_s.transpose(1, 0, 2)], axis=1)[:, -POOL_BUF:][None]

    d_in_odd = rec_w_in.shape[2]
    w_in1 = jnp.pad(rec_w_in[0], ((0, 0), (0, LANES - SSD_HEADS))).astype(BF16)
    assert d_in_odd == 2 * D_LRU + D_SSD + D_XBC + SSD_HEADS
    cx, cg, z, xbc, dt = _norm_proj((x,), norm_mix[1], w_in1, (D_LRU, D_LRU, D_SSD, D_XBC, LANES))
    wgate = jnp.concatenate([_block_diag(lru_w_rg[0]), _block_diag(lru_w_ig[0])], axis=1).astype(BF16)
    bgate = jnp.concatenate([lru_b_rg[0], lru_b_ig[0]]).reshape(1, 2 * D_LRU)
    pad8 = (0, LANES - SSD_HEADS)
    dtb = jnp.pad(ssd_dt_bias[0], pad8).reshape(1, LANES)
    alog = jnp.pad(ssd_a_log[0], pad8).reshape(1, LANES)
    dskip = jnp.repeat(ssd_d[0], SSD_HEAD_DIM).reshape(1, D_SSD)
    lru_w = (lru_conv_w[0], lru_conv_b[0].reshape(1, D_LRU), wgate, bgate, lru_lambda[0].reshape(1, D_LRU))
    ssd_w = (ssd_conv_w[0], ssd_conv_b[0].reshape(1, D_XBC), dtb, alog, dskip, ssd_norm_w[0].reshape(1, D_SSD))
    mix, lru_h_p, ssd_h_p = _mix1_prompt(cx, cg, z, xbc, dt, lru_w + ssd_w, t_rows, meta_blk)
    cx_s, cg_s, z_s, xbc_s, dt_s = (sample_tm(a) for a in (cx, cg, z, xbc, dt))
    yc_s, lru_h_s = _lru_sample(cx_s, cg_s, state_lru_conv[0].transpose(1, 0, 2), state_lru_h[0], lru_w)
    h0 = state_ssd_h[0].reshape(b, SSD_HEADS * SSD_HEAD_DIM, SSD_STATE)
    yd_s, ssd_h_s = _ssd_sample(xbc_s, dt_s, z_s, state_ssd_conv[0].transpose(1, 0, 2), h0, ssd_w)
    mix_s = jnp.concatenate([yc_s, yd_s], axis=-1).reshape(ns, D_MIX_ODD)
    wr_hi, wr_lo, br = _router_weights(moe_w_rg[1], moe_b_rg[1], moe_w_re[1], moe_b_re[1])
    x1, xn, info, infot, cnt = _outproj_route((x,), mix, mix_s, rec_w_out[0].astype(BF16), norm_ffn[1],
                                              wr_hi, wr_lo, br, t_rows, valid_lo, valid_mid)
    y_p, y_small = _moe(x1, xn, info, infot, cnt, moe_w_gate, moe_w_up, moe_w_down, 1, norm_final,
                        t_rows, ns, True)

    tail = CONV_WIDTH - 1
    lru_conv_p = cx[t_rows - tail:t_rows].reshape(1, 1, tail, D_LRU)
    ssd_conv_p = xbc[t_rows - tail:t_rows].reshape(1, 1, tail, D_XBC)
    lru_conv_s = jnp.concatenate([state_lru_conv[0], cx_s.transpose(1, 0, 2)], axis=1)[:, -tail:][None]
    ssd_conv_s = jnp.concatenate([state_ssd_conv[0], xbc_s.transpose(1, 0, 2)], axis=1)[:, -tail:][None]

    y_prompt = y_p[None]
    y_sample = y_small[:ns].reshape(s_new, b, D_MODEL).transpose(1, 0, 2)
    return (y_prompt, y_sample, swa_k_p, swa_v_p, pool_p, lru_conv_p,
            lru_h_p.reshape(1, 1, D_LRU), ssd_conv_p,
            ssd_h_p.reshape(1, 1, SSD_HEADS, SSD_HEAD_DIM, SSD_STATE),
            swa_k_s, swa_v_s, pool_s, lru_conv_s, lru_h_s[None], ssd_conv_s,
            ssd_h_s.reshape(1, b, SSD_HEADS, SSD_HEAD_DIM, SSD_STATE))
```

```python
import functools

import jax
import jax.numpy as jnp
from jax import lax
from jax.experimental import pallas as pl
from jax.experimental.pallas import tpu as pltpu

F32 = jnp.float32
BF16 = jnp.bfloat16
I32 = jnp.int32

D_MODEL = 1024
N_META = 16
EPS = 1e-6
PAST_LEN = 16384
N_Q_HEADS = 16
N_KV_HEADS = 2
GQA = N_Q_HEADS // N_KV_HEADS
HEAD_DIM = 64
WINDOW = 128
D_ATTN = N_Q_HEADS * HEAD_DIM
D_KV = N_KV_HEADS * HEAD_DIM
POOL_WINDOWS = (2, 4, 8, 16)
POOL_GROUP_DIM = 96
D_POOL = len(POOL_WINDOWS) * POOL_GROUP_DIM
POOL_BUF = max(POOL_WINDOWS) - 1
D_MIX_EVEN = D_ATTN + D_POOL
D_LRU = 512
LRU_HEADS = 8
LRU_BLOCK = D_LRU // LRU_HEADS
LRU_C = 8.0
CONV_WIDTH = 4
SSD_HEADS = 8
SSD_HEAD_DIM = 64
D_SSD = SSD_HEADS * SSD_HEAD_DIM
SSD_GROUPS = 2
SSD_STATE = 128
D_XBC = D_SSD + 2 * SSD_GROUPS * SSD_STATE
D_MIX_ODD = D_LRU + D_SSD
MOE_GROUPS = 4
EXPERTS_PER_GROUP = 8
N_EXPERTS = MOE_GROUPS * EXPERTS_PER_GROUP
D_EXPERT = 256

BLK = 128
META_PAD = BLK - N_META
LANES = 128
MOE_TM = 256
NEG_INF = float("-inf")
VMEM_LIMIT = 56 * 1024 * 1024


def _cparams(*sem):
    return pltpu.CompilerParams(dimension_semantics=sem, vmem_limit_bytes=VMEM_LIMIT)


TT = 512
SUB = 8


def _small_tile_ranges(t_rows, ns, n):
    out = {}
    for k in range(t_rows // TT, n // TT):
        lo = k * TT
        ranges = []
        a0, a1 = max(lo, t_rows), min(lo + TT, t_rows + ns)
        if a1 > a0:
            ranges.append((a0 - lo, a1 - lo))
        b0 = max(lo, n - N_META)
        if lo + TT > b0:
            ranges.append((b0 - lo, TT))
        out[k] = tuple(ranges)
    return out


def _for_tile_rows(i, n_full, ranges, fn):
    @pl.when(i < n_full)
    def _():
        fn(0, TT)

    for k, rs in ranges.items():
        if rs:
            @pl.when(i == k)
            def _(rs=rs):
                for lo, hi in rs:
                    fn(lo, hi)


def _slab(c, rows, base=0, sub=SUB):
    return pl.ds(base + c, rows, stride=sub)


def _tile_of(r, sub=SUB):
    return pl.ds(pl.multiple_of(r * sub, sub), sub)


XSUB = 4
U32 = jnp.uint32
HI16 = 0xFFFF0000


def _pack_bf16_pairs(x):
    half = x.shape[1] // 2
    lo = lax.bitcast_convert_type(x[:, :half].astype(BF16).astype(F32), U32)
    hi = lax.bitcast_convert_type(x[:, half:].astype(BF16).astype(F32), U32)
    return (hi & jnp.uint32(HI16)) | (lo >> 16)


def _unpack_bf16_pairs(w):
    lo = lax.bitcast_convert_type(w << 16, F32).astype(BF16)
    hi = lax.bitcast_convert_type(w & jnp.uint32(HI16), F32).astype(BF16)
    return jnp.concatenate([lo, hi], axis=1)


def _sigmoid(x):
    return 1.0 / (1.0 + jnp.exp(-x))


def _silu(x):
    return x * _sigmoid(x)


def _softplus(x):
    return jnp.maximum(x, 0.0) + jnp.log1p(jnp.exp(-jnp.abs(x)))


def _gelu_tanh(x):
    return 0.5 * x * (1.0 + jnp.tanh(0.7978845608028654 * (x + 0.044715 * x * x * x)))


def _dot(a, b):
    return jnp.dot(a.astype(BF16), b.astype(BF16), preferred_element_type=F32)


def _dot_nt(a, b):
    return lax.dot_general(a.astype(BF16), b.astype(BF16), (((1,), (1,)), ((), ())),
                           preferred_element_type=F32)


def _rms(x, g):
    ms = jnp.mean(x * x, axis=-1, keepdims=True)
    return x * lax.rsqrt(ms + EPS) * g


def _row_specs(xs, width):
    if len(xs) == 1:
        return [pl.BlockSpec((TT, width), lambda i: (i, 0))]
    na = xs[0].shape[0] // TT
    return [pl.BlockSpec((TT, width), lambda i: (jnp.minimum(i, na - 1), 0)),
            pl.BlockSpec((TT, width), lambda i: (jnp.maximum(i - na, 0), 0))]


def _pick_rows(refs, n_first):
    if len(refs) == 1:
        return refs[0][...]
    return jnp.where(pl.program_id(0) < n_first, refs[0][...], refs[1][...])


def _norm_proj_kernel(*refs, splits, n_x, n_first):
    x_refs, (g_ref, w_ref), out_refs = refs[:n_x], refs[n_x:n_x + 2], refs[n_x + 2:]
    y = _rms(_pick_rows(x_refs, n_first), g_ref[...]).astype(BF16)
    off = 0
    for o_ref, n in zip(out_refs, splits):
        o_ref[...] = jnp.dot(y, w_ref[:, off:off + n], preferred_element_type=F32).astype(o_ref.dtype)
        off += n


def _norm_proj(xs, g, w_bf16, splits, dtypes=None):
    n = sum(x.shape[0] for x in xs)
    dtypes = dtypes or (F32,) * len(splits)
    return pl.pallas_call(
        functools.partial(_norm_proj_kernel, splits=splits, n_x=len(xs), n_first=xs[0].shape[0] // TT),
        grid=(n // TT,),
        in_specs=_row_specs(xs, D_MODEL) + [pl.BlockSpec((1, D_MODEL), lambda i: (0, 0)),
                                             pl.BlockSpec(w_bf16.shape, lambda i: (0, 0))],
        out_specs=[pl.BlockSpec((TT, s), lambda i: (i, 0)) for s in splits],
        out_shape=[jax.ShapeDtypeStruct((n, s), dt) for s, dt in zip(splits, dtypes)],
        compiler_params=_cparams("parallel"),
        name="norm_proj",
    )(*xs, g.reshape(1, D_MODEL), w_bf16)


def _mix0_prompt_kernel(*refs, nblk):
    j = pl.program_id(0)
    o_ref = refs[8]

    @pl.when(j <= nblk)
    def _():
        _mix0_prompt_step(*refs)

    @pl.when(j > nblk)
    def _():
        o_ref[...] = jnp.zeros(o_ref.shape, o_ref.dtype)


def _mix0_prompt_step(sink_ref, q_ref, kvc_ref, kvp_ref, kvm_ref, p_ref, wp_ref, ps_ref,
                      o_ref, seq_ref, s_ref, pr_ref, rd_ref):
    j = pl.program_id(0)
    kv_all = jnp.concatenate([kvp_ref[...], kvc_ref[...], kvm_ref[...],
                              jnp.zeros((META_PAD, 2 * D_KV), F32)], axis=0)
    k_all = kv_all[:, :D_KV]
    v_all = kv_all[:, D_KV:]
    k_rot = pltpu.roll(k_all, HEAD_DIM, 1)
    v_rot = pltpu.roll(v_all, HEAD_DIM, 1)
    lo_half = lax.broadcasted_iota(I32, k_all.shape, 1) < HEAD_DIM

    def halves(own_lanes, other_lanes, g):
        src_lo, src_hi = (own_lanes, other_lanes) if g == 0 else (other_lanes, own_lanes)
        return (jnp.where(lo_half, src_lo, 0.0).astype(BF16), jnp.where(lo_half, 0.0, src_hi).astype(BF16))

    r = lax.broadcasted_iota(I32, (BLK, BLK), 0)
    c = lax.broadcasted_iota(I32, (BLK, BLK), 1)
    own = r >= c
    dist_loc = jnp.where(own, r - c, r - c + BLK).astype(F32)
    d_min = jnp.where(j >= 2, -BLK, 0)
    c_min = jnp.where(j == 0, META_PAD, 0)
    mask_loc = jnp.where((r - c >= d_min) & (c >= c_min), 0.0, NEG_INF)
    meta_hi = jnp.where(j >= 1, N_META, 0)
    dist_meta = jnp.where(c < meta_hi, (N_META + (j - 1) * BLK + r - c).astype(F32), jnp.inf)
    slabs = GQA // 2
    nkeys = 3 * BLK
    for g in range(N_KV_HEADS):
        k_even, k_odd = halves(k_all, k_rot, g)
        v_even, v_odd = halves(v_all, v_rot, g)
        qg = jnp.concatenate([q_ref[:, (g * slabs + i) * LANES:(g * slabs + i + 1) * LANES]
                              for i in range(slabs)], axis=0) * (HEAD_DIM ** -0.5)
        s_ref[...] = _dot_nt(qg, jnp.concatenate([k_even, k_odd], axis=0))
        for i in range(slabs):
            rows = slice(i * BLK, (i + 1) * BLK)
            for par in range(2):
                h = g * GQA + 2 * i + par
                slope = 2.0 ** (-8.0 * (h + 1) / N_Q_HEADS)
                sink = sink_ref[h]
                k0 = par * nkeys
                s_loc = (jnp.where(own, s_ref[rows, k0 + BLK:k0 + 2 * BLK], s_ref[rows, k0:k0 + BLK])
                         - slope * dist_loc + mask_loc)
                s_met = s_ref[rows, k0 + 2 * BLK:k0 + 3 * BLK] - slope * dist_meta
                m = jnp.maximum(jnp.max(jnp.maximum(s_loc, s_met), axis=-1, keepdims=True), sink)
                p_loc = jnp.exp(s_loc - m)
                p_met = jnp.exp(s_met - m)
                den = jnp.sum(p_loc + p_met, axis=-1, keepdims=True) + jnp.exp(sink - m)
                pr_ref[par, rows, 0:BLK] = jnp.where(own, 0.0, p_loc).astype(BF16)
                pr_ref[par, rows, BLK:2 * BLK] = jnp.where(own, p_loc, 0.0).astype(BF16)
                pr_ref[par, rows, 2 * BLK:] = p_met.astype(BF16)
                rd_ref[par, rows, :] = jnp.broadcast_to(1.0 / den, (BLK, LANES))
        o = (jnp.dot(pr_ref[0], v_even, preferred_element_type=F32) * rd_ref[0]
             + jnp.dot(pr_ref[1], v_odd, preferred_element_type=F32) * rd_ref[1])
        for i in range(slabs):
            o_ref[:, (g * slabs + i) * LANES:(g * slabs + i + 1) * LANES] = (
                o[i * BLK:(i + 1) * BLK, :].astype(o_ref.dtype))

    @pl.when(j == 0)
    def _():
        seq_ref[0:N_META, :] = jnp.zeros((N_META, D_POOL), F32)

    seq_ref[N_META:N_META + BLK, :] = p_ref[...]
    s0 = seq_ref[...]
    s1 = s0 + pltpu.roll(s0, 1, 0)
    s2 = s1 + pltpu.roll(s1, 2, 0)
    s3 = s2 + pltpu.roll(s2, 4, 0)
    s4 = s3 + pltpu.roll(s3, 8, 0)
    rows = N_META + BLK
    col = lax.broadcasted_iota(I32, (rows, D_POOL), 1)
    row = lax.broadcasted_iota(I32, (rows, D_POOL), 0) - N_META
    wsum = jnp.where(col < POOL_GROUP_DIM, s1,
                     jnp.where(col < 2 * POOL_GROUP_DIM, s2,
                               jnp.where(col < 3 * POOL_GROUP_DIM, s3, s4)))
    wlen = jnp.where(col < POOL_GROUP_DIM, 2,
                     jnp.where(col < 2 * POOL_GROUP_DIM, 4,
                               jnp.where(col < 3 * POOL_GROUP_DIM, 8, 16)))
    pos = row + jnp.where(j == 0, -META_PAD, N_META + (j - 1) * BLK)
    cnt = jnp.clip(jnp.minimum(pos + 1, wlen), 1, 16).astype(F32)
    d = (wsum / cnt - s0)[N_META:, :]
    o_ref[:, D_ATTN:] = (_dot(d, wp_ref[...]) * ps_ref[...]).astype(o_ref.dtype)
    seq_ref[0:N_META, :] = p_ref[BLK - N_META:, :]


def _mix0_prompt(q, kv, p, sinks, wpool_bd, pool_scale, t_rows, meta_blk):
    n = q.shape[0]
    nblk = t_rows // BLK

    def cur(j):
        return jnp.where(j == 0, meta_blk, j - 1)

    meta16 = (meta_blk * BLK + META_PAD) // N_META
    return pl.pallas_call(
        functools.partial(_mix0_prompt_kernel, nblk=nblk),
        grid=(meta_blk + 1,),
        in_specs=[pl.BlockSpec(memory_space=pltpu.SMEM),
                  pl.BlockSpec((BLK, D_ATTN), lambda j: (cur(j), 0)),
                  pl.BlockSpec((BLK, 2 * D_KV), lambda j: (cur(j), 0)),
                  pl.BlockSpec((BLK, 2 * D_KV), lambda j: (jnp.maximum(j - 2, 0), 0)),
                  pl.BlockSpec((N_META, 2 * D_KV), lambda j: (meta16, 0)),
                  pl.BlockSpec((BLK, D_POOL), lambda j: (cur(j), 0)),
                  pl.BlockSpec((D_POOL, D_POOL), lambda j: (0, 0)),
                  pl.BlockSpec((1, D_POOL), lambda j: (0, 0))],
        out_specs=pl.BlockSpec((BLK, D_MIX_EVEN), lambda j: (cur(j), 0)),
        out_shape=jax.ShapeDtypeStruct((n, D_MIX_EVEN), BF16),
        scratch_shapes=[pltpu.VMEM((N_META + BLK, D_POOL), F32),
                        pltpu.VMEM((GQA // 2 * BLK, 2 * 3 * BLK), F32),
                        pltpu.VMEM((2, GQA // 2 * BLK, 3 * BLK), BF16),
                        pltpu.VMEM((2, GQA // 2 * BLK, LANES), F32)],
        compiler_params=_cparams("arbitrary"),
        name="mix0_prompt",
    )(sinks, q, kv, kv, kv, p, wpool_bd, pool_scale.reshape(1, D_POOL))


def _attn_sample_kernel(sink_ref, q_ref, ck_ref, cv_ref, kn_ref, vn_ref, o_ref, *, n_new):
    nk = N_META + WINDOW + n_new
    nq = n_new * GQA
    k_all = jnp.concatenate([ck_ref[...], kn_ref[...]], axis=1).astype(BF16)
    v_all = jnp.concatenate([cv_ref[...], vn_ref[...]], axis=1).astype(BF16)
    r = lax.broadcasted_iota(I32, (nq, nk), 0)
    slot = lax.broadcasted_iota(I32, (nq, nk), 1)
    t = r // GQA
    hh = r - t * GQA
    kpos = jnp.where(slot < N_META, slot,
                     jnp.where(slot < N_META + WINDOW, PAST_LEN - WINDOW - N_META + slot,
                               PAST_LEN - N_META - WINDOW + slot))
    dist = PAST_LEN + t - kpos
    allowed = (dist >= 0) & ((slot < N_META) | (dist < WINDOW))
    distf = dist.astype(F32)
    hh1 = hh[:, :1]
    for g in range(N_KV_HEADS):
        slope = jnp.exp2(-8.0 * (g * GQA + hh + 1).astype(F32) / N_Q_HEADS)
        sink = jnp.zeros((nq, 1), F32)
        for i in range(GQA):
            sink = jnp.where(hh1 == i, sink_ref[g * GQA + i], sink)
        kg = k_all[:, :, g * HEAD_DIM:(g + 1) * HEAD_DIM]
        vg = v_all[:, :, g * HEAD_DIM:(g + 1) * HEAD_DIM]
        s = jnp.einsum("bqd,bkd->bqk", q_ref[:, g].astype(BF16), kg,
                       preferred_element_type=F32) * (HEAD_DIM ** -0.5)
        s = jnp.where(allowed[None], s - (slope * distf)[None], NEG_INF)
        sink = sink[None]
        m = jnp.maximum(jnp.max(s, axis=-1, keepdims=True), sink)
        pr = jnp.exp(s - m)
        den = jnp.sum(pr, axis=-1, keepdims=True) + jnp.exp(sink - m)
        o = jnp.einsum("bqk,bkd->bqd", pr.astype(BF16), vg, preferred_element_type=F32)
        o_ref[:, g] = o / den


def _attn_sample(qs, ck, cv, kn, vn, sinks):
    b, _, nq, _ = qs.shape
    n_new = kn.shape[1]
    bs = 16 if b % 16 == 0 else b
    nc = N_META + WINDOW
    return pl.pallas_call(
        functools.partial(_attn_sample_kernel, n_new=n_new),
        grid=(b // bs,),
        in_specs=[pl.BlockSpec(memory_space=pltpu.SMEM),
                  pl.BlockSpec((bs, N_KV_HEADS, nq, HEAD_DIM), lambda i: (i, 0, 0, 0)),
                  pl.BlockSpec((bs, nc, D_KV), lambda i: (i, 0, 0)),
                  pl.BlockSpec((bs, nc, D_KV), lambda i: (i, 0, 0)),
                  pl.BlockSpec((bs, n_new, D_KV), lambda i: (i, 0, 0)),
                  pl.BlockSpec((bs, n_new, D_KV), lambda i: (i, 0, 0))],
        out_specs=pl.BlockSpec((bs, N_KV_HEADS, nq, HEAD_DIM), lambda i: (i, 0, 0, 0)),
        out_shape=jax.ShapeDtypeStruct(qs.shape, F32),
        compiler_params=_cparams("parallel"),
        name="attn_sample",
    )(sinks, qs, ck, cv, kn, vn)


def _pool_sample_kernel(buf_ref, p_ref, wp_ref, ps_ref, o_ref, *, n_new):
    seq = [buf_ref[i] for i in range(POOL_BUF)] + [p_ref[i] for i in range(n_new)]
    b = p_ref.shape[1]
    col = lax.broadcasted_iota(I32, (b, D_POOL), 1)
    for t in range(n_new):
        cur = seq[POOL_BUF + t]
        acc = cur
        sums = {}
        for back in range(1, max(POOL_WINDOWS)):
            acc = acc + seq[POOL_BUF + t - back]
            if back + 1 in POOL_WINDOWS:
                sums[back + 1] = acc
        mean = jnp.where(col < POOL_GROUP_DIM, sums[2] / 2.0,
                         jnp.where(col < 2 * POOL_GROUP_DIM, sums[4] / 4.0,
                                   jnp.where(col < 3 * POOL_GROUP_DIM, sums[8] / 8.0,
                                             sums[16] / 16.0)))
        o_ref[t] = _dot(mean - cur, wp_ref[...]) * ps_ref[...]


def _pool_sample(buf_tm, p_tm, wpool_bd, pool_scale):
    n_new = p_tm.shape[0]
    return pl.pallas_call(
        functools.partial(_pool_sample_kernel, n_new=n_new),
        out_shape=jax.ShapeDtypeStruct(p_tm.shape, F32),
        compiler_params=pltpu.CompilerParams(vmem_limit_bytes=VMEM_LIMIT),
        name="pool_sample",
    )(buf_tm, p_tm, wpool_bd, pool_scale.reshape(1, D_POOL))


def _outproj_route_kernel(*refs, valid_lo, valid_mid, n_x, n_first, ns):
    tt = TT
    x_refs = refs[:n_x]
    (mix_ref, mixs_ref, w_ref, g_ref, wrh_ref, wrl_ref, br_ref,
     x1_ref, xn_ref, info_ref, infot_ref, cnt_ref, acc_ref) = refs[n_x:]
    i = pl.program_id(0)
    x = _pick_rows(x_refs, n_first)

    @pl.when(i != n_first)
    def _():
        x1_ref[...] = x + _dot(mix_ref[...], w_ref[...])

    @pl.when(i == n_first)
    def _():
        x1_ref[:ns, :] = x[:ns] + _dot(mixs_ref[...], w_ref[...])
        if ns < tt:
            x1_ref[ns:, :] = x[ns:] + _dot(mix_ref[ns:, :], w_ref[...])

    x1 = x1_ref[...]
    xn = _rms(x1, g_ref[...])
    words = _pack_bf16_pairs(xn)
    for cc_ in range(XSUB):
        xn_ref[_slab(cc_, tt, sub=XSUB), :] = words[:, cc_ * LANES:(cc_ + 1) * LANES]
    hi = xn.astype(BF16)
    lo = (xn - hi.astype(F32)).astype(BF16)
    hi_out = jnp.dot(hi, jnp.concatenate([wrh_ref[...], wrl_ref[...]], axis=1), preferred_element_type=F32)
    logits = (hi_out[:, :LANES] + jnp.dot(lo, wrh_ref[...], preferred_element_type=F32)
              + hi_out[:, LANES:]) + br_ref[...]
    lane = lax.broadcasted_iota(I32, (tt, LANES), 1)
    lanef = lane.astype(F32)
    lg = jnp.where(lane < MOE_GROUPS, logits, NEG_INF)
    mg = jnp.max(lg, axis=-1, keepdims=True)
    gstar = jnp.min(jnp.where(lg == mg, lanef, 1e9), axis=-1, keepdims=True)
    pgroup = 1.0 / jnp.sum(jnp.exp(lg - mg), axis=-1, keepdims=True)
    lo_lane = MOE_GROUPS + gstar * EXPERTS_PER_GROUP
    le = jnp.where((lanef >= lo_lane) & (lanef < lo_lane + EXPERTS_PER_GROUP), logits, NEG_INF)
    v1 = jnp.max(le, axis=-1, keepdims=True)
    i1 = jnp.min(jnp.where(le == v1, lanef, 1e9), axis=-1, keepdims=True)
    le2 = jnp.where(lanef == i1, NEG_INF, le)
    v2 = jnp.max(le2, axis=-1, keepdims=True)
    i2 = jnp.min(jnp.where(le2 == v2, lanef, 1e9), axis=-1, keepdims=True)
    tq = jnp.exp(v2 - v1)
    w1 = pgroup / (1.0 + tq)
    w2 = pgroup * tq / (1.0 + tq)
    e1 = i1 - MOE_GROUPS
    e2 = i2 - MOE_GROUPS
    rowg = i * tt + lax.broadcasted_iota(I32, (tt, LANES), 0)
    valid = (rowg < valid_lo) | (rowg >= valid_mid)
    oh1 = jnp.where(valid & (lanef == e1), 1.0, 0.0)
    oh2 = jnp.where(valid & (lanef == e2), 1.0, 0.0)
    both = oh1 + oh2
    rr = lax.broadcasted_iota(I32, (tt, tt), 0)
    cc = lax.broadcasted_iota(I32, (tt, tt), 1)
    tril = jnp.where(cc < rr, 1.0, 0.0).astype(BF16)

    @pl.when(i == 0)
    def _():
        acc_ref[...] = jnp.zeros((1, LANES), F32)

    base = acc_ref[...] + jnp.dot(tril, both.astype(BF16), preferred_element_type=F32)
    rank1 = jnp.sum(oh1 * base, axis=-1, keepdims=True)
    rank2 = jnp.sum(oh2 * base, axis=-1, keepdims=True)
    total = acc_ref[...] + jnp.sum(both, axis=0, keepdims=True)
    acc_ref[...] = total
    cnt_ref[...] = jnp.broadcast_to(total, (8, LANES))
    info = jnp.where(lane == 0, e1, jnp.where(lane == 1, e2, jnp.where(lane == 2, w1, jnp.where(
        lane == 3, w2, jnp.where(lane == 4, rank1, jnp.where(lane == 5, rank2, 0.0))))))
    info_ref[...] = info
    infot_ref[0] = info.T[:SUB, :]


def _outproj_route(xs, mix, mix_s, w_out_bf16, g_ffn, wr_hi, wr_lo, br, t_rows, valid_lo, valid_mid):
    n = mix.shape[0]
    ns, dm = mix_s.shape
    nt = n // TT

    def const2(i):
        return (0, 0)

    return pl.pallas_call(
        functools.partial(_outproj_route_kernel, valid_lo=valid_lo, valid_mid=valid_mid,
                          n_x=len(xs), n_first=t_rows // TT, ns=ns),
        grid=(nt,),
        in_specs=_row_specs(xs, D_MODEL) + [
            pl.BlockSpec((TT, dm), lambda i: (i, 0)),
            pl.BlockSpec((ns, dm), const2),
            pl.BlockSpec((dm, D_MODEL), const2),
            pl.BlockSpec((1, D_MODEL), const2),
            pl.BlockSpec((D_MODEL, LANES), const2),
            pl.BlockSpec((D_MODEL, LANES), const2),
            pl.BlockSpec((1, LANES), const2)],
        out_specs=[pl.BlockSpec((TT, D_MODEL), lambda i: (i, 0)),
                   pl.BlockSpec((TT * XSUB, LANES), lambda i: (i, 0)),
                   pl.BlockSpec((TT, LANES), lambda i: (i, 0)),
                   pl.BlockSpec((1, SUB, TT), lambda i: (i, 0, 0)),
                   pl.BlockSpec((8, LANES), const2)],
        out_shape=[jax.ShapeDtypeStruct((n, D_MODEL), F32),
                   jax.ShapeDtypeStruct((n * XSUB, LANES), U32),
                   jax.ShapeDtypeStruct((n, LANES), F32),
                   jax.ShapeDtypeStruct((nt, SUB, TT), F32),
                   jax.ShapeDtypeStruct((8, LANES), F32)],
        scratch_shapes=[pltpu.VMEM((1, LANES), F32)],
        compiler_params=_cparams("arbitrary"),
        name="outproj_route",
    )(*xs, mix, mix_s, w_out_bf16, g_ffn.reshape(1, D_MODEL), wr_hi, wr_lo, br)


ROWS_PER_ISSUE = 8


def _issue_rows(lo, hi, row_copy):
    assert (hi - lo) % ROWS_PER_ISSUE == 0

    def body(it, c):
        for u in range(ROWS_PER_ISSUE):
            r = lo + it * ROWS_PER_ISSUE + u
            row_copy(r, 0).start(priority=0)
            row_copy(r, 1).start(priority=1)
        return c

    lax.fori_loop(0, (hi - lo) // ROWS_PER_ISSUE, body, 0)


def _dispatch_kernel(pos_ref, zt_ref, xn_ref, xs_ref, zbuf, sem, zsem, *, n_full, ranges, n_tiles):
    i = pl.program_id(0)

    @pl.when(i == 0)
    def _():
        zbuf[...] = jnp.zeros(zbuf.shape, zbuf.dtype)

        def zero_copy(t):
            rows = MOE_TM * XSUB
            return pltpu.make_async_copy(zbuf, xs_ref.at[pl.ds(pl.multiple_of(t * rows, rows), rows)], zsem)

        def zstart(t, c):
            @pl.when(zt_ref[t] != 0)
            def _():
                zero_copy(t).start()
            return c

        def zwait(t, c):
            @pl.when(zt_ref[t] != 0)
            def _():
                zero_copy(t).wait()
            return c

        lax.fori_loop(0, n_tiles, zstart, 0)
        lax.fori_loop(0, n_tiles, zwait, 0)

    def row_copy(r, k):
        return pltpu.make_async_copy(xn_ref.at[_tile_of(r, XSUB)],
                                     xs_ref.at[_tile_of(pos_ref[0, 0, k * TT + r], XSUB)], sem)

    def scatter(lo, hi):
        _issue_rows(lo, hi, row_copy)
        nrow = (hi - lo) * XSUB
        for _ in range(2):
            pltpu.make_async_copy(xn_ref.at[pl.ds(lo * XSUB, nrow)], xs_ref.at[pl.ds(0, nrow)], sem).wait()

    _for_tile_rows(i, n_full, ranges, scatter)


def _dispatch(xn, pos, zero_tiles, n_sorted, t_rows, ranges):
    n = xn.shape[0] // XSUB
    return pl.pallas_call(
        functools.partial(_dispatch_kernel, n_full=t_rows // TT, ranges=ranges,
                          n_tiles=n_sorted // MOE_TM),
        grid=(n // TT,),
        in_specs=[pl.BlockSpec((1, 1, 2 * TT), lambda i: (i, 0, 0), memory_space=pltpu.SMEM),
                  pl.BlockSpec(memory_space=pltpu.SMEM),
                  pl.BlockSpec((TT * XSUB, LANES), lambda i: (i, 0))],
        out_specs=pl.BlockSpec(memory_space=pl.ANY),
        out_shape=jax.ShapeDtypeStruct((n_sorted * XSUB, LANES), U32),
        scratch_shapes=[pltpu.VMEM((MOE_TM * XSUB, LANES), U32),
                        pltpu.SemaphoreType.DMA(()),
                        pltpu.SemaphoreType.DMA(())],
        compiler_params=_cparams("arbitrary"),
        name="moe_dispatch",
    )(pos, zero_tiles, xn)


def _expert_ffn_kernel(te_ref, nu_ref, xs_ref, wg_ref, wu_ref, wd_ref, ys_ref, wgb, wub, wdb):
    j = pl.program_id(0)
    changed = (j == 0) | (te_ref[j] != te_ref[jnp.maximum(j - 1, 0)])

    @pl.when((j < nu_ref[0]) & changed)
    def _():
        wgb[...] = wg_ref[0, 0].astype(BF16)
        wub[...] = wu_ref[0, 0].astype(BF16)
        wdb[...] = wd_ref[0, 0].astype(BF16)

    n_slab = D_MODEL // LANES

    @pl.when(j < nu_ref[0])
    def _():
        x = _unpack_bf16_pairs(jnp.concatenate(
            [xs_ref[_slab(c, MOE_TM, sub=XSUB), :] for c in range(XSUB)], axis=1))
        hg = jnp.dot(x, wgb[...], preferred_element_type=F32)
        hu = jnp.dot(x, wub[...], preferred_element_type=F32)
        h = (_silu(hg) * hu).astype(BF16)
        y = jnp.dot(h, wdb[...], preferred_element_type=F32)
        for c in range(n_slab):
            ys_ref[_slab(c, MOE_TM), :] = y[:, c * LANES:(c + 1) * LANES]

    @pl.when(j >= nu_ref[0])
    def _():
        ys_ref[...] = jnp.zeros(ys_ref.shape, F32)


def _expert_ffn(xs, tile_expert, n_used, w_gate, w_up, w_down, layer):
    n_sorted = xs.shape[0] // XSUB
    nt = n_sorted // MOE_TM

    def row_map(j, te, nu):
        return (jnp.minimum(j, nu[0] - 1), 0)

    def w_map(j, te, nu):
        return (layer, te[j], 0, 0)

    return pl.pallas_call(
        _expert_ffn_kernel,
        grid_spec=pltpu.PrefetchScalarGridSpec(
            num_scalar_prefetch=2,
            grid=(nt,),
            in_specs=[pl.BlockSpec((MOE_TM * XSUB, LANES), row_map),
                      pl.BlockSpec((1, 1, D_MODEL, D_EXPERT), w_map),
                      pl.BlockSpec((1, 1, D_MODEL, D_EXPERT), w_map),
                      pl.BlockSpec((1, 1, D_EXPERT, D_MODEL), w_map)],
            out_specs=pl.BlockSpec((MOE_TM * SUB, LANES), lambda j, te, nu: (j, 0)),
            scratch_shapes=[pltpu.VMEM((D_MODEL, D_EXPERT), BF16),
                            pltpu.VMEM((D_MODEL, D_EXPERT), BF16),
                            pltpu.VMEM((D_EXPERT, D_MODEL), BF16)]),
        out_shape=jax.ShapeDtypeStruct((n_sorted * SUB, LANES), F32),
        compiler_params=_cparams("arbitrary"),
        name="expert_ffn",
    )(tile_expert, n_used, xs, w_gate, w_up, w_down)


def _combine_kernel(posc_ref, posn_ref, x1_ref, info_ref, g_ref, ys_ref, *rest,
                    n_full, n_tiles, ranges, valid_lo, valid_mid, final_norm):
    out_refs, (ybuf, sem) = rest[:-2], rest[-2:]
    i = pl.program_id(0)
    slot = lax.rem(i, 2)

    def buf_base(s, k):
        return (s * 2 + k) * (TT * SUB)

    def gather(pos_ref, dst_slot):
        def row_copy(r, k):
            dst = pl.ds(pl.multiple_of(buf_base(dst_slot, k) + r * SUB, SUB), SUB)
            return pltpu.make_async_copy(ys_ref.at[_tile_of(pos_ref[0, 0, k * TT + r])],
                                         ybuf.at[dst], sem.at[dst_slot])
        return lambda lo, hi: _issue_rows(lo, hi, row_copy)

    @pl.when(i == 0)
    def _():
        ybuf[...] = jnp.zeros(ybuf.shape, F32)
        _for_tile_rows(i, n_full, ranges, gather(posc_ref, 0))

    @pl.when(i + 1 < n_tiles)
    def _():
        _for_tile_rows(i + 1, n_full, ranges, gather(posn_ref, 1 - slot))

    def drain(lo, hi):
        nrow = (hi - lo) * SUB
        for k in range(2):
            dst = pl.ds(pl.multiple_of(buf_base(slot, k) + lo * SUB, SUB), nrow)
            pltpu.make_async_copy(ys_ref.at[pl.ds(0, nrow)], ybuf.at[dst], sem.at[slot]).wait()

    _for_tile_rows(i, n_full, ranges, drain)

    info = info_ref[...]
    w1 = info[:, 2:3]
    w2 = info[:, 3:4]
    rowg = i * TT + lax.broadcasted_iota(I32, (TT, 1), 0)
    valid = (rowg < valid_lo) | (rowg >= valid_mid)
    outs = []
    for c in range(D_MODEL // LANES):
        y1 = ybuf[_slab(c, TT, buf_base(slot, 0)), :]
        y2 = ybuf[_slab(c, TT, buf_base(slot, 1)), :]
        outs.append(x1_ref[:, c * LANES:(c + 1) * LANES] + jnp.where(valid, w1 * y1 + w2 * y2, 0.0))
    out = jnp.concatenate(outs, axis=1)
    if not final_norm:
        out_refs[0][...] = out
    else:
        out = _rms(out, g_ref[...])

        @pl.when(i < n_full)
        def _():
            out_refs[0][...] = out

        @pl.when(i >= n_full)
        def _():
            out_refs[1][...] = out


def _combine(x1, info, pos, ys, g_final, t_rows, ranges, valid_lo, valid_mid, final_norm):
    n = x1.shape[0]
    nt = n // TT
    n_full = t_rows // TT
    if final_norm:
        out_specs = [pl.BlockSpec((TT, D_MODEL), lambda i: (jnp.minimum(i, n_full - 1), 0)),
                     pl.BlockSpec((TT, D_MODEL), lambda i: (jnp.maximum(i - n_full, 0), 0))]
        out_shape = [jax.ShapeDtypeStruct((t_rows, D_MODEL), F32),
                     jax.ShapeDtypeStruct((n - t_rows, D_MODEL), F32)]
    else:
        out_specs = [pl.BlockSpec((TT, D_MODEL), lambda i: (i, 0))]
        out_shape = [jax.ShapeDtypeStruct((n, D_MODEL), F32)]
    return pl.pallas_call(
        functools.partial(_combine_kernel, n_full=n_full, n_tiles=nt, ranges=ranges,
                          valid_lo=valid_lo, valid_mid=valid_mid, final_norm=final_norm),
        grid=(nt,),
        in_specs=[pl.BlockSpec((1, 1, 2 * TT), lambda i: (i, 0, 0), memory_space=pltpu.SMEM),
                  pl.BlockSpec((1, 1, 2 * TT), lambda i: (jnp.minimum(i + 1, nt - 1), 0, 0),
                               memory_space=pltpu.SMEM),
                  pl.BlockSpec((TT, D_MODEL), lambda i: (i, 0)),
                  pl.BlockSpec((TT, LANES), lambda i: (i, 0)),
                  pl.BlockSpec((1, D_MODEL), lambda i: (0, 0)),
                  pl.BlockSpec(memory_space=pl.ANY)],
        out_specs=out_specs,
        out_shape=out_shape,
        scratch_shapes=[pltpu.VMEM((2 * 2 * TT * SUB, LANES), F32),
                        pltpu.SemaphoreType.DMA((2,))],
        compiler_params=_cparams("arbitrary"),
        name="moe_combine",
    )(pos, pos, x1, info, g_final.reshape(1, D_MODEL), ys)


def _moe(x1, xn, info, infot, cnt, w_gate, w_up, w_down, layer, g_final, t_rows, ns, final_norm):
    n = x1.shape[0]
    valid_lo, valid_mid = t_rows + ns, n - N_META
    ranges = _small_tile_ranges(t_rows, ns, n)
    n_valid = valid_lo + N_META
    nt = (2 * n_valid + N_EXPERTS * (MOE_TM - 1)) // MOE_TM
    counts = cnt[0, :N_EXPERTS].astype(I32)
    tiles = (counts + MOE_TM - 1) // MOE_TM
    tile_end = jnp.cumsum(tiles)
    row_start = (tile_end - tiles) * MOE_TM
    n_used = tile_end[-1:]
    experts = jnp.arange(N_EXPERTS, dtype=I32)

    def slot_rows(e, rank):
        start = jnp.sum(jnp.where(e[..., None] == experts, row_start, 0), axis=-1)
        return start + rank

    pos = jnp.concatenate([slot_rows(infot[:, 0].astype(I32), infot[:, 4].astype(I32)),
                           slot_rows(infot[:, 1].astype(I32), infot[:, 5].astype(I32))], axis=-1)
    rows = jnp.arange(n).reshape(n // TT, TT)
    valid = (rows < valid_lo) | (rows >= valid_mid)
    pos = jnp.where(jnp.concatenate([valid, valid], axis=-1), pos, 0)[:, None, :]
    tile_ids = jnp.arange(nt, dtype=I32)
    tile_expert = jnp.sum(tile_end[None, :] <= jnp.minimum(tile_ids, n_used[0] - 1)[:, None],
                          axis=1).astype(I32)
    is_last = jnp.any((tile_ids[:, None] == tile_end[None, :] - 1) & (tiles[None, :] > 0), axis=1)
    zero_tiles = (is_last | (tile_ids >= n_used[0])).astype(I32)
    xs = _dispatch(xn, pos, zero_tiles, nt * MOE_TM, t_rows, ranges)
    ys = _expert_ffn(xs, tile_expert, n_used.astype(I32), w_gate, w_up, w_down, layer)
    return _combine(x1, info, pos, ys, g_final, t_rows, ranges, valid_lo, valid_mid, final_norm)


def _shift_rows(x, k, fill):
    if k % SUB == 0:
        return jnp.concatenate([jnp.full((k, x.shape[1]), fill, x.dtype), x[:x.shape[0] - k]], axis=0)
    row = lax.broadcasted_iota(I32, x.shape, 0)
    return jnp.where(row >= k, pltpu.roll(x, k, 0), fill)


def _lru_gates(xc, wgate_ref, bgate_ref, lam_sp):
    gates = _dot(xc, wgate_ref[...]) + bgate_ref[...]
    r = _sigmoid(gates[:, :D_LRU])
    ig = _sigmoid(gates[:, D_LRU:])
    log_a = -LRU_C * r * lam_sp
    a = jnp.exp(log_a)
    bx = jnp.sqrt(-jnp.tanh(log_a) * (a * a + 1.0)) * (ig * xc)
    return a, bx


def _conv_block(seq_ref, u_ref, w_ref, b_ref, j):
    @pl.when(j == 0)
    def _():
        seq_ref[0:8, :] = jnp.zeros((8, seq_ref.shape[1]), F32)

    u = u_ref[...]
    seq_ref[8:8 + BLK, :] = u
    seq = seq_ref[...]
    out = b_ref[...] + u * w_ref[CONV_WIDTH - 1:CONV_WIDTH, :]
    for back in range(1, CONV_WIDTH):
        tap = CONV_WIDTH - 1 - back
        out = out + pltpu.roll(seq, back, 0)[8:, :] * w_ref[tap:tap + 1, :]
    seq_ref[0:8, :] = u[BLK - 8:, :]
    return out


def _mix1_prompt_kernel(*refs, nblk):
    j = pl.program_id(0)
    o_ref = refs[16]

    @pl.when(j <= nblk)
    def _():
        _mix1_prompt_step(*refs)

    @pl.when(j > nblk)
    def _():
        o_ref[...] = jnp.zeros(o_ref.shape, o_ref.dtype)


def _mix1_prompt_step(cx_ref, cg_ref, z_ref, xbc_ref, dt_ref,
                      cwc_ref, cbc_ref, wgate_ref, bgate_ref, lam_ref,
                      cwd_ref, cbd_ref, dtb_ref, alog_ref, dskip_ref, nw_ref,
                      o_ref, lruh_ref, ssdh_ref,
                      seqc_ref, seqd_ref, hl_ref, hs_ref):
    j = pl.program_id(0)
    rowc = lax.broadcasted_iota(I32, (BLK, 1), 0)
    live = rowc >= jnp.where(j >= 1, 0, META_PAD)

    @pl.when(j == 0)
    def _():
        hl_ref[...] = jnp.zeros(hl_ref.shape, F32)
        hs_ref[...] = jnp.zeros(hs_ref.shape, F32)

    xc = _conv_block(seqc_ref, cx_ref, cwc_ref, cbc_ref, j)
    a, bx = _lru_gates(xc, wgate_ref, bgate_ref, _softplus(-lam_ref[...]))
    a = jnp.where(live, a, 1.0)
    bx = jnp.where(live, bx, 0.0)
    k = 1
    while k < BLK:
        bx = a * _shift_rows(bx, k, 0.0) + bx
        a = a * _shift_rows(a, k, 1.0)
        k *= 2
    h = a * hl_ref[...] + bx
    hl_ref[...] = h[BLK - 1:BLK, :]
    lruh_ref[...] = h[BLK - 1:BLK, :]
    o_ref[:, :D_LRU] = (h * _gelu_tanh(cg_ref[...])).astype(o_ref.dtype)

    xbc = _silu(_conv_block(seqd_ref, xbc_ref, cwd_ref, cbd_ref, j))
    xs = xbc[:, :D_SSD]
    dt = jnp.where(live, _softplus(dt_ref[...] + dtb_ref[...]), 0.0)
    a_neg = -jnp.exp(alog_ref[...])
    acum = dt * a_neg
    k = 1
    while k < BLK:
        acum = acum + _shift_rows(acum, k, 0.0)
        k *= 2
    acum_t = acum.T
    dt_t = dt.T
    xs_t = xs.T
    tri = (lax.broadcasted_iota(I32, (BLK, BLK), 0) >= lax.broadcasted_iota(I32, (BLK, BLK), 1))
    rep = SSD_HEADS // SSD_GROUPS
    ys = []
    for g in range(SSD_GROUPS):
        bg = xbc[:, D_SSD + g * SSD_STATE:D_SSD + (g + 1) * SSD_STATE]
        cgm = xbc[:, D_SSD + (SSD_GROUPS + g) * SSD_STATE:D_SSD + (SSD_GROUPS + g + 1) * SSD_STATE]
        cb = _dot_nt(cgm, bg)
        for hh in range(rep):
            hd = g * rep + hh
            col = acum[:, hd:hd + 1]
            rowv = acum_t[hd:hd + 1, :]
            decay = jnp.exp(jnp.where(tri, col - rowv, NEG_INF))
            scores = cb * decay * dt_t[hd:hd + 1, :]
            xh = xs[:, hd * SSD_HEAD_DIM:(hd + 1) * SSD_HEAD_DIM]
            y = _dot(scores, xh)
            hprev = hs_ref[hd]
            y = y + jnp.exp(col) * _dot_nt(cgm, hprev)
            last = acum_t[hd:hd + 1, BLK - 1:BLK]
            w_end = jnp.exp(last - rowv) * dt_t[hd:hd + 1, :]
            xw_t = xs_t[hd * SSD_HEAD_DIM:(hd + 1) * SSD_HEAD_DIM, :] * w_end
            hnew = jnp.exp(last) * hprev + _dot(xw_t, bg)
            hs_ref[hd] = hnew
            ssdh_ref[hd] = hnew
            ys.append(y + dskip_ref[:, hd * SSD_HEAD_DIM:(hd + 1) * SSD_HEAD_DIM] * xh)
    y = jnp.concatenate(ys, axis=1) * _silu(z_ref[...])
    o_ref[:, D_LRU:] = _rms(y, nw_ref[...]).astype(o_ref.dtype)


def _mix1_prompt(cx, cg, z, xbc, dt, wts, t_rows, meta_blk):
    n = cx.shape[0]
    nblk = t_rows // BLK

    def cur(j):
        return (jnp.where(j == 0, meta_blk, j - 1), 0)

    def const2(j):
        return (0, 0)

    in_specs = [pl.BlockSpec((BLK, D_LRU), cur), pl.BlockSpec((BLK, D_LRU), cur),
                pl.BlockSpec((BLK, D_SSD), cur), pl.BlockSpec((BLK, D_XBC), cur),
                pl.BlockSpec((BLK, LANES), cur)]
    in_specs += [pl.BlockSpec(w.shape, const2) for w in wts]
    return pl.pallas_call(
        functools.partial(_mix1_prompt_kernel, nblk=nblk),
        grid=(meta_blk + 1,),
        in_specs=in_specs,
        out_specs=[pl.BlockSpec((BLK, D_MIX_ODD), cur),
                   pl.BlockSpec((1, D_LRU), const2),
                   pl.BlockSpec((SSD_HEADS, SSD_HEAD_DIM, SSD_STATE), lambda j: (0, 0, 0))],
        out_shape=[jax.ShapeDtypeStruct((n, D_MIX_ODD), BF16),
                   jax.ShapeDtypeStruct((1, D_LRU), F32),
                   jax.ShapeDtypeStruct((SSD_HEADS, SSD_HEAD_DIM, SSD_STATE), F32)],
        scratch_shapes=[pltpu.VMEM((8 + BLK, D_LRU), F32),
                        pltpu.VMEM((8 + BLK, D_XBC), F32),
                        pltpu.VMEM((1, D_LRU), F32),
                        pltpu.VMEM((SSD_HEADS, SSD_HEAD_DIM, SSD_STATE), F32)],
        compiler_params=_cparams("arbitrary"),
        name="mix1_prompt",
    )(cx, cg, z, xbc, dt, *wts)


def _conv_step(seq, t, w_ref, b_ref):
    out = b_ref[...]
    for tap in range(CONV_WIDTH):
        out = out + seq[t + tap] * w_ref[tap:tap + 1, :]
    return out


def _lru_sample_kernel(cx_ref, cg_ref, buf_ref, h0_ref, cwc_ref, cbc_ref, wgate_ref, bgate_ref,
                       lam_ref, o_ref, hout_ref, *, n_new):
    seq = [buf_ref[i] for i in range(CONV_WIDTH - 1)] + [cx_ref[i] for i in range(n_new)]
    lam_sp = _softplus(-lam_ref[...])
    h = h0_ref[...]
    for t in range(n_new):
        xc = _conv_step(seq, t, cwc_ref, cbc_ref)
        a, bx = _lru_gates(xc, wgate_ref, bgate_ref, lam_sp)
        h = a * h + bx
        o_ref[t] = h * _gelu_tanh(cg_ref[t])
    hout_ref[...] = h


def _lru_sample(cx_tm, cg_tm, buf_tm, h0, wts):
    n_new = cx_tm.shape[0]
    return pl.pallas_call(
        functools.partial(_lru_sample_kernel, n_new=n_new),
        out_shape=[jax.ShapeDtypeStruct(cx_tm.shape, F32), jax.ShapeDtypeStruct(h0.shape, F32)],
        compiler_params=pltpu.CompilerParams(vmem_limit_bytes=VMEM_LIMIT),
        name="lru_sample",
    )(cx_tm, cg_tm, buf_tm, h0, *wts)


def _split3(x):
    hi = x.astype(BF16)
    r = x - hi.astype(F32)
    mid = r.astype(BF16)
    lo = (r - mid.astype(F32)).astype(BF16)
    return hi, mid, lo


def _ssd_sample_kernel(xbc_ref, dt_ref, z_ref, buf_ref, h0_ref, cwd_ref, cbd_ref, dtb_ref, alog_ref,
                       dskip_ref, nw_ref, o_ref, hout_ref, xc_ref, dec_ref, dtx_ref, y_ref, *, n_new, bs):
    seq = [buf_ref[i] for i in range(CONV_WIDTH - 1)] + [xbc_ref[i] for i in range(n_new)]
    a_neg = -jnp.exp(alog_ref[...])
    hcol = lax.broadcasted_iota(I32, (LANES, D_SSD), 1) // SSD_HEAD_DIM
    expand = jnp.where(lax.broadcasted_iota(I32, (LANES, D_SSD), 0) == hcol, 1.0, 0.0).astype(BF16)

    def widen(v):
        parts = _split3(v)
        return sum(jnp.dot(p, expand, preferred_element_type=F32) for p in parts)

    for t in range(n_new):
        xc = _silu(_conv_step(seq, t, cwd_ref, cbd_ref))
        xc_ref[t] = xc
        dt = _softplus(dt_ref[t] + dtb_ref[...])
        dtw = widen(dt)
        dec_ref[t] = jnp.exp(widen(dt * a_neg))
        dtx_ref[t] = dtw * xc[:, :D_SSD]

    zero_rows = jnp.zeros((LANES - 2 * n_new * SSD_GROUPS, SSD_STATE), F32)
    half = D_SSD // SSD_GROUPS

    def per_seq(b, c):
        rows = [xc_ref[t, pl.ds(b, 1), :] for t in range(n_new)]
        bc = [r[:, D_SSD + k * SSD_STATE:D_SSD + (k + 1) * SSD_STATE]
              for r in rows for k in range(2 * SSD_GROUPS)]
        bc_t = jnp.concatenate(bc + [zero_rows], axis=0).T
        ht = h0_ref[b].T
        for t in range(n_new):
            bcol = [jnp.broadcast_to(bc_t[:, t * 4 + g:t * 4 + g + 1], (SSD_STATE, half))
                    for g in range(SSD_GROUPS)]
            ccol = [jnp.broadcast_to(bc_t[:, t * 4 + SSD_GROUPS + g:t * 4 + SSD_GROUPS + g + 1],
                                     (SSD_STATE, half)) for g in range(SSD_GROUPS)]
            bw = jnp.concatenate(bcol, axis=1)
            cw = jnp.concatenate(ccol, axis=1)
            ht = ht * dec_ref[t, pl.ds(b, 1), :] + bw * dtx_ref[t, pl.ds(b, 1), :]
            y = jnp.sum(cw * ht, axis=0, keepdims=True)
            y_ref[t, pl.ds(b, 1), :] = y + dskip_ref[...] * rows[t][:, :D_SSD]
        hout_ref[b] = ht.T
        return c

    lax.fori_loop(0, bs, per_seq, 0)
    for t in range(n_new):
        o_ref[t] = _rms(y_ref[t] * _silu(z_ref[t]), nw_ref[...])


def _ssd_sample(xbc_tm, dt_tm, z_tm, buf_tm, h0, wts):
    n_new, b, _ = xbc_tm.shape
    bs = 16 if b % 16 == 0 else b
    hp = SSD_HEADS * SSD_HEAD_DIM

    def tm(c):
        return pl.BlockSpec((n_new, bs, c), lambda i: (0, i, 0))

    in_specs = [tm(D_XBC), tm(LANES), tm(D_SSD),
                pl.BlockSpec((CONV_WIDTH - 1, bs, D_XBC), lambda i: (0, i, 0)),
                pl.BlockSpec((bs, hp, SSD_STATE), lambda i: (i, 0, 0))]
    in_specs += [pl.BlockSpec(w.shape, lambda i: (0, 0)) for w in wts]
    return pl.pallas_call(
        functools.partial(_ssd_sample_kernel, n_new=n_new, bs=bs),
        grid=(b // bs,),
        in_specs=in_specs,
        out_specs=[tm(D_SSD), pl.BlockSpec((bs, hp, SSD_STATE), lambda i: (i, 0, 0))],
        out_shape=[jax.ShapeDtypeStruct((n_new, b, D_SSD), F32),
                   jax.ShapeDtypeStruct((b, hp, SSD_STATE), F32)],
        scratch_shapes=[pltpu.VMEM((n_new, bs, D_XBC), F32),
                        pltpu.VMEM((n_new, bs, D_SSD), F32),
                        pltpu.VMEM((n_new, bs, D_SSD), F32),
                        pltpu.VMEM((n_new, bs, D_SSD), F32)],
        compiler_params=_cparams("parallel"),
        name="ssd_sample",
    )(xbc_tm, dt_tm, z_tm, buf_tm, h0, *wts)


def _block_diag(w):
    g, c, _ = w.shape
    eye = jnp.eye(g, dtype=w.dtype)
    return (eye[:, None, :, None] * w[:, :, None, :]).reshape(g * c, g * c)


def _router_weights(w_rg, b_rg, w_re, b_re):
    w = jnp.concatenate([w_rg, w_re], axis=1)
    w = jnp.pad(w, ((0, 0), (0, LANES - w.shape[1])))
    hi = w.astype(BF16)
    lo = (w - hi.astype(F32)).astype(BF16)
    b = jnp.pad(jnp.concatenate([b_rg, b_re]), (0, LANES - MOE_GROUPS - N_EXPERTS)).reshape(1, LANES)
    return hi, lo, b


def kernel(x_prompt, x_sample, cache_swa_k, cache_swa_v, cache_pool, state_lru_conv, state_lru_h, state_ssd_conv, state_ssd_h, meta_tokens, norm_mix, norm_ffn, norm_final, attn_w_in, attn_w_out, attn_sinks, pool_w, pool_scale, rec_w_in, rec_w_out, lru_conv_w, lru_conv_b, lru_w_rg, lru_b_rg, lru_w_ig, lru_b_ig, lru_lambda, ssd_conv_w, ssd_conv_b, ssd_dt_bias, ssd_a_log, ssd_d, ssd_norm_w, moe_w_rg, moe_b_rg, moe_w_re, moe_b_re, moe_w_gate, moe_w_up, moe_w_down):
    t_rows = x_prompt.shape[1]
    b, s_new = x_sample.shape[:2]
    ns = b * s_new
    small = -(-(ns + BLK) // TT) * TT
    n = t_rows + small
    meta_blk = n // BLK - 1
    valid_lo = t_rows + ns
    valid_mid = n - N_META
    assert x_prompt.shape[0] == 1 and t_rows % TT == 0 and ns % BLK == 0 and ns <= TT

    x_small = jnp.concatenate([
        x_sample.transpose(1, 0, 2).reshape(ns, D_MODEL),
        jnp.zeros((small - ns - N_META, D_MODEL), F32),
        meta_tokens.astype(F32)], axis=0)
    xs0 = (x_prompt[0], x_small)

    def sample_tm(a):
        return a[t_rows:t_rows + ns].reshape(s_new, b, a.shape[1])

    q, kv, p = _norm_proj(xs0, norm_mix[0], attn_w_in[0].astype(BF16), (D_ATTN, 2 * D_KV, D_POOL),
                          (BF16, F32, F32))
    wpool_bd = _block_diag(pool_w[0]).astype(BF16)
    mix = _mix0_prompt(q, kv, p, attn_sinks[0], wpool_bd, pool_scale[0], t_rows, meta_blk)
    nq = s_new * GQA
    qs = sample_tm(q).reshape(s_new, b, N_KV_HEADS, GQA, HEAD_DIM).transpose(1, 2, 0, 3, 4)
    qs = qs.reshape(b, N_KV_HEADS, nq, HEAD_DIM)
    kv_s = sample_tm(kv).transpose(1, 0, 2)
    kn, vn = kv_s[:, :, :D_KV], kv_s[:, :, D_KV:]
    ck = cache_swa_k[0].reshape(b, N_META + WINDOW, D_KV)
    cv = cache_swa_v[0].reshape(b, N_META + WINDOW, D_KV)
    o_s = _attn_sample(qs, ck, cv, kn, vn, attn_sinks[0])
    o_s = o_s.reshape(b, N_KV_HEADS, s_new, GQA, HEAD_DIM).transpose(2, 0, 1, 3, 4).reshape(s_new, b, D_ATTN)
    p_s = sample_tm(p)
    pool_buf_tm = cache_pool[0].transpose(1, 0, 2)
    yp_s = _pool_sample(pool_buf_tm, p_s, wpool_bd, pool_scale[0])
    mix_s = jnp.concatenate([o_s, yp_s], axis=-1).reshape(ns, D_MIX_EVEN)
    wr_hi, wr_lo, br = _router_weights(moe_w_rg[0], moe_b_rg[0], moe_w_re[0], moe_b_re[0])
    x1, xn, info, infot, cnt = _outproj_route(xs0, mix, mix_s, attn_w_out[0].astype(BF16), norm_ffn[0],
                                              wr_hi, wr_lo, br, t_rows, valid_lo, valid_mid)
    x, = _moe(x1, xn, info, infot, cnt, moe_w_gate, moe_w_up, moe_w_down, 0, norm_final,
              t_rows, ns, False)

    meta_rows = slice(n - N_META, n)
    last_rows = slice(t_rows - WINDOW, t_rows)
    swa_k_p = jnp.concatenate([kv[meta_rows, :D_KV], kv[last_rows, :D_KV]], axis=0)
    swa_v_p = jnp.concatenate([kv[meta_rows, D_KV:], kv[last_rows, D_KV:]], axis=0)
    swa_k_p = swa_k_p.reshape(1, 1, N_META + WINDOW, N_KV_HEADS, HEAD_DIM)
    swa_v_p = swa_v_p.reshape(1, 1, N_META + WINDOW, N_KV_HEADS, HEAD_DIM)
    pool_p = p[t_rows - POOL_BUF:t_rows].reshape(1, 1, POOL_BUF, D_POOL)
    swa_k_s = jnp.concatenate([ck[:, :N_META], ck[:, N_META + s_new:], kn], axis=1)
    swa_v_s = jnp.concatenate([cv[:, :N_META], cv[:, N_META + s_new:], vn], axis=1)
    swa_k_s = swa_k_s.reshape(1, b, N_META + WINDOW, N_KV_HEADS, HEAD_DIM)
    swa_v_s = swa_v_s.reshape(1, b, N_META + WINDOW, N_KV_HEADS, HEAD_DIM)
    pool_s = jnp.concatenate([cache_pool[0], p_s.transpose(1, 0, 2)], axis=1)[:, -POOL_BUF:][None]

    d_in_odd = rec_w_in.shape[2]
    w_in1 = jnp.pad(rec_w_in[0], ((0, 0), (0, LANES - SSD_HEADS))).astype(BF16)
    assert d_in_odd == 2 * D_LRU + D_SSD + D_XBC + SSD_HEADS
    cx, cg, z, xbc, dt = _norm_proj((x,), norm_mix[1], w_in1, (D_LRU, D_LRU, D_SSD, D_XBC, LANES))
    wgate = jnp.concatenate([_block_diag(lru_w_rg[0]), _block_diag(lru_w_ig[0])], axis=1).astype(BF16)
    bgate = jnp.concatenate([lru_b_rg[0], lru_b_ig[0]]).reshape(1, 2 * D_LRU)
    pad8 = (0, LANES - SSD_HEADS)
    dtb = jnp.pad(ssd_dt_bias[0], pad8).reshape(1, LANES)
    alog = jnp.pad(ssd_a_log[0], pad8).reshape(1, LANES)
    dskip = jnp.repeat(ssd_d[0], SSD_HEAD_DIM).reshape(1, D_SSD)
    lru_w = (lru_conv_w[0], lru_conv_b[0].reshape(1, D_LRU), wgate, bgate, lru_lambda[0].reshape(1, D_LRU))
    ssd_w = (ssd_conv_w[0], ssd_conv_b[0].reshape(1, D_XBC), dtb, alog, dskip, ssd_norm_w[0].reshape(1, D_SSD))
    mix, lru_h_p, ssd_h_p = _mix1_prompt(cx, cg, z, xbc, dt, lru_w + ssd_w, t_rows, meta_blk)
    cx_s, cg_s, z_s, xbc_s, dt_s = (sample_tm(a) for a in (cx, cg, z, xbc, dt))
    yc_s, lru_h_s = _lru_sample(cx_s, cg_s, state_lru_conv[0].transpose(1, 0, 2), state_lru_h[0], lru_w)
    h0 = state_ssd_h[0].reshape(b, SSD_HEADS * SSD_HEAD_DIM, SSD_STATE)
    yd_s, ssd_h_s = _ssd_sample(xbc_s, dt_s, z_s, state_ssd_conv[0].transpose(1, 0, 2), h0, ssd_w)
    mix_s = jnp.concatenate([yc_s, yd_s], axis=-1).reshape(ns, D_MIX_ODD)
    wr_hi, wr_lo, br = _router_weights(moe_w_rg[1], moe_b_rg[1], moe_w_re[1], moe_b_re[1])
    x1, xn, info, infot, cnt = _outproj_route((x,), mix, mix_s, rec_w_out[0].astype(BF16), norm_ffn[1],
                                              wr_hi, wr_lo, br, t_rows, valid_lo, valid_mid)
    y_p, y_small = _moe(x1, xn, info, infot, cnt, moe_w_gate, moe_w_up, moe_w_down, 1, norm_final,
                        t_rows, ns, True)

    tail = CONV_WIDTH - 1
    lru_conv_p = cx[t_rows - tail:t_rows].reshape(1, 1, tail, D_LRU)
    ssd_conv_p = xbc[t_rows - tail:t_rows].reshape(1, 1, tail, D_XBC)
    lru_conv_s = jnp.concatenate([state_lru_conv[0], cx_s.transpose(1, 0, 2)], axis=1)[:, -tail:][None]
    ssd_conv_s = jnp.concatenate([state_ssd_conv[0], xbc_s.transpose(1, 0, 2)], axis=1)[:, -tail:][None]

    y_prompt = y_p[None]
    y_sample = y_small[:ns].reshape(s_new, b, D_MODEL).transpose(1, 0, 2)
    return (y_prompt, y_sample, swa_k_p, swa_v_p, pool_p, lru_conv_p,
            lru_h_p.reshape(1, 1, D_LRU), ssd_conv_p,
            ssd_h_p.reshape(1, 1, SSD_HEADS, SSD_HEAD_DIM, SSD_STATE),
            swa_k_s, swa_v_s, pool_s, lru_conv_s, lru_h_s[None], ssd_conv_s,
            ssd_h_s.reshape(1, b, SSD_HEADS, SSD_HEAD_DIM, SSD_STATE))
```

```python
import functools

import jax
import jax.numpy as jnp
from jax import lax
from jax.experimental import pallas as pl
from jax.experimental.pallas import tpu as pltpu

F32 = jnp.float32
BF16 = jnp.bfloat16
I32 = jnp.int32

D_MODEL = 1024
N_META = 16
EPS = 1e-6
PAST_LEN = 16384
N_Q_HEADS = 16
N_KV_HEADS = 2
GQA = N_Q_HEADS // N_KV_HEADS
HEAD_DIM = 64
WINDOW = 128
D_ATTN = N_Q_HEADS * HEAD_DIM
D_KV = N_KV_HEADS * HEAD_DIM
POOL_WINDOWS = (2, 4, 8, 16)
POOL_GROUP_DIM = 96
D_POOL = len(POOL_WINDOWS) * POOL_GROUP_DIM
POOL_BUF = max(POOL_WINDOWS) - 1
D_MIX_EVEN = D_ATTN + D_POOL
D_LRU = 512
LRU_HEADS = 8
LRU_BLOCK = D_LRU // LRU_HEADS
LRU_C = 8.0
CONV_WIDTH = 4
SSD_HEADS = 8
SSD_HEAD_DIM = 64
D_SSD = SSD_HEADS * SSD_HEAD_DIM
SSD_GROUPS = 2
SSD_STATE = 128
D_XBC = D_SSD + 2 * SSD_GROUPS * SSD_STATE
D_MIX_ODD = D_LRU + D_SSD
MOE_GROUPS = 4
EXPERTS_PER_GROUP = 8
N_EXPERTS = MOE_GROUPS * EXPERTS_PER_GROUP
D_EXPERT = 256

BLK = 128
META_PAD = BLK - N_META
LANES = 128
MOE_TM = 256
FFN_ROWS = 256
NEG_INF = float("-inf")
VMEM_LIMIT = 56 * 1024 * 1024


def _cparams(*sem):
    return pltpu.CompilerParams(dimension_semantics=sem, vmem_limit_bytes=VMEM_LIMIT)


TT = 512
SUB = 8


def _small_tile_ranges(t_rows, ns, n):
    out = {}
    for k in range(t_rows // TT, n // TT):
        lo = k * TT
        ranges = []
        a0, a1 = max(lo, t_rows), min(lo + TT, t_rows + ns)
        if a1 > a0:
            ranges.append((a0 - lo, a1 - lo))
        b0 = max(lo, n - N_META)
        if lo + TT > b0:
            ranges.append((b0 - lo, TT))
        out[k] = tuple(ranges)
    return out


def _for_tile_rows(i, n_full, ranges, fn):
    @pl.when(i < n_full)
    def _():
        fn(0, TT)

    for k, rs in ranges.items():
        if rs:
            @pl.when(i == k)
            def _(rs=rs):
                for lo, hi in rs:
                    fn(lo, hi)


def _slab(c, rows, base=0):
    return pl.ds(base + c, rows, stride=SUB)


def _tile_of(r):
    return pl.ds(pl.multiple_of(r * SUB, SUB), SUB)


def _sigmoid(x):
    return 1.0 / (1.0 + jnp.exp(-x))


def _silu(x):
    return x * _sigmoid(x)


def _softplus(x):
    return jnp.maximum(x, 0.0) + jnp.log1p(jnp.exp(-jnp.abs(x)))


def _gelu_tanh(x):
    return 0.5 * x * (1.0 + jnp.tanh(0.7978845608028654 * (x + 0.044715 * x * x * x)))


def _dot(a, b):
    return jnp.dot(a.astype(BF16), b.astype(BF16), preferred_element_type=F32)


def _dot_nt(a, b):
    return lax.dot_general(a.astype(BF16), b.astype(BF16), (((1,), (1,)), ((), ())),
                           preferred_element_type=F32)


def _rms(x, g):
    ms = jnp.mean(x * x, axis=-1, keepdims=True)
    return x * lax.rsqrt(ms + EPS) * g


def _row_specs(xs, width):
    if len(xs) == 1:
        return [pl.BlockSpec((TT, width), lambda i: (i, 0))]
    na = xs[0].shape[0] // TT
    return [pl.BlockSpec((TT, width), lambda i: (jnp.minimum(i, na - 1), 0)),
            pl.BlockSpec((TT, width), lambda i: (jnp.maximum(i - na, 0), 0))]


def _pick_rows(refs, n_first):
    if len(refs) == 1:
        return refs[0][...]
    return jnp.where(pl.program_id(0) < n_first, refs[0][...], refs[1][...])


def _norm_proj_kernel(*refs, splits, n_x, n_first):
    x_refs, (g_ref, w_ref), out_refs = refs[:n_x], refs[n_x:n_x + 2], refs[n_x + 2:]
    y = _rms(_pick_rows(x_refs, n_first), g_ref[...]).astype(BF16)
    off = 0
    for o_ref, n in zip(out_refs, splits):
        o_ref[...] = jnp.dot(y, w_ref[:, off:off + n], preferred_element_type=F32).astype(o_ref.dtype)
        off += n


def _norm_proj(xs, g, w_bf16, splits, dtypes=None):
    n = sum(x.shape[0] for x in xs)
    dtypes = dtypes or (F32,) * len(splits)
    return pl.pallas_call(
        functools.partial(_norm_proj_kernel, splits=splits, n_x=len(xs), n_first=xs[0].shape[0] // TT),
        grid=(n // TT,),
        in_specs=_row_specs(xs, D_MODEL) + [pl.BlockSpec((1, D_MODEL), lambda i: (0, 0)),
                                             pl.BlockSpec(w_bf16.shape, lambda i: (0, 0))],
        out_specs=[pl.BlockSpec((TT, s), lambda i: (i, 0)) for s in splits],
        out_shape=[jax.ShapeDtypeStruct((n, s), dt) for s, dt in zip(splits, dtypes)],
        compiler_params=_cparams("parallel"),
        name="norm_proj",
    )(*xs, g.reshape(1, D_MODEL), w_bf16)


def _mix0_prompt_kernel(*refs, nblk):
    j = pl.program_id(0)
    o_ref = refs[8]

    @pl.when(j <= nblk)
    def _():
        _mix0_prompt_step(*refs)

    @pl.when(j > nblk)
    def _():
        o_ref[...] = jnp.zeros(o_ref.shape, o_ref.dtype)


def _mix0_prompt_step(sink_ref, q_ref, kvc_ref, kvp_ref, kvm_ref, p_ref, wp_ref, ps_ref,
                      o_ref, seq_ref, s_ref, pr_ref, rd_ref):
    j = pl.program_id(0)
    kv_all = jnp.concatenate([kvp_ref[...], kvc_ref[...], kvm_ref[...],
                              jnp.zeros((META_PAD, 2 * D_KV), F32)], axis=0)
    k_all = kv_all[:, :D_KV]
    v_all = kv_all[:, D_KV:]
    k_rot = pltpu.roll(k_all, HEAD_DIM, 1)
    v_rot = pltpu.roll(v_all, HEAD_DIM, 1)
    lo_half = lax.broadcasted_iota(I32, k_all.shape, 1) < HEAD_DIM

    def halves(own_lanes, other_lanes, g):
        src_lo, src_hi = (own_lanes, other_lanes) if g == 0 else (other_lanes, own_lanes)
        return (jnp.where(lo_half, src_lo, 0.0).astype(BF16), jnp.where(lo_half, 0.0, src_hi).astype(BF16))

    r = lax.broadcasted_iota(I32, (BLK, BLK), 0)
    c = lax.broadcasted_iota(I32, (BLK, BLK), 1)
    own = r >= c
    dist_loc = jnp.where(own, r - c, r - c + BLK).astype(F32)
    d_min = jnp.where(j >= 2, -BLK, 0)
    c_min = jnp.where(j == 0, META_PAD, 0)
    mask_loc = jnp.where((r - c >= d_min) & (c >= c_min), 0.0, NEG_INF)
    meta_hi = jnp.where(j >= 1, N_META, 0)
    dist_meta = jnp.where(c < meta_hi, (N_META + (j - 1) * BLK + r - c).astype(F32), jnp.inf)
    slabs = GQA // 2
    nkeys = 3 * BLK
    for g in range(N_KV_HEADS):
        k_even, k_odd = halves(k_all, k_rot, g)
        v_even, v_odd = halves(v_all, v_rot, g)
        qg = jnp.concatenate([q_ref[:, (g * slabs + i) * LANES:(g * slabs + i + 1) * LANES]
                              for i in range(slabs)], axis=0) * (HEAD_DIM ** -0.5)
        s_ref[...] = _dot_nt(qg, jnp.concatenate([k_even, k_odd], axis=0))
        for i in range(slabs):
            rows = slice(i * BLK, (i + 1) * BLK)
            for par in range(2):
                h = g * GQA + 2 * i + par
                slope = 2.0 ** (-8.0 * (h + 1) / N_Q_HEADS)
                sink = sink_ref[h]
                k0 = par * nkeys
                s_loc = (jnp.where(own, s_ref[rows, k0 + BLK:k0 + 2 * BLK], s_ref[rows, k0:k0 + BLK])
                         - slope * dist_loc + mask_loc)
                s_met = s_ref[rows, k0 + 2 * BLK:k0 + 3 * BLK] - slope * dist_meta
                m = jnp.maximum(jnp.max(jnp.maximum(s_loc, s_met), axis=-1, keepdims=True), sink)
                p_loc = jnp.exp(s_loc - m)
                p_met = jnp.exp(s_met - m)
                den = jnp.sum(p_loc + p_met, axis=-1, keepdims=True) + jnp.exp(sink - m)
                pr_ref[par, rows, 0:BLK] = jnp.where(own, 0.0, p_loc).astype(BF16)
                pr_ref[par, rows, BLK:2 * BLK] = jnp.where(own, p_loc, 0.0).astype(BF16)
                pr_ref[par, rows, 2 * BLK:] = p_met.astype(BF16)
                rd_ref[par, rows, :] = jnp.broadcast_to(1.0 / den, (BLK, LANES))
        o = (jnp.dot(pr_ref[0], v_even, preferred_element_type=F32) * rd_ref[0]
             + jnp.dot(pr_ref[1], v_odd, preferred_element_type=F32) * rd_ref[1])
        for i in range(slabs):
            o_ref[:, (g * slabs + i) * LANES:(g * slabs + i + 1) * LANES] = (
                o[i * BLK:(i + 1) * BLK, :].astype(o_ref.dtype))

    carry = jnp.where(j == 0, 0.0, seq_ref[...])
    s0 = jnp.concatenate([carry, p_ref[...]], axis=0)
    s1 = s0 + pltpu.roll(s0, 1, 0)
    s2 = s1 + pltpu.roll(s1, 2, 0)
    s3 = s2 + pltpu.roll(s2, 4, 0)
    s4 = s3 + pltpu.roll(s3, 8, 0)
    rows = N_META + BLK
    col = lax.broadcasted_iota(I32, (rows, D_POOL), 1)
    row = lax.broadcasted_iota(I32, (rows, D_POOL), 0) - N_META
    wsum = jnp.where(col < POOL_GROUP_DIM, s1,
                     jnp.where(col < 2 * POOL_GROUP_DIM, s2,
                               jnp.where(col < 3 * POOL_GROUP_DIM, s3, s4)))
    wlen = jnp.where(col < POOL_GROUP_DIM, 2,
                     jnp.where(col < 2 * POOL_GROUP_DIM, 4,
                               jnp.where(col < 3 * POOL_GROUP_DIM, 8, 16)))
    pos = row + jnp.where(j == 0, -META_PAD, N_META + (j - 1) * BLK)
    cnt = jnp.clip(jnp.minimum(pos + 1, wlen), 1, 16).astype(F32)
    d = (wsum / cnt - s0)[N_META:, :]
    o_ref[:, D_ATTN:] = (_dot(d, wp_ref[...]) * ps_ref[...]).astype(o_ref.dtype)
    seq_ref[...] = p_ref[BLK - N_META:, :]


def _mix0_prompt(q, kv, p, sinks, wpool_bd, pool_scale, t_rows, meta_blk):
    n = q.shape[0]
    nblk = t_rows // BLK

    def cur(j):
        return jnp.where(j == 0, meta_blk, j - 1)

    meta16 = (meta_blk * BLK + META_PAD) // N_META
    return pl.pallas_call(
        functools.partial(_mix0_prompt_kernel, nblk=nblk),
        grid=(meta_blk + 1,),
        in_specs=[pl.BlockSpec(memory_space=pltpu.SMEM),
                  pl.BlockSpec((BLK, D_ATTN), lambda j: (cur(j), 0)),
                  pl.BlockSpec((BLK, 2 * D_KV), lambda j: (cur(j), 0)),
                  pl.BlockSpec((BLK, 2 * D_KV), lambda j: (jnp.maximum(j - 2, 0), 0)),
                  pl.BlockSpec((N_META, 2 * D_KV), lambda j: (meta16, 0)),
                  pl.BlockSpec((BLK, D_POOL), lambda j: (cur(j), 0)),
                  pl.BlockSpec((D_POOL, D_POOL), lambda j: (0, 0)),
                  pl.BlockSpec((1, D_POOL), lambda j: (0, 0))],
        out_specs=pl.BlockSpec((BLK, D_MIX_EVEN), lambda j: (cur(j), 0)),
        out_shape=jax.ShapeDtypeStruct((n, D_MIX_EVEN), BF16),
        scratch_shapes=[pltpu.VMEM((N_META, D_POOL), F32),
                        pltpu.VMEM((GQA // 2 * BLK, 2 * 3 * BLK), F32),
                        pltpu.VMEM((2, GQA // 2 * BLK, 3 * BLK), BF16),
                        pltpu.VMEM((2, GQA // 2 * BLK, LANES), F32)],
        compiler_params=_cparams("arbitrary"),
        name="mix0_prompt",
    )(sinks, q, kv, kv, kv, p, wpool_bd, pool_scale.reshape(1, D_POOL))


def _attn_sample_kernel(sink_ref, q_ref, ck_ref, cv_ref, kn_ref, vn_ref, o_ref, *, n_new):
    nk = N_META + WINDOW + n_new
    nq = n_new * GQA
    k_all = jnp.concatenate([ck_ref[...], kn_ref[...]], axis=1).astype(BF16)
    v_all = jnp.concatenate([cv_ref[...], vn_ref[...]], axis=1).astype(BF16)
    r = lax.broadcasted_iota(I32, (nq, nk), 0)
    slot = lax.broadcasted_iota(I32, (nq, nk), 1)
    t = r // GQA
    hh = r - t * GQA
    kpos = jnp.where(slot < N_META, slot,
                     jnp.where(slot < N_META + WINDOW, PAST_LEN - WINDOW - N_META + slot,
                               PAST_LEN - N_META - WINDOW + slot))
    dist = PAST_LEN + t - kpos
    allowed = (dist >= 0) & ((slot < N_META) | (dist < WINDOW))
    distf = dist.astype(F32)
    hh1 = hh[:, :1]
    for g in range(N_KV_HEADS):
        slope = jnp.exp2(-8.0 * (g * GQA + hh + 1).astype(F32) / N_Q_HEADS)
        sink = jnp.zeros((nq, 1), F32)
        for i in range(GQA):
            sink = jnp.where(hh1 == i, sink_ref[g * GQA + i], sink)
        kg = k_all[:, :, g * HEAD_DIM:(g + 1) * HEAD_DIM]
        vg = v_all[:, :, g * HEAD_DIM:(g + 1) * HEAD_DIM]
        s = jnp.einsum("bqd,bkd->bqk", q_ref[:, g].astype(BF16), kg,
                       preferred_element_type=F32) * (HEAD_DIM ** -0.5)
        s = jnp.where(allowed[None], s - (slope * distf)[None], NEG_INF)
        sink = sink[None]
        m = jnp.maximum(jnp.max(s, axis=-1, keepdims=True), sink)
        pr = jnp.exp(s - m)
        den = jnp.sum(pr, axis=-1, keepdims=True) + jnp.exp(sink - m)
        o = jnp.einsum("bqk,bkd->bqd", pr.astype(BF16), vg, preferred_element_type=F32)
        o_ref[:, g] = o / den


def _attn_sample(qs, ck, cv, kn, vn, sinks):
    b, _, nq, _ = qs.shape
    n_new = kn.shape[1]
    bs = 16 if b % 16 == 0 else b
    nc = N_META + WINDOW
    return pl.pallas_call(
        functools.partial(_attn_sample_kernel, n_new=n_new),
        grid=(b // bs,),
        in_specs=[pl.BlockSpec(memory_space=pltpu.SMEM),
                  pl.BlockSpec((bs, N_KV_HEADS, nq, HEAD_DIM), lambda i: (i, 0, 0, 0)),
                  pl.BlockSpec((bs, nc, D_KV), lambda i: (i, 0, 0)),
                  pl.BlockSpec((bs, nc, D_KV), lambda i: (i, 0, 0)),
                  pl.BlockSpec((bs, n_new, D_KV), lambda i: (i, 0, 0)),
                  pl.BlockSpec((bs, n_new, D_KV), lambda i: (i, 0, 0))],
        out_specs=pl.BlockSpec((bs, N_KV_HEADS, nq, HEAD_DIM), lambda i: (i, 0, 0, 0)),
        out_shape=jax.ShapeDtypeStruct(qs.shape, F32),
        compiler_params=_cparams("parallel"),
        name="attn_sample",
    )(sinks, qs, ck, cv, kn, vn)


def _pool_sample_kernel(buf_ref, p_ref, wp_ref, ps_ref, o_ref, *, n_new):
    seq = [buf_ref[i] for i in range(POOL_BUF)] + [p_ref[i] for i in range(n_new)]
    b = p_ref.shape[1]
    col = lax.broadcasted_iota(I32, (b, D_POOL), 1)
    for t in range(n_new):
        cur = seq[POOL_BUF + t]
        acc = cur
        sums = {}
        for back in range(1, max(POOL_WINDOWS)):
            acc = acc + seq[POOL_BUF + t - back]
            if back + 1 in POOL_WINDOWS:
                sums[back + 1] = acc
        mean = jnp.where(col < POOL_GROUP_DIM, sums[2] / 2.0,
                         jnp.where(col < 2 * POOL_GROUP_DIM, sums[4] / 4.0,
                                   jnp.where(col < 3 * POOL_GROUP_DIM, sums[8] / 8.0,
                                             sums[16] / 16.0)))
        o_ref[t] = _dot(mean - cur, wp_ref[...]) * ps_ref[...]


def _pool_sample(buf_tm, p_tm, wpool_bd, pool_scale):
    n_new = p_tm.shape[0]
    return pl.pallas_call(
        functools.partial(_pool_sample_kernel, n_new=n_new),
        out_shape=jax.ShapeDtypeStruct(p_tm.shape, F32),
        compiler_params=pltpu.CompilerParams(vmem_limit_bytes=VMEM_LIMIT),
        name="pool_sample",
    )(buf_tm, p_tm, wpool_bd, pool_scale.reshape(1, D_POOL))


def _outproj_route_kernel(*refs, valid_lo, valid_mid, n_x, n_first):
    tt = TT
    x_refs = refs[:n_x]
    (mix_ref, mixs_ref, w_ref, g_ref, wrh_ref, wrl_ref, br_ref,
     x1_ref, xn_ref, info_ref, infot_ref, cnt_ref, acc_ref) = refs[n_x:]
    i = pl.program_id(0)
    is_sample_tile = i == n_first
    acc = jnp.where(i == 0, 0.0, acc_ref[...])
    mix = mix_ref[...] + jnp.where(is_sample_tile, mixs_ref[...], 0).astype(mix_ref.dtype)
    x1 = _pick_rows(x_refs, n_first) + _dot(mix, w_ref[...])
    x1_ref[...] = x1
    xn = _rms(x1, g_ref[...])
    for cc_ in range(D_MODEL // LANES):
        xn_ref[_slab(cc_, tt), :] = xn[:, cc_ * LANES:(cc_ + 1) * LANES]
    hi = xn.astype(BF16)
    lo = (xn - hi.astype(F32)).astype(BF16)
    hi_out = jnp.dot(hi, jnp.concatenate([wrh_ref[...], wrl_ref[...]], axis=1), preferred_element_type=F32)
    logits = (hi_out[:, :LANES] + jnp.dot(lo, wrh_ref[...], preferred_element_type=F32)
              + hi_out[:, LANES:]) + br_ref[...]
    lane = lax.broadcasted_iota(I32, (tt, LANES), 1)
    lanef = lane.astype(F32)
    lg = jnp.where(lane < MOE_GROUPS, logits, NEG_INF)
    mg = jnp.max(lg, axis=-1, keepdims=True)
    gstar = jnp.min(jnp.where(lg == mg, lanef, 1e9), axis=-1, keepdims=True)
    pgroup = 1.0 / jnp.sum(jnp.exp(lg - mg), axis=-1, keepdims=True)
    lo_lane = MOE_GROUPS + gstar * EXPERTS_PER_GROUP
    le = jnp.where((lanef >= lo_lane) & (lanef < lo_lane + EXPERTS_PER_GROUP), logits, NEG_INF)
    v1 = jnp.max(le, axis=-1, keepdims=True)
    i1 = jnp.min(jnp.where(le == v1, lanef, 1e9), axis=-1, keepdims=True)
    le2 = jnp.where(lanef == i1, NEG_INF, le)
    v2 = jnp.max(le2, axis=-1, keepdims=True)
    i2 = jnp.min(jnp.where(le2 == v2, lanef, 1e9), axis=-1, keepdims=True)
    tq = jnp.exp(v2 - v1)
    w1 = pgroup / (1.0 + tq)
    w2 = pgroup * tq / (1.0 + tq)
    e1 = i1 - MOE_GROUPS
    e2 = i2 - MOE_GROUPS
    rowg = i * tt + lax.broadcasted_iota(I32, (tt, LANES), 0)
    valid = (rowg < valid_lo) | (rowg >= valid_mid)
    oh1 = jnp.where(valid & (lanef == e1), 1.0, 0.0)
    oh2 = jnp.where(valid & (lanef == e2), 1.0, 0.0)
    both = oh1 + oh2
    rr = lax.broadcasted_iota(I32, (tt, tt), 0)
    cc = lax.broadcasted_iota(I32, (tt, tt), 1)
    tril = jnp.where(cc < rr, 1.0, 0.0).astype(BF16)
    base = acc + jnp.dot(tril, both.astype(BF16), preferred_element_type=F32)
    rank1 = jnp.sum(oh1 * base, axis=-1, keepdims=True)
    rank2 = jnp.sum(oh2 * base, axis=-1, keepdims=True)
    info = jnp.where(lane == 0, e1, jnp.where(lane == 1, e2, jnp.where(lane == 2, w1, jnp.where(
        lane == 3, w2, jnp.where(lane == 4, rank1, jnp.where(lane == 5, rank2, 0.0))))))
    info_ref[...] = info
    infot_ref[0] = info.T[:SUB, :]
    total = acc + jnp.sum(both, axis=0, keepdims=True)
    acc_ref[...] = total
    cnt_ref[...] = jnp.broadcast_to(total, (8, LANES))


def _outproj_route(xs, mix, mix_s, w_out_bf16, g_ffn, wr_hi, wr_lo, br, t_rows, valid_lo, valid_mid):
    n = mix.shape[0]
    ns, dm = mix_s.shape
    mix_s = jnp.pad(mix_s, ((0, TT - ns), (0, 0))).astype(mix.dtype)
    nt = n // TT

    def const2(i):
        return (0, 0)

    return pl.pallas_call(
        functools.partial(_outproj_route_kernel, valid_lo=valid_lo, valid_mid=valid_mid,
                          n_x=len(xs), n_first=t_rows // TT),
        grid=(nt,),
        in_specs=_row_specs(xs, D_MODEL) + [
            pl.BlockSpec((TT, dm), lambda i: (i, 0)),
            pl.BlockSpec((TT, dm), const2),
            pl.BlockSpec((dm, D_MODEL), const2),
            pl.BlockSpec((1, D_MODEL), const2),
            pl.BlockSpec((D_MODEL, LANES), const2),
            pl.BlockSpec((D_MODEL, LANES), const2),
            pl.BlockSpec((1, LANES), const2)],
        out_specs=[pl.BlockSpec((TT, D_MODEL), lambda i: (i, 0)),
                   pl.BlockSpec((TT * SUB, LANES), lambda i: (i, 0)),
                   pl.BlockSpec((TT, LANES), lambda i: (i, 0)),
                   pl.BlockSpec((1, SUB, TT), lambda i: (i, 0, 0)),
                   pl.BlockSpec((8, LANES), const2)],
        out_shape=[jax.ShapeDtypeStruct((n, D_MODEL), F32),
                   jax.ShapeDtypeStruct((n * SUB, LANES), F32),
                   jax.ShapeDtypeStruct((n, LANES), F32),
                   jax.ShapeDtypeStruct((nt, SUB, TT), F32),
                   jax.ShapeDtypeStruct((8, LANES), F32)],
        scratch_shapes=[pltpu.VMEM((1, LANES), F32)],
        compiler_params=_cparams("arbitrary"),
        name="outproj_route",
    )(*xs, mix, mix_s, w_out_bf16, g_ffn.reshape(1, D_MODEL), wr_hi, wr_lo, br)


ROWS_PER_ISSUE = 8


def _issue_rows(lo, hi, row_copy):
    assert (hi - lo) % ROWS_PER_ISSUE == 0

    def body(it, c):
        for u in range(ROWS_PER_ISSUE):
            r = lo + it * ROWS_PER_ISSUE + u
            row_copy(r, 0).start(priority=0)
            row_copy(r, 1).start(priority=1)
        return c

    lax.fori_loop(0, (hi - lo) // ROWS_PER_ISSUE, body, 0)


def _dispatch_kernel(pos_ref, zt_ref, xn_ref, xs_ref, zbuf, sem, zsem, *, n_full, ranges, n_tiles):
    i = pl.program_id(0)

    @pl.when(i == 0)
    def _():
        zbuf[...] = jnp.zeros(zbuf.shape, F32)

        def zero_copy(t):
            rows = MOE_TM * SUB
            return pltpu.make_async_copy(zbuf, xs_ref.at[pl.ds(pl.multiple_of(t * rows, rows), rows)], zsem)

        def zstart(t, c):
            @pl.when(zt_ref[t] != 0)
            def _():
                zero_copy(t).start()
            return c

        def zwait(t, c):
            @pl.when(zt_ref[t] != 0)
            def _():
                zero_copy(t).wait()
            return c

        lax.fori_loop(0, n_tiles, zstart, 0)
        lax.fori_loop(0, n_tiles, zwait, 0)

    def row_copy(r, k):
        return pltpu.make_async_copy(xn_ref.at[_tile_of(r)], xs_ref.at[_tile_of(pos_ref[0, 0, k * TT + r])], sem)

    def scatter(lo, hi):
        _issue_rows(lo, hi, row_copy)
        nrow = (hi - lo) * SUB
        for _ in range(2):
            pltpu.make_async_copy(xn_ref.at[pl.ds(lo * SUB, nrow)], xs_ref.at[pl.ds(0, nrow)], sem).wait()

    _for_tile_rows(i, n_full, ranges, scatter)


def _dispatch(xn, pos, zero_tiles, n_sorted, t_rows, ranges):
    n = xn.shape[0] // SUB
    return pl.pallas_call(
        functools.partial(_dispatch_kernel, n_full=t_rows // TT, ranges=ranges,
                          n_tiles=n_sorted // MOE_TM),
        grid=(n // TT,),
        in_specs=[pl.BlockSpec((1, 1, 2 * TT), lambda i: (i, 0, 0), memory_space=pltpu.SMEM),
                  pl.BlockSpec(memory_space=pltpu.SMEM),
                  pl.BlockSpec((TT * SUB, LANES), lambda i: (i, 0))],
        out_specs=pl.BlockSpec(memory_space=pl.ANY),
        out_shape=jax.ShapeDtypeStruct((n_sorted * SUB, LANES), F32),
        scratch_shapes=[pltpu.VMEM((MOE_TM * SUB, LANES), F32),
                        pltpu.SemaphoreType.DMA(()),
                        pltpu.SemaphoreType.DMA(())],
        compiler_params=_cparams("arbitrary"),
        name="moe_dispatch",
    )(pos, zero_tiles, xn)


def _expert_ffn_kernel(te_ref, nu_ref, xs_ref, wg_ref, wu_ref, wd_ref, ys_ref, wgb, wub, wdb):
    j = pl.program_id(0)
    changed = (j == 0) | (te_ref[j] != te_ref[jnp.maximum(j - 1, 0)])

    @pl.when((j < nu_ref[0]) & changed)
    def _():
        wgb[...] = wg_ref[0, 0].astype(BF16)
        wub[...] = wu_ref[0, 0].astype(BF16)
        wdb[...] = wd_ref[0, 0].astype(BF16)

    n_slab = D_MODEL // LANES

    @pl.when(j < nu_ref[0])
    def _():
        for part in range(MOE_TM // FFN_ROWS):
            base = part * FFN_ROWS * SUB
            x = jnp.concatenate([xs_ref[_slab(c, FFN_ROWS, base), :] for c in range(n_slab)],
                                axis=1).astype(BF16)
            hg = jnp.dot(x, wgb[...], preferred_element_type=F32)
            hu = jnp.dot(x, wub[...], preferred_element_type=F32)
            h = (_silu(hg) * hu).astype(BF16)
            y = jnp.dot(h, wdb[...], preferred_element_type=F32)
            for c in range(n_slab):
                ys_ref[_slab(c, FFN_ROWS, base), :] = y[:, c * LANES:(c + 1) * LANES]

    @pl.when(j >= nu_ref[0])
    def _():
        ys_ref[...] = jnp.zeros(ys_ref.shape, F32)


def _expert_ffn(xs, tile_expert, n_used, w_gate, w_up, w_down, layer):
    n_sorted = xs.shape[0] // SUB
    nt = n_sorted // MOE_TM

    def row_map(j, te, nu):
        return (jnp.minimum(j, nu[0] - 1), 0)

    def w_map(j, te, nu):
        return (layer, te[j], 0, 0)

    return pl.pallas_call(
        _expert_ffn_kernel,
        grid_spec=pltpu.PrefetchScalarGridSpec(
            num_scalar_prefetch=2,
            grid=(nt,),
            in_specs=[pl.BlockSpec((MOE_TM * SUB, LANES), row_map),
                      pl.BlockSpec((1, 1, D_MODEL, D_EXPERT), w_map),
                      pl.BlockSpec((1, 1, D_MODEL, D_EXPERT), w_map),
                      pl.BlockSpec((1, 1, D_EXPERT, D_MODEL), w_map)],
            out_specs=pl.BlockSpec((MOE_TM * SUB, LANES), lambda j, te, nu: (j, 0)),
            scratch_shapes=[pltpu.VMEM((D_MODEL, D_EXPERT), BF16),
                            pltpu.VMEM((D_MODEL, D_EXPERT), BF16),
                            pltpu.VMEM((D_EXPERT, D_MODEL), BF16)]),
        out_shape=jax.ShapeDtypeStruct((n_sorted * SUB, LANES), F32),
        compiler_params=_cparams("arbitrary"),
        name="expert_ffn",
    )(tile_expert, n_used, xs, w_gate, w_up, w_down)


def _combine_kernel(posc_ref, posn_ref, x1_ref, info_ref, g_ref, ys_ref, *rest,
                    n_full, n_tiles, ranges, valid_lo, valid_mid, final_norm):
    out_refs, (ybuf, sem) = rest[:-2], rest[-2:]
    i = pl.program_id(0)
    slot = lax.rem(i, 2)

    def buf_base(s, k):
        return (s * 2 + k) * (TT * SUB)

    def gather(pos_ref, dst_slot):
        def row_copy(r, k):
            dst = pl.ds(pl.multiple_of(buf_base(dst_slot, k) + r * SUB, SUB), SUB)
            return pltpu.make_async_copy(ys_ref.at[_tile_of(pos_ref[0, 0, k * TT + r])],
                                         ybuf.at[dst], sem.at[dst_slot])
        return lambda lo, hi: _issue_rows(lo, hi, row_copy)

    @pl.when(i == 0)
    def _():
        ybuf[...] = jnp.zeros(ybuf.shape, F32)
        _for_tile_rows(i, n_full, ranges, gather(posc_ref, 0))

    @pl.when(i + 1 < n_tiles)
    def _():
        _for_tile_rows(i + 1, n_full, ranges, gather(posn_ref, 1 - slot))

    def drain(lo, hi):
        nrow = (hi - lo) * SUB
        for k in range(2):
            dst = pl.ds(pl.multiple_of(buf_base(slot, k) + lo * SUB, SUB), nrow)
            pltpu.make_async_copy(ys_ref.at[pl.ds(0, nrow)], ybuf.at[dst], sem.at[slot]).wait()

    _for_tile_rows(i, n_full, ranges, drain)

    info = info_ref[...]
    w1 = info[:, 2:3]
    w2 = info[:, 3:4]
    rowg = i * TT + lax.broadcasted_iota(I32, (TT, 1), 0)
    valid = (rowg < valid_lo) | (rowg >= valid_mid)
    outs = []
    for c in range(D_MODEL // LANES):
        y1 = ybuf[_slab(c, TT, buf_base(slot, 0)), :]
        y2 = ybuf[_slab(c, TT, buf_base(slot, 1)), :]
        outs.append(x1_ref[:, c * LANES:(c + 1) * LANES] + jnp.where(valid, w1 * y1 + w2 * y2, 0.0))
    out = jnp.concatenate(outs, axis=1)
    if not final_norm:
        out_refs[0][...] = out
    else:
        out = _rms(out, g_ref[...])

        @pl.when(i < n_full)
        def _():
            out_refs[0][...] = out

        @pl.when(i >= n_full)
        def _():
            out_refs[1][...] = out


def _combine(x1, info, pos, ys, g_final, t_rows, ranges, valid_lo, valid_mid, final_norm):
    n = x1.shape[0]
    nt = n // TT
    n_full = t_rows // TT
    if final_norm:
        out_specs = [pl.BlockSpec((TT, D_MODEL), lambda i: (jnp.minimum(i, n_full - 1), 0)),
                     pl.BlockSpec((TT, D_MODEL), lambda i: (jnp.maximum(i - n_full, 0), 0))]
        out_shape = [jax.ShapeDtypeStruct((t_rows, D_MODEL), F32),
                     jax.ShapeDtypeStruct((n - t_rows, D_MODEL), F32)]
    else:
        out_specs = [pl.BlockSpec((TT, D_MODEL), lambda i: (i, 0))]
        out_shape = [jax.ShapeDtypeStruct((n, D_MODEL), F32)]
    return pl.pallas_call(
        functools.partial(_combine_kernel, n_full=n_full, n_tiles=nt, ranges=ranges,
                          valid_lo=valid_lo, valid_mid=valid_mid, final_norm=final_norm),
        grid=(nt,),
        in_specs=[pl.BlockSpec((1, 1, 2 * TT), lambda i: (i, 0, 0), memory_space=pltpu.SMEM),
                  pl.BlockSpec((1, 1, 2 * TT), lambda i: (jnp.minimum(i + 1, nt - 1), 0, 0),
                               memory_space=pltpu.SMEM),
                  pl.BlockSpec((TT, D_MODEL), lambda i: (i, 0)),
                  pl.BlockSpec((TT, LANES), lambda i: (i, 0)),
                  pl.BlockSpec((1, D_MODEL), lambda i: (0, 0)),
                  pl.BlockSpec(memory_space=pl.ANY)],
        out_specs=out_specs,
        out_shape=out_shape,
        scratch_shapes=[pltpu.VMEM((2 * 2 * TT * SUB, LANES), F32),
                        pltpu.SemaphoreType.DMA((2,))],
        compiler_params=_cparams("arbitrary"),
        name="moe_combine",
    )(pos, pos, x1, info, g_final.reshape(1, D_MODEL), ys)


def _moe(x1, xn, info, infot, cnt, w_gate, w_up, w_down, layer, g_final, t_rows, ns, final_norm):
    n = x1.shape[0]
    valid_lo, valid_mid = t_rows + ns, n - N_META
    ranges = _small_tile_ranges(t_rows, ns, n)
    n_valid = valid_lo + N_META
    nt = (2 * n_valid + N_EXPERTS * (MOE_TM - 1)) // MOE_TM
    counts = cnt[0, :N_EXPERTS].astype(I32)
    tiles = (counts + MOE_TM - 1) // MOE_TM
    tile_end = jnp.cumsum(tiles)
    row_start = (tile_end - tiles) * MOE_TM
    n_used = tile_end[-1:]
    experts = jnp.arange(N_EXPERTS, dtype=I32)

    def slot_rows(e, rank):
        start = jnp.sum(jnp.where(e[..., None] == experts, row_start, 0), axis=-1)
        return start + rank

    pos = jnp.concatenate([slot_rows(infot[:, 0].astype(I32), infot[:, 4].astype(I32)),
                           slot_rows(infot[:, 1].astype(I32), infot[:, 5].astype(I32))], axis=-1)
    rows = jnp.arange(n).reshape(n // TT, TT)
    valid = (rows < valid_lo) | (rows >= valid_mid)
    pos = jnp.where(jnp.concatenate([valid, valid], axis=-1), pos, 0)[:, None, :]
    tile_ids = jnp.arange(nt, dtype=I32)
    tile_expert = jnp.sum(tile_end[None, :] <= jnp.minimum(tile_ids, n_used[0] - 1)[:, None],
                          axis=1).astype(I32)
    is_last = jnp.any((tile_ids[:, None] == tile_end[None, :] - 1) & (tiles[None, :] > 0), axis=1)
    zero_tiles = (is_last | (tile_ids >= n_used[0])).astype(I32)
    xs = _dispatch(xn, pos, zero_tiles, nt * MOE_TM, t_rows, ranges)
    ys = _expert_ffn(xs, tile_expert, n_used.astype(I32), w_gate, w_up, w_down, layer)
    return _combine(x1, info, pos, ys, g_final, t_rows, ranges, valid_lo, valid_mid, final_norm)


def _shift_rows(x, k, fill):
    if k % SUB == 0:
        return jnp.concatenate([jnp.full((k, x.shape[1]), fill, x.dtype), x[:x.shape[0] - k]], axis=0)
    row = lax.broadcasted_iota(I32, x.shape, 0)
    return jnp.where(row >= k, pltpu.roll(x, k, 0), fill)


def _lru_gates(xc, wgate_ref, bgate_ref, lam_sp):
    gates = _dot(xc, wgate_ref[...]) + bgate_ref[...]
    r = _sigmoid(gates[:, :D_LRU])
    ig = _sigmoid(gates[:, D_LRU:])
    log_a = -LRU_C * r * lam_sp
    a = jnp.exp(log_a)
    bx = jnp.sqrt(-jnp.tanh(log_a) * (a * a + 1.0)) * (ig * xc)
    return a, bx


def _conv_block(seq_ref, u_ref, w_ref, b_ref, j):
    u = u_ref[...]
    seq = jnp.concatenate([jnp.where(j == 0, 0.0, seq_ref[...]), u], axis=0)
    out = b_ref[...] + u * w_ref[CONV_WIDTH - 1:CONV_WIDTH, :]
    for back in range(1, CONV_WIDTH):
        tap = CONV_WIDTH - 1 - back
        out = out + pltpu.roll(seq, back, 0)[8:, :] * w_ref[tap:tap + 1, :]
    seq_ref[...] = u[BLK - 8:, :]
    return out


def _mix1_prompt_kernel(*refs, nblk):
    j = pl.program_id(0)
    o_ref = refs[16]

    @pl.when(j <= nblk)
    def _():
        _mix1_prompt_step(*refs)

    @pl.when(j > nblk)
    def _():
        o_ref[...] = jnp.zeros(o_ref.shape, o_ref.dtype)


def _mix1_prompt_step(cx_ref, cg_ref, z_ref, xbc_ref, dt_ref,
                      cwc_ref, cbc_ref, wgate_ref, bgate_ref, lam_ref,
                      cwd_ref, cbd_ref, dtb_ref, alog_ref, dskip_ref, nw_ref,
                      o_ref, lruh_ref, ssdh_ref,
                      seqc_ref, seqd_ref, hl_ref, hs_ref):
    j = pl.program_id(0)
    rowc = lax.broadcasted_iota(I32, (BLK, 1), 0)
    live = rowc >= jnp.where(j >= 1, 0, META_PAD)

    xc = _conv_block(seqc_ref, cx_ref, cwc_ref, cbc_ref, j)
    a, bx = _lru_gates(xc, wgate_ref, bgate_ref, _softplus(-lam_ref[...]))
    a = jnp.where(live, a, 1.0)
    bx = jnp.where(live, bx, 0.0)
    k = 1
    while k < BLK:
        bx = a * _shift_rows(bx, k, 0.0) + bx
        a = a * _shift_rows(a, k, 1.0)
        k *= 2
    h = a * jnp.where(j == 0, 0.0, hl_ref[...]) + bx
    hl_ref[...] = h[BLK - 1:BLK, :]
    lruh_ref[...] = h[BLK - 1:BLK, :]
    o_ref[:, :D_LRU] = (h * _gelu_tanh(cg_ref[...])).astype(o_ref.dtype)

    xbc = _silu(_conv_block(seqd_ref, xbc_ref, cwd_ref, cbd_ref, j))
    xs = xbc[:, :D_SSD]
    dt = jnp.where(live, _softplus(dt_ref[...] + dtb_ref[...]), 0.0)
    a_neg = -jnp.exp(alog_ref[...])
    acum = dt * a_neg
    k = 1
    while k < BLK:
        acum = acum + _shift_rows(acum, k, 0.0)
        k *= 2
    acum_t = acum.T
    dt_t = dt.T
    xs_t = xs.T
    tri = (lax.broadcasted_iota(I32, (BLK, BLK), 0) >= lax.broadcasted_iota(I32, (BLK, BLK), 1))
    rep = SSD_HEADS // SSD_GROUPS
    ys = []
    for g in range(SSD_GROUPS):
        bg = xbc[:, D_SSD + g * SSD_STATE:D_SSD + (g + 1) * SSD_STATE]
        cgm = xbc[:, D_SSD + (SSD_GROUPS + g) * SSD_STATE:D_SSD + (SSD_GROUPS + g + 1) * SSD_STATE]
        cb = _dot_nt(cgm, bg)
        for hh in range(rep):
            hd = g * rep + hh
            col = acum[:, hd:hd + 1]
            rowv = acum_t[hd:hd + 1, :]
            decay = jnp.exp(jnp.where(tri, col - rowv, NEG_INF))
            scores = cb * decay * dt_t[hd:hd + 1, :]
            xh = xs[:, hd * SSD_HEAD_DIM:(hd + 1) * SSD_HEAD_DIM]
            y = _dot(scores, xh)
            hprev = jnp.where(j == 0, 0.0, hs_ref[hd])
            y = y + jnp.exp(col) * _dot_nt(cgm, hprev)
            last = acum_t[hd:hd + 1, BLK - 1:BLK]
            w_end = jnp.exp(last - rowv) * dt_t[hd:hd + 1, :]
            xw_t = xs_t[hd * SSD_HEAD_DIM:(hd + 1) * SSD_HEAD_DIM, :] * w_end
            hnew = jnp.exp(last) * hprev + _dot(xw_t, bg)
            hs_ref[hd] = hnew
            ssdh_ref[hd] = hnew
            ys.append(y + dskip_ref[:, hd * SSD_HEAD_DIM:(hd + 1) * SSD_HEAD_DIM] * xh)
    y = jnp.concatenate(ys, axis=1) * _silu(z_ref[...])
    o_ref[:, D_LRU:] = _rms(y, nw_ref[...]).astype(o_ref.dtype)


def _mix1_prompt(cx, cg, z, xbc, dt, wts, t_rows, meta_blk):
    n = cx.shape[0]
    nblk = t_rows // BLK

    def cur(j):
        return (jnp.where(j == 0, meta_blk, j - 1), 0)

    def const2(j):
        return (0, 0)

    in_specs = [pl.BlockSpec((BLK, D_LRU), cur), pl.BlockSpec((BLK, D_LRU), cur),
                pl.BlockSpec((BLK, D_SSD), cur), pl.BlockSpec((BLK, D_XBC), cur),
                pl.BlockSpec((BLK, LANES), cur)]
    in_specs += [pl.BlockSpec(w.shape, const2) for w in wts]
    return pl.pallas_call(
        functools.partial(_mix1_prompt_kernel, nblk=nblk),
        grid=(meta_blk + 1,),
        in_specs=in_specs,
        out_specs=[pl.BlockSpec((BLK, D_MIX_ODD), cur),
                   pl.BlockSpec((1, D_LRU), const2),
                   pl.BlockSpec((SSD_HEADS, SSD_HEAD_DIM, SSD_STATE), lambda j: (0, 0, 0))],
        out_shape=[jax.ShapeDtypeStruct((n, D_MIX_ODD), BF16),
                   jax.ShapeDtypeStruct((1, D_LRU), F32),
                   jax.ShapeDtypeStruct((SSD_HEADS, SSD_HEAD_DIM, SSD_STATE), F32)],
        scratch_shapes=[pltpu.VMEM((8, D_LRU), F32),
                        pltpu.VMEM((8, D_XBC), F32),
                        pltpu.VMEM((1, D_LRU), F32),
                        pltpu.VMEM((SSD_HEADS, SSD_HEAD_DIM, SSD_STATE), F32)],
        compiler_params=_cparams("arbitrary"),
        name="mix1_prompt",
    )(cx, cg, z, xbc, dt, *wts)


def _conv_step(seq, t, w_ref, b_ref):
    out = b_ref[...]
    for tap in range(CONV_WIDTH):
        out = out + seq[t + tap] * w_ref[tap:tap + 1, :]
    return out


def _lru_sample_kernel(cx_ref, cg_ref, buf_ref, h0_ref, cwc_ref, cbc_ref, wgate_ref, bgate_ref,
                       lam_ref, o_ref, hout_ref, *, n_new):
    seq = [buf_ref[i] for i in range(CONV_WIDTH - 1)] + [cx_ref[i] for i in range(n_new)]
    lam_sp = _softplus(-lam_ref[...])
    h = h0_ref[...]
    for t in range(n_new):
        xc = _conv_step(seq, t, cwc_ref, cbc_ref)
        a, bx = _lru_gates(xc, wgate_ref, bgate_ref, lam_sp)
        h = a * h + bx
        o_ref[t] = h * _gelu_tanh(cg_ref[t])
    hout_ref[...] = h


def _lru_sample(cx_tm, cg_tm, buf_tm, h0, wts):
    n_new = cx_tm.shape[0]
    return pl.pallas_call(
        functools.partial(_lru_sample_kernel, n_new=n_new),
        out_shape=[jax.ShapeDtypeStruct(cx_tm.shape, F32), jax.ShapeDtypeStruct(h0.shape, F32)],
        compiler_params=pltpu.CompilerParams(vmem_limit_bytes=VMEM_LIMIT),
        name="lru_sample",
    )(cx_tm, cg_tm, buf_tm, h0, *wts)


def _split3(x):
    hi = x.astype(BF16)
    r = x - hi.astype(F32)
    mid = r.astype(BF16)
    lo = (r - mid.astype(F32)).astype(BF16)
    return hi, mid, lo


def _ssd_sample_kernel(xbc_ref, dt_ref, z_ref, buf_ref, h0_ref, cwd_ref, cbd_ref, dtb_ref, alog_ref,
                       dskip_ref, nw_ref, o_ref, hout_ref, xc_ref, dec_ref, dtx_ref, y_ref, *, n_new, bs):
    seq = [buf_ref[i] for i in range(CONV_WIDTH - 1)] + [xbc_ref[i] for i in range(n_new)]
    a_neg = -jnp.exp(alog_ref[...])
    hcol = lax.broadcasted_iota(I32, (LANES, D_SSD), 1) // SSD_HEAD_DIM
    expand = jnp.where(lax.broadcasted_iota(I32, (LANES, D_SSD), 0) == hcol, 1.0, 0.0).astype(BF16)

    def widen(v):
        parts = _split3(v)
        return sum(jnp.dot(p, expand, preferred_element_type=F32) for p in parts)

    for t in range(n_new):
        xc = _silu(_conv_step(seq, t, cwd_ref, cbd_ref))
        xc_ref[t] = xc
        dt = _softplus(dt_ref[t] + dtb_ref[...])
        dtw = widen(dt)
        dec_ref[t] = jnp.exp(widen(dt * a_neg))
        dtx_ref[t] = dtw * xc[:, :D_SSD]

    zero_rows = jnp.zeros((LANES - 2 * n_new * SSD_GROUPS, SSD_STATE), F32)
    half = D_SSD // SSD_GROUPS

    def per_seq(b, c):
        rows = [xc_ref[t, pl.ds(b, 1), :] for t in range(n_new)]
        bc = [r[:, D_SSD + k * SSD_STATE:D_SSD + (k + 1) * SSD_STATE]
              for r in rows for k in range(2 * SSD_GROUPS)]
        bc_t = jnp.concatenate(bc + [zero_rows], axis=0).T
        ht = h0_ref[b].T
        for t in range(n_new):
            bcol = [jnp.broadcast_to(bc_t[:, t * 4 + g:t * 4 + g + 1], (SSD_STATE, half))
                    for g in range(SSD_GROUPS)]
            ccol = [jnp.broadcast_to(bc_t[:, t * 4 + SSD_GROUPS + g:t * 4 + SSD_GROUPS + g + 1],
                                     (SSD_STATE, half)) for g in range(SSD_GROUPS)]
            bw = jnp.concatenate(bcol, axis=1)
            cw = jnp.concatenate(ccol, axis=1)
            ht = ht * dec_ref[t, pl.ds(b, 1), :] + bw * dtx_ref[t, pl.ds(b, 1), :]
            y = jnp.sum(cw * ht, axis=0, keepdims=True)
            y_ref[t, pl.ds(b, 1), :] = y + dskip_ref[...] * rows[t][:, :D_SSD]
        hout_ref[b] = ht.T
        return c

    lax.fori_loop(0, bs, per_seq, 0)
    for t in range(n_new):
        o_ref[t] = _rms(y_ref[t] * _silu(z_ref[t]), nw_ref[...])


def _ssd_sample(xbc_tm, dt_tm, z_tm, buf_tm, h0, wts):
    n_new, b, _ = xbc_tm.shape
    bs = 16 if b % 16 == 0 else b
    hp = SSD_HEADS * SSD_HEAD_DIM

    def tm(c):
        return pl.BlockSpec((n_new, bs, c), lambda i: (0, i, 0))

    in_specs = [tm(D_XBC), tm(LANES), tm(D_SSD),
                pl.BlockSpec((CONV_WIDTH - 1, bs, D_XBC), lambda i: (0, i, 0)),
                pl.BlockSpec((bs, hp, SSD_STATE), lambda i: (i, 0, 0))]
    in_specs += [pl.BlockSpec(w.shape, lambda i: (0, 0)) for w in wts]
    return pl.pallas_call(
        functools.partial(_ssd_sample_kernel, n_new=n_new, bs=bs),
        grid=(b // bs,),
        in_specs=in_specs,
        out_specs=[tm(D_SSD), pl.BlockSpec((bs, hp, SSD_STATE), lambda i: (i, 0, 0))],
        out_shape=[jax.ShapeDtypeStruct((n_new, b, D_SSD), F32),
                   jax.ShapeDtypeStruct((b, hp, SSD_STATE), F32)],
        scratch_shapes=[pltpu.VMEM((n_new, bs, D_XBC), F32),
                        pltpu.VMEM((n_new, bs, D_SSD), F32),
                        pltpu.VMEM((n_new, bs, D_SSD), F32),
                        pltpu.VMEM((n_new, bs, D_SSD), F32)],
        compiler_params=_cparams("parallel"),
        name="ssd_sample",
    )(xbc_tm, dt_tm, z_tm, buf_tm, h0, *wts)


def _block_diag(w):
    g, c, _ = w.shape
    eye = jnp.eye(g, dtype=w.dtype)
    return (eye[:, None, :, None] * w[:, :, None, :]).reshape(g * c, g * c)


def _router_weights(w_rg, b_rg, w_re, b_re):
    w = jnp.concatenate([w_rg, w_re], axis=1)
    w = jnp.pad(w, ((0, 0), (0, LANES - w.shape[1])))
    hi = w.astype(BF16)
    lo = (w - hi.astype(F32)).astype(BF16)
    b = jnp.pad(jnp.concatenate([b_rg, b_re]), (0, LANES - MOE_GROUPS - N_EXPERTS)).reshape(1, LANES)
    return hi, lo, b


def kernel(x_prompt, x_sample, cache_swa_k, cache_swa_v, cache_pool, state_lru_conv, state_lru_h, state_ssd_conv, state_ssd_h, meta_tokens, norm_mix, norm_ffn, norm_final, attn_w_in, attn_w_out, attn_sinks, pool_w, pool_scale, rec_w_in, rec_w_out, lru_conv_w, lru_conv_b, lru_w_rg, lru_b_rg, lru_w_ig, lru_b_ig, lru_lambda, ssd_conv_w, ssd_conv_b, ssd_dt_bias, ssd_a_log, ssd_d, ssd_norm_w, moe_w_rg, moe_b_rg, moe_w_re, moe_b_re, moe_w_gate, moe_w_up, moe_w_down):
    t_rows = x_prompt.shape[1]
    b, s_new = x_sample.shape[:2]
    ns = b * s_new
    small = -(-(ns + BLK) // TT) * TT
    n = t_rows + small
    meta_blk = n // BLK - 1
    valid_lo = t_rows + ns
    valid_mid = n - N_META
    assert x_prompt.shape[0] == 1 and t_rows % TT == 0 and ns % BLK == 0 and ns <= TT

    x_small = jnp.concatenate([
        x_sample.transpose(1, 0, 2).reshape(ns, D_MODEL),
        jnp.zeros((small - ns - N_META, D_MODEL), F32),
        meta_tokens.astype(F32)], axis=0)
    xs0 = (x_prompt[0], x_small)

    def sample_tm(a):
        return a[t_rows:t_rows + ns].reshape(s_new, b, a.shape[1])

    q, kv, p = _norm_proj(xs0, norm_mix[0], attn_w_in[0].astype(BF16), (D_ATTN, 2 * D_KV, D_POOL),
                          (BF16, F32, F32))
    wpool_bd = _block_diag(pool_w[0]).astype(BF16)
    mix = _mix0_prompt(q, kv, p, attn_sinks[0], wpool_bd, pool_scale[0], t_rows, meta_blk)
    nq = s_new * GQA
    qs = sample_tm(q).reshape(s_new, b, N_KV_HEADS, GQA, HEAD_DIM).transpose(1, 2, 0, 3, 4)
    qs = qs.reshape(b, N_KV_HEADS, nq, HEAD_DIM)
    kv_s = sample_tm(kv).transpose(1, 0, 2)
    kn, vn = kv_s[:, :, :D_KV], kv_s[:, :, D_KV:]
    ck = cache_swa_k[0].reshape(b, N_META + WINDOW, D_KV)
    cv = cache_swa_v[0].reshape(b, N_META + WINDOW, D_KV)
    o_s = _attn_sample(qs, ck, cv, kn, vn, attn_sinks[0])
    o_s = o_s.reshape(b, N_KV_HEADS, s_new, GQA, HEAD_DIM).transpose(2, 0, 1, 3, 4).reshape(s_new, b, D_ATTN)
    p_s = sample_tm(p)
    pool_buf_tm = cache_pool[0].transpose(1, 0, 2)
    yp_s = _pool_sample(pool_buf_tm, p_s, wpool_bd, pool_scale[0])
    mix_s = jnp.concatenate([o_s, yp_s], axis=-1).reshape(ns, D_MIX_EVEN)
    wr_hi, wr_lo, br = _router_weights(moe_w_rg[0], moe_b_rg[0], moe_w_re[0], moe_b_re[0])
    x1, xn, info, infot, cnt = _outproj_route(xs0, mix, mix_s, attn_w_out[0].astype(BF16), norm_ffn[0],
                                              wr_hi, wr_lo, br, t_rows, valid_lo, valid_mid)
    x, = _moe(x1, xn, info, infot, cnt, moe_w_gate, moe_w_up, moe_w_down, 0, norm_final,
              t_rows, ns, False)

    meta_rows = slice(n - N_META, n)
    last_rows = slice(t_rows - WINDOW, t_rows)
    swa_k_p = jnp.concatenate([kv[meta_rows, :D_KV], kv[last_rows, :D_KV]], axis=0)
    swa_v_p = jnp.concatenate([kv[meta_rows, D_KV:], kv[last_rows, D_KV:]], axis=0)
    swa_k_p = swa_k_p.reshape(1, 1, N_META + WINDOW, N_KV_HEADS, HEAD_DIM)
    swa_v_p = swa_v_p.reshape(1, 1, N_META + WINDOW, N_KV_HEADS, HEAD_DIM)
    pool_p = p[t_rows - POOL_BUF:t_rows].reshape(1, 1, POOL_BUF, D_POOL)
    swa_k_s = jnp.concatenate([ck[:, :N_META], ck[:, N_META + s_new:], kn], axis=1)
    swa_v_s = jnp.concatenate([cv[:, :N_META], cv[:, N_META + s_new:], vn], axis=1)
    swa_k_s = swa_k_s.reshape(1, b, N_META + WINDOW, N_KV_HEADS, HEAD_DIM)
    swa_v_s = swa_v_s.reshape(1, b, N_META + WINDOW, N_KV_HEADS, HEAD_DIM)
    pool_s = jnp.concatenate([cache_pool[0], p_s.transpose(1, 0, 2)], axis=1)[:, -POOL_BUF:][None]

    d_in_odd = rec_w_in.shape[2]
    w_in1 = jnp.pad(rec_w_in[0], ((0, 0), (0, LANES - SSD_HEADS))).astype(BF16)
    assert d_in_odd == 2 * D_LRU + D_SSD + D_XBC + SSD_HEADS
    cx, cg, z, xbc, dt = _norm_proj((x,), norm_mix[1], w_in1, (D_LRU, D_LRU, D_SSD, D_XBC, LANES))
    wgate = jnp.concatenate([_block_diag(lru_w_rg[0]), _block_diag(lru_w_ig[0])], axis=1).astype(BF16)
    bgate = jnp.concatenate([lru_b_rg[0], lru_b_ig[0]]).reshape(1, 2 * D_LRU)
    pad8 = (0, LANES - SSD_HEADS)
    dtb = jnp.pad(ssd_dt_bias[0], pad8).reshape(1, LANES)
    alog = jnp.pad(ssd_a_log[0], pad8).reshape(1, LANES)
    dskip = jnp.repeat(ssd_d[0], SSD_HEAD_DIM).reshape(1, D_SSD)
    lru_w = (lru_conv_w[0], lru_conv_b[0].reshape(1, D_LRU), wgate, bgate, lru_lambda[0].reshape(1, D_LRU))
    ssd_w = (ssd_conv_w[0], ssd_conv_b[0].reshape(1, D_XBC), dtb, alog, dskip, ssd_norm_w[0].reshape(1, D_SSD))
    mix, lru_h_p, ssd_h_p = _mix1_prompt(cx, cg, z, xbc, dt, lru_w + ssd_w, t_rows, meta_blk)
    cx_s, cg_s, z_s, xbc_s, dt_s = (sample_tm(a) for a in (cx, cg, z, xbc, dt))
    yc_s, lru_h_s = _lru_sample(cx_s, cg_s, state_lru_conv[0].transpose(1, 0, 2), state_lru_h[0], lru_w)
    h0 = state_ssd_h[0].reshape(b, SSD_HEADS * SSD_HEAD_DIM, SSD_STATE)
    yd_s, ssd_h_s = _ssd_sample(xbc_s, dt_s, z_s, state_ssd_conv[0].transpose(1, 0, 2), h0, ssd_w)
    mix_s = jnp.concatenate([yc_s, yd_s], axis=-1).reshape(ns, D_MIX_ODD)
    wr_hi, wr_lo, br = _router_weights(moe_w_rg[1], moe_b_rg[1], moe_w_re[1], moe_b_re[1])
    x1, xn, info, infot, cnt = _outproj_route((x,), mix, mix_s, rec_w_out[0].astype(BF16), norm_ffn[1],
                                              wr_hi, wr_lo, br, t_rows, valid_lo, valid_mid)
    y_p, y_small = _moe(x1, xn, info, infot, cnt, moe_w_gate, moe_w_up, moe_w_down, 1, norm_final,
                        t_rows, ns, True)

    tail = CONV_WIDTH - 1
    lru_conv_p = cx[t_rows - tail:t_rows].reshape(1, 1, tail, D_LRU)
    ssd_conv_p = xbc[t_rows - tail:t_rows].reshape(1, 1, tail, D_XBC)
    lru_conv_s = jnp.concatenate([state_lru_conv[0], cx_s.transpose(1, 0, 2)], axis=1)[:, -tail:][None]
    ssd_conv_s = jnp.concatenate([state_ssd_conv[0], xbc_s.transpose(1, 0, 2)], axis=1)[:, -tail:][None]

    y_prompt = y_p[None]
    y_sample = y_small[:ns].reshape(s_new, b, D_MODEL).transpose(1, 0, 2)
    return (y_prompt, y_sample, swa_k_p, swa_v_p, pool_p, lru_conv_p,
            lru_h_p.reshape(1, 1, D_LRU), ssd_conv_p,
            ssd_h_p.reshape(1, 1, SSD_HEADS, SSD_HEAD_DIM, SSD_STATE),
            swa_k_s, swa_v_s, pool_s, lru_conv_s, lru_h_s[None], ssd_conv_s,
            ssd_h_s.reshape(1, b, SSD_HEADS, SSD_HEAD_DIM, SSD_STATE))
```

```python
import functools

import jax
import jax.numpy as jnp
from jax import lax
from jax.experimental import pallas as pl
from jax.experimental.pallas import tpu as pltpu

F32 = jnp.float32
BF16 = jnp.bfloat16
I32 = jnp.int32

D_MODEL = 1024
N_META = 16
EPS = 1e-6
PAST_LEN = 16384
N_Q_HEADS = 16
N_KV_HEADS = 2
GQA = N_Q_HEADS // N_KV_HEADS
HEAD_DIM = 64
WINDOW = 128
D_ATTN = N_Q_HEADS * HEAD_DIM
D_KV = N_KV_HEADS * HEAD_DIM
POOL_WINDOWS = (2, 4, 8, 16)
POOL_GROUP_DIM = 96
D_POOL = len(POOL_WINDOWS) * POOL_GROUP_DIM
POOL_BUF = max(POOL_WINDOWS) - 1
D_MIX_EVEN = D_ATTN + D_POOL
D_LRU = 512
LRU_HEADS = 8
LRU_BLOCK = D_LRU // LRU_HEADS
LRU_C = 8.0
CONV_WIDTH = 4
SSD_HEADS = 8
SSD_HEAD_DIM = 64
D_SSD = SSD_HEADS * SSD_HEAD_DIM
SSD_GROUPS = 2
SSD_STATE = 128
D_XBC = D_SSD + 2 * SSD_GROUPS * SSD_STATE
D_MIX_ODD = D_LRU + D_SSD
MOE_GROUPS = 4
EXPERTS_PER_GROUP = 8
N_EXPERTS = MOE_GROUPS * EXPERTS_PER_GROUP
D_EXPERT = 256

BLK = 128
META_PAD = BLK - N_META
LANES = 128
MOE_TM = 256
FFN_ROWS = 256
MIX0_BLKS = 4
MIX1_BLKS = 4
NEG_INF = float("-inf")
VMEM_LIMIT = 56 * 1024 * 1024


def _cparams(*sem):
    return pltpu.CompilerParams(dimension_semantics=sem, vmem_limit_bytes=VMEM_LIMIT)


TT = 512
SUB = 8


def _small_tile_ranges(t_rows, ns, n):
    out = {}
    for k in range(t_rows // TT, n // TT):
        lo = k * TT
        ranges = []
        a0, a1 = max(lo, t_rows), min(lo + TT, t_rows + ns)
        if a1 > a0:
            ranges.append((a0 - lo, a1 - lo))
        b0 = max(lo, n - N_META)
        if lo + TT > b0:
            ranges.append((b0 - lo, TT))
        out[k] = tuple(ranges)
    return out


def _for_tile_rows(i, n_full, ranges, fn):
    @pl.when(i < n_full)
    def _():
        fn(0, TT)

    for k, rs in ranges.items():
        if rs:
            @pl.when(i == k)
            def _(rs=rs):
                for lo, hi in rs:
                    fn(lo, hi)


def _slab(c, rows, base=0):
    return pl.ds(base + c, rows, stride=SUB)


def _tile_of(r):
    return pl.ds(pl.multiple_of(r * SUB, SUB), SUB)


def _sigmoid(x):
    return 1.0 / (1.0 + jnp.exp(-x))


def _silu(x):
    return x * _sigmoid(x)


def _softplus(x):
    return jnp.maximum(x, 0.0) + jnp.log1p(jnp.exp(-jnp.abs(x)))


def _gelu_tanh(x):
    return 0.5 * x * (1.0 + jnp.tanh(0.7978845608028654 * (x + 0.044715 * x * x * x)))


def _dot(a, b):
    return jnp.dot(a.astype(BF16), b.astype(BF16), preferred_element_type=F32)


def _dot_nt(a, b):
    return lax.dot_general(a.astype(BF16), b.astype(BF16), (((1,), (1,)), ((), ())),
                           preferred_element_type=F32)


def _rms(x, g):
    ms = jnp.mean(x * x, axis=-1, keepdims=True)
    return x * lax.rsqrt(ms + EPS) * g


def _row_specs(xs, width):
    if len(xs) == 1:
        return [pl.BlockSpec((TT, width), lambda i: (i, 0))]
    na = xs[0].shape[0] // TT
    return [pl.BlockSpec((TT, width), lambda i: (jnp.minimum(i, na - 1), 0)),
            pl.BlockSpec((TT, width), lambda i: (jnp.maximum(i - na, 0), 0))]


def _pick_rows(refs, n_first):
    if len(refs) == 1:
        return refs[0][...]
    return jnp.where(pl.program_id(0) < n_first, refs[0][...], refs[1][...])


def _norm_proj_kernel(*refs, splits, n_x, n_first):
    x_refs, (g_ref, w_ref), out_refs = refs[:n_x], refs[n_x:n_x + 2], refs[n_x + 2:]
    y = _rms(_pick_rows(x_refs, n_first), g_ref[...]).astype(BF16)
    off = 0
    for o_ref, n in zip(out_refs, splits):
        o_ref[...] = jnp.dot(y, w_ref[:, off:off + n], preferred_element_type=F32).astype(o_ref.dtype)
        off += n


def _norm_proj(xs, g, w_bf16, splits, dtypes=None):
    n = sum(x.shape[0] for x in xs)
    dtypes = dtypes or (F32,) * len(splits)
    return pl.pallas_call(
        functools.partial(_norm_proj_kernel, splits=splits, n_x=len(xs), n_first=xs[0].shape[0] // TT),
        grid=(n // TT,),
        in_specs=_row_specs(xs, D_MODEL) + [pl.BlockSpec((1, D_MODEL), lambda i: (0, 0)),
                                             pl.BlockSpec(w_bf16.shape, lambda i: (0, 0))],
        out_specs=[pl.BlockSpec((TT, s), lambda i: (i, 0)) for s in splits],
        out_shape=[jax.ShapeDtypeStruct((n, s), dt) for s, dt in zip(splits, dtypes)],
        compiler_params=_cparams("parallel"),
        name="norm_proj",
    )(*xs, g.reshape(1, D_MODEL), w_bf16)


def _mix0_prompt_kernel(*refs, nblk):
    j = pl.program_id(0)
    o_ref = refs[8]

    @pl.when(j <= nblk)
    def _():
        _mix0_prompt_step(*refs)

    @pl.when(j > nblk)
    def _():
        o_ref[...] = jnp.zeros(o_ref.shape, o_ref.dtype)


def _mix0_prompt_step(sink_ref, q_ref, kvc_ref, kvp_ref, kvm_ref, p_ref, wp_ref, ps_ref,
                      o_ref, seq_ref, s_ref, pr_ref, rd_ref):
    j = pl.program_id(0)
    for sub in range(MIX0_BLKS):
        ro = sub * BLK
        kv_prev = kvp_ref[...] if sub == 0 else kvc_ref[ro - BLK:ro, :]
        blk = (j - 1) * MIX0_BLKS + sub
        j_prev = 2 if sub == 0 else 1
        c_min_meta = META_PAD if sub == MIX0_BLKS - 1 else BLK
        _attn_block(sink_ref, q_ref, kvc_ref, kv_prev, kvm_ref, o_ref, s_ref, pr_ref, rd_ref,
                    ro, j, blk, j_prev, c_min_meta)
    _pool_rows(j, p_ref, wp_ref, ps_ref, o_ref, seq_ref)


def _attn_block(sink_ref, q_ref, kvc_ref, kv_prev, kvm_ref, o_ref, s_ref, pr_ref, rd_ref,
                ro, j, blk, j_prev, c_min_meta):
    kv_all = jnp.concatenate([kv_prev, kvc_ref[ro:ro + BLK, :], kvm_ref[...],
                              jnp.zeros((META_PAD, 2 * D_KV), F32)], axis=0)
    k_all = kv_all[:, :D_KV]
    v_all = kv_all[:, D_KV:]
    k_rot = pltpu.roll(k_all, HEAD_DIM, 1)
    v_rot = pltpu.roll(v_all, HEAD_DIM, 1)
    lo_half = lax.broadcasted_iota(I32, k_all.shape, 1) < HEAD_DIM

    def halves(own_lanes, other_lanes, g):
        src_lo, src_hi = (own_lanes, other_lanes) if g == 0 else (other_lanes, own_lanes)
        return (jnp.where(lo_half, src_lo, 0.0).astype(BF16), jnp.where(lo_half, 0.0, src_hi).astype(BF16))

    r = lax.broadcasted_iota(I32, (BLK, BLK), 0)
    c = lax.broadcasted_iota(I32, (BLK, BLK), 1)
    own = r >= c
    dist_loc = jnp.where(own, r - c, r - c + BLK).astype(F32)
    d_min = jnp.where(j >= j_prev, -BLK, 0)
    c_min = jnp.where(j == 0, c_min_meta, 0)
    mask_loc = jnp.where((r - c >= d_min) & (c >= c_min), 0.0, NEG_INF)
    meta_hi = jnp.where(j >= 1, N_META, 0)
    dist_meta = jnp.where(c < meta_hi, (N_META + blk * BLK + r - c).astype(F32), jnp.inf)
    slabs = GQA // 2
    nkeys = 3 * BLK
    for g in range(N_KV_HEADS):
        k_even, k_odd = halves(k_all, k_rot, g)
        v_even, v_odd = halves(v_all, v_rot, g)
        qg = jnp.concatenate([q_ref[ro:ro + BLK, (g * slabs + i) * LANES:(g * slabs + i + 1) * LANES]
                              for i in range(slabs)], axis=0) * (HEAD_DIM ** -0.5)
        s_ref[...] = _dot_nt(qg, jnp.concatenate([k_even, k_odd], axis=0))
        for i in range(slabs):
            rows = slice(i * BLK, (i + 1) * BLK)
            for par in range(2):
                h = g * GQA + 2 * i + par
                slope = 2.0 ** (-8.0 * (h + 1) / N_Q_HEADS)
                sink = sink_ref[h]
                k0 = par * nkeys
                s_loc = (jnp.where(own, s_ref[rows, k0 + BLK:k0 + 2 * BLK], s_ref[rows, k0:k0 + BLK])
                         - slope * dist_loc + mask_loc)
                s_met = s_ref[rows, k0 + 2 * BLK:k0 + 3 * BLK] - slope * dist_meta
                m = jnp.maximum(jnp.max(jnp.maximum(s_loc, s_met), axis=-1, keepdims=True), sink)
                p_loc = jnp.exp(s_loc - m)
                p_met = jnp.exp(s_met - m)
                den = jnp.sum(p_loc + p_met, axis=-1, keepdims=True) + jnp.exp(sink - m)
                pr_ref[par, rows, 0:BLK] = jnp.where(own, 0.0, p_loc).astype(BF16)
                pr_ref[par, rows, BLK:2 * BLK] = jnp.where(own, p_loc, 0.0).astype(BF16)
                pr_ref[par, rows, 2 * BLK:] = p_met.astype(BF16)
                rd_ref[par, rows, :] = jnp.broadcast_to(1.0 / den, (BLK, LANES))
        o = (jnp.dot(pr_ref[0], v_even, preferred_element_type=F32) * rd_ref[0]
             + jnp.dot(pr_ref[1], v_odd, preferred_element_type=F32) * rd_ref[1])
        for i in range(slabs):
            o_ref[ro:ro + BLK, (g * slabs + i) * LANES:(g * slabs + i + 1) * LANES] = (
                o[i * BLK:(i + 1) * BLK, :].astype(o_ref.dtype))


def _pool_rows(j, p_ref, wp_ref, ps_ref, o_ref, seq_ref):
    n_rows = p_ref.shape[0]
    carry = jnp.where(j == 0, 0.0, seq_ref[...])
    s0 = jnp.concatenate([carry, p_ref[...]], axis=0)
    s1 = s0 + pltpu.roll(s0, 1, 0)
    s2 = s1 + pltpu.roll(s1, 2, 0)
    s3 = s2 + pltpu.roll(s2, 4, 0)
    s4 = s3 + pltpu.roll(s3, 8, 0)
    rows = N_META + n_rows
    col = lax.broadcasted_iota(I32, (rows, D_POOL), 1)
    row = lax.broadcasted_iota(I32, (rows, D_POOL), 0) - N_META
    wsum = jnp.where(col < POOL_GROUP_DIM, s1,
                     jnp.where(col < 2 * POOL_GROUP_DIM, s2,
                               jnp.where(col < 3 * POOL_GROUP_DIM, s3, s4)))
    wlen = jnp.where(col < POOL_GROUP_DIM, 2,
                     jnp.where(col < 2 * POOL_GROUP_DIM, 4,
                               jnp.where(col < 3 * POOL_GROUP_DIM, 8, 16)))
    pos = row + jnp.where(j == 0, N_META - n_rows, N_META + (j - 1) * n_rows)
    cnt = jnp.clip(jnp.minimum(pos + 1, wlen), 1, 16).astype(F32)
    d = (wsum / cnt - s0)[N_META:, :]
    o_ref[:, D_ATTN:] = (_dot(d, wp_ref[...]) * ps_ref[...]).astype(o_ref.dtype)
    seq_ref[...] = p_ref[n_rows - N_META:, :]


def _mix0_prompt(q, kv, p, sinks, wpool_bd, pool_scale, t_rows, meta_blk):
    n = q.shape[0]
    step = MIX0_BLKS * BLK
    assert t_rows % step == 0 and n % step == 0 and step <= TT
    nstep = t_rows // step
    last = n // step - 1

    def cur(j):
        return jnp.where(j == 0, last, j - 1)

    meta16 = (meta_blk * BLK + META_PAD) // N_META
    return pl.pallas_call(
        functools.partial(_mix0_prompt_kernel, nblk=nstep),
        grid=(last + 1,),
        in_specs=[pl.BlockSpec(memory_space=pltpu.SMEM),
                  pl.BlockSpec((step, D_ATTN), lambda j: (cur(j), 0)),
                  pl.BlockSpec((step, 2 * D_KV), lambda j: (cur(j), 0)),
                  pl.BlockSpec((BLK, 2 * D_KV), lambda j: (jnp.maximum((j - 1) * MIX0_BLKS - 1, 0), 0)),
                  pl.BlockSpec((N_META, 2 * D_KV), lambda j: (meta16, 0)),
                  pl.BlockSpec((step, D_POOL), lambda j: (cur(j), 0)),
                  pl.BlockSpec((D_POOL, D_POOL), lambda j: (0, 0)),
                  pl.BlockSpec((1, D_POOL), lambda j: (0, 0))],
        out_specs=pl.BlockSpec((step, D_MIX_EVEN), lambda j: (cur(j), 0)),
        out_shape=jax.ShapeDtypeStruct((n, D_MIX_EVEN), BF16),
        scratch_shapes=[pltpu.VMEM((N_META, D_POOL), F32),
                        pltpu.VMEM((GQA // 2 * BLK, 2 * 3 * BLK), F32),
                        pltpu.VMEM((2, GQA // 2 * BLK, 3 * BLK), BF16),
                        pltpu.VMEM((2, GQA // 2 * BLK, LANES), F32)],
        compiler_params=_cparams("arbitrary"),
        name="mix0_prompt",
    )(sinks, q, kv, kv, kv, p, wpool_bd, pool_scale.reshape(1, D_POOL))


def _attn_sample_kernel(sink_ref, q_ref, ck_ref, cv_ref, kn_ref, vn_ref, o_ref, *, n_new):
    nk = N_META + WINDOW + n_new
    nq = n_new * GQA
    k_all = jnp.concatenate([ck_ref[...], kn_ref[...]], axis=1).astype(BF16)
    v_all = jnp.concatenate([cv_ref[...], vn_ref[...]], axis=1).astype(BF16)
    r = lax.broadcasted_iota(I32, (nq, nk), 0)
    slot = lax.broadcasted_iota(I32, (nq, nk), 1)
    t = r // GQA
    hh = r - t * GQA
    kpos = jnp.where(slot < N_META, slot,
                     jnp.where(slot < N_META + WINDOW, PAST_LEN - WINDOW - N_META + slot,
                               PAST_LEN - N_META - WINDOW + slot))
    dist = PAST_LEN + t - kpos
    allowed = (dist >= 0) & ((slot < N_META) | (dist < WINDOW))
    distf = dist.astype(F32)
    hh1 = hh[:, :1]
    for g in range(N_KV_HEADS):
        slope = jnp.exp2(-8.0 * (g * GQA + hh + 1).astype(F32) / N_Q_HEADS)
        sink = jnp.zeros((nq, 1), F32)
        for i in range(GQA):
            sink = jnp.where(hh1 == i, sink_ref[g * GQA + i], sink)
        kg = k_all[:, :, g * HEAD_DIM:(g + 1) * HEAD_DIM]
        vg = v_all[:, :, g * HEAD_DIM:(g + 1) * HEAD_DIM]
        s = jnp.einsum("bqd,bkd->bqk", q_ref[:, g].astype(BF16), kg,
                       preferred_element_type=F32) * (HEAD_DIM ** -0.5)
        s = jnp.where(allowed[None], s - (slope * distf)[None], NEG_INF)
        sink = sink[None]
        m = jnp.maximum(jnp.max(s, axis=-1, keepdims=True), sink)
        pr = jnp.exp(s - m)
        den = jnp.sum(pr, axis=-1, keepdims=True) + jnp.exp(sink - m)
        o = jnp.einsum("bqk,bkd->bqd", pr.astype(BF16), vg, preferred_element_type=F32)
        o_ref[:, g] = o / den


def _attn_sample(qs, ck, cv, kn, vn, sinks):
    b, _, nq, _ = qs.shape
    n_new = kn.shape[1]
    bs = 16 if b % 16 == 0 else b
    nc = N_META + WINDOW
    return pl.pallas_call(
        functools.partial(_attn_sample_kernel, n_new=n_new),
        grid=(b // bs,),
        in_specs=[pl.BlockSpec(memory_space=pltpu.SMEM),
                  pl.BlockSpec((bs, N_KV_HEADS, nq, HEAD_DIM), lambda i: (i, 0, 0, 0)),
                  pl.BlockSpec((bs, nc, D_KV), lambda i: (i, 0, 0)),
                  pl.BlockSpec((bs, nc, D_KV), lambda i: (i, 0, 0)),
                  pl.BlockSpec((bs, n_new, D_KV), lambda i: (i, 0, 0)),
                  pl.BlockSpec((bs, n_new, D_KV), lambda i: (i, 0, 0))],
        out_specs=pl.BlockSpec((bs, N_KV_HEADS, nq, HEAD_DIM), lambda i: (i, 0, 0, 0)),
        out_shape=jax.ShapeDtypeStruct(qs.shape, F32),
        compiler_params=_cparams("parallel"),
        name="attn_sample",
    )(sinks, qs, ck, cv, kn, vn)


def _pool_sample_kernel(buf_ref, p_ref, wp_ref, ps_ref, o_ref, *, n_new):
    seq = [buf_ref[i] for i in range(POOL_BUF)] + [p_ref[i] for i in range(n_new)]
    b = p_ref.shape[1]
    col = lax.broadcasted_iota(I32, (b, D_POOL), 1)
    for t in range(n_new):
        cur = seq[POOL_BUF + t]
        acc = cur
        sums = {}
        for back in range(1, max(POOL_WINDOWS)):
            acc = acc + seq[POOL_BUF + t - back]
            if back + 1 in POOL_WINDOWS:
                sums[back + 1] = acc
        mean = jnp.where(col < POOL_GROUP_DIM, sums[2] / 2.0,
                         jnp.where(col < 2 * POOL_GROUP_DIM, sums[4] / 4.0,
                                   jnp.where(col < 3 * POOL_GROUP_DIM, sums[8] / 8.0,
                                             sums[16] / 16.0)))
        o_ref[t] = _dot(mean - cur, wp_ref[...]) * ps_ref[...]


def _pool_sample(buf_tm, p_tm, wpool_bd, pool_scale):
    n_new = p_tm.shape[0]
    return pl.pallas_call(
        functools.partial(_pool_sample_kernel, n_new=n_new),
        out_shape=jax.ShapeDtypeStruct(p_tm.shape, F32),
        compiler_params=pltpu.CompilerParams(vmem_limit_bytes=VMEM_LIMIT),
        name="pool_sample",
    )(buf_tm, p_tm, wpool_bd, pool_scale.reshape(1, D_POOL))


def _outproj_route_kernel(*refs, valid_lo, valid_mid, n_x, n_first):
    tt = TT
    x_refs = refs[:n_x]
    (mix_ref, mixs_ref, w_ref, g_ref, wrh_ref, wrl_ref, br_ref,
     x1_ref, xn_ref, info_ref, infot_ref, cnt_ref, acc_ref) = refs[n_x:]
    i = pl.program_id(0)
    is_sample_tile = i == n_first
    acc = jnp.where(i == 0, 0.0, acc_ref[...])
    mix = mix_ref[...] + jnp.where(is_sample_tile, mixs_ref[...], 0).astype(mix_ref.dtype)
    x1 = _pick_rows(x_refs, n_first) + _dot(mix, w_ref[...])
    x1_ref[...] = x1
    xn = _rms(x1, g_ref[...])
    for cc_ in range(D_MODEL // LANES):
        xn_ref[_slab(cc_, tt), :] = xn[:, cc_ * LANES:(cc_ + 1) * LANES]
    hi = xn.astype(BF16)
    lo = (xn - hi.astype(F32)).astype(BF16)
    hi_out = jnp.dot(hi, jnp.concatenate([wrh_ref[...], wrl_ref[...]], axis=1), preferred_element_type=F32)
    logits = (hi_out[:, :LANES] + jnp.dot(lo, wrh_ref[...], preferred_element_type=F32)
              + hi_out[:, LANES:]) + br_ref[...]
    lane = lax.broadcasted_iota(I32, (tt, LANES), 1)
    lanef = lane.astype(F32)
    lg = jnp.where(lane < MOE_GROUPS, logits, NEG_INF)
    mg = jnp.max(lg, axis=-1, keepdims=True)
    gstar = jnp.min(jnp.where(lg == mg, lanef, 1e9), axis=-1, keepdims=True)
    pgroup = 1.0 / jnp.sum(jnp.exp(lg - mg), axis=-1, keepdims=True)
    lo_lane = MOE_GROUPS + gstar * EXPERTS_PER_GROUP
    le = jnp.where((lanef >= lo_lane) & (lanef < lo_lane + EXPERTS_PER_GROUP), logits, NEG_INF)
    v1 = jnp.max(le, axis=-1, keepdims=True)
    i1 = jnp.min(jnp.where(le == v1, lanef, 1e9), axis=-1, keepdims=True)
    le2 = jnp.where(lanef == i1, NEG_INF, le)
    v2 = jnp.max(le2, axis=-1, keepdims=True)
    i2 = jnp.min(jnp.where(le2 == v2, lanef, 1e9), axis=-1, keepdims=True)
    tq = jnp.exp(v2 - v1)
    w1 = pgroup / (1.0 + tq)
    w2 = pgroup * tq / (1.0 + tq)
    e1 = i1 - MOE_GROUPS
    e2 = i2 - MOE_GROUPS
    rowg = i * tt + lax.broadcasted_iota(I32, (tt, LANES), 0)
    valid = (rowg < valid_lo) | (rowg >= valid_mid)
    oh1 = jnp.where(valid & (lanef == e1), 1.0, 0.0)
    oh2 = jnp.where(valid & (lanef == e2), 1.0, 0.0)
    both = oh1 + oh2
    rr = lax.broadcasted_iota(I32, (tt, tt), 0)
    cc = lax.broadcasted_iota(I32, (tt, tt), 1)
    tril = jnp.where(cc < rr, 1.0, 0.0).astype(BF16)
    base = acc + jnp.dot(tril, both.astype(BF16), preferred_element_type=F32)
    rank1 = jnp.sum(oh1 * base, axis=-1, keepdims=True)
    rank2 = jnp.sum(oh2 * base, axis=-1, keepdims=True)
    info = jnp.where(lane == 0, e1, jnp.where(lane == 1, e2, jnp.where(lane == 2, w1, jnp.where(
        lane == 3, w2, jnp.where(lane == 4, rank1, jnp.where(lane == 5, rank2, 0.0))))))
    info_ref[...] = info
    infot_ref[0] = info.T[:SUB, :]
    total = acc + jnp.sum(both, axis=0, keepdims=True)
    acc_ref[...] = total
    cnt_ref[...] = jnp.broadcast_to(total, (8, LANES))


def _outproj_route(xs, mix, mix_s, w_out_bf16, g_ffn, wr_hi, wr_lo, br, t_rows, valid_lo, valid_mid):
    n = mix.shape[0]
    ns, dm = mix_s.shape
    mix_s = jnp.pad(mix_s, ((0, TT - ns), (0, 0))).astype(mix.dtype)
    nt = n // TT

    def const2(i):
        return (0, 0)

    return pl.pallas_call(
        functools.partial(_outproj_route_kernel, valid_lo=valid_lo, valid_mid=valid_mid,
                          n_x=len(xs), n_first=t_rows // TT),
        grid=(nt,),
        in_specs=_row_specs(xs, D_MODEL) + [
            pl.BlockSpec((TT, dm), lambda i: (i, 0)),
            pl.BlockSpec((TT, dm), const2),
            pl.BlockSpec((dm, D_MODEL), const2),
            pl.BlockSpec((1, D_MODEL), const2),
            pl.BlockSpec((D_MODEL, LANES), const2),
            pl.BlockSpec((D_MODEL, LANES), const2),
            pl.BlockSpec((1, LANES), const2)],
        out_specs=[pl.BlockSpec((TT, D_MODEL), lambda i: (i, 0)),
                   pl.BlockSpec((TT * SUB, LANES), lambda i: (i, 0)),
                   pl.BlockSpec((TT, LANES), lambda i: (i, 0)),
                   pl.BlockSpec((1, SUB, TT), lambda i: (i, 0, 0)),
                   pl.BlockSpec((8, LANES), const2)],
        out_shape=[jax.ShapeDtypeStruct((n, D_MODEL), F32),
                   jax.ShapeDtypeStruct((n * SUB, LANES), F32),
                   jax.ShapeDtypeStruct((n, LANES), F32),
                   jax.ShapeDtypeStruct((nt, SUB, TT), F32),
                   jax.ShapeDtypeStruct((8, LANES), F32)],
        scratch_shapes=[pltpu.VMEM((1, LANES), F32)],
        compiler_params=_cparams("arbitrary"),
        name="outproj_route",
    )(*xs, mix, mix_s, w_out_bf16, g_ffn.reshape(1, D_MODEL), wr_hi, wr_lo, br)


ROWS_PER_ISSUE = 8


def _issue_rows(lo, hi, row_copy):
    assert (hi - lo) % ROWS_PER_ISSUE == 0

    def body(it, c):
        for u in range(ROWS_PER_ISSUE):
            r = lo + it * ROWS_PER_ISSUE + u
            row_copy(r, 0).start(priority=0)
            row_copy(r, 1).start(priority=1)
        return c

    lax.fori_loop(0, (hi - lo) // ROWS_PER_ISSUE, body, 0)


def _dispatch_kernel(pos_ref, zt_ref, xn_ref, xs_ref, zbuf, sem, zsem, *, n_full, ranges, n_tiles):
    i = pl.program_id(0)

    @pl.when(i == 0)
    def _():
        zbuf[...] = jnp.zeros(zbuf.shape, F32)

        def zero_copy(t):
            rows = MOE_TM * SUB
            return pltpu.make_async_copy(zbuf, xs_ref.at[pl.ds(pl.multiple_of(t * rows, rows), rows)], zsem)

        def zstart(t, c):
            @pl.when(zt_ref[t] != 0)
            def _():
                zero_copy(t).start()
            return c

        def zwait(t, c):
            @pl.when(zt_ref[t] != 0)
            def _():
                zero_copy(t).wait()
            return c

        lax.fori_loop(0, n_tiles, zstart, 0)
        lax.fori_loop(0, n_tiles, zwait, 0)

    def row_copy(r, k):
        return pltpu.make_async_copy(xn_ref.at[_tile_of(r)], xs_ref.at[_tile_of(pos_ref[0, 0, k * TT + r])], sem)

    def scatter(lo, hi):
        _issue_rows(lo, hi, row_copy)
        nrow = (hi - lo) * SUB
        for _ in range(2):
            pltpu.make_async_copy(xn_ref.at[pl.ds(lo * SUB, nrow)], xs_ref.at[pl.ds(0, nrow)], sem).wait()

    _for_tile_rows(i, n_full, ranges, scatter)


def _dispatch(xn, pos, zero_tiles, n_sorted, t_rows, ranges):
    n = xn.shape[0] // SUB
    return pl.pallas_call(
        functools.partial(_dispatch_kernel, n_full=t_rows // TT, ranges=ranges,
                          n_tiles=n_sorted // MOE_TM),
        grid=(n // TT,),
        in_specs=[pl.BlockSpec((1, 1, 2 * TT), lambda i: (i, 0, 0), memory_space=pltpu.SMEM),
                  pl.BlockSpec(memory_space=pltpu.SMEM),
                  pl.BlockSpec((TT * SUB, LANES), lambda i: (i, 0))],
        out_specs=pl.BlockSpec(memory_space=pl.ANY),
        out_shape=jax.ShapeDtypeStruct((n_sorted * SUB, LANES), F32),
        scratch_shapes=[pltpu.VMEM((MOE_TM * SUB, LANES), F32),
                        pltpu.SemaphoreType.DMA(()),
                        pltpu.SemaphoreType.DMA(())],
        compiler_params=_cparams("arbitrary"),
        name="moe_dispatch",
    )(pos, zero_tiles, xn)


def _expert_ffn_kernel(te_ref, nu_ref, xs_ref, wg_ref, wu_ref, wd_ref, ys_ref, wgb, wub, wdb):
    j = pl.program_id(0)
    changed = (j == 0) | (te_ref[j] != te_ref[jnp.maximum(j - 1, 0)])

    @pl.when((j < nu_ref[0]) & changed)
    def _():
        wgb[...] = wg_ref[0, 0].astype(BF16)
        wub[...] = wu_ref[0, 0].astype(BF16)
        wdb[...] = wd_ref[0, 0].astype(BF16)

    n_slab = D_MODEL // LANES

    @pl.when(j < nu_ref[0])
    def _():
        for part in range(MOE_TM // FFN_ROWS):
            base = part * FFN_ROWS * SUB
            x = jnp.concatenate([xs_ref[_slab(c, FFN_ROWS, base), :] for c in range(n_slab)],
                                axis=1).astype(BF16)
            hg = jnp.dot(x, wgb[...], preferred_element_type=F32)
            hu = jnp.dot(x, wub[...], preferred_element_type=F32)
            h = (_silu(hg) * hu).astype(BF16)
            y = jnp.dot(h, wdb[...], preferred_element_type=F32)
            for c in range(n_slab):
                ys_ref[_slab(c, FFN_ROWS, base), :] = y[:, c * LANES:(c + 1) * LANES]

    @pl.when(j >= nu_ref[0])
    def _():
        ys_ref[...] = jnp.zeros(ys_ref.shape, F32)


def _expert_ffn(xs, tile_expert, n_used, w_gate, w_up, w_down, layer):
    n_sorted = xs.shape[0] // SUB
    nt = n_sorted // MOE_TM

    def row_map(j, te, nu):
        return (jnp.minimum(j, nu[0] - 1), 0)

    def w_map(j, te, nu):
        return (layer, te[j], 0, 0)

    return pl.pallas_call(
        _expert_ffn_kernel,
        grid_spec=pltpu.PrefetchScalarGridSpec(
            num_scalar_prefetch=2,
            grid=(nt,),
            in_specs=[pl.BlockSpec((MOE_TM * SUB, LANES), row_map),
                      pl.BlockSpec((1, 1, D_MODEL, D_EXPERT), w_map),
                      pl.BlockSpec((1, 1, D_MODEL, D_EXPERT), w_map),
                      pl.BlockSpec((1, 1, D_EXPERT, D_MODEL), w_map)],
            out_specs=pl.BlockSpec((MOE_TM * SUB, LANES), lambda j, te, nu: (j, 0)),
            scratch_shapes=[pltpu.VMEM((D_MODEL, D_EXPERT), BF16),
                            pltpu.VMEM((D_MODEL, D_EXPERT), BF16),
                            pltpu.VMEM((D_EXPERT, D_MODEL), BF16)]),
        out_shape=jax.ShapeDtypeStruct((n_sorted * SUB, LANES), F32),
        compiler_params=_cparams("arbitrary"),
        name="expert_ffn",
    )(tile_expert, n_used, xs, w_gate, w_up, w_down)


def _combine_kernel(posc_ref, posn_ref, x1_ref, info_ref, g_ref, ys_ref, *rest,
                    n_full, n_tiles, ranges, valid_lo, valid_mid, final_norm):
    out_refs, (ybuf, sem) = rest[:-2], rest[-2:]
    i = pl.program_id(0)
    slot = lax.rem(i, 2)

    def buf_base(s, k):
        return (s * 2 + k) * (TT * SUB)

    def gather(pos_ref, dst_slot):
        def row_copy(r, k):
            dst = pl.ds(pl.multiple_of(buf_base(dst_slot, k) + r * SUB, SUB), SUB)
            return pltpu.make_async_copy(ys_ref.at[_tile_of(pos_ref[0, 0, k * TT + r])],
                                         ybuf.at[dst], sem.at[dst_slot])
        return lambda lo, hi: _issue_rows(lo, hi, row_copy)

    @pl.when(i == 0)
    def _():
        ybuf[...] = jnp.zeros(ybuf.shape, F32)
        _for_tile_rows(i, n_full, ranges, gather(posc_ref, 0))

    @pl.when(i + 1 < n_tiles)
    def _():
        _for_tile_rows(i + 1, n_full, ranges, gather(posn_ref, 1 - slot))

    def drain(lo, hi):
        nrow = (hi - lo) * SUB
        for k in range(2):
            dst = pl.ds(pl.multiple_of(buf_base(slot, k) + lo * SUB, SUB), nrow)
            pltpu.make_async_copy(ys_ref.at[pl.ds(0, nrow)], ybuf.at[dst], sem.at[slot]).wait()

    _for_tile_rows(i, n_full, ranges, drain)

    info = info_ref[...]
    w1 = info[:, 2:3]
    w2 = info[:, 3:4]
    rowg = i * TT + lax.broadcasted_iota(I32, (TT, 1), 0)
    valid = (rowg < valid_lo) | (rowg >= valid_mid)
    outs = []
    for c in range(D_MODEL // LANES):
        y1 = ybuf[_slab(c, TT, buf_base(slot, 0)), :]
        y2 = ybuf[_slab(c, TT, buf_base(slot, 1)), :]
        outs.append(x1_ref[:, c * LANES:(c + 1) * LANES] + jnp.where(valid, w1 * y1 + w2 * y2, 0.0))
    out = jnp.concatenate(outs, axis=1)
    if not final_norm:
        out_refs[0][...] = out
    else:
        out = _rms(out, g_ref[...])

        @pl.when(i < n_full)
        def _():
            out_refs[0][...] = out

        @pl.when(i >= n_full)
        def _():
            out_refs[1][...] = out


def _combine(x1, info, pos, ys, g_final, t_rows, ranges, valid_lo, valid_mid, final_norm):
    n = x1.shape[0]
    nt = n // TT
    n_full = t_rows // TT
    if final_norm:
        out_specs = [pl.BlockSpec((TT, D_MODEL), lambda i: (jnp.minimum(i, n_full - 1), 0)),
                     pl.BlockSpec((TT, D_MODEL), lambda i: (jnp.maximum(i - n_full, 0), 0))]
        out_shape = [jax.ShapeDtypeStruct((t_rows, D_MODEL), F32),
                     jax.ShapeDtypeStruct((n - t_rows, D_MODEL), F32)]
    else:
        out_specs = [pl.BlockSpec((TT, D_MODEL), lambda i: (i, 0))]
        out_shape = [jax.ShapeDtypeStruct((n, D_MODEL), F32)]
    return pl.pallas_call(
        functools.partial(_combine_kernel, n_full=n_full, n_tiles=nt, ranges=ranges,
                          valid_lo=valid_lo, valid_mid=valid_mid, final_norm=final_norm),
        grid=(nt,),
        in_specs=[pl.BlockSpec((1, 1, 2 * TT), lambda i: (i, 0, 0), memory_space=pltpu.SMEM),
                  pl.BlockSpec((1, 1, 2 * TT), lambda i: (jnp.minimum(i + 1, nt - 1), 0, 0),
                               memory_space=pltpu.SMEM),
                  pl.BlockSpec((TT, D_MODEL), lambda i: (i, 0)),
                  pl.BlockSpec((TT, LANES), lambda i: (i, 0)),
                  pl.BlockSpec((1, D_MODEL), lambda i: (0, 0)),
                  pl.BlockSpec(memory_space=pl.ANY)],
        out_specs=out_specs,
        out_shape=out_shape,
        scratch_shapes=[pltpu.VMEM((2 * 2 * TT * SUB, LANES), F32),
                        pltpu.SemaphoreType.DMA((2,))],
        compiler_params=_cparams("arbitrary"),
        name="moe_combine",
    )(pos, pos, x1, info, g_final.reshape(1, D_MODEL), ys)


def _moe(x1, xn, info, infot, cnt, w_gate, w_up, w_down, layer, g_final, t_rows, ns, final_norm):
    n = x1.shape[0]
    valid_lo, valid_mid = t_rows + ns, n - N_META
    ranges = _small_tile_ranges(t_rows, ns, n)
    n_valid = valid_lo + N_META
    nt = (2 * n_valid + N_EXPERTS * (MOE_TM - 1)) // MOE_TM
    counts = cnt[0, :N_EXPERTS].astype(I32)
    tiles = (counts + MOE_TM - 1) // MOE_TM
    tile_end = jnp.cumsum(tiles)
    row_start = (tile_end - tiles) * MOE_TM
    n_used = tile_end[-1:]
    experts = jnp.arange(N_EXPERTS, dtype=I32)

    def slot_rows(e, rank):
        start = jnp.sum(jnp.where(e[..., None] == experts, row_start, 0), axis=-1)
        return start + rank

    pos = jnp.concatenate([slot_rows(infot[:, 0].astype(I32), infot[:, 4].astype(I32)),
                           slot_rows(infot[:, 1].astype(I32), infot[:, 5].astype(I32))], axis=-1)
    rows = jnp.arange(n).reshape(n // TT, TT)
    valid = (rows < valid_lo) | (rows >= valid_mid)
    pos = jnp.where(jnp.concatenate([valid, valid], axis=-1), pos, 0)[:, None, :]
    tile_ids = jnp.arange(nt, dtype=I32)
    tile_expert = jnp.sum(tile_end[None, :] <= jnp.minimum(tile_ids, n_used[0] - 1)[:, None],
                          axis=1).astype(I32)
    is_last = jnp.any((tile_ids[:, None] == tile_end[None, :] - 1) & (tiles[None, :] > 0), axis=1)
    zero_tiles = (is_last | (tile_ids >= n_used[0])).astype(I32)
    xs = _dispatch(xn, pos, zero_tiles, nt * MOE_TM, t_rows, ranges)
    ys = _expert_ffn(xs, tile_expert, n_used.astype(I32), w_gate, w_up, w_down, layer)
    return _combine(x1, info, pos, ys, g_final, t_rows, ranges, valid_lo, valid_mid, final_norm)


def _shift_rows(x, k, fill):
    if k % SUB == 0:
        return jnp.concatenate([jnp.full((k, x.shape[1]), fill, x.dtype), x[:x.shape[0] - k]], axis=0)
    row = lax.broadcasted_iota(I32, x.shape, 0)
    return jnp.where(row >= k, pltpu.roll(x, k, 0), fill)


def _lru_gates(xc, wgate_ref, bgate_ref, lam_sp):
    gates = _dot(xc, wgate_ref[...]) + bgate_ref[...]
    r = _sigmoid(gates[:, :D_LRU])
    ig = _sigmoid(gates[:, D_LRU:])
    log_a = -LRU_C * r * lam_sp
    a = jnp.exp(log_a)
    bx = jnp.sqrt(-jnp.tanh(log_a) * (a * a + 1.0)) * (ig * xc)
    return a, bx


def _conv_block(seq_ref, u, w_ref, b_ref, fresh):
    seq = jnp.concatenate([jnp.where(fresh, 0.0, seq_ref[...]), u], axis=0)
    out = b_ref[...] + u * w_ref[CONV_WIDTH - 1:CONV_WIDTH, :]
    for back in range(1, CONV_WIDTH):
        tap = CONV_WIDTH - 1 - back
        out = out + pltpu.roll(seq, back, 0)[8:, :] * w_ref[tap:tap + 1, :]
    seq_ref[...] = u[BLK - 8:, :]
    return out


def _mix1_prompt_kernel(*refs, nblk):
    j = pl.program_id(0)
    o_ref = refs[16]

    @pl.when(j <= nblk)
    def _():
        _mix1_prompt_step(*refs)

    @pl.when(j > nblk)
    def _():
        o_ref[...] = jnp.zeros(o_ref.shape, o_ref.dtype)


def _mix1_prompt_step(cx_ref, cg_ref, z_ref, xbc_ref, dt_ref,
                      cwc_ref, cbc_ref, wgate_ref, bgate_ref, lam_ref,
                      cwd_ref, cbd_ref, dtb_ref, alog_ref, dskip_ref, nw_ref,
                      o_ref, lruh_ref, ssdh_ref,
                      seqc_ref, seqd_ref, hl_ref, hs_ref):
    refs = (cx_ref, cg_ref, z_ref, xbc_ref, dt_ref, cwc_ref, cbc_ref, wgate_ref, bgate_ref, lam_ref,
            cwd_ref, cbd_ref, dtb_ref, alog_ref, dskip_ref, nw_ref, o_ref, lruh_ref, ssdh_ref,
            seqc_ref, seqd_ref, hl_ref, hs_ref)
    for sub in range(MIX1_BLKS):
        _mix1_block(*refs, sub=sub)


def _mix1_block(cx_ref, cg_ref, z_ref, xbc_ref, dt_ref,
                cwc_ref, cbc_ref, wgate_ref, bgate_ref, lam_ref,
                cwd_ref, cbd_ref, dtb_ref, alog_ref, dskip_ref, nw_ref,
                o_ref, lruh_ref, ssdh_ref,
                seqc_ref, seqd_ref, hl_ref, hs_ref, *, sub):
    j = pl.program_id(0)
    rows = slice(sub * BLK, (sub + 1) * BLK)
    fresh = (j == 0) if sub == 0 else False
    rowc = lax.broadcasted_iota(I32, (BLK, 1), 0)
    live_min_meta = META_PAD if sub == MIX1_BLKS - 1 else BLK
    live = rowc >= jnp.where(j >= 1, 0, live_min_meta)

    xc = _conv_block(seqc_ref, cx_ref[rows, :], cwc_ref, cbc_ref, fresh)
    a, bx = _lru_gates(xc, wgate_ref, bgate_ref, _softplus(-lam_ref[...]))
    a = jnp.where(live, a, 1.0)
    bx = jnp.where(live, bx, 0.0)
    k = 1
    while k < BLK:
        bx = a * _shift_rows(bx, k, 0.0) + bx
        a = a * _shift_rows(a, k, 1.0)
        k *= 2
    h = a * jnp.where(fresh, 0.0, hl_ref[...]) + bx
    hl_ref[...] = h[BLK - 1:BLK, :]
    lruh_ref[...] = h[BLK - 1:BLK, :]
    o_ref[rows, :D_LRU] = (h * _gelu_tanh(cg_ref[rows, :])).astype(o_ref.dtype)

    xbc = _silu(_conv_block(seqd_ref, xbc_ref[rows, :], cwd_ref, cbd_ref, fresh))
    xs = xbc[:, :D_SSD]
    dt = jnp.where(live, _softplus(dt_ref[rows, :] + dtb_ref[...]), 0.0)
    a_neg = -jnp.exp(alog_ref[...])
    acum = dt * a_neg
    k = 1
    while k < BLK:
        acum = acum + _shift_rows(acum, k, 0.0)
        k *= 2
    acum_t = acum.T
    dt_t = dt.T
    xs_t = xs.T
    tri = (lax.broadcasted_iota(I32, (BLK, BLK), 0) >= lax.broadcasted_iota(I32, (BLK, BLK), 1))
    rep = SSD_HEADS // SSD_GROUPS
    ys = []
    for g in range(SSD_GROUPS):
        bg = xbc[:, D_SSD + g * SSD_STATE:D_SSD + (g + 1) * SSD_STATE]
        cgm = xbc[:, D_SSD + (SSD_GROUPS + g) * SSD_STATE:D_SSD + (SSD_GROUPS + g + 1) * SSD_STATE]
        cb = _dot_nt(cgm, bg)
        for hh in range(rep):
            hd = g * rep + hh
            col = acum[:, hd:hd + 1]
            rowv = acum_t[hd:hd + 1, :]
            decay = jnp.exp(jnp.where(tri, col - rowv, NEG_INF))
            scores = cb * decay * dt_t[hd:hd + 1, :]
            xh = xs[:, hd * SSD_HEAD_DIM:(hd + 1) * SSD_HEAD_DIM]
            y = _dot(scores, xh)
            hprev = jnp.where(fresh, 0.0, hs_ref[hd])
            y = y + jnp.exp(col) * _dot_nt(cgm, hprev)
            last = acum_t[hd:hd + 1, BLK - 1:BLK]
            w_end = jnp.exp(last - rowv) * dt_t[hd:hd + 1, :]
            xw_t = xs_t[hd * SSD_HEAD_DIM:(hd + 1) * SSD_HEAD_DIM, :] * w_end
            hnew = jnp.exp(last) * hprev + _dot(xw_t, bg)
            hs_ref[hd] = hnew
            ssdh_ref[hd] = hnew
            ys.append(y + dskip_ref[:, hd * SSD_HEAD_DIM:(hd + 1) * SSD_HEAD_DIM] * xh)
    y = jnp.concatenate(ys, axis=1) * _silu(z_ref[rows, :])
    o_ref[rows, D_LRU:] = _rms(y, nw_ref[...]).astype(o_ref.dtype)


def _mix1_prompt(cx, cg, z, xbc, dt, wts, t_rows):
    n = cx.shape[0]
    step = MIX1_BLKS * BLK
    assert t_rows % step == 0 and n % step == 0 and step <= TT
    nstep = t_rows // step
    last = n // step - 1

    def cur(j):
        return (jnp.where(j == 0, last, j - 1), 0)

    def const2(j):
        return (0, 0)

    in_specs = [pl.BlockSpec((step, D_LRU), cur), pl.BlockSpec((step, D_LRU), cur),
                pl.BlockSpec((step, D_SSD), cur), pl.BlockSpec((step, D_XBC), cur),
                pl.BlockSpec((step, LANES), cur)]
    in_specs += [pl.BlockSpec(w.shape, const2) for w in wts]
    return pl.pallas_call(
        functools.partial(_mix1_prompt_kernel, nblk=nstep),
        grid=(last + 1,),
        in_specs=in_specs,
        out_specs=[pl.BlockSpec((step, D_MIX_ODD), cur),
                   pl.BlockSpec((1, D_LRU), const2),
                   pl.BlockSpec((SSD_HEADS, SSD_HEAD_DIM, SSD_STATE), lambda j: (0, 0, 0))],
        out_shape=[jax.ShapeDtypeStruct((n, D_MIX_ODD), BF16),
                   jax.ShapeDtypeStruct((1, D_LRU), F32),
                   jax.ShapeDtypeStruct((SSD_HEADS, SSD_HEAD_DIM, SSD_STATE), F32)],
        scratch_shapes=[pltpu.VMEM((8, D_LRU), F32),
                        pltpu.VMEM((8, D_XBC), F32),
                        pltpu.VMEM((1, D_LRU), F32),
                        pltpu.VMEM((SSD_HEADS, SSD_HEAD_DIM, SSD_STATE), F32)],
        compiler_params=_cparams("arbitrary"),
        name="mix1_prompt",
    )(cx, cg, z, xbc, dt, *wts)


def _conv_step(seq, t, w_ref, b_ref):
    out = b_ref[...]
    for tap in range(CONV_WIDTH):
        out = out + seq[t + tap] * w_ref[tap:tap + 1, :]
    return out


def _lru_sample_kernel(cx_ref, cg_ref, buf_ref, h0_ref, cwc_ref, cbc_ref, wgate_ref, bgate_ref,
                       lam_ref, o_ref, hout_ref, *, n_new):
    seq = [buf_ref[i] for i in range(CONV_WIDTH - 1)] + [cx_ref[i] for i in range(n_new)]
    lam_sp = _softplus(-lam_ref[...])
    h = h0_ref[...]
    for t in range(n_new):
        xc = _conv_step(seq, t, cwc_ref, cbc_ref)
        a, bx = _lru_gates(xc, wgate_ref, bgate_ref, lam_sp)
        h = a * h + bx
        o_ref[t] = h * _gelu_tanh(cg_ref[t])
    hout_ref[...] = h


def _lru_sample(cx_tm, cg_tm, buf_tm, h0, wts):
    n_new = cx_tm.shape[0]
    return pl.pallas_call(
        functools.partial(_lru_sample_kernel, n_new=n_new),
        out_shape=[jax.ShapeDtypeStruct(cx_tm.shape, F32), jax.ShapeDtypeStruct(h0.shape, F32)],
        compiler_params=pltpu.CompilerParams(vmem_limit_bytes=VMEM_LIMIT),
        name="lru_sample",
    )(cx_tm, cg_tm, buf_tm, h0, *wts)


def _split3(x):
    hi = x.astype(BF16)
    r = x - hi.astype(F32)
    mid = r.astype(BF16)
    lo = (r - mid.astype(F32)).astype(BF16)
    return hi, mid, lo


def _ssd_sample_kernel(xbc_ref, dt_ref, z_ref, buf_ref, h0_ref, cwd_ref, cbd_ref, dtb_ref, alog_ref,
                       dskip_ref, nw_ref, o_ref, hout_ref, xc_ref, dec_ref, dtx_ref, y_ref, *, n_new, bs):
    seq = [buf_ref[i] for i in range(CONV_WIDTH - 1)] + [xbc_ref[i] for i in range(n_new)]
    a_neg = -jnp.exp(alog_ref[...])
    hcol = lax.broadcasted_iota(I32, (LANES, D_SSD), 1) // SSD_HEAD_DIM
    expand = jnp.where(lax.broadcasted_iota(I32, (LANES, D_SSD), 0) == hcol, 1.0, 0.0).astype(BF16)

    def widen(v):
        parts = _split3(v)
        return sum(jnp.dot(p, expand, preferred_element_type=F32) for p in parts)

    for t in range(n_new):
        xc = _silu(_conv_step(seq, t, cwd_ref, cbd_ref))
        xc_ref[t] = xc
        dt = _softplus(dt_ref[t] + dtb_ref[...])
        dtw = widen(dt)
        dec_ref[t] = jnp.exp(widen(dt * a_neg))
        dtx_ref[t] = dtw * xc[:, :D_SSD]

    zero_rows = jnp.zeros((LANES - 2 * n_new * SSD_GROUPS, SSD_STATE), F32)
    half = D_SSD // SSD_GROUPS

    def per_seq(b, c):
        rows = [xc_ref[t, pl.ds(b, 1), :] for t in range(n_new)]
        bc = [r[:, D_SSD + k * SSD_STATE:D_SSD + (k + 1) * SSD_STATE]
              for r in rows for k in range(2 * SSD_GROUPS)]
        bc_t = jnp.concatenate(bc + [zero_rows], axis=0).T
        ht = h0_ref[b].T
        for t in range(n_new):
            bcol = [jnp.broadcast_to(bc_t[:, t * 4 + g:t * 4 + g + 1], (SSD_STATE, half))
                    for g in range(SSD_GROUPS)]
            ccol = [jnp.broadcast_to(bc_t[:, t * 4 + SSD_GROUPS + g:t * 4 + SSD_GROUPS + g + 1],
                                     (SSD_STATE, half)) for g in range(SSD_GROUPS)]
            bw = jnp.concatenate(bcol, axis=1)
            cw = jnp.concatenate(ccol, axis=1)
            ht = ht * dec_ref[t, pl.ds(b, 1), :] + bw * dtx_ref[t, pl.ds(b, 1), :]
            y = jnp.sum(cw * ht, axis=0, keepdims=True)
            y_ref[t, pl.ds(b, 1), :] = y + dskip_ref[...] * rows[t][:, :D_SSD]
        hout_ref[b] = ht.T
        return c

    lax.fori_loop(0, bs, per_seq, 0)
    for t in range(n_new):
        o_ref[t] = _rms(y_ref[t] * _silu(z_ref[t]), nw_ref[...])


def _ssd_sample(xbc_tm, dt_tm, z_tm, buf_tm, h0, wts):
    n_new, b, _ = xbc_tm.shape
    bs = 16 if b % 16 == 0 else b
    hp = SSD_HEADS * SSD_HEAD_DIM

    def tm(c):
        return pl.BlockSpec((n_new, bs, c), lambda i: (0, i, 0))

    in_specs = [tm(D_XBC), tm(LANES), tm(D_SSD),
                pl.BlockSpec((CONV_WIDTH - 1, bs, D_XBC), lambda i: (0, i, 0)),
                pl.BlockSpec((bs, hp, SSD_STATE), lambda i: (i, 0, 0))]
    in_specs += [pl.BlockSpec(w.shape, lambda i: (0, 0)) for w in wts]
    return pl.pallas_call(
        functools.partial(_ssd_sample_kernel, n_new=n_new, bs=bs),
        grid=(b // bs,),
        in_specs=in_specs,
        out_specs=[tm(D_SSD), pl.BlockSpec((bs, hp, SSD_STATE), lambda i: (i, 0, 0))],
        out_shape=[jax.ShapeDtypeStruct((n_new, b, D_SSD), F32),
                   jax.ShapeDtypeStruct((b, hp, SSD_STATE), F32)],
        scratch_shapes=[pltpu.VMEM((n_new, bs, D_XBC), F32),
                        pltpu.VMEM((n_new, bs, D_SSD), F32),
                        pltpu.VMEM((n_new, bs, D_SSD), F32),
                        pltpu.VMEM((n_new, bs, D_SSD), F32)],
        compiler_params=_cparams("parallel"),
        name="ssd_sample",
    )(xbc_tm, dt_tm, z_tm, buf_tm, h0, *wts)


def _block_diag(w):
    g, c, _ = w.shape
    eye = jnp.eye(g, dtype=w.dtype)
    return (eye[:, None, :, None] * w[:, :, None, :]).reshape(g * c, g * c)


def _router_weights(w_rg, b_rg, w_re, b_re):
    w = jnp.concatenate([w_rg, w_re], axis=1)
    w = jnp.pad(w, ((0, 0), (0, LANES - w.shape[1])))
    hi = w.astype(BF16)
    lo = (w - hi.astype(F32)).astype(BF16)
    b = jnp.pad(jnp.concatenate([b_rg, b_re]), (0, LANES - MOE_GROUPS - N_EXPERTS)).reshape(1, LANES)
    return hi, lo, b


def kernel(x_prompt, x_sample, cache_swa_k, cache_swa_v, cache_pool, state_lru_conv, state_lru_h, state_ssd_conv, state_ssd_h, meta_tokens, norm_mix, norm_ffn, norm_final, attn_w_in, attn_w_out, attn_sinks, pool_w, pool_scale, rec_w_in, rec_w_out, lru_conv_w, lru_conv_b, lru_w_rg, lru_b_rg, lru_w_ig, lru_b_ig, lru_lambda, ssd_conv_w, ssd_conv_b, ssd_dt_bias, ssd_a_log, ssd_d, ssd_norm_w, moe_w_rg, moe_b_rg, moe_w_re, moe_b_re, moe_w_gate, moe_w_up, moe_w_down):
    t_rows = x_prompt.shape[1]
    b, s_new = x_sample.shape[:2]
    ns = b * s_new
    small = -(-ns // TT) * TT + TT
    n = t_rows + small
    meta_blk = n // BLK - 1
    valid_lo = t_rows + ns
    valid_mid = n - N_META
    assert x_prompt.shape[0] == 1 and t_rows % TT == 0 and ns % BLK == 0 and ns <= TT

    x_small = jnp.concatenate([
        x_sample.transpose(1, 0, 2).reshape(ns, D_MODEL),
        jnp.zeros((small - ns - N_META, D_MODEL), F32),
        meta_tokens.astype(F32)], axis=0)
    xs0 = (x_prompt[0], x_small)

    def sample_tm(a):
        return a[t_rows:t_rows + ns].reshape(s_new, b, a.shape[1])

    q, kv, p = _norm_proj(xs0, norm_mix[0], attn_w_in[0].astype(BF16), (D_ATTN, 2 * D_KV, D_POOL),
                          (BF16, F32, F32))
    wpool_bd = _block_diag(pool_w[0]).astype(BF16)
    mix = _mix0_prompt(q, kv, p, attn_sinks[0], wpool_bd, pool_scale[0], t_rows, meta_blk)
    nq = s_new * GQA
    qs = sample_tm(q).reshape(s_new, b, N_KV_HEADS, GQA, HEAD_DIM).transpose(1, 2, 0, 3, 4)
    qs = qs.reshape(b, N_KV_HEADS, nq, HEAD_DIM)
    kv_s = sample_tm(kv).transpose(1, 0, 2)
    kn, vn = kv_s[:, :, :D_KV], kv_s[:, :, D_KV:]
    ck = cache_swa_k[0].reshape(b, N_META + WINDOW, D_KV)
    cv = cache_swa_v[0].reshape(b, N_META + WINDOW, D_KV)
    o_s = _attn_sample(qs, ck, cv, kn, vn, attn_sinks[0])
    o_s = o_s.reshape(b, N_KV_HEADS, s_new, GQA, HEAD_DIM).transpose(2, 0, 1, 3, 4).reshape(s_new, b, D_ATTN)
    p_s = sample_tm(p)
    pool_buf_tm = cache_pool[0].transpose(1, 0, 2)
    yp_s = _pool_sample(pool_buf_tm, p_s, wpool_bd, pool_scale[0])
    mix_s = jnp.concatenate([o_s, yp_s], axis=-1).reshape(ns, D_MIX_EVEN)
    wr_hi, wr_lo, br = _router_weights(moe_w_rg[0], moe_b_rg[0], moe_w_re[0], moe_b_re[0])
    x1, xn, info, infot, cnt = _outproj_route(xs0, mix, mix_s, attn_w_out[0].astype(BF16), norm_ffn[0],
                                              wr_hi, wr_lo, br, t_rows, valid_lo, valid_mid)
    x, = _moe(x1, xn, info, infot, cnt, moe_w_gate, moe_w_up, moe_w_down, 0, norm_final,
              t_rows, ns, False)

    meta_rows = slice(n - N_META, n)
    last_rows = slice(t_rows - WINDOW, t_rows)
    swa_k_p = jnp.concatenate([kv[meta_rows, :D_KV], kv[last_rows, :D_KV]], axis=0)
    swa_v_p = jnp.concatenate([kv[meta_rows, D_KV:], kv[last_rows, D_KV:]], axis=0)
    swa_k_p = swa_k_p.reshape(1, 1, N_META + WINDOW, N_KV_HEADS, HEAD_DIM)
    swa_v_p = swa_v_p.reshape(1, 1, N_META + WINDOW, N_KV_HEADS, HEAD_DIM)
    pool_p = p[t_rows - POOL_BUF:t_rows].reshape(1, 1, POOL_BUF, D_POOL)
    swa_k_s = jnp.concatenate([ck[:, :N_META], ck[:, N_META + s_new:], kn], axis=1)
    swa_v_s = jnp.concatenate([cv[:, :N_META], cv[:, N_META + s_new:], vn], axis=1)
    swa_k_s = swa_k_s.reshape(1, b, N_META + WINDOW, N_KV_HEADS, HEAD_DIM)
    swa_v_s = swa_v_s.reshape(1, b, N_META + WINDOW, N_KV_HEADS, HEAD_DIM)
    pool_s = jnp.concatenate([cache_pool[0], p_s.transpose(1, 0, 2)], axis=1)[:, -POOL_BUF:][None]

    d_in_odd = rec_w_in.shape[2]
    w_in1 = jnp.pad(rec_w_in[0], ((0, 0), (0, LANES - SSD_HEADS))).astype(BF16)
    assert d_in_odd == 2 * D_LRU + D_SSD + D_XBC + SSD_HEADS
    cx, cg, z, xbc, dt = _norm_proj((x,), norm_mix[1], w_in1, (D_LRU, D_LRU, D_SSD, D_XBC, LANES))
    wgate = jnp.concatenate([_block_diag(lru_w_rg[0]), _block_diag(lru_w_ig[0])], axis=1).astype(BF16)
    bgate = jnp.concatenate([lru_b_rg[0], lru_b_ig[0]]).reshape(1, 2 * D_LRU)
    pad8 = (0, LANES - SSD_HEADS)
    dtb = jnp.pad(ssd_dt_bias[0], pad8).reshape(1, LANES)
    alog = jnp.pad(ssd_a_log[0], pad8).reshape(1, LANES)
    dskip = jnp.repeat(ssd_d[0], SSD_HEAD_DIM).reshape(1, D_SSD)
    lru_w = (lru_conv_w[0], lru_conv_b[0].reshape(1, D_LRU), wgate, bgate, lru_lambda[0].reshape(1, D_LRU))
    ssd_w = (ssd_conv_w[0], ssd_conv_b[0].reshape(1, D_XBC), dtb, alog, dskip, ssd_norm_w[0].reshape(1, D_SSD))
    mix, lru_h_p, ssd_h_p = _mix1_prompt(cx, cg, z, xbc, dt, lru_w + ssd_w, t_rows)
    cx_s, cg_s, z_s, xbc_s, dt_s = (sample_tm(a) for a in (cx, cg, z, xbc, dt))
    yc_s, lru_h_s = _lru_sample(cx_s, cg_s, state_lru_conv[0].transpose(1, 0, 2), state_lru_h[0], lru_w)
    h0 = state_ssd_h[0].reshape(b, SSD_HEADS * SSD_HEAD_DIM, SSD_STATE)
    yd_s, ssd_h_s = _ssd_sample(xbc_s, dt_s, z_s, state_ssd_conv[0].transpose(1, 0, 2), h0, ssd_w)
    mix_s = jnp.concatenate([yc_s, yd_s], axis=-1).reshape(ns, D_MIX_ODD)
    wr_hi, wr_lo, br = _router_weights(moe_w_rg[1], moe_b_rg[1], moe_w_re[1], moe_b_re[1])
    x1, xn, info, infot, cnt = _outproj_route((x,), mix, mix_s, rec_w_out[0].astype(BF16), norm_ffn[1],
                                              wr_hi, wr_lo, br, t_rows, valid_lo, valid_mid)
    y_p, y_small = _moe(x1, xn, info, infot, cnt, moe_w_gate, moe_w_up, moe_w_down, 1, norm_final,
                        t_rows, ns, True)

    tail = CONV_WIDTH - 1
    lru_conv_p = cx[t_rows - tail:t_rows].reshape(1, 1, tail, D_LRU)
    ssd_conv_p = xbc[t_rows - tail:t_rows].reshape(1, 1, tail, D_XBC)
    lru_conv_s = jnp.concatenate([state_lru_conv[0], cx_s.transpose(1, 0, 2)], axis=1)[:, -tail:][None]
    ssd_conv_s = jnp.concatenate([state_ssd_conv[0], xbc_s.transpose(1, 0, 2)], axis=1)[:, -tail:][None]

    y_prompt = y_p[None]
    y_sample = y_small[:ns].reshape(s_new, b, D_MODEL).transpose(1, 0, 2)
    return (y_prompt, y_sample, swa_k_p, swa_v_p, pool_p, lru_conv_p,
            lru_h_p.reshape(1, 1, D_LRU), ssd_conv_p,
            ssd_h_p.reshape(1, 1, SSD_HEADS, SSD_HEAD_DIM, SSD_STATE),
            swa_k_s, swa_v_s, pool_s, lru_conv_s, lru_h_s[None], ssd_conv_s,
            ssd_h_s.reshape(1, b, SSD_HEADS, SSD_HEAD_DIM, SSD_STATE))
```

```python
import functools

import jax
import jax.numpy as jnp
from jax import lax
from jax.experimental import pallas as pl
from jax.experimental.pallas import tpu as pltpu

F32 = jnp.float32
BF16 = jnp.bfloat16
I32 = jnp.int32

D_MODEL = 1024
N_META = 16
EPS = 1e-6
PAST_LEN = 16384
N_Q_HEADS = 16
N_KV_HEADS = 2
GQA = N_Q_HEADS // N_KV_HEADS
HEAD_DIM = 64
WINDOW = 128
D_ATTN = N_Q_HEADS * HEAD_DIM
D_KV = N_KV_HEADS * HEAD_DIM
POOL_WINDOWS = (2, 4, 8, 16)
POOL_GROUP_DIM = 96
D_POOL = len(POOL_WINDOWS) * POOL_GROUP_DIM
POOL_BUF = max(POOL_WINDOWS) - 1
D_MIX_EVEN = D_ATTN + D_POOL
D_LRU = 512
LRU_HEADS = 8
LRU_BLOCK = D_LRU // LRU_HEADS
LRU_C = 8.0
CONV_WIDTH = 4
SSD_HEADS = 8
SSD_HEAD_DIM = 64
D_SSD = SSD_HEADS * SSD_HEAD_DIM
SSD_GROUPS = 2
SSD_STATE = 128
D_XBC = D_SSD + 2 * SSD_GROUPS * SSD_STATE
D_MIX_ODD = D_LRU + D_SSD
MOE_GROUPS = 4
EXPERTS_PER_GROUP = 8
N_EXPERTS = MOE_GROUPS * EXPERTS_PER_GROUP
D_EXPERT = 256

BLK = 128
META_PAD = BLK - N_META
LANES = 128
MOE_TM = 512
FFN_ROWS = 256
MIX0_BLKS = 4
MIX1_BLKS = 4
NEG_INF = float("-inf")
VMEM_LIMIT = 56 * 1024 * 1024


def _cparams(*sem):
    return pltpu.CompilerParams(dimension_semantics=sem, vmem_limit_bytes=VMEM_LIMIT)


TT = 512
SUB = 8


def _small_tile_ranges(t_rows, ns, n):
    out = {}
    for k in range(t_rows // TT, n // TT):
        lo = k * TT
        ranges = []
        a0, a1 = max(lo, t_rows), min(lo + TT, t_rows + ns)
        if a1 > a0:
            ranges.append((a0 - lo, a1 - lo))
        b0 = max(lo, n - N_META)
        if lo + TT > b0:
            ranges.append((b0 - lo, TT))
        out[k] = tuple(ranges)
    return out


def _for_tile_rows(i, n_full, ranges, fn):
    @pl.when(i < n_full)
    def _():
        fn(0, TT)

    for k, rs in ranges.items():
        if rs:
            @pl.when(i == k)
            def _(rs=rs):
                for lo, hi in rs:
                    fn(lo, hi)


def _slab(c, rows, base=0):
    return pl.ds(base + c, rows, stride=SUB)


def _tile_of(r):
    return pl.ds(pl.multiple_of(r * SUB, SUB), SUB)


def _sigmoid(x):
    return 1.0 / (1.0 + jnp.exp(-x))


def _silu(x):
    return x * _sigmoid(x)


def _softplus(x):
    return jnp.maximum(x, 0.0) + jnp.log1p(jnp.exp(-jnp.abs(x)))


def _gelu_tanh(x):
    return 0.5 * x * (1.0 + jnp.tanh(0.7978845608028654 * (x + 0.044715 * x * x * x)))


def _dot(a, b):
    return jnp.dot(a.astype(BF16), b.astype(BF16), preferred_element_type=F32)


def _dot_nt(a, b):
    return lax.dot_general(a.astype(BF16), b.astype(BF16), (((1,), (1,)), ((), ())),
                           preferred_element_type=F32)


def _rms(x, g):
    ms = jnp.mean(x * x, axis=-1, keepdims=True)
    return x * lax.rsqrt(ms + EPS) * g


def _row_specs(xs, width):
    if len(xs) == 1:
        return [pl.BlockSpec((TT, width), lambda i: (i, 0))]
    na = xs[0].shape[0] // TT
    return [pl.BlockSpec((TT, width), lambda i: (jnp.minimum(i, na - 1), 0)),
            pl.BlockSpec((TT, width), lambda i: (jnp.maximum(i - na, 0), 0))]


def _pick_rows(refs, n_first):
    if len(refs) == 1:
        return refs[0][...]
    return jnp.where(pl.program_id(0) < n_first, refs[0][...], refs[1][...])


def _norm_proj_kernel(*refs, splits, n_x, n_first):
    x_refs, (g_ref, w_ref), out_refs = refs[:n_x], refs[n_x:n_x + 2], refs[n_x + 2:]
    y = _rms(_pick_rows(x_refs, n_first), g_ref[...]).astype(BF16)
    off = 0
    for o_ref, n in zip(out_refs, splits):
        o_ref[...] = jnp.dot(y, w_ref[:, off:off + n], preferred_element_type=F32).astype(o_ref.dtype)
        off += n


def _norm_proj(xs, g, w_bf16, splits, dtypes=None):
    n = sum(x.shape[0] for x in xs)
    dtypes = dtypes or (F32,) * len(splits)
    return pl.pallas_call(
        functools.partial(_norm_proj_kernel, splits=splits, n_x=len(xs), n_first=xs[0].shape[0] // TT),
        grid=(n // TT,),
        in_specs=_row_specs(xs, D_MODEL) + [pl.BlockSpec((1, D_MODEL), lambda i: (0, 0)),
                                             pl.BlockSpec(w_bf16.shape, lambda i: (0, 0))],
        out_specs=[pl.BlockSpec((TT, s), lambda i: (i, 0)) for s in splits],
        out_shape=[jax.ShapeDtypeStruct((n, s), dt) for s, dt in zip(splits, dtypes)],
        compiler_params=_cparams("parallel"),
        name="norm_proj",
    )(*xs, g.reshape(1, D_MODEL), w_bf16)


def _mix0_prompt_kernel(*refs, nblk):
    j = pl.program_id(0)
    o_ref = refs[8]

    @pl.when(j <= nblk)
    def _():
        _mix0_prompt_step(*refs)

    @pl.when(j > nblk)
    def _():
        o_ref[...] = jnp.zeros(o_ref.shape, o_ref.dtype)


def _mix0_prompt_step(sink_ref, q_ref, kvc_ref, kvp_ref, kvm_ref, p_ref, wp_ref, ps_ref,
                      o_ref, seq_ref, s_ref, pr_ref, rd_ref):
    j = pl.program_id(0)
    kv_meta = jnp.concatenate([kvm_ref[...], jnp.zeros((META_PAD, 2 * D_KV), F32)], axis=0)
    kv_blocks = [_kv_halves(kvp_ref[...])]
    kv_blocks += [_kv_halves(kvc_ref[sub * BLK:(sub + 1) * BLK, :]) for sub in range(MIX0_BLKS)]
    kv_meta = _kv_halves(kv_meta)
    r = lax.broadcasted_iota(I32, (BLK, BLK), 0)
    c = lax.broadcasted_iota(I32, (BLK, BLK), 1)
    own = r >= c
    dist_loc = jnp.where(own, r - c, r - c + BLK).astype(F32)
    for sub in range(MIX0_BLKS):
        blk = (j - 1) * MIX0_BLKS + sub
        j_prev = 2 if sub == 0 else 1
        c_min_meta = META_PAD if sub == MIX0_BLKS - 1 else BLK
        keys = [kv_blocks[sub], kv_blocks[sub + 1], kv_meta]
        _attn_block(sink_ref, q_ref, keys, o_ref, s_ref, pr_ref, rd_ref,
                    sub * BLK, j, blk, j_prev, c_min_meta, r, c, own, dist_loc)
    _pool_rows(j, p_ref, wp_ref, ps_ref, o_ref, seq_ref)


def _kv_halves(kv):
    k_all, v_all = kv[:, :D_KV], kv[:, D_KV:]
    k_rot = pltpu.roll(k_all, HEAD_DIM, 1)
    v_rot = pltpu.roll(v_all, HEAD_DIM, 1)
    lo_half = lax.broadcasted_iota(I32, k_all.shape, 1) < HEAD_DIM

    def halves(own_lanes, other_lanes, g):
        src_lo, src_hi = (own_lanes, other_lanes) if g == 0 else (other_lanes, own_lanes)
        return (jnp.where(lo_half, src_lo, 0.0).astype(BF16), jnp.where(lo_half, 0.0, src_hi).astype(BF16))

    return [halves(k_all, k_rot, g) + halves(v_all, v_rot, g) for g in range(N_KV_HEADS)]


def _attn_block(sink_ref, q_ref, keys, o_ref, s_ref, pr_ref, rd_ref,
                ro, j, blk, j_prev, c_min_meta, r, c, own, dist_loc):
    d_min = jnp.where(j >= j_prev, -BLK, 0)
    c_min = jnp.where(j == 0, c_min_meta, 0)
    mask_loc = jnp.where((r - c >= d_min) & (c >= c_min), 0.0, NEG_INF)
    meta_hi = jnp.where(j >= 1, N_META, 0)
    dist_meta = jnp.where(c < meta_hi, (N_META + blk * BLK + r - c).astype(F32), jnp.inf)
    slabs = GQA // 2
    nkeys = 3 * BLK
    for g in range(N_KV_HEADS):
        k_even, k_odd, v_even, v_odd = (jnp.concatenate([blk_halves[g][part] for blk_halves in keys], axis=0)
                                        for part in range(4))
        qg = jnp.concatenate([q_ref[ro:ro + BLK, (g * slabs + i) * LANES:(g * slabs + i + 1) * LANES]
                              for i in range(slabs)], axis=0) * (HEAD_DIM ** -0.5)
        s_ref[...] = _dot_nt(qg, jnp.concatenate([k_even, k_odd], axis=0))
        for i in range(slabs):
            rows = slice(i * BLK, (i + 1) * BLK)
            for par in range(2):
                h = g * GQA + 2 * i + par
                slope = 2.0 ** (-8.0 * (h + 1) / N_Q_HEADS)
                sink = sink_ref[h]
                k0 = par * nkeys
                s_loc = (jnp.where(own, s_ref[rows, k0 + BLK:k0 + 2 * BLK], s_ref[rows, k0:k0 + BLK])
                         - slope * dist_loc + mask_loc)
                s_met = s_ref[rows, k0 + 2 * BLK:k0 + 3 * BLK] - slope * dist_meta
                m = jnp.maximum(jnp.max(jnp.maximum(s_loc, s_met), axis=-1, keepdims=True), sink)
                p_loc = jnp.exp(s_loc - m)
                p_met = jnp.exp(s_met - m)
                den = jnp.sum(p_loc + p_met, axis=-1, keepdims=True) + jnp.exp(sink - m)
                pr_ref[par, rows, 0:BLK] = jnp.where(own, 0.0, p_loc).astype(BF16)
                pr_ref[par, rows, BLK:2 * BLK] = jnp.where(own, p_loc, 0.0).astype(BF16)
                pr_ref[par, rows, 2 * BLK:] = p_met.astype(BF16)
                rd_ref[par, rows, :] = jnp.broadcast_to(1.0 / den, (BLK, LANES))
        o = (jnp.dot(pr_ref[0], v_even, preferred_element_type=F32) * rd_ref[0]
             + jnp.dot(pr_ref[1], v_odd, preferred_element_type=F32) * rd_ref[1])
        for i in range(slabs):
            o_ref[ro:ro + BLK, (g * slabs + i) * LANES:(g * slabs + i + 1) * LANES] = (
                o[i * BLK:(i + 1) * BLK, :].astype(o_ref.dtype))


def _pool_rows(j, p_ref, wp_ref, ps_ref, o_ref, seq_ref):
    n_rows = p_ref.shape[0]
    carry = jnp.where(j == 0, 0.0, seq_ref[...])
    s0 = jnp.concatenate([carry, p_ref[...]], axis=0)
    s1 = s0 + pltpu.roll(s0, 1, 0)
    s2 = s1 + pltpu.roll(s1, 2, 0)
    s3 = s2 + pltpu.roll(s2, 4, 0)
    s4 = s3 + pltpu.roll(s3, 8, 0)
    rows = N_META + n_rows
    col = lax.broadcasted_iota(I32, (rows, D_POOL), 1)
    row = lax.broadcasted_iota(I32, (rows, D_POOL), 0) - N_META
    wsum = jnp.where(col < POOL_GROUP_DIM, s1,
                     jnp.where(col < 2 * POOL_GROUP_DIM, s2,
                               jnp.where(col < 3 * POOL_GROUP_DIM, s3, s4)))
    wlen = jnp.where(col < POOL_GROUP_DIM, 2,
                     jnp.where(col < 2 * POOL_GROUP_DIM, 4,
                               jnp.where(col < 3 * POOL_GROUP_DIM, 8, 16)))
    pos = row + jnp.where(j == 0, N_META - n_rows, N_META + (j - 1) * n_rows)
    cnt = jnp.clip(jnp.minimum(pos + 1, wlen), 1, 16).astype(F32)
    d = (wsum / cnt - s0)[N_META:, :]
    o_ref[:, D_ATTN:] = (_dot(d, wp_ref[...]) * ps_ref[...]).astype(o_ref.dtype)
    seq_ref[...] = p_ref[n_rows - N_META:, :]


def _mix0_prompt(q, kv, p, sinks, wpool_bd, pool_scale, t_rows, meta_blk):
    n = q.shape[0]
    step = MIX0_BLKS * BLK
    assert t_rows % step == 0 and n % step == 0 and step <= TT
    nstep = t_rows // step
    last = n // step - 1

    def cur(j):
        return jnp.where(j == 0, last, j - 1)

    meta16 = (meta_blk * BLK + META_PAD) // N_META
    return pl.pallas_call(
        functools.partial(_mix0_prompt_kernel, nblk=nstep),
        grid=(last + 1,),
        in_specs=[pl.BlockSpec(memory_space=pltpu.SMEM),
                  pl.BlockSpec((step, D_ATTN), lambda j: (cur(j), 0)),
                  pl.BlockSpec((step, 2 * D_KV), lambda j: (cur(j), 0)),
                  pl.BlockSpec((BLK, 2 * D_KV), lambda j: (jnp.maximum((j - 1) * MIX0_BLKS - 1, 0), 0)),
                  pl.BlockSpec((N_META, 2 * D_KV), lambda j: (meta16, 0)),
                  pl.BlockSpec((step, D_POOL), lambda j: (cur(j), 0)),
                  pl.BlockSpec((D_POOL, D_POOL), lambda j: (0, 0)),
                  pl.BlockSpec((1, D_POOL), lambda j: (0, 0))],
        out_specs=pl.BlockSpec((step, D_MIX_EVEN), lambda j: (cur(j), 0)),
        out_shape=jax.ShapeDtypeStruct((n, D_MIX_EVEN), BF16),
        scratch_shapes=[pltpu.VMEM((N_META, D_POOL), F32),
                        pltpu.VMEM((GQA // 2 * BLK, 2 * 3 * BLK), F32),
                        pltpu.VMEM((2, GQA // 2 * BLK, 3 * BLK), BF16),
                        pltpu.VMEM((2, GQA // 2 * BLK, LANES), F32)],
        compiler_params=_cparams("arbitrary"),
        name="mix0_prompt",
    )(sinks, q, kv, kv, kv, p, wpool_bd, pool_scale.reshape(1, D_POOL))


def _attn_sample_kernel(sink_ref, q_ref, ck_ref, cv_ref, kn_ref, vn_ref, o_ref, *, n_new):
    nk = N_META + WINDOW + n_new
    nq = n_new * GQA
    k_all = jnp.concatenate([ck_ref[...], kn_ref[...]], axis=1).astype(BF16)
    v_all = jnp.concatenate([cv_ref[...], vn_ref[...]], axis=1).astype(BF16)
    r = lax.broadcasted_iota(I32, (nq, nk), 0)
    slot = lax.broadcasted_iota(I32, (nq, nk), 1)
    t = r // GQA
    hh = r - t * GQA
    kpos = jnp.where(slot < N_META, slot,
                     jnp.where(slot < N_META + WINDOW, PAST_LEN - WINDOW - N_META + slot,
                               PAST_LEN - N_META - WINDOW + slot))
    dist = PAST_LEN + t - kpos
    allowed = (dist >= 0) & ((slot < N_META) | (dist < WINDOW))
    distf = dist.astype(F32)
    hh1 = hh[:, :1]
    for g in range(N_KV_HEADS):
        slope = jnp.exp2(-8.0 * (g * GQA + hh + 1).astype(F32) / N_Q_HEADS)
        sink = jnp.zeros((nq, 1), F32)
        for i in range(GQA):
            sink = jnp.where(hh1 == i, sink_ref[g * GQA + i], sink)
        kg = k_all[:, :, g * HEAD_DIM:(g + 1) * HEAD_DIM]
        vg = v_all[:, :, g * HEAD_DIM:(g + 1) * HEAD_DIM]
        s = jnp.einsum("bqd,bkd->bqk", q_ref[:, g].astype(BF16), kg,
                       preferred_element_type=F32) * (HEAD_DIM ** -0.5)
        s = jnp.where(allowed[None], s - (slope * distf)[None], NEG_INF)
        sink = sink[None]
        m = jnp.maximum(jnp.max(s, axis=-1, keepdims=True), sink)
        pr = jnp.exp(s - m)
        den = jnp.sum(pr, axis=-1, keepdims=True) + jnp.exp(sink - m)
        o = jnp.einsum("bqk,bkd->bqd", pr.astype(BF16), vg, preferred_element_type=F32)
        o_ref[:, g] = o / den


def _attn_sample(qs, ck, cv, kn, vn, sinks):
    b, _, nq, _ = qs.shape
    n_new = kn.shape[1]
    bs = 16 if b % 16 == 0 else b
    nc = N_META + WINDOW
    return pl.pallas_call(
        functools.partial(_attn_sample_kernel, n_new=n_new),
        grid=(b // bs,),
        in_specs=[pl.BlockSpec(memory_space=pltpu.SMEM),
                  pl.BlockSpec((bs, N_KV_HEADS, nq, HEAD_DIM), lambda i: (i, 0, 0, 0)),
                  pl.BlockSpec((bs, nc, D_KV), lambda i: (i, 0, 0)),
                  pl.BlockSpec((bs, nc, D_KV), lambda i: (i, 0, 0)),
                  pl.BlockSpec((bs, n_new, D_KV), lambda i: (i, 0, 0)),
                  pl.BlockSpec((bs, n_new, D_KV), lambda i: (i, 0, 0))],
        out_specs=pl.BlockSpec((bs, N_KV_HEADS, nq, HEAD_DIM), lambda i: (i, 0, 0, 0)),
        out_shape=jax.ShapeDtypeStruct(qs.shape, F32),
        compiler_params=_cparams("parallel"),
        name="attn_sample",
    )(sinks, qs, ck, cv, kn, vn)


def _pool_sample_kernel(buf_ref, p_ref, wp_ref, ps_ref, o_ref, *, n_new):
    seq = [buf_ref[i] for i in range(POOL_BUF)] + [p_ref[i] for i in range(n_new)]
    b = p_ref.shape[1]
    col = lax.broadcasted_iota(I32, (b, D_POOL), 1)
    for t in range(n_new):
        cur = seq[POOL_BUF + t]
        acc = cur
        sums = {}
        for back in range(1, max(POOL_WINDOWS)):
            acc = acc + seq[POOL_BUF + t - back]
            if back + 1 in POOL_WINDOWS:
                sums[back + 1] = acc
        mean = jnp.where(col < POOL_GROUP_DIM, sums[2] / 2.0,
                         jnp.where(col < 2 * POOL_GROUP_DIM, sums[4] / 4.0,
                                   jnp.where(col < 3 * POOL_GROUP_DIM, sums[8] / 8.0,
                                             sums[16] / 16.0)))
        o_ref[t] = _dot(mean - cur, wp_ref[...]) * ps_ref[...]


def _pool_sample(buf_tm, p_tm, wpool_bd, pool_scale):
    n_new = p_tm.shape[0]
    return pl.pallas_call(
        functools.partial(_pool_sample_kernel, n_new=n_new),
        out_shape=jax.ShapeDtypeStruct(p_tm.shape, F32),
        compiler_params=pltpu.CompilerParams(vmem_limit_bytes=VMEM_LIMIT),
        name="pool_sample",
    )(buf_tm, p_tm, wpool_bd, pool_scale.reshape(1, D_POOL))


def _outproj_route_kernel(*refs, valid_lo, valid_mid, n_x, n_first):
    tt = TT
    x_refs = refs[:n_x]
    (mix_ref, mixs_ref, w_ref, g_ref, wrh_ref, wrl_ref, br_ref,
     x1_ref, xn_ref, info_ref, infot_ref, cnt_ref, acc_ref) = refs[n_x:]
    i = pl.program_id(0)
    is_sample_tile = i == n_first
    acc = jnp.where(i == 0, 0.0, acc_ref[...])
    mix = mix_ref[...] + jnp.where(is_sample_tile, mixs_ref[...], 0).astype(mix_ref.dtype)
    x1 = _pick_rows(x_refs, n_first) + _dot(mix, w_ref[...])
    x1_ref[...] = x1
    xn = _rms(x1, g_ref[...])
    for cc_ in range(D_MODEL // LANES):
        xn_ref[_slab(cc_, tt), :] = xn[:, cc_ * LANES:(cc_ + 1) * LANES]
    hi = xn.astype(BF16)
    lo = (xn - hi.astype(F32)).astype(BF16)
    hi_out = jnp.dot(hi, jnp.concatenate([wrh_ref[...], wrl_ref[...]], axis=1), preferred_element_type=F32)
    logits = (hi_out[:, :LANES] + jnp.dot(lo, wrh_ref[...], preferred_element_type=F32)
              + hi_out[:, LANES:]) + br_ref[...]
    lane = lax.broadcasted_iota(I32, (tt, LANES), 1)
    lanef = lane.astype(F32)
    lg = jnp.where(lane < MOE_GROUPS, logits, NEG_INF)
    mg = jnp.max(lg, axis=-1, keepdims=True)
    gstar = jnp.min(jnp.where(lg == mg, lanef, 1e9), axis=-1, keepdims=True)
    pgroup = 1.0 / jnp.sum(jnp.exp(lg - mg), axis=-1, keepdims=True)
    lo_lane = MOE_GROUPS + gstar * EXPERTS_PER_GROUP
    le = jnp.where((lanef >= lo_lane) & (lanef < lo_lane + EXPERTS_PER_GROUP), logits, NEG_INF)
    v1 = jnp.max(le, axis=-1, keepdims=True)
    i1 = jnp.min(jnp.where(le == v1, lanef, 1e9), axis=-1, keepdims=True)
    le2 = jnp.where(lanef == i1, NEG_INF, le)
    v2 = jnp.max(le2, axis=-1, keepdims=True)
    i2 = jnp.min(jnp.where(le2 == v2, lanef, 1e9), axis=-1, keepdims=True)
    tq = jnp.exp(v2 - v1)
    w1 = pgroup / (1.0 + tq)
    w2 = pgroup * tq / (1.0 + tq)
    e1 = i1 - MOE_GROUPS
    e2 = i2 - MOE_GROUPS
    rowg = i * tt + lax.broadcasted_iota(I32, (tt, LANES), 0)
    valid = (rowg < valid_lo) | (rowg >= valid_mid)
    oh1 = jnp.where(valid & (lanef == e1), 1.0, 0.0)
    oh2 = jnp.where(valid & (lanef == e2), 1.0, 0.0)
    both = oh1 + oh2
    rr = lax.broadcasted_iota(I32, (tt, tt), 0)
    cc = lax.broadcasted_iota(I32, (tt, tt), 1)
    tril = jnp.where(cc < rr, 1.0, 0.0).astype(BF16)
    base = acc + jnp.dot(tril, both.astype(BF16), preferred_element_type=F32)
    rank1 = jnp.sum(oh1 * base, axis=-1, keepdims=True)
    rank2 = jnp.sum(oh2 * base, axis=-1, keepdims=True)
    info = jnp.where(lane == 0, e1, jnp.where(lane == 1, e2, jnp.where(lane == 2, w1, jnp.where(
        lane == 3, w2, jnp.where(lane == 4, rank1, jnp.where(lane == 5, rank2, 0.0))))))
    info_ref[...] = info
    infot_ref[0] = info.T[:SUB, :]
    total = acc + jnp.sum(both, axis=0, keepdims=True)
    acc_ref[...] = total
    cnt_ref[...] = jnp.broadcast_to(total, (8, LANES))


def _outproj_route(xs, mix, mix_s, w_out_bf16, g_ffn, wr_hi, wr_lo, br, t_rows, valid_lo, valid_mid):
    n = mix.shape[0]
    ns, dm = mix_s.shape
    mix_s = jnp.pad(mix_s, ((0, TT - ns), (0, 0))).astype(mix.dtype)
    nt = n // TT

    def const2(i):
        return (0, 0)

    return pl.pallas_call(
        functools.partial(_outproj_route_kernel, valid_lo=valid_lo, valid_mid=valid_mid,
                          n_x=len(xs), n_first=t_rows // TT),
        grid=(nt,),
        in_specs=_row_specs(xs, D_MODEL) + [
            pl.BlockSpec((TT, dm), lambda i: (i, 0)),
            pl.BlockSpec((TT, dm), const2),
            pl.BlockSpec((dm, D_MODEL), const2),
            pl.BlockSpec((1, D_MODEL), const2),
            pl.BlockSpec((D_MODEL, LANES), const2),
            pl.BlockSpec((D_MODEL, LANES), const2),
            pl.BlockSpec((1, LANES), const2)],
        out_specs=[pl.BlockSpec((TT, D_MODEL), lambda i: (i, 0)),
                   pl.BlockSpec((TT * SUB, LANES), lambda i: (i, 0)),
                   pl.BlockSpec((TT, LANES), lambda i: (i, 0)),
                   pl.BlockSpec((1, SUB, TT), lambda i: (i, 0, 0)),
                   pl.BlockSpec((8, LANES), const2)],
        out_shape=[jax.ShapeDtypeStruct((n, D_MODEL), F32),
                   jax.ShapeDtypeStruct((n * SUB, LANES), F32),
                   jax.ShapeDtypeStruct((n, LANES), F32),
                   jax.ShapeDtypeStruct((nt, SUB, TT), F32),
                   jax.ShapeDtypeStruct((8, LANES), F32)],
        scratch_shapes=[pltpu.VMEM((1, LANES), F32)],
        compiler_params=_cparams("arbitrary"),
        name="outproj_route",
    )(*xs, mix, mix_s, w_out_bf16, g_ffn.reshape(1, D_MODEL), wr_hi, wr_lo, br)


ROWS_PER_ISSUE = 8


def _issue_rows(lo, hi, row_copy):
    assert (hi - lo) % ROWS_PER_ISSUE == 0

    def body(it, c):
        for u in range(ROWS_PER_ISSUE):
            r = lo + it * ROWS_PER_ISSUE + u
            row_copy(r, 0).start(priority=0)
            row_copy(r, 1).start(priority=1)
        return c

    lax.fori_loop(0, (hi - lo) // ROWS_PER_ISSUE, body, 0)


def _dispatch_kernel(pos_ref, zt_ref, xn_ref, xs_ref, zbuf, sem, zsem, *, n_full, ranges, n_tiles):
    i = pl.program_id(0)

    @pl.when(i == 0)
    def _():
        zbuf[...] = jnp.zeros(zbuf.shape, F32)

        def zero_copy(t):
            rows = MOE_TM * SUB
            return pltpu.make_async_copy(zbuf, xs_ref.at[pl.ds(pl.multiple_of(t * rows, rows), rows)], zsem)

        def zstart(t, c):
            @pl.when(zt_ref[t] != 0)
            def _():
                zero_copy(t).start()
            return c

        def zwait(t, c):
            @pl.when(zt_ref[t] != 0)
            def _():
                zero_copy(t).wait()
            return c

        lax.fori_loop(0, n_tiles, zstart, 0)
        lax.fori_loop(0, n_tiles, zwait, 0)

    def row_copy(r, k):
        return pltpu.make_async_copy(xn_ref.at[_tile_of(r)], xs_ref.at[_tile_of(pos_ref[0, 0, k * TT + r])], sem)

    def scatter(lo, hi):
        _issue_rows(lo, hi, row_copy)
        nrow = (hi - lo) * SUB
        for _ in range(2):
            pltpu.make_async_copy(xn_ref.at[pl.ds(lo * SUB, nrow)], xs_ref.at[pl.ds(0, nrow)], sem).wait()

    _for_tile_rows(i, n_full, ranges, scatter)


def _dispatch(xn, pos, zero_tiles, n_sorted, t_rows, ranges):
    n = xn.shape[0] // SUB
    return pl.pallas_call(
        functools.partial(_dispatch_kernel, n_full=t_rows // TT, ranges=ranges,
                          n_tiles=n_sorted // MOE_TM),
        grid=(n // TT,),
        in_specs=[pl.BlockSpec((1, 1, 2 * TT), lambda i: (i, 0, 0), memory_space=pltpu.SMEM),
                  pl.BlockSpec(memory_space=pltpu.SMEM),
                  pl.BlockSpec((TT * SUB, LANES), lambda i: (i, 0))],
        out_specs=pl.BlockSpec(memory_space=pl.ANY),
        out_shape=jax.ShapeDtypeStruct((n_sorted * SUB, LANES), F32),
        scratch_shapes=[pltpu.VMEM((MOE_TM * SUB, LANES), F32),
                        pltpu.SemaphoreType.DMA(()),
                        pltpu.SemaphoreType.DMA(())],
        compiler_params=_cparams("arbitrary"),
        name="moe_dispatch",
    )(pos, zero_tiles, xn)


def _expert_ffn_kernel(te_ref, nu_ref, xs_ref, wg_ref, wu_ref, wd_ref, ys_ref, wgb, wub, wdb):
    j = pl.program_id(0)
    changed = (j == 0) | (te_ref[j] != te_ref[jnp.maximum(j - 1, 0)])

    @pl.when((j < nu_ref[0]) & changed)
    def _():
        wgb[...] = wg_ref[0, 0].astype(BF16)
        wub[...] = wu_ref[0, 0].astype(BF16)
        wdb[...] = wd_ref[0, 0].astype(BF16)

    n_slab = D_MODEL // LANES

    @pl.when(j < nu_ref[0])
    def _():
        for part in range(MOE_TM // FFN_ROWS):
            base = part * FFN_ROWS * SUB
            x = jnp.concatenate([xs_ref[_slab(c, FFN_ROWS, base), :] for c in range(n_slab)],
                                axis=1).astype(BF16)
            hg = jnp.dot(x, wgb[...], preferred_element_type=F32)
            hu = jnp.dot(x, wub[...], preferred_element_type=F32)
            h = (_silu(hg) * hu).astype(BF16)
            y = jnp.dot(h, wdb[...], preferred_element_type=F32)
            for c in range(n_slab):
                ys_ref[_slab(c, FFN_ROWS, base), :] = y[:, c * LANES:(c + 1) * LANES]

    @pl.when(j >= nu_ref[0])
    def _():
        ys_ref[...] = jnp.zeros(ys_ref.shape, F32)


def _expert_ffn(xs, tile_expert, n_used, w_gate, w_up, w_down, layer):
    n_sorted = xs.shape[0] // SUB
    nt = n_sorted // MOE_TM

    def row_map(j, te, nu):
        return (jnp.minimum(j, nu[0] - 1), 0)

    def w_map(j, te, nu):
        return (layer, te[j], 0, 0)

    return pl.pallas_call(
        _expert_ffn_kernel,
        grid_spec=pltpu.PrefetchScalarGridSpec(
            num_scalar_prefetch=2,
            grid=(nt,),
            in_specs=[pl.BlockSpec((MOE_TM * SUB, LANES), row_map),
                      pl.BlockSpec((1, 1, D_MODEL, D_EXPERT), w_map),
                      pl.BlockSpec((1, 1, D_MODEL, D_EXPERT), w_map),
                      pl.BlockSpec((1, 1, D_EXPERT, D_MODEL), w_map)],
            out_specs=pl.BlockSpec((MOE_TM * SUB, LANES), lambda j, te, nu: (j, 0)),
            scratch_shapes=[pltpu.VMEM((D_MODEL, D_EXPERT), BF16),
                            pltpu.VMEM((D_MODEL, D_EXPERT), BF16),
                            pltpu.VMEM((D_EXPERT, D_MODEL), BF16)]),
        out_shape=jax.ShapeDtypeStruct((n_sorted * SUB, LANES), F32),
        compiler_params=_cparams("arbitrary"),
        name="expert_ffn",
    )(tile_expert, n_used, xs, w_gate, w_up, w_down)


def _combine_kernel(posc_ref, posn_ref, x1_ref, info_ref, g_ref, ys_ref, *rest,
                    n_full, n_tiles, ranges, valid_lo, valid_mid, final_norm):
    out_refs, (ybuf, sem) = rest[:-2], rest[-2:]
    i = pl.program_id(0)
    slot = lax.rem(i, 2)

    def buf_base(s, k):
        return (s * 2 + k) * (TT * SUB)

    def gather(pos_ref, dst_slot):
        def row_copy(r, k):
            dst = pl.ds(pl.multiple_of(buf_base(dst_slot, k) + r * SUB, SUB), SUB)
            return pltpu.make_async_copy(ys_ref.at[_tile_of(pos_ref[0, 0, k * TT + r])],
                                         ybuf.at[dst], sem.at[dst_slot])
        return lambda lo, hi: _issue_rows(lo, hi, row_copy)

    @pl.when(i == 0)
    def _():
        ybuf[...] = jnp.zeros(ybuf.shape, F32)
        _for_tile_rows(i, n_full, ranges, gather(posc_ref, 0))

    @pl.when(i + 1 < n_tiles)
    def _():
        _for_tile_rows(i + 1, n_full, ranges, gather(posn_ref, 1 - slot))

    def drain(lo, hi):
        nrow = (hi - lo) * SUB
        for k in range(2):
            dst = pl.ds(pl.multiple_of(buf_base(slot, k) + lo * SUB, SUB), nrow)
            pltpu.make_async_copy(ys_ref.at[pl.ds(0, nrow)], ybuf.at[dst], sem.at[slot]).wait()

    _for_tile_rows(i, n_full, ranges, drain)

    info = info_ref[...]
    w1 = info[:, 2:3]
    w2 = info[:, 3:4]
    rowg = i * TT + lax.broadcasted_iota(I32, (TT, 1), 0)
    valid = (rowg < valid_lo) | (rowg >= valid_mid)
    outs = []
    for c in range(D_MODEL // LANES):
        y1 = ybuf[_slab(c, TT, buf_base(slot, 0)), :]
        y2 = ybuf[_slab(c, TT, buf_base(slot, 1)), :]
        outs.append(x1_ref[:, c * LANES:(c + 1) * LANES] + jnp.where(valid, w1 * y1 + w2 * y2, 0.0))
    out = jnp.concatenate(outs, axis=1)
    if not final_norm:
        out_refs[0][...] = out
    else:
        out = _rms(out, g_ref[...])

        @pl.when(i < n_full)
        def _():
            out_refs[0][...] = out

        @pl.when(i >= n_full)
        def _():
            out_refs[1][...] = out


def _combine(x1, info, pos, ys, g_final, t_rows, ranges, valid_lo, valid_mid, final_norm):
    n = x1.shape[0]
    nt = n // TT
    n_full = t_rows // TT
    if final_norm:
        out_specs = [pl.BlockSpec((TT, D_MODEL), lambda i: (jnp.minimum(i, n_full - 1), 0)),
                     pl.BlockSpec((TT, D_MODEL), lambda i: (jnp.maximum(i - n_full, 0), 0))]
        out_shape = [jax.ShapeDtypeStruct((t_rows, D_MODEL), F32),
                     jax.ShapeDtypeStruct((n - t_rows, D_MODEL), F32)]
    else:
        out_specs = [pl.BlockSpec((TT, D_MODEL), lambda i: (i, 0))]
        out_shape = [jax.ShapeDtypeStruct((n, D_MODEL), F32)]
    return pl.pallas_call(
        functools.partial(_combine_kernel, n_full=n_full, n_tiles=nt, ranges=ranges,
                          valid_lo=valid_lo, valid_mid=valid_mid, final_norm=final_norm),
        grid=(nt,),
        in_specs=[pl.BlockSpec((1, 1, 2 * TT), lambda i: (i, 0, 0), memory_space=pltpu.SMEM),
                  pl.BlockSpec((1, 1, 2 * TT), lambda i: (jnp.minimum(i + 1, nt - 1), 0, 0),
                               memory_space=pltpu.SMEM),
                  pl.BlockSpec((TT, D_MODEL), lambda i: (i, 0)),
                  pl.BlockSpec((TT, LANES), lambda i: (i, 0)),
                  pl.BlockSpec((1, D_MODEL), lambda i: (0, 0)),
                  pl.BlockSpec(memory_space=pl.ANY)],
        out_specs=out_specs,
        out_shape=out_shape,
        scratch_shapes=[pltpu.VMEM((2 * 2 * TT * SUB, LANES), F32),
                        pltpu.SemaphoreType.DMA((2,))],
        compiler_params=_cparams("arbitrary"),
        name="moe_combine",
    )(pos, pos, x1, info, g_final.reshape(1, D_MODEL), ys)


def _moe(x1, xn, info, infot, cnt, w_gate, w_up, w_down, layer, g_final, t_rows, ns, final_norm):
    n = x1.shape[0]
    valid_lo, valid_mid = t_rows + ns, n - N_META
    ranges = _small_tile_ranges(t_rows, ns, n)
    n_valid = valid_lo + N_META
    nt = (2 * n_valid + N_EXPERTS * (MOE_TM - 1)) // MOE_TM
    counts = cnt[0, :N_EXPERTS].astype(I32)
    tiles = (counts + MOE_TM - 1) // MOE_TM
    tile_end = jnp.cumsum(tiles)
    row_start = (tile_end - tiles) * MOE_TM
    n_used = tile_end[-1:]
    experts = jnp.arange(N_EXPERTS, dtype=I32)

    def slot_rows(e, rank):
        start = jnp.sum(jnp.where(e[..., None] == experts, row_start, 0), axis=-1)
        return start + rank

    pos = jnp.concatenate([slot_rows(infot[:, 0].astype(I32), infot[:, 4].astype(I32)),
                           slot_rows(infot[:, 1].astype(I32), infot[:, 5].astype(I32))], axis=-1)
    rows = jnp.arange(n).reshape(n // TT, TT)
    valid = (rows < valid_lo) | (rows >= valid_mid)
    pos = jnp.where(jnp.concatenate([valid, valid], axis=-1), pos, 0)[:, None, :]
    tile_ids = jnp.arange(nt, dtype=I32)
    tile_expert = jnp.sum(tile_end[None, :] <= jnp.minimum(tile_ids, n_used[0] - 1)[:, None],
                          axis=1).astype(I32)
    is_last = jnp.any((tile_ids[:, None] == tile_end[None, :] - 1) & (tiles[None, :] > 0), axis=1)
    zero_tiles = (is_last | (tile_ids >= n_used[0])).astype(I32)
    xs = _dispatch(xn, pos, zero_tiles, nt * MOE_TM, t_rows, ranges)
    ys = _expert_ffn(xs, tile_expert, n_used.astype(I32), w_gate, w_up, w_down, layer)
    return _combine(x1, info, pos, ys, g_final, t_rows, ranges, valid_lo, valid_mid, final_norm)


def _shift_rows(x, k, fill):
    if k % SUB == 0:
        return jnp.concatenate([jnp.full((k, x.shape[1]), fill, x.dtype), x[:x.shape[0] - k]], axis=0)
    row = lax.broadcasted_iota(I32, x.shape, 0)
    return jnp.where(row >= k, pltpu.roll(x, k, 0), fill)


def _lru_gates(xc, wgate_ref, bgate_ref, lam_sp):
    gates = _dot(xc, wgate_ref[...]) + bgate_ref[...]
    r = _sigmoid(gates[:, :D_LRU])
    ig = _sigmoid(gates[:, D_LRU:])
    log_a = -LRU_C * r * lam_sp
    a = jnp.exp(log_a)
    bx = jnp.sqrt(-jnp.tanh(log_a) * (a * a + 1.0)) * (ig * xc)
    return a, bx


def _conv_block(seq_ref, u, w_ref, b_ref, fresh):
    seq = jnp.concatenate([jnp.where(fresh, 0.0, seq_ref[...]), u], axis=0)
    out = b_ref[...] + u * w_ref[CONV_WIDTH - 1:CONV_WIDTH, :]
    for back in range(1, CONV_WIDTH):
        tap = CONV_WIDTH - 1 - back
        out = out + pltpu.roll(seq, back, 0)[8:, :] * w_ref[tap:tap + 1, :]
    seq_ref[...] = u[BLK - 8:, :]
    return out


def _mix1_prompt_kernel(*refs, nblk):
    j = pl.program_id(0)
    o_ref = refs[16]

    @pl.when(j <= nblk)
    def _():
        _mix1_prompt_step(*refs)

    @pl.when(j > nblk)
    def _():
        o_ref[...] = jnp.zeros(o_ref.shape, o_ref.dtype)


def _mix1_prompt_step(cx_ref, cg_ref, z_ref, xbc_ref, dt_ref,
                      cwc_ref, cbc_ref, wgate_ref, bgate_ref, lam_ref,
                      cwd_ref, cbd_ref, dtb_ref, alog_ref, dskip_ref, nw_ref,
                      o_ref, lruh_ref, ssdh_ref,
                      seqc_ref, seqd_ref, hl_ref, hs_ref):
    refs = (cx_ref, cg_ref, z_ref, xbc_ref, dt_ref, cwc_ref, cbc_ref, wgate_ref, bgate_ref, lam_ref,
            cwd_ref, cbd_ref, dtb_ref, alog_ref, dskip_ref, nw_ref, o_ref, lruh_ref, ssdh_ref,
            seqc_ref, seqd_ref, hl_ref, hs_ref)
    for sub in range(MIX1_BLKS):
        _mix1_block(*refs, sub=sub)


def _mix1_block(cx_ref, cg_ref, z_ref, xbc_ref, dt_ref,
                cwc_ref, cbc_ref, wgate_ref, bgate_ref, lam_ref,
                cwd_ref, cbd_ref, dtb_ref, alog_ref, dskip_ref, nw_ref,
                o_ref, lruh_ref, ssdh_ref,
                seqc_ref, seqd_ref, hl_ref, hs_ref, *, sub):
    j = pl.program_id(0)
    rows = slice(sub * BLK, (sub + 1) * BLK)
    fresh = (j == 0) if sub == 0 else False
    rowc = lax.broadcasted_iota(I32, (BLK, 1), 0)
    live_min_meta = META_PAD if sub == MIX1_BLKS - 1 else BLK
    live = rowc >= jnp.where(j >= 1, 0, live_min_meta)

    xc = _conv_block(seqc_ref, cx_ref[rows, :], cwc_ref, cbc_ref, fresh)
    a, bx = _lru_gates(xc, wgate_ref, bgate_ref, _softplus(-lam_ref[...]))
    a = jnp.where(live, a, 1.0)
    bx = jnp.where(live, bx, 0.0)
    k = 1
    while k < BLK:
        bx = a * _shift_rows(bx, k, 0.0) + bx
        a = a * _shift_rows(a, k, 1.0)
        k *= 2
    h = a * jnp.where(fresh, 0.0, hl_ref[...]) + bx
    hl_ref[...] = h[BLK - 1:BLK, :]
    lruh_ref[...] = h[BLK - 1:BLK, :]
    o_ref[rows, :D_LRU] = (h * _gelu_tanh(cg_ref[rows, :])).astype(o_ref.dtype)

    xbc = _silu(_conv_block(seqd_ref, xbc_ref[rows, :], cwd_ref, cbd_ref, fresh))
    xs = xbc[:, :D_SSD]
    dt = jnp.where(live, _softplus(dt_ref[rows, :] + dtb_ref[...]), 0.0)
    a_neg = -jnp.exp(alog_ref[...])
    acum = dt * a_neg
    k = 1
    while k < BLK:
        acum = acum + _shift_rows(acum, k, 0.0)
        k *= 2
    acum_t = acum.T
    dt_t = dt.T
    xs_t = xs.T
    tri = (lax.broadcasted_iota(I32, (BLK, BLK), 0) >= lax.broadcasted_iota(I32, (BLK, BLK), 1))
    rep = SSD_HEADS // SSD_GROUPS
    ys = []
    for g in range(SSD_GROUPS):
        bg = xbc[:, D_SSD + g * SSD_STATE:D_SSD + (g + 1) * SSD_STATE]
        cgm = xbc[:, D_SSD + (SSD_GROUPS + g) * SSD_STATE:D_SSD + (SSD_GROUPS + g + 1) * SSD_STATE]
        cb = _dot_nt(cgm, bg)
        for hh in range(rep):
            hd = g * rep + hh
            col = acum[:, hd:hd + 1]
            rowv = acum_t[hd:hd + 1, :]
            decay = jnp.exp(jnp.where(tri, col - rowv, NEG_INF))
            scores = cb * decay * dt_t[hd:hd + 1, :]
            xh = xs[:, hd * SSD_HEAD_DIM:(hd + 1) * SSD_HEAD_DIM]
            y = _dot(scores, xh)
            hprev = jnp.where(fresh, 0.0, hs_ref[hd])
            y = y + jnp.exp(col) * _dot_nt(cgm, hprev)
            last = acum_t[hd:hd + 1, BLK - 1:BLK]
            w_end = jnp.exp(last - rowv) * dt_t[hd:hd + 1, :]
            xw_t = xs_t[hd * SSD_HEAD_DIM:(hd + 1) * SSD_HEAD_DIM, :] * w_end
            hnew = jnp.exp(last) * hprev + _dot(xw_t, bg)
            hs_ref[hd] = hnew
            ssdh_ref[hd] = hnew
            ys.append(y + dskip_ref[:, hd * SSD_HEAD_DIM:(hd + 1) * SSD_HEAD_DIM] * xh)
    y = jnp.concatenate(ys, axis=1) * _silu(z_ref[rows, :])
    o_ref[rows, D_LRU:] = _rms(y, nw_ref[...]).astype(o_ref.dtype)


def _mix1_prompt(cx, cg, z, xbc, dt, wts, t_rows):
    n = cx.shape[0]
    step = MIX1_BLKS * BLK
    assert t_rows % step == 0 and n % step == 0 and step <= TT
    nstep = t_rows // step
    last = n // step - 1

    def cur(j):
        return (jnp.where(j == 0, last, j - 1), 0)

    def const2(j):
        return (0, 0)

    in_specs = [pl.BlockSpec((step, D_LRU), cur), pl.BlockSpec((step, D_LRU), cur),
                pl.BlockSpec((step, D_SSD), cur), pl.BlockSpec((step, D_XBC), cur),
                pl.BlockSpec((step, LANES), cur)]
    in_specs += [pl.BlockSpec(w.shape, const2) for w in wts]
    return pl.pallas_call(
        functools.partial(_mix1_prompt_kernel, nblk=nstep),
        grid=(last + 1,),
        in_specs=in_specs,
        out_specs=[pl.BlockSpec((step, D_MIX_ODD), cur),
                   pl.BlockSpec((1, D_LRU), const2),
                   pl.BlockSpec((SSD_HEADS, SSD_HEAD_DIM, SSD_STATE), lambda j: (0, 0, 0))],
        out_shape=[jax.ShapeDtypeStruct((n, D_MIX_ODD), BF16),
                   jax.ShapeDtypeStruct((1, D_LRU), F32),
                   jax.ShapeDtypeStruct((SSD_HEADS, SSD_HEAD_DIM, SSD_STATE), F32)],
        scratch_shapes=[pltpu.VMEM((8, D_LRU), F32),
                        pltpu.VMEM((8, D_XBC), F32),
                        pltpu.VMEM((1, D_LRU), F32),
                        pltpu.VMEM((SSD_HEADS, SSD_HEAD_DIM, SSD_STATE), F32)],
        compiler_params=_cparams("arbitrary"),
        name="mix1_prompt",
    )(cx, cg, z, xbc, dt, *wts)


def _conv_step(seq, t, w_ref, b_ref):
    out = b_ref[...]
    for tap in range(CONV_WIDTH):
        out = out + seq[t + tap] * w_ref[tap:tap + 1, :]
    return out


def _lru_sample_kernel(cx_ref, cg_ref, buf_ref, h0_ref, cwc_ref, cbc_ref, wgate_ref, bgate_ref,
                       lam_ref, o_ref, hout_ref, *, n_new):
    seq = [buf_ref[i] for i in range(CONV_WIDTH - 1)] + [cx_ref[i] for i in range(n_new)]
    lam_sp = _softplus(-lam_ref[...])
    h = h0_ref[...]
    for t in range(n_new):
        xc = _conv_step(seq, t, cwc_ref, cbc_ref)
        a, bx = _lru_gates(xc, wgate_ref, bgate_ref, lam_sp)
        h = a * h + bx
        o_ref[t] = h * _gelu_tanh(cg_ref[t])
    hout_ref[...] = h


def _lru_sample(cx_tm, cg_tm, buf_tm, h0, wts):
    n_new = cx_tm.shape[0]
    return pl.pallas_call(
        functools.partial(_lru_sample_kernel, n_new=n_new),
        out_shape=[jax.ShapeDtypeStruct(cx_tm.shape, F32), jax.ShapeDtypeStruct(h0.shape, F32)],
        compiler_params=pltpu.CompilerParams(vmem_limit_bytes=VMEM_LIMIT),
        name="lru_sample",
    )(cx_tm, cg_tm, buf_tm, h0, *wts)


def _split3(x):
    hi = x.astype(BF16)
    r = x - hi.astype(F32)
    mid = r.astype(BF16)
    lo = (r - mid.astype(F32)).astype(BF16)
    return hi, mid, lo


def _ssd_sample_kernel(xbc_ref, dt_ref, z_ref, buf_ref, h0_ref, cwd_ref, cbd_ref, dtb_ref, alog_ref,
                       dskip_ref, nw_ref, o_ref, hout_ref, xc_ref, dec_ref, dtx_ref, y_ref, *, n_new, bs):
    seq = [buf_ref[i] for i in range(CONV_WIDTH - 1)] + [xbc_ref[i] for i in range(n_new)]
    a_neg = -jnp.exp(alog_ref[...])
    hcol = lax.broadcasted_iota(I32, (LANES, D_SSD), 1) // SSD_HEAD_DIM
    expand = jnp.where(lax.broadcasted_iota(I32, (LANES, D_SSD), 0) == hcol, 1.0, 0.0).astype(BF16)

    def widen(v):
        parts = _split3(v)
        return sum(jnp.dot(p, expand, preferred_element_type=F32) for p in parts)

    for t in range(n_new):
        xc = _silu(_conv_step(seq, t, cwd_ref, cbd_ref))
        xc_ref[t] = xc
        dt = _softplus(dt_ref[t] + dtb_ref[...])
        dtw = widen(dt)
        dec_ref[t] = jnp.exp(widen(dt * a_neg))
        dtx_ref[t] = dtw * xc[:, :D_SSD]

    zero_rows = jnp.zeros((LANES - 2 * n_new * SSD_GROUPS, SSD_STATE), F32)
    half = D_SSD // SSD_GROUPS

    def per_seq(b, c):
        rows = [xc_ref[t, pl.ds(b, 1), :] for t in range(n_new)]
        bc = [r[:, D_SSD + k * SSD_STATE:D_SSD + (k + 1) * SSD_STATE]
              for r in rows for k in range(2 * SSD_GROUPS)]
        bc_t = jnp.concatenate(bc + [zero_rows], axis=0).T
        ht = h0_ref[b].T
        for t in range(n_new):
            bcol = [jnp.broadcast_to(bc_t[:, t * 4 + g:t * 4 + g + 1], (SSD_STATE, half))
                    for g in range(SSD_GROUPS)]
            ccol = [jnp.broadcast_to(bc_t[:, t * 4 + SSD_GROUPS + g:t * 4 + SSD_GROUPS + g + 1],
                                     (SSD_STATE, half)) for g in range(SSD_GROUPS)]
            bw = jnp.concatenate(bcol, axis=1)
            cw = jnp.concatenate(ccol, axis=1)
            ht = ht * dec_ref[t, pl.ds(b, 1), :] + bw * dtx_ref[t, pl.ds(b, 1), :]
            y = jnp.sum(cw * ht, axis=0, keepdims=True)
            y_ref[t, pl.ds(b, 1), :] = y + dskip_ref[...] * rows[t][:, :D_SSD]
        hout_ref[b] = ht.T
        return c

    lax.fori_loop(0, bs, per_seq, 0)
    for t in range(n_new):
        o_ref[t] = _rms(y_ref[t] * _silu(z_ref[t]), nw_ref[...])


def _ssd_sample(xbc_tm, dt_tm, z_tm, buf_tm, h0, wts):
    n_new, b, _ = xbc_tm.shape
    bs = 16 if b % 16 == 0 else b
    hp = SSD_HEADS * SSD_HEAD_DIM

    def tm(c):
        return pl.BlockSpec((n_new, bs, c), lambda i: (0, i, 0))

    in_specs = [tm(D_XBC), tm(LANES), tm(D_SSD),
                pl.BlockSpec((CONV_WIDTH - 1, bs, D_XBC), lambda i: (0, i, 0)),
                pl.BlockSpec((bs, hp, SSD_STATE), lambda i: (i, 0, 0))]
    in_specs += [pl.BlockSpec(w.shape, lambda i: (0, 0)) for w in wts]
    return pl.pallas_call(
        functools.partial(_ssd_sample_kernel, n_new=n_new, bs=bs),
        grid=(b // bs,),
        in_specs=in_specs,
        out_specs=[tm(D_SSD), pl.BlockSpec((bs, hp, SSD_STATE), lambda i: (i, 0, 0))],
        out_shape=[jax.ShapeDtypeStruct((n_new, b, D_SSD), F32),
                   jax.ShapeDtypeStruct((b, hp, SSD_STATE), F32)],
        scratch_shapes=[pltpu.VMEM((n_new, bs, D_XBC), F32),
                        pltpu.VMEM((n_new, bs, D_SSD), F32),
                        pltpu.VMEM((n_new, bs, D_SSD), F32),
                        pltpu.VMEM((n_new, bs, D_SSD), F32)],
        compiler_params=_cparams("parallel"),
        name="ssd_sample",
    )(xbc_tm, dt_tm, z_tm, buf_tm, h0, *wts)


def _block_diag(w):
    g, c, _ = w.shape
    eye = jnp.eye(g, dtype=w.dtype)
    return (eye[:, None, :, None] * w[:, :, None, :]).reshape(g * c, g * c)


def _router_weights(w_rg, b_rg, w_re, b_re):
    w = jnp.concatenate([w_rg, w_re], axis=1)
    w = jnp.pad(w, ((0, 0), (0, LANES - w.shape[1])))
    hi = w.astype(BF16)
    lo = (w - hi.astype(F32)).astype(BF16)
    b = jnp.pad(jnp.concatenate([b_rg, b_re]), (0, LANES - MOE_GROUPS - N_EXPERTS)).reshape(1, LANES)
    return hi, lo, b


def kernel(x_prompt, x_sample, cache_swa_k, cache_swa_v, cache_pool, state_lru_conv, state_lru_h, state_ssd_conv, state_ssd_h, meta_tokens, norm_mix, norm_ffn, norm_final, attn_w_in, attn_w_out, attn_sinks, pool_w, pool_scale, rec_w_in, rec_w_out, lru_conv_w, lru_conv_b, lru_w_rg, lru_b_rg, lru_w_ig, lru_b_ig, lru_lambda, ssd_conv_w, ssd_conv_b, ssd_dt_bias, ssd_a_log, ssd_d, ssd_norm_w, moe_w_rg, moe_b_rg, moe_w_re, moe_b_re, moe_w_gate, moe_w_up, moe_w_down):
    t_rows = x_prompt.shape[1]
    b, s_new = x_sample.shape[:2]
    ns = b * s_new
    small = -(-ns // TT) * TT + TT
    n = t_rows + small
    meta_blk = n // BLK - 1
    valid_lo = t_rows + ns
    valid_mid = n - N_META
    assert x_prompt.shape[0] == 1 and t_rows % TT == 0 and ns % BLK == 0 and ns <= TT

    x_small = jnp.concatenate([
        x_sample.transpose(1, 0, 2).reshape(ns, D_MODEL),
        jnp.zeros((small - ns - N_META, D_MODEL), F32),
        meta_tokens.astype(F32)], axis=0)
    xs0 = (x_prompt[0], x_small)

    def sample_tm(a):
        return a[t_rows:t_rows + ns].reshape(s_new, b, a.shape[1])

    q, kv, p = _norm_proj(xs0, norm_mix[0], attn_w_in[0].astype(BF16), (D_ATTN, 2 * D_KV, D_POOL),
                          (BF16, F32, F32))
    wpool_bd = _block_diag(pool_w[0]).astype(BF16)
    mix = _mix0_prompt(q, kv, p, attn_sinks[0], wpool_bd, pool_scale[0], t_rows, meta_blk)
    nq = s_new * GQA
    qs = sample_tm(q).reshape(s_new, b, N_KV_HEADS, GQA, HEAD_DIM).transpose(1, 2, 0, 3, 4)
    qs = qs.reshape(b, N_KV_HEADS, nq, HEAD_DIM)
    kv_s = sample_tm(kv).transpose(1, 0, 2)
    kn, vn = kv_s[:, :, :D_KV], kv_s[:, :, D_KV:]
    ck = cache_swa_k[0].reshape(b, N_META + WINDOW, D_KV)
    cv = cache_swa_v[0].reshape(b, N_META + WINDOW, D_KV)
    o_s = _attn_sample(qs, ck, cv, kn, vn, attn_sinks[0])
    o_s = o_s.reshape(b, N_KV_HEADS, s_new, GQA, HEAD_DIM).transpose(2, 0, 1, 3, 4).reshape(s_new, b, D_ATTN)
    p_s = sample_tm(p)
    pool_buf_tm = cache_pool[0].transpose(1, 0, 2)
    yp_s = _pool_sample(pool_buf_tm, p_s, wpool_bd, pool_scale[0])
    mix_s = jnp.concatenate([o_s, yp_s], axis=-1).reshape(ns, D_MIX_EVEN)
    wr_hi, wr_lo, br = _router_weights(moe_w_rg[0], moe_b_rg[0], moe_w_re[0], moe_b_re[0])
    x1, xn, info, infot, cnt = _outproj_route(xs0, mix, mix_s, attn_w_out[0].astype(BF16), norm_ffn[0],
                                              wr_hi, wr_lo, br, t_rows, valid_lo, valid_mid)
    x, = _moe(x1, xn, info, infot, cnt, moe_w_gate, moe_w_up, moe_w_down, 0, norm_final,
              t_rows, ns, False)

    meta_rows = slice(n - N_META, n)
    last_rows = slice(t_rows - WINDOW, t_rows)
    swa_k_p = jnp.concatenate([kv[meta_rows, :D_KV], kv[last_rows, :D_KV]], axis=0)
    swa_v_p = jnp.concatenate([kv[meta_rows, D_KV:], kv[last_rows, D_KV:]], axis=0)
    swa_k_p = swa_k_p.reshape(1, 1, N_META + WINDOW, N_KV_HEADS, HEAD_DIM)
    swa_v_p = swa_v_p.reshape(1, 1, N_META + WINDOW, N_KV_HEADS, HEAD_DIM)
    pool_p = p[t_rows - POOL_BUF:t_rows].reshape(1, 1, POOL_BUF, D_POOL)
    swa_k_s = jnp.concatenate([ck[:, :N_META], ck[:, N_META + s_new:], kn], axis=1)
    swa_v_s = jnp.concatenate([cv[:, :N_META], cv[:, N_META + s_new:], vn], axis=1)
    swa_k_s = swa_k_s.reshape(1, b, N_META + WINDOW, N_KV_HEADS, HEAD_DIM)
    swa_v_s = swa_v_s.reshape(1, b, N_META + WINDOW, N_KV_HEADS, HEAD_DIM)
    pool_s = jnp.concatenate([cache_pool[0], p_s.transpose(1, 0, 2)], axis=1)[:, -POOL_BUF:][None]

    d_in_odd = rec_w_in.shape[2]
    w_in1 = jnp.pad(rec_w_in[0], ((0, 0), (0, LANES - SSD_HEADS))).astype(BF16)
    assert d_in_odd == 2 * D_LRU + D_SSD + D_XBC + SSD_HEADS
    cx, cg, z, xbc, dt = _norm_proj((x,), norm_mix[1], w_in1, (D_LRU, D_LRU, D_SSD, D_XBC, LANES))
    wgate = jnp.concatenate([_block_diag(lru_w_rg[0]), _block_diag(lru_w_ig[0])], axis=1).astype(BF16)
    bgate = jnp.concatenate([lru_b_rg[0], lru_b_ig[0]]).reshape(1, 2 * D_LRU)
    pad8 = (0, LANES - SSD_HEADS)
    dtb = jnp.pad(ssd_dt_bias[0], pad8).reshape(1, LANES)
    alog = jnp.pad(ssd_a_log[0], pad8).reshape(1, LANES)
    dskip = jnp.repeat(ssd_d[0], SSD_HEAD_DIM).reshape(1, D_SSD)
    lru_w = (lru_conv_w[0], lru_conv_b[0].reshape(1, D_LRU), wgate, bgate, lru_lambda[0].reshape(1, D_LRU))
    ssd_w = (ssd_conv_w[0], ssd_conv_b[0].reshape(1, D_XBC), dtb, alog, dskip, ssd_norm_w[0].reshape(1, D_SSD))
    mix, lru_h_p, ssd_h_p = _mix1_prompt(cx, cg, z, xbc, dt, lru_w + ssd_w, t_rows)
    cx_s, cg_s, z_s, xbc_s, dt_s = (sample_tm(a) for a in (cx, cg, z, xbc, dt))
    yc_s, lru_h_s = _lru_sample(cx_s, cg_s, state_lru_conv[0].transpose(1, 0, 2), state_lru_h[0], lru_w)
    h0 = state_ssd_h[0].reshape(b, SSD_HEADS * SSD_HEAD_DIM, SSD_STATE)
    yd_s, ssd_h_s = _ssd_sample(xbc_s, dt_s, z_s, state_ssd_conv[0].transpose(1, 0, 2), h0, ssd_w)
    mix_s = jnp.concatenate([yc_s, yd_s], axis=-1).reshape(ns, D_MIX_ODD)
    wr_hi, wr_lo, br = _router_weights(moe_w_rg[1], moe_b_rg[1], moe_w_re[1], moe_b_re[1])
    x1, xn, info, infot, cnt = _outproj_route((x,), mix, mix_s, rec_w_out[0].astype(BF16), norm_ffn[1],
                                              wr_hi, wr_lo, br, t_rows, valid_lo, valid_mid)
    y_p, y_small = _moe(x1, xn, info, infot, cnt, moe_w_gate, moe_w_up, moe_w_down, 1, norm_final,
                        t_rows, ns, True)

    tail = CONV_WIDTH - 1
    lru_conv_p = cx[t_rows - tail:t_rows].reshape(1, 1, tail, D_LRU)
    ssd_conv_p = xbc[t_rows - tail:t_rows].reshape(1, 1, tail, D_XBC)
    lru_conv_s = jnp.concatenate([state_lru_conv[0], cx_s.transpose(1, 0, 2)], axis=1)[:, -tail:][None]
    ssd_conv_s = jnp.concatenate([state_ssd_conv[0], xbc_s.transpose(1, 0, 2)], axis=1)[:, -tail:][None]

    y_prompt = y_p[None]
    y_sample = y_small[:ns].reshape(s_new, b, D_MODEL).transpose(1, 0, 2)
    return (y_prompt, y_sample, swa_k_p, swa_v_p, pool_p, lru_conv_p,
            lru_h_p.reshape(1, 1, D_LRU), ssd_conv_p,
            ssd_h_p.reshape(1, 1, SSD_HEADS, SSD_HEAD_DIM, SSD_STATE),
            swa_k_s, swa_v_s, pool_s, lru_conv_s, lru_h_s[None], ssd_conv_s,
            ssd_h_s.reshape(1, b, SSD_HEADS, SSD_HEAD_DIM, SSD_STATE))
```

```python
import functools

import jax
import jax.numpy as jnp
from jax import lax
from jax.experimental import pallas as pl
from jax.experimental.pallas import tpu as pltpu

F32 = jnp.float32
BF16 = jnp.bfloat16
I32 = jnp.int32

D_MODEL = 1024
N_META = 16
EPS = 1e-6
PAST_LEN = 16384
N_Q_HEADS = 16
N_KV_HEADS = 2
GQA = N_Q_HEADS // N_KV_HEADS
HEAD_DIM = 64
WINDOW = 128
D_ATTN = N_Q_HEADS * HEAD_DIM
D_KV = N_KV_HEADS * HEAD_DIM
POOL_WINDOWS = (2, 4, 8, 16)
POOL_GROUP_DIM = 96
D_POOL = len(POOL_WINDOWS) * POOL_GROUP_DIM
POOL_BUF = max(POOL_WINDOWS) - 1
D_MIX_EVEN = D_ATTN + D_POOL
D_LRU = 512
LRU_HEADS = 8
LRU_BLOCK = D_LRU // LRU_HEADS
LRU_C = 8.0
CONV_WIDTH = 4
SSD_HEADS = 8
SSD_HEAD_DIM = 64
D_SSD = SSD_HEADS * SSD_HEAD_DIM
SSD_GROUPS = 2
SSD_STATE = 128
D_XBC = D_SSD + 2 * SSD_GROUPS * SSD_STATE
D_MIX_ODD = D_LRU + D_SSD
MOE_GROUPS = 4
EXPERTS_PER_GROUP = 8
N_EXPERTS = MOE_GROUPS * EXPERTS_PER_GROUP
D_EXPERT = 256

BLK = 128
META_PAD = BLK - N_META
LANES = 128
MOE_TM = 512
FFN_ROWS = 256
ZERO_ROWS = 256
MIX0_BLKS = 4
MIX1_BLKS = 4
NEG_INF = float("-inf")
VMEM_LIMIT = 56 * 1024 * 1024


def _cparams(*sem):
    return pltpu.CompilerParams(dimension_semantics=sem, vmem_limit_bytes=VMEM_LIMIT)


TT = 512
SUB = 8


def _small_tile_ranges(t_rows, ns, n):
    out = {}
    for k in range(t_rows // TT, n // TT):
        lo = k * TT
        ranges = []
        a0, a1 = max(lo, t_rows), min(lo + TT, t_rows + ns)
        if a1 > a0:
            ranges.append((a0 - lo, a1 - lo))
        b0 = max(lo, n - N_META)
        if lo + TT > b0:
            ranges.append((b0 - lo, TT))
        out[k] = tuple(ranges)
    return out


def _for_tile_rows(i, n_full, ranges, fn):
    @pl.when(i < n_full)
    def _():
        fn(0, TT)

    for k, rs in ranges.items():
        if rs:
            @pl.when(i == k)
            def _(rs=rs):
                for lo, hi in rs:
                    fn(lo, hi)


def _slab(c, rows, base=0):
    return pl.ds(base + c, rows, stride=SUB)


def _tile_of(r):
    return pl.ds(pl.multiple_of(r * SUB, SUB), SUB)


def _sigmoid(x):
    return 1.0 / (1.0 + jnp.exp(-x))


def _silu(x):
    return x * _sigmoid(x)


def _softplus(x):
    return jnp.maximum(x, 0.0) + jnp.log1p(jnp.exp(-jnp.abs(x)))


def _gelu_tanh(x):
    return 0.5 * x * (1.0 + jnp.tanh(0.7978845608028654 * (x + 0.044715 * x * x * x)))


def _dot(a, b):
    return jnp.dot(a.astype(BF16), b.astype(BF16), preferred_element_type=F32)


def _dot_nt(a, b):
    return lax.dot_general(a.astype(BF16), b.astype(BF16), (((1,), (1,)), ((), ())),
                           preferred_element_type=F32)


def _rms(x, g):
    ms = jnp.mean(x * x, axis=-1, keepdims=True)
    return x * lax.rsqrt(ms + EPS) * g


def _row_specs(xs, width):
    if len(xs) == 1:
        return [pl.BlockSpec((TT, width), lambda i: (i, 0))]
    na = xs[0].shape[0] // TT
    return [pl.BlockSpec((TT, width), lambda i: (jnp.minimum(i, na - 1), 0)),
            pl.BlockSpec((TT, width), lambda i: (jnp.maximum(i - na, 0), 0))]


def _pick_rows(refs, n_first):
    if len(refs) == 1:
        return refs[0][...]
    return jnp.where(pl.program_id(0) < n_first, refs[0][...], refs[1][...])


def _norm_proj_kernel(*refs, splits, n_x, n_first):
    x_refs, (g_ref, w_ref), out_refs = refs[:n_x], refs[n_x:n_x + 2], refs[n_x + 2:]
    y = _rms(_pick_rows(x_refs, n_first), g_ref[...]).astype(BF16)
    off = 0
    for o_ref, n in zip(out_refs, splits):
        o_ref[...] = jnp.dot(y, w_ref[:, off:off + n], preferred_element_type=F32).astype(o_ref.dtype)
        off += n


def _norm_proj(xs, g, w_bf16, splits, dtypes=None):
    n = sum(x.shape[0] for x in xs)
    dtypes = dtypes or (F32,) * len(splits)
    return pl.pallas_call(
        functools.partial(_norm_proj_kernel, splits=splits, n_x=len(xs), n_first=xs[0].shape[0] // TT),
        grid=(n // TT,),
        in_specs=_row_specs(xs, D_MODEL) + [pl.BlockSpec((1, D_MODEL), lambda i: (0, 0)),
                                             pl.BlockSpec(w_bf16.shape, lambda i: (0, 0))],
        out_specs=[pl.BlockSpec((TT, s), lambda i: (i, 0)) for s in splits],
        out_shape=[jax.ShapeDtypeStruct((n, s), dt) for s, dt in zip(splits, dtypes)],
        compiler_params=_cparams("parallel"),
        name="norm_proj",
    )(*xs, g.reshape(1, D_MODEL), w_bf16)


def _mix0_prompt_kernel(*refs, nblk):
    j = pl.program_id(0)
    o_ref = refs[8]

    @pl.when(j <= nblk)
    def _():
        _mix0_prompt_step(*refs)

    @pl.when(j > nblk)
    def _():
        o_ref[...] = jnp.zeros(o_ref.shape, o_ref.dtype)


def _mix0_prompt_step(sink_ref, q_ref, kvc_ref, kvp_ref, kvm_ref, p_ref, wp_ref, ps_ref,
                      o_ref, seq_ref, s_ref, pr_ref, rd_ref):
    j = pl.program_id(0)
    kv_meta = jnp.concatenate([kvm_ref[...], jnp.zeros((META_PAD, 2 * D_KV), F32)], axis=0)
    kv_blocks = [_kv_halves(kvp_ref[...])]
    kv_blocks += [_kv_halves(kvc_ref[sub * BLK:(sub + 1) * BLK, :]) for sub in range(MIX0_BLKS)]
    kv_meta = _kv_halves(kv_meta)
    r = lax.broadcasted_iota(I32, (BLK, BLK), 0)
    c = lax.broadcasted_iota(I32, (BLK, BLK), 1)
    own = r >= c
    dist_loc = jnp.where(own, r - c, r - c + BLK).astype(F32)
    for sub in range(MIX0_BLKS):
        blk = (j - 1) * MIX0_BLKS + sub
        j_prev = 2 if sub == 0 else 1
        c_min_meta = META_PAD if sub == MIX0_BLKS - 1 else BLK
        keys = [kv_blocks[sub], kv_blocks[sub + 1], kv_meta]
        _attn_block(sink_ref, q_ref, keys, o_ref, s_ref, pr_ref, rd_ref,
                    sub * BLK, j, blk, j_prev, c_min_meta, r, c, own, dist_loc)
    _pool_rows(j, p_ref, wp_ref, ps_ref, o_ref, seq_ref)


def _kv_halves(kv):
    k_all, v_all = kv[:, :D_KV], kv[:, D_KV:]
    k_rot = pltpu.roll(k_all, HEAD_DIM, 1)
    v_rot = pltpu.roll(v_all, HEAD_DIM, 1)
    lo_half = lax.broadcasted_iota(I32, k_all.shape, 1) < HEAD_DIM

    def halves(own_lanes, other_lanes, g):
        src_lo, src_hi = (own_lanes, other_lanes) if g == 0 else (other_lanes, own_lanes)
        return (jnp.where(lo_half, src_lo, 0.0).astype(BF16), jnp.where(lo_half, 0.0, src_hi).astype(BF16))

    return [halves(k_all, k_rot, g) + halves(v_all, v_rot, g) for g in range(N_KV_HEADS)]


def _attn_block(sink_ref, q_ref, keys, o_ref, s_ref, pr_ref, rd_ref,
                ro, j, blk, j_prev, c_min_meta, r, c, own, dist_loc):
    d_min = jnp.where(j >= j_prev, -BLK, 0)
    c_min = jnp.where(j == 0, c_min_meta, 0)
    mask_loc = jnp.where((r - c >= d_min) & (c >= c_min), 0.0, NEG_INF)
    meta_hi = jnp.where(j >= 1, N_META, 0)
    dist_meta = jnp.where(c < meta_hi, (N_META + blk * BLK + r - c).astype(F32), jnp.inf)
    slabs = GQA // 2
    nkeys = 3 * BLK
    for g in range(N_KV_HEADS):
        k_even, k_odd, v_even, v_odd = (jnp.concatenate([blk_halves[g][part] for blk_halves in keys], axis=0)
                                        for part in range(4))
        qg = jnp.concatenate([q_ref[ro:ro + BLK, (g * slabs + i) * LANES:(g * slabs + i + 1) * LANES]
                              for i in range(slabs)], axis=0) * (HEAD_DIM ** -0.5)
        s_ref[...] = _dot_nt(qg, jnp.concatenate([k_even, k_odd], axis=0))
        for i in range(slabs):
            rows = slice(i * BLK, (i + 1) * BLK)
            for par in range(2):
                h = g * GQA + 2 * i + par
                slope = 2.0 ** (-8.0 * (h + 1) / N_Q_HEADS)
                sink = sink_ref[h]
                k0 = par * nkeys
                s_loc = (jnp.where(own, s_ref[rows, k0 + BLK:k0 + 2 * BLK], s_ref[rows, k0:k0 + BLK])
                         - slope * dist_loc + mask_loc)
                s_met = s_ref[rows, k0 + 2 * BLK:k0 + 3 * BLK] - slope * dist_meta
                m = jnp.maximum(jnp.max(jnp.maximum(s_loc, s_met), axis=-1, keepdims=True), sink)
                p_loc = jnp.exp(s_loc - m)
                p_met = jnp.exp(s_met - m)
                den = jnp.sum(p_loc + p_met, axis=-1, keepdims=True) + jnp.exp(sink - m)
                pr_ref[par, rows, 0:BLK] = jnp.where(own, 0.0, p_loc).astype(BF16)
                pr_ref[par, rows, BLK:2 * BLK] = jnp.where(own, p_loc, 0.0).astype(BF16)
                pr_ref[par, rows, 2 * BLK:] = p_met.astype(BF16)
                rd_ref[par, rows, :] = jnp.broadcast_to(1.0 / den, (BLK, LANES))
        o = (jnp.dot(pr_ref[0], v_even, preferred_element_type=F32) * rd_ref[0]
             + jnp.dot(pr_ref[1], v_odd, preferred_element_type=F32) * rd_ref[1])
        for i in range(slabs):
            o_ref[ro:ro + BLK, (g * slabs + i) * LANES:(g * slabs + i + 1) * LANES] = (
                o[i * BLK:(i + 1) * BLK, :].astype(o_ref.dtype))


def _pool_rows(j, p_ref, wp_ref, ps_ref, o_ref, seq_ref):
    n_rows = p_ref.shape[0]
    carry = jnp.where(j == 0, 0.0, seq_ref[...])
    s0 = jnp.concatenate([carry, p_ref[...]], axis=0)
    s1 = s0 + pltpu.roll(s0, 1, 0)
    s2 = s1 + pltpu.roll(s1, 2, 0)
    s3 = s2 + pltpu.roll(s2, 4, 0)
    s4 = s3 + pltpu.roll(s3, 8, 0)
    rows = N_META + n_rows
    col = lax.broadcasted_iota(I32, (rows, D_POOL), 1)
    row = lax.broadcasted_iota(I32, (rows, D_POOL), 0) - N_META
    wsum = jnp.where(col < POOL_GROUP_DIM, s1,
                     jnp.where(col < 2 * POOL_GROUP_DIM, s2,
                               jnp.where(col < 3 * POOL_GROUP_DIM, s3, s4)))
    wlen = jnp.where(col < POOL_GROUP_DIM, 2,
                     jnp.where(col < 2 * POOL_GROUP_DIM, 4,
                               jnp.where(col < 3 * POOL_GROUP_DIM, 8, 16)))
    pos = row + jnp.where(j == 0, N_META - n_rows, N_META + (j - 1) * n_rows)
    cnt = jnp.clip(jnp.minimum(pos + 1, wlen), 1, 16).astype(F32)
    d = (wsum / cnt - s0)[N_META:, :]
    o_ref[:, D_ATTN:] = (_dot(d, wp_ref[...]) * ps_ref[...]).astype(o_ref.dtype)
    seq_ref[...] = p_ref[n_rows - N_META:, :]


def _mix0_prompt(q, kv, p, sinks, wpool_bd, pool_scale, t_rows, meta_blk):
    n = q.shape[0]
    step = MIX0_BLKS * BLK
    assert t_rows % step == 0 and n % step == 0 and step <= TT
    nstep = t_rows // step
    last = n // step - 1

    def cur(j):
        return jnp.where(j == 0, last, j - 1)

    meta16 = (meta_blk * BLK + META_PAD) // N_META
    return pl.pallas_call(
        functools.partial(_mix0_prompt_kernel, nblk=nstep),
        grid=(last + 1,),
        in_specs=[pl.BlockSpec(memory_space=pltpu.SMEM),
                  pl.BlockSpec((step, D_ATTN), lambda j: (cur(j), 0)),
                  pl.BlockSpec((step, 2 * D_KV), lambda j: (cur(j), 0)),
                  pl.BlockSpec((BLK, 2 * D_KV), lambda j: (jnp.maximum((j - 1) * MIX0_BLKS - 1, 0), 0)),
                  pl.BlockSpec((N_META, 2 * D_KV), lambda j: (meta16, 0)),
                  pl.BlockSpec((step, D_POOL), lambda j: (cur(j), 0)),
                  pl.BlockSpec((D_POOL, D_POOL), lambda j: (0, 0)),
                  pl.BlockSpec((1, D_POOL), lambda j: (0, 0))],
        out_specs=pl.BlockSpec((step, D_MIX_EVEN), lambda j: (cur(j), 0)),
        out_shape=jax.ShapeDtypeStruct((n, D_MIX_EVEN), BF16),
        scratch_shapes=[pltpu.VMEM((N_META, D_POOL), F32),
                        pltpu.VMEM((GQA // 2 * BLK, 2 * 3 * BLK), F32),
                        pltpu.VMEM((2, GQA // 2 * BLK, 3 * BLK), BF16),
                        pltpu.VMEM((2, GQA // 2 * BLK, LANES), F32)],
        compiler_params=_cparams("arbitrary"),
        name="mix0_prompt",
    )(sinks, q, kv, kv, kv, p, wpool_bd, pool_scale.reshape(1, D_POOL))


def _attn_sample_kernel(sink_ref, q_ref, ck_ref, cv_ref, kn_ref, vn_ref, o_ref, *, n_new):
    nk = N_META + WINDOW + n_new
    nq = n_new * GQA
    k_all = jnp.concatenate([ck_ref[...], kn_ref[...]], axis=1).astype(BF16)
    v_all = jnp.concatenate([cv_ref[...], vn_ref[...]], axis=1).astype(BF16)
    r = lax.broadcasted_iota(I32, (nq, nk), 0)
    slot = lax.broadcasted_iota(I32, (nq, nk), 1)
    t = r // GQA
    hh = r - t * GQA
    kpos = jnp.where(slot < N_META, slot,
                     jnp.where(slot < N_META + WINDOW, PAST_LEN - WINDOW - N_META + slot,
                               PAST_LEN - N_META - WINDOW + slot))
    dist = PAST_LEN + t - kpos
    allowed = (dist >= 0) & ((slot < N_META) | (dist < WINDOW))
    distf = dist.astype(F32)
    hh1 = hh[:, :1]
    for g in range(N_KV_HEADS):
        slope = jnp.exp2(-8.0 * (g * GQA + hh + 1).astype(F32) / N_Q_HEADS)
        sink = jnp.zeros((nq, 1), F32)
        for i in range(GQA):
            sink = jnp.where(hh1 == i, sink_ref[g * GQA + i], sink)
        kg = k_all[:, :, g * HEAD_DIM:(g + 1) * HEAD_DIM]
        vg = v_all[:, :, g * HEAD_DIM:(g + 1) * HEAD_DIM]
        s = jnp.einsum("bqd,bkd->bqk", q_ref[:, g].astype(BF16), kg,
                       preferred_element_type=F32) * (HEAD_DIM ** -0.5)
        s = jnp.where(allowed[None], s - (slope * distf)[None], NEG_INF)
        sink = sink[None]
        m = jnp.maximum(jnp.max(s, axis=-1, keepdims=True), sink)
        pr = jnp.exp(s - m)
        den = jnp.sum(pr, axis=-1, keepdims=True) + jnp.exp(sink - m)
        o = jnp.einsum("bqk,bkd->bqd", pr.astype(BF16), vg, preferred_element_type=F32)
        o_ref[:, g] = o / den


def _attn_sample(qs, ck, cv, kn, vn, sinks):
    b, _, nq, _ = qs.shape
    n_new = kn.shape[1]
    bs = 16 if b % 16 == 0 else b
    nc = N_META + WINDOW
    return pl.pallas_call(
        functools.partial(_attn_sample_kernel, n_new=n_new),
        grid=(b // bs,),
        in_specs=[pl.BlockSpec(memory_space=pltpu.SMEM),
                  pl.BlockSpec((bs, N_KV_HEADS, nq, HEAD_DIM), lambda i: (i, 0, 0, 0)),
                  pl.BlockSpec((bs, nc, D_KV), lambda i: (i, 0, 0)),
                  pl.BlockSpec((bs, nc, D_KV), lambda i: (i, 0, 0)),
                  pl.BlockSpec((bs, n_new, D_KV), lambda i: (i, 0, 0)),
                  pl.BlockSpec((bs, n_new, D_KV), lambda i: (i, 0, 0))],
        out_specs=pl.BlockSpec((bs, N_KV_HEADS, nq, HEAD_DIM), lambda i: (i, 0, 0, 0)),
        out_shape=jax.ShapeDtypeStruct(qs.shape, F32),
        compiler_params=_cparams("parallel"),
        name="attn_sample",
    )(sinks, qs, ck, cv, kn, vn)


def _pool_sample_kernel(buf_ref, p_ref, wp_ref, ps_ref, o_ref, *, n_new):
    seq = [buf_ref[i] for i in range(POOL_BUF)] + [p_ref[i] for i in range(n_new)]
    b = p_ref.shape[1]
    col = lax.broadcasted_iota(I32, (b, D_POOL), 1)
    for t in range(n_new):
        cur = seq[POOL_BUF + t]
        acc = cur
        sums = {}
        for back in range(1, max(POOL_WINDOWS)):
            acc = acc + seq[POOL_BUF + t - back]
            if back + 1 in POOL_WINDOWS:
                sums[back + 1] = acc
        mean = jnp.where(col < POOL_GROUP_DIM, sums[2] / 2.0,
                         jnp.where(col < 2 * POOL_GROUP_DIM, sums[4] / 4.0,
                                   jnp.where(col < 3 * POOL_GROUP_DIM, sums[8] / 8.0,
                                             sums[16] / 16.0)))
        o_ref[t] = _dot(mean - cur, wp_ref[...]) * ps_ref[...]


def _pool_sample(buf_tm, p_tm, wpool_bd, pool_scale):
    n_new = p_tm.shape[0]
    return pl.pallas_call(
        functools.partial(_pool_sample_kernel, n_new=n_new),
        out_shape=jax.ShapeDtypeStruct(p_tm.shape, F32),
        compiler_params=pltpu.CompilerParams(vmem_limit_bytes=VMEM_LIMIT),
        name="pool_sample",
    )(buf_tm, p_tm, wpool_bd, pool_scale.reshape(1, D_POOL))


def _outproj_route_kernel(*refs, valid_lo, valid_mid, n_x, n_first):
    tt = TT
    x_refs = refs[:n_x]
    (mix_ref, mixs_ref, w_ref, g_ref, wrh_ref, wrl_ref, br_ref,
     x1_ref, xn_ref, info_ref, infot_ref, cnt_ref, acc_ref) = refs[n_x:]
    i = pl.program_id(0)
    is_sample_tile = i == n_first
    acc = jnp.where(i == 0, 0.0, acc_ref[...])
    mix = mix_ref[...] + jnp.where(is_sample_tile, mixs_ref[...], 0).astype(mix_ref.dtype)
    x1 = _pick_rows(x_refs, n_first) + _dot(mix, w_ref[...])
    x1_ref[...] = x1
    xn = _rms(x1, g_ref[...])
    for cc_ in range(D_MODEL // LANES):
        xn_ref[_slab(cc_, tt), :] = xn[:, cc_ * LANES:(cc_ + 1) * LANES]
    hi = xn.astype(BF16)
    lo = (xn - hi.astype(F32)).astype(BF16)
    hi_out = jnp.dot(hi, jnp.concatenate([wrh_ref[...], wrl_ref[...]], axis=1), preferred_element_type=F32)
    logits = (hi_out[:, :LANES] + jnp.dot(lo, wrh_ref[...], preferred_element_type=F32)
              + hi_out[:, LANES:]) + br_ref[...]
    lane = lax.broadcasted_iota(I32, (tt, LANES), 1)
    lanef = lane.astype(F32)
    lg = jnp.where(lane < MOE_GROUPS, logits, NEG_INF)
    mg = jnp.max(lg, axis=-1, keepdims=True)
    gstar = jnp.min(jnp.where(lg == mg, lanef, 1e9), axis=-1, keepdims=True)
    pgroup = 1.0 / jnp.sum(jnp.exp(lg - mg), axis=-1, keepdims=True)
    lo_lane = MOE_GROUPS + gstar * EXPERTS_PER_GROUP
    le = jnp.where((lanef >= lo_lane) & (lanef < lo_lane + EXPERTS_PER_GROUP), logits, NEG_INF)
    v1 = jnp.max(le, axis=-1, keepdims=True)
    i1 = jnp.min(jnp.where(le == v1, lanef, 1e9), axis=-1, keepdims=True)
    le2 = jnp.where(lanef == i1, NEG_INF, le)
    v2 = jnp.max(le2, axis=-1, keepdims=True)
    i2 = jnp.min(jnp.where(le2 == v2, lanef, 1e9), axis=-1, keepdims=True)
    tq = jnp.exp(v2 - v1)
    w1 = pgroup / (1.0 + tq)
    w2 = pgroup * tq / (1.0 + tq)
    e1 = i1 - MOE_GROUPS
    e2 = i2 - MOE_GROUPS
    rowg = i * tt + lax.broadcasted_iota(I32, (tt, LANES), 0)
    valid = (rowg < valid_lo) | (rowg >= valid_mid)
    oh1 = jnp.where(valid & (lanef == e1), 1.0, 0.0)
    oh2 = jnp.where(valid & (lanef == e2), 1.0, 0.0)
    both = oh1 + oh2
    rr = lax.broadcasted_iota(I32, (tt, tt), 0)
    cc = lax.broadcasted_iota(I32, (tt, tt), 1)
    tril = jnp.where(cc < rr, 1.0, 0.0).astype(BF16)
    base = acc + jnp.dot(tril, both.astype(BF16), preferred_element_type=F32)
    rank1 = jnp.sum(oh1 * base, axis=-1, keepdims=True)
    rank2 = jnp.sum(oh2 * base, axis=-1, keepdims=True)
    info = jnp.where(lane == 0, e1, jnp.where(lane == 1, e2, jnp.where(lane == 2, w1, jnp.where(
        lane == 3, w2, jnp.where(lane == 4, rank1, jnp.where(lane == 5, rank2, 0.0))))))
    info_ref[...] = info
    infot_ref[0] = info.T[:SUB, :]
    total = acc + jnp.sum(both, axis=0, keepdims=True)
    acc_ref[...] = total
    cnt_ref[...] = jnp.broadcast_to(total, (8, LANES))


def _outproj_route(xs, mix, mix_s, w_out_bf16, g_ffn, wr_hi, wr_lo, br, t_rows, valid_lo, valid_mid):
    n = mix.shape[0]
    ns, dm = mix_s.shape
    mix_s = jnp.pad(mix_s, ((0, TT - ns), (0, 0))).astype(mix.dtype)
    nt = n // TT

    def const2(i):
        return (0, 0)

    return pl.pallas_call(
        functools.partial(_outproj_route_kernel, valid_lo=valid_lo, valid_mid=valid_mid,
                          n_x=len(xs), n_first=t_rows // TT),
        grid=(nt,),
        in_specs=_row_specs(xs, D_MODEL) + [
            pl.BlockSpec((TT, dm), lambda i: (i, 0)),
            pl.BlockSpec((TT, dm), const2),
            pl.BlockSpec((dm, D_MODEL), const2),
            pl.BlockSpec((1, D_MODEL), const2),
            pl.BlockSpec((D_MODEL, LANES), const2),
            pl.BlockSpec((D_MODEL, LANES), const2),
            pl.BlockSpec((1, LANES), const2)],
        out_specs=[pl.BlockSpec((TT, D_MODEL), lambda i: (i, 0)),
                   pl.BlockSpec((TT * SUB, LANES), lambda i: (i, 0)),
                   pl.BlockSpec((TT, LANES), lambda i: (i, 0)),
                   pl.BlockSpec((1, SUB, TT), lambda i: (i, 0, 0)),
                   pl.BlockSpec((8, LANES), const2)],
        out_shape=[jax.ShapeDtypeStruct((n, D_MODEL), F32),
                   jax.ShapeDtypeStruct((n * SUB, LANES), F32),
                   jax.ShapeDtypeStruct((n, LANES), F32),
                   jax.ShapeDtypeStruct((nt, SUB, TT), F32),
                   jax.ShapeDtypeStruct((8, LANES), F32)],
        scratch_shapes=[pltpu.VMEM((1, LANES), F32)],
        compiler_params=_cparams("arbitrary"),
        name="outproj_route",
    )(*xs, mix, mix_s, w_out_bf16, g_ffn.reshape(1, D_MODEL), wr_hi, wr_lo, br)


ROWS_PER_ISSUE = 8


def _issue_rows(lo, hi, row_copy):
    assert (hi - lo) % ROWS_PER_ISSUE == 0

    def body(it, c):
        for u in range(ROWS_PER_ISSUE):
            r = lo + it * ROWS_PER_ISSUE + u
            row_copy(r, 0).start(priority=0)
            row_copy(r, 1).start(priority=1)
        return c

    lax.fori_loop(0, (hi - lo) // ROWS_PER_ISSUE, body, 0)


def _dispatch_kernel(pos_ref, zt_ref, xn_ref, xs_ref, zbuf, sem, zsem, *, n_full, ranges, n_chunks):
    i = pl.program_id(0)

    def zero_copy(u, sem_):
        rows = ZERO_ROWS * SUB
        return pltpu.make_async_copy(zbuf, xs_ref.at[pl.ds(pl.multiple_of(u * rows, rows), rows)], sem_)

    def zero_loop(flag, sem_, start):
        def body(u, c):
            @pl.when(zt_ref[u] == flag)
            def _():
                if start:
                    zero_copy(u, sem_).start()
                else:
                    zero_copy(u, sem_).wait()
            return c
        lax.fori_loop(0, n_chunks, body, 0)

    @pl.when(i == 0)
    def _():
        zbuf[...] = jnp.zeros(zbuf.shape, F32)
        zero_loop(1, zsem.at[0], True)
        zero_loop(2, zsem.at[1], True)
        zero_loop(1, zsem.at[0], False)

    @pl.when(i == pl.num_programs(0) - 1)
    def _():
        zero_loop(2, zsem.at[1], False)

    def row_copy(r, k):
        return pltpu.make_async_copy(xn_ref.at[_tile_of(r)], xs_ref.at[_tile_of(pos_ref[0, 0, k * TT + r])], sem)

    def scatter(lo, hi):
        _issue_rows(lo, hi, row_copy)
        nrow = (hi - lo) * SUB
        for _ in range(2):
            pltpu.make_async_copy(xn_ref.at[pl.ds(lo * SUB, nrow)], xs_ref.at[pl.ds(0, nrow)], sem).wait()

    _for_tile_rows(i, n_full, ranges, scatter)


def _dispatch(xn, pos, zero_tiles, n_sorted, t_rows, ranges):
    n = xn.shape[0] // SUB
    return pl.pallas_call(
        functools.partial(_dispatch_kernel, n_full=t_rows // TT, ranges=ranges,
                          n_chunks=n_sorted // ZERO_ROWS),
        grid=(n // TT,),
        in_specs=[pl.BlockSpec((1, 1, 2 * TT), lambda i: (i, 0, 0), memory_space=pltpu.SMEM),
                  pl.BlockSpec(memory_space=pltpu.SMEM),
                  pl.BlockSpec((TT * SUB, LANES), lambda i: (i, 0))],
        out_specs=pl.BlockSpec(memory_space=pl.ANY),
        out_shape=jax.ShapeDtypeStruct((n_sorted * SUB, LANES), F32),
        scratch_shapes=[pltpu.VMEM((ZERO_ROWS * SUB, LANES), F32),
                        pltpu.SemaphoreType.DMA(()),
                        pltpu.SemaphoreType.DMA((2,))],
        compiler_params=_cparams("arbitrary"),
        name="moe_dispatch",
    )(pos, zero_tiles, xn)


def _expert_ffn_kernel(te_ref, nu_ref, xs_ref, wg_ref, wu_ref, wd_ref, ys_ref, wgb, wub, wdb):
    j = pl.program_id(0)
    changed = (j == 0) | (te_ref[j] != te_ref[jnp.maximum(j - 1, 0)])

    @pl.when((j < nu_ref[0]) & changed)
    def _():
        wgb[...] = wg_ref[0, 0].astype(BF16)
        wub[...] = wu_ref[0, 0].astype(BF16)
        wdb[...] = wd_ref[0, 0].astype(BF16)

    n_slab = D_MODEL // LANES

    @pl.when(j < nu_ref[0])
    def _():
        for part in range(MOE_TM // FFN_ROWS):
            base = part * FFN_ROWS * SUB
            x = jnp.concatenate([xs_ref[_slab(c, FFN_ROWS, base), :] for c in range(n_slab)],
                                axis=1).astype(BF16)
            hg = jnp.dot(x, wgb[...], preferred_element_type=F32)
            hu = jnp.dot(x, wub[...], preferred_element_type=F32)
            h = (_silu(hg) * hu).astype(BF16)
            y = jnp.dot(h, wdb[...], preferred_element_type=F32)
            for c in range(n_slab):
                ys_ref[_slab(c, FFN_ROWS, base), :] = y[:, c * LANES:(c + 1) * LANES]

    @pl.when(j >= nu_ref[0])
    def _():
        ys_ref[...] = jnp.zeros(ys_ref.shape, F32)


def _expert_ffn(xs, tile_expert, n_used, w_gate, w_up, w_down, layer):
    n_sorted = xs.shape[0] // SUB
    nt = n_sorted // MOE_TM

    def row_map(j, te, nu):
        return (jnp.minimum(j, nu[0] - 1), 0)

    def w_map(j, te, nu):
        return (layer, te[j], 0, 0)

    return pl.pallas_call(
        _expert_ffn_kernel,
        grid_spec=pltpu.PrefetchScalarGridSpec(
            num_scalar_prefetch=2,
            grid=(nt,),
            in_specs=[pl.BlockSpec((MOE_TM * SUB, LANES), row_map),
                      pl.BlockSpec((1, 1, D_MODEL, D_EXPERT), w_map),
                      pl.BlockSpec((1, 1, D_MODEL, D_EXPERT), w_map),
                      pl.BlockSpec((1, 1, D_EXPERT, D_MODEL), w_map)],
            out_specs=pl.BlockSpec((MOE_TM * SUB, LANES), lambda j, te, nu: (j, 0)),
            scratch_shapes=[pltpu.VMEM((D_MODEL, D_EXPERT), BF16),
                            pltpu.VMEM((D_MODEL, D_EXPERT), BF16),
                            pltpu.VMEM((D_EXPERT, D_MODEL), BF16)]),
        out_shape=jax.ShapeDtypeStruct((n_sorted * SUB, LANES), F32),
        compiler_params=_cparams("arbitrary"),
        name="expert_ffn",
    )(tile_expert, n_used, xs, w_gate, w_up, w_down)


def _combine_kernel(posc_ref, posn_ref, x1_ref, info_ref, g_ref, ys_ref, *rest,
                    n_full, n_tiles, ranges, valid_lo, valid_mid, final_norm):
    out_refs, (ybuf, sem) = rest[:-2], rest[-2:]
    i = pl.program_id(0)
    slot = lax.rem(i, 2)

    def buf_base(s, k):
        return (s * 2 + k) * (TT * SUB)

    def gather(pos_ref, dst_slot):
        def row_copy(r, k):
            dst = pl.ds(pl.multiple_of(buf_base(dst_slot, k) + r * SUB, SUB), SUB)
            return pltpu.make_async_copy(ys_ref.at[_tile_of(pos_ref[0, 0, k * TT + r])],
                                         ybuf.at[dst], sem.at[dst_slot])
        return lambda lo, hi: _issue_rows(lo, hi, row_copy)

    @pl.when(i == 0)
    def _():
        ybuf[...] = jnp.zeros(ybuf.shape, F32)
        _for_tile_rows(i, n_full, ranges, gather(posc_ref, 0))

    @pl.when(i + 1 < n_tiles)
    def _():
        _for_tile_rows(i + 1, n_full, ranges, gather(posn_ref, 1 - slot))

    def drain(lo, hi):
        nrow = (hi - lo) * SUB
        for k in range(2):
            dst = pl.ds(pl.multiple_of(buf_base(slot, k) + lo * SUB, SUB), nrow)
            pltpu.make_async_copy(ys_ref.at[pl.ds(0, nrow)], ybuf.at[dst], sem.at[slot]).wait()

    _for_tile_rows(i, n_full, ranges, drain)

    info = info_ref[...]
    w1 = info[:, 2:3]
    w2 = info[:, 3:4]
    rowg = i * TT + lax.broadcasted_iota(I32, (TT, 1), 0)
    valid = (rowg < valid_lo) | (rowg >= valid_mid)
    outs = []
    for c in range(D_MODEL // LANES):
        y1 = ybuf[_slab(c, TT, buf_base(slot, 0)), :]
        y2 = ybuf[_slab(c, TT, buf_base(slot, 1)), :]
        outs.append(x1_ref[:, c * LANES:(c + 1) * LANES] + jnp.where(valid, w1 * y1 + w2 * y2, 0.0))
    out = jnp.concatenate(outs, axis=1)
    if not final_norm:
        out_refs[0][...] = out
    else:
        out = _rms(out, g_ref[...])

        @pl.when(i < n_full)
        def _():
            out_refs[0][...] = out

        @pl.when(i >= n_full)
        def _():
            out_refs[1][...] = out


def _combine(x1, info, pos, ys, g_final, t_rows, ranges, valid_lo, valid_mid, final_norm):
    n = x1.shape[0]
    nt = n // TT
    n_full = t_rows // TT
    if final_norm:
        out_specs = [pl.BlockSpec((TT, D_MODEL), lambda i: (jnp.minimum(i, n_full - 1), 0)),
                     pl.BlockSpec((TT, D_MODEL), lambda i: (jnp.maximum(i - n_full, 0), 0))]
        out_shape = [jax.ShapeDtypeStruct((t_rows, D_MODEL), F32),
                     jax.ShapeDtypeStruct((n - t_rows, D_MODEL), F32)]
    else:
        out_specs = [pl.BlockSpec((TT, D_MODEL), lambda i: (i, 0))]
        out_shape = [jax.ShapeDtypeStruct((n, D_MODEL), F32)]
    return pl.pallas_call(
        functools.partial(_combine_kernel, n_full=n_full, n_tiles=nt, ranges=ranges,
                          valid_lo=valid_lo, valid_mid=valid_mid, final_norm=final_norm),
        grid=(nt,),
        in_specs=[pl.BlockSpec((1, 1, 2 * TT), lambda i: (i, 0, 0), memory_space=pltpu.SMEM),
                  pl.BlockSpec((1, 1, 2 * TT), lambda i: (jnp.minimum(i + 1, nt - 1), 0, 0),
                               memory_space=pltpu.SMEM),
                  pl.BlockSpec((TT, D_MODEL), lambda i: (i, 0)),
                  pl.BlockSpec((TT, LANES), lambda i: (i, 0)),
                  pl.BlockSpec((1, D_MODEL), lambda i: (0, 0)),
                  pl.BlockSpec(memory_space=pl.ANY)],
        out_specs=out_specs,
        out_shape=out_shape,
        scratch_shapes=[pltpu.VMEM((2 * 2 * TT * SUB, LANES), F32),
                        pltpu.SemaphoreType.DMA((2,))],
        compiler_params=_cparams("arbitrary"),
        name="moe_combine",
    )(pos, pos, x1, info, g_final.reshape(1, D_MODEL), ys)


def _moe(x1, xn, info, infot, cnt, w_gate, w_up, w_down, layer, g_final, t_rows, ns, final_norm):
    n = x1.shape[0]
    valid_lo, valid_mid = t_rows + ns, n - N_META
    ranges = _small_tile_ranges(t_rows, ns, n)
    n_valid = valid_lo + N_META
    nt = (2 * n_valid + N_EXPERTS * (MOE_TM - 1)) // MOE_TM
    counts = cnt[0, :N_EXPERTS].astype(I32)
    tiles = (counts + MOE_TM - 1) // MOE_TM
    tile_end = jnp.cumsum(tiles)
    row_start = (tile_end - tiles) * MOE_TM
    n_used = tile_end[-1:]
    experts = jnp.arange(N_EXPERTS, dtype=I32)

    def slot_rows(e, rank):
        start = jnp.sum(jnp.where(e[..., None] == experts, row_start, 0), axis=-1)
        return start + rank

    pos = jnp.concatenate([slot_rows(infot[:, 0].astype(I32), infot[:, 4].astype(I32)),
                           slot_rows(infot[:, 1].astype(I32), infot[:, 5].astype(I32))], axis=-1)
    rows = jnp.arange(n).reshape(n // TT, TT)
    valid = (rows < valid_lo) | (rows >= valid_mid)
    pos = jnp.where(jnp.concatenate([valid, valid], axis=-1), pos, 0)[:, None, :]
    tile_ids = jnp.arange(nt, dtype=I32)
    tile_expert = jnp.sum(tile_end[None, :] <= jnp.minimum(tile_ids, n_used[0] - 1)[:, None],
                          axis=1).astype(I32)
    chunk_lo = jnp.arange(nt * MOE_TM // ZERO_ROWS, dtype=I32)[:, None] * ZERO_ROWS
    pad_lo, pad_hi = (row_start + counts)[None, :], (tile_end * MOE_TM)[None, :]
    has_pad = jnp.any(jnp.maximum(chunk_lo, pad_lo) < jnp.minimum(chunk_lo + ZERO_ROWS, pad_hi), axis=1)
    zero_tiles = jnp.where(chunk_lo[:, 0] >= n_used[0] * MOE_TM, 2, has_pad.astype(I32))
    xs = _dispatch(xn, pos, zero_tiles, nt * MOE_TM, t_rows, ranges)
    ys = _expert_ffn(xs, tile_expert, n_used.astype(I32), w_gate, w_up, w_down, layer)
    return _combine(x1, info, pos, ys, g_final, t_rows, ranges, valid_lo, valid_mid, final_norm)


def _shift_rows(x, k, fill):
    if k % SUB == 0:
        return jnp.concatenate([jnp.full((k, x.shape[1]), fill, x.dtype), x[:x.shape[0] - k]], axis=0)
    row = lax.broadcasted_iota(I32, x.shape, 0)
    return jnp.where(row >= k, pltpu.roll(x, k, 0), fill)


def _lru_gates(xc, wgate_ref, bgate_ref, lam_sp):
    gates = _dot(xc, wgate_ref[...]) + bgate_ref[...]
    r = _sigmoid(gates[:, :D_LRU])
    ig = _sigmoid(gates[:, D_LRU:])
    log_a = -LRU_C * r * lam_sp
    a = jnp.exp(log_a)
    bx = jnp.sqrt(-jnp.tanh(log_a) * (a * a + 1.0)) * (ig * xc)
    return a, bx


def _conv_block(seq_ref, u, w_ref, b_ref, fresh):
    seq = jnp.concatenate([jnp.where(fresh, 0.0, seq_ref[...]), u], axis=0)
    out = b_ref[...] + u * w_ref[CONV_WIDTH - 1:CONV_WIDTH, :]
    for back in range(1, CONV_WIDTH):
        tap = CONV_WIDTH - 1 - back
        out = out + pltpu.roll(seq, back, 0)[8:, :] * w_ref[tap:tap + 1, :]
    seq_ref[...] = u[BLK - 8:, :]
    return out


def _mix1_prompt_kernel(*refs, nblk):
    j = pl.program_id(0)
    o_ref = refs[16]

    @pl.when(j <= nblk)
    def _():
        _mix1_prompt_step(*refs)

    @pl.when(j > nblk)
    def _():
        o_ref[...] = jnp.zeros(o_ref.shape, o_ref.dtype)


def _mix1_prompt_step(cx_ref, cg_ref, z_ref, xbc_ref, dt_ref,
                      cwc_ref, cbc_ref, wgate_ref, bgate_ref, lam_ref,
                      cwd_ref, cbd_ref, dtb_ref, alog_ref, dskip_ref, nw_ref,
                      o_ref, lruh_ref, ssdh_ref,
                      seqc_ref, seqd_ref, hl_ref, hs_ref):
    refs = (cx_ref, cg_ref, z_ref, xbc_ref, dt_ref, cwc_ref, cbc_ref, wgate_ref, bgate_ref, lam_ref,
            cwd_ref, cbd_ref, dtb_ref, alog_ref, dskip_ref, nw_ref, o_ref, lruh_ref, ssdh_ref,
            seqc_ref, seqd_ref, hl_ref, hs_ref)
    for sub in range(MIX1_BLKS):
        _mix1_block(*refs, sub=sub)


def _mix1_block(cx_ref, cg_ref, z_ref, xbc_ref, dt_ref,
                cwc_ref, cbc_ref, wgate_ref, bgate_ref, lam_ref,
                cwd_ref, cbd_ref, dtb_ref, alog_ref, dskip_ref, nw_ref,
                o_ref, lruh_ref, ssdh_ref,
                seqc_ref, seqd_ref, hl_ref, hs_ref, *, sub):
    j = pl.program_id(0)
    rows = slice(sub * BLK, (sub + 1) * BLK)
    fresh = (j == 0) if sub == 0 else False
    rowc = lax.broadcasted_iota(I32, (BLK, 1), 0)
    live_min_meta = META_PAD if sub == MIX1_BLKS - 1 else BLK
    live = rowc >= jnp.where(j >= 1, 0, live_min_meta)

    xc = _conv_block(seqc_ref, cx_ref[rows, :], cwc_ref, cbc_ref, fresh)
    a, bx = _lru_gates(xc, wgate_ref, bgate_ref, _softplus(-lam_ref[...]))
    a = jnp.where(live, a, 1.0)
    bx = jnp.where(live, bx, 0.0)
    k = 1
    while k < BLK:
        bx = a * _shift_rows(bx, k, 0.0) + bx
        a = a * _shift_rows(a, k, 1.0)
        k *= 2
    h = a * jnp.where(fresh, 0.0, hl_ref[...]) + bx
    hl_ref[...] = h[BLK - 1:BLK, :]
    lruh_ref[...] = h[BLK - 1:BLK, :]
    o_ref[rows, :D_LRU] = (h * _gelu_tanh(cg_ref[rows, :])).astype(o_ref.dtype)

    xbc = _silu(_conv_block(seqd_ref, xbc_ref[rows, :], cwd_ref, cbd_ref, fresh))
    xs = xbc[:, :D_SSD]
    dt = jnp.where(live, _softplus(dt_ref[rows, :] + dtb_ref[...]), 0.0)
    a_neg = -jnp.exp(alog_ref[...])
    acum = dt * a_neg
    k = 1
    while k < BLK:
        acum = acum + _shift_rows(acum, k, 0.0)
        k *= 2
    acum_t = acum.T
    dt_t = dt.T
    xs_t = xs.T
    tri = (lax.broadcasted_iota(I32, (BLK, BLK), 0) >= lax.broadcasted_iota(I32, (BLK, BLK), 1))
    rep = SSD_HEADS // SSD_GROUPS
    ys = []
    for g in range(SSD_GROUPS):
        bg = xbc[:, D_SSD + g * SSD_STATE:D_SSD + (g + 1) * SSD_STATE]
        cgm = xbc[:, D_SSD + (SSD_GROUPS + g) * SSD_STATE:D_SSD + (SSD_GROUPS + g + 1) * SSD_STATE]
        cb = _dot_nt(cgm, bg)
        for hh in range(rep):
            hd = g * rep + hh
            col = acum[:, hd:hd + 1]
            rowv = acum_t[hd:hd + 1, :]
            decay = jnp.exp(jnp.where(tri, col - rowv, NEG_INF))
            scores = cb * decay * dt_t[hd:hd + 1, :]
            xh = xs[:, hd * SSD_HEAD_DIM:(hd + 1) * SSD_HEAD_DIM]
            y = _dot(scores, xh)
            hprev = jnp.where(fresh, 0.0, hs_ref[hd])
            y = y + jnp.exp(col) * _dot_nt(cgm, hprev)
            last = acum_t[hd:hd + 1, BLK - 1:BLK]
            w_end = jnp.exp(last - rowv) * dt_t[hd:hd + 1, :]
            xw_t = xs_t[hd * SSD_HEAD_DIM:(hd + 1) * SSD_HEAD_DIM, :] * w_end
            hnew = jnp.exp(last) * hprev + _dot(xw_t, bg)
            hs_ref[hd] = hnew
            ssdh_ref[hd] = hnew
            ys.append(y + dskip_ref[:, hd * SSD_HEAD_DIM:(hd + 1) * SSD_HEAD_DIM] * xh)
    y = jnp.concatenate(ys, axis=1) * _silu(z_ref[rows, :])
    o_ref[rows, D_LRU:] = _rms(y, nw_ref[...]).astype(o_ref.dtype)


def _mix1_prompt(cx, cg, z, xbc, dt, wts, t_rows):
    n = cx.shape[0]
    step = MIX1_BLKS * BLK
    assert t_rows % step == 0 and n % step == 0 and step <= TT
    nstep = t_rows // step
    last = n // step - 1

    def cur(j):
        return (jnp.where(j == 0, last, j - 1), 0)

    def const2(j):
        return (0, 0)

    in_specs = [pl.BlockSpec((step, D_LRU), cur), pl.BlockSpec((step, D_LRU), cur),
                pl.BlockSpec((step, D_SSD), cur), pl.BlockSpec((step, D_XBC), cur),
                pl.BlockSpec((step, LANES), cur)]
    in_specs += [pl.BlockSpec(w.shape, const2) for w in wts]
    return pl.pallas_call(
        functools.partial(_mix1_prompt_kernel, nblk=nstep),
        grid=(last + 1,),
        in_specs=in_specs,
        out_specs=[pl.BlockSpec((step, D_MIX_ODD), cur),
                   pl.BlockSpec((1, D_LRU), const2),
                   pl.BlockSpec((SSD_HEADS, SSD_HEAD_DIM, SSD_STATE), lambda j: (0, 0, 0))],
        out_shape=[jax.ShapeDtypeStruct((n, D_MIX_ODD), BF16),
                   jax.ShapeDtypeStruct((1, D_LRU), F32),
                   jax.ShapeDtypeStruct((SSD_HEADS, SSD_HEAD_DIM, SSD_STATE), F32)],
        scratch_shapes=[pltpu.VMEM((8, D_LRU), F32),
                        pltpu.VMEM((8, D_XBC), F32),
                        pltpu.VMEM((1, D_LRU), F32),
                        pltpu.VMEM((SSD_HEADS, SSD_HEAD_DIM, SSD_STATE), F32)],
        compiler_params=_cparams("arbitrary"),
        name="mix1_prompt",
    )(cx, cg, z, xbc, dt, *wts)


def _conv_step(seq, t, w_ref, b_ref):
    out = b_ref[...]
    for tap in range(CONV_WIDTH):
        out = out + seq[t + tap] * w_ref[tap:tap + 1, :]
    return out


def _lru_sample_kernel(cx_ref, cg_ref, buf_ref, h0_ref, cwc_ref, cbc_ref, wgate_ref, bgate_ref,
                       lam_ref, o_ref, hout_ref, *, n_new):
    seq = [buf_ref[i] for i in range(CONV_WIDTH - 1)] + [cx_ref[i] for i in range(n_new)]
    lam_sp = _softplus(-lam_ref[...])
    h = h0_ref[...]
    for t in range(n_new):
        xc = _conv_step(seq, t, cwc_ref, cbc_ref)
        a, bx = _lru_gates(xc, wgate_ref, bgate_ref, lam_sp)
        h = a * h + bx
        o_ref[t] = h * _gelu_tanh(cg_ref[t])
    hout_ref[...] = h


def _lru_sample(cx_tm, cg_tm, buf_tm, h0, wts):
    n_new = cx_tm.shape[0]
    return pl.pallas_call(
        functools.partial(_lru_sample_kernel, n_new=n_new),
        out_shape=[jax.ShapeDtypeStruct(cx_tm.shape, F32), jax.ShapeDtypeStruct(h0.shape, F32)],
        compiler_params=pltpu.CompilerParams(vmem_limit_bytes=VMEM_LIMIT),
        name="lru_sample",
    )(cx_tm, cg_tm, buf_tm, h0, *wts)


def _split3(x):
    hi = x.astype(BF16)
    r = x - hi.astype(F32)
    mid = r.astype(BF16)
    lo = (r - mid.astype(F32)).astype(BF16)
    return hi, mid, lo


def _ssd_sample_kernel(xbc_ref, dt_ref, z_ref, buf_ref, h0_ref, cwd_ref, cbd_ref, dtb_ref, alog_ref,
                       dskip_ref, nw_ref, o_ref, hout_ref, xc_ref, dec_ref, dtx_ref, y_ref, *, n_new, bs):
    seq = [buf_ref[i] for i in range(CONV_WIDTH - 1)] + [xbc_ref[i] for i in range(n_new)]
    a_neg = -jnp.exp(alog_ref[...])
    hcol = lax.broadcasted_iota(I32, (LANES, D_SSD), 1) // SSD_HEAD_DIM
    expand = jnp.where(lax.broadcasted_iota(I32, (LANES, D_SSD), 0) == hcol, 1.0, 0.0).astype(BF16)

    def widen(v):
        parts = _split3(v)
        return sum(jnp.dot(p, expand, preferred_element_type=F32) for p in parts)

    for t in range(n_new):
        xc = _silu(_conv_step(seq, t, cwd_ref, cbd_ref))
        xc_ref[t] = xc
        dt = _softplus(dt_ref[t] + dtb_ref[...])
        dtw = widen(dt)
        dec_ref[t] = jnp.exp(widen(dt * a_neg))
        dtx_ref[t] = dtw * xc[:, :D_SSD]

    zero_rows = jnp.zeros((LANES - 2 * n_new * SSD_GROUPS, SSD_STATE), F32)
    half = D_SSD // SSD_GROUPS

    def per_seq(b, c):
        rows = [xc_ref[t, pl.ds(b, 1), :] for t in range(n_new)]
        bc = [r[:, D_SSD + k * SSD_STATE:D_SSD + (k + 1) * SSD_STATE]
              for r in rows for k in range(2 * SSD_GROUPS)]
        bc_t = jnp.concatenate(bc + [zero_rows], axis=0).T
        ht = h0_ref[b].T
        for t in range(n_new):
            bcol = [jnp.broadcast_to(bc_t[:, t * 4 + g:t * 4 + g + 1], (SSD_STATE, half))
                    for g in range(SSD_GROUPS)]
            ccol = [jnp.broadcast_to(bc_t[:, t * 4 + SSD_GROUPS + g:t * 4 + SSD_GROUPS + g + 1],
                                     (SSD_STATE, half)) for g in range(SSD_GROUPS)]
            bw = jnp.concatenate(bcol, axis=1)
            cw = jnp.concatenate(ccol, axis=1)
            ht = ht * dec_ref[t, pl.ds(b, 1), :] + bw * dtx_ref[t, pl.ds(b, 1), :]
            y = jnp.sum(cw * ht, axis=0, keepdims=True)
            y_ref[t, pl.ds(b, 1), :] = y + dskip_ref[...] * rows[t][:, :D_SSD]
        hout_ref[b] = ht.T
        return c

    lax.fori_loop(0, bs, per_seq, 0)
    for t in range(n_new):
        o_ref[t] = _rms(y_ref[t] * _silu(z_ref[t]), nw_ref[...])


def _ssd_sample(xbc_tm, dt_tm, z_tm, buf_tm, h0, wts):
    n_new, b, _ = xbc_tm.shape
    bs = 16 if b % 16 == 0 else b
    hp = SSD_HEADS * SSD_HEAD_DIM

    def tm(c):
        return pl.BlockSpec((n_new, bs, c), lambda i: (0, i, 0))

    in_specs = [tm(D_XBC), tm(LANES), tm(D_SSD),
                pl.BlockSpec((CONV_WIDTH - 1, bs, D_XBC), lambda i: (0, i, 0)),
                pl.BlockSpec((bs, hp, SSD_STATE), lambda i: (i, 0, 0))]
    in_specs += [pl.BlockSpec(w.shape, lambda i: (0, 0)) for w in wts]
    return pl.pallas_call(
        functools.partial(_ssd_sample_kernel, n_new=n_new, bs=bs),
        grid=(b // bs,),
        in_specs=in_specs,
        out_specs=[tm(D_SSD), pl.BlockSpec((bs, hp, SSD_STATE), lambda i: (i, 0, 0))],
        out_shape=[jax.ShapeDtypeStruct((n_new, b, D_SSD), F32),
                   jax.ShapeDtypeStruct((b, hp, SSD_STATE), F32)],
        scratch_shapes=[pltpu.VMEM((n_new, bs, D_XBC), F32),
                        pltpu.VMEM((n_new, bs, D_SSD), F32),
                        pltpu.VMEM((n_new, bs, D_SSD), F32),
                        pltpu.VMEM((n_new, bs, D_SSD), F32)],
        compiler_params=_cparams("parallel"),
        name="ssd_sample",
    )(xbc_tm, dt_tm, z_tm, buf_tm, h0, *wts)


def _block_diag(w):
    g, c, _ = w.shape
    eye = jnp.eye(g, dtype=w.dtype)
    return (eye[:, None, :, None] * w[:, :, None, :]).reshape(g * c, g * c)


def _router_weights(w_rg, b_rg, w_re, b_re):
    w = jnp.concatenate([w_rg, w_re], axis=1)
    w = jnp.pad(w, ((0, 0), (0, LANES - w.shape[1])))
    hi = w.astype(BF16)
    lo = (w - hi.astype(F32)).astype(BF16)
    b = jnp.pad(jnp.concatenate([b_rg, b_re]), (0, LANES - MOE_GROUPS - N_EXPERTS)).reshape(1, LANES)
    return hi, lo, b


def kernel(x_prompt, x_sample, cache_swa_k, cache_swa_v, cache_pool, state_lru_conv, state_lru_h, state_ssd_conv, state_ssd_h, meta_tokens, norm_mix, norm_ffn, norm_final, attn_w_in, attn_w_out, attn_sinks, pool_w, pool_scale, rec_w_in, rec_w_out, lru_conv_w, lru_conv_b, lru_w_rg, lru_b_rg, lru_w_ig, lru_b_ig, lru_lambda, ssd_conv_w, ssd_conv_b, ssd_dt_bias, ssd_a_log, ssd_d, ssd_norm_w, moe_w_rg, moe_b_rg, moe_w_re, moe_b_re, moe_w_gate, moe_w_up, moe_w_down):
    t_rows = x_prompt.shape[1]
    b, s_new = x_sample.shape[:2]
    ns = b * s_new
    small = -(-ns // TT) * TT + TT
    n = t_rows + small
    meta_blk = n // BLK - 1
    valid_lo = t_rows + ns
    valid_mid = n - N_META
    assert x_prompt.shape[0] == 1 and t_rows % TT == 0 and ns % BLK == 0 and ns <= TT

    x_small = jnp.concatenate([
        x_sample.transpose(1, 0, 2).reshape(ns, D_MODEL),
        jnp.zeros((small - ns - N_META, D_MODEL), F32),
        meta_tokens.astype(F32)], axis=0)
    xs0 = (x_prompt[0], x_small)

    def sample_tm(a):
        return a[t_rows:t_rows + ns].reshape(s_new, b, a.shape[1])

    q, kv, p = _norm_proj(xs0, norm_mix[0], attn_w_in[0].astype(BF16), (D_ATTN, 2 * D_KV, D_POOL),
                          (BF16, F32, F32))
    wpool_bd = _block_diag(pool_w[0]).astype(BF16)
    mix = _mix0_prompt(q, kv, p, attn_sinks[0], wpool_bd, pool_scale[0], t_rows, meta_blk)
    nq = s_new * GQA
    qs = sample_tm(q).reshape(s_new, b, N_KV_HEADS, GQA, HEAD_DIM).transpose(1, 2, 0, 3, 4)
    qs = qs.reshape(b, N_KV_HEADS, nq, HEAD_DIM)
    kv_s = sample_tm(kv).transpose(1, 0, 2)
    kn, vn = kv_s[:, :, :D_KV], kv_s[:, :, D_KV:]
    ck = cache_swa_k[0].reshape(b, N_META + WINDOW, D_KV)
    cv = cache_swa_v[0].reshape(b, N_META + WINDOW, D_KV)
    o_s = _attn_sample(qs, ck, cv, kn, vn, attn_sinks[0])
    o_s = o_s.reshape(b, N_KV_HEADS, s_new, GQA, HEAD_DIM).transpose(2, 0, 1, 3, 4).reshape(s_new, b, D_ATTN)
    p_s = sample_tm(p)
    pool_buf_tm = cache_pool[0].transpose(1, 0, 2)
    yp_s = _pool_sample(pool_buf_tm, p_s, wpool_bd, pool_scale[0])
    mix_s = jnp.concatenate([o_s, yp_s], axis=-1).reshape(ns, D_MIX_EVEN)
    wr_hi, wr_lo, br = _router_weights(moe_w_rg[0], moe_b_rg[0], moe_w_re[0], moe_b_re[0])
    x1, xn, info, infot, cnt = _outproj_route(xs0, mix, mix_s, attn_w_out[0].astype(BF16), norm_ffn[0],
                                              wr_hi, wr_lo, br, t_rows, valid_lo, valid_mid)
    x, = _moe(x1, xn, info, infot, cnt, moe_w_gate, moe_w_up, moe_w_down, 0, norm_final,
              t_rows, ns, False)

    meta_rows = slice(n - N_META, n)
    last_rows = slice(t_rows - WINDOW, t_rows)
    swa_k_p = jnp.concatenate([kv[meta_rows, :D_KV], kv[last_rows, :D_KV]], axis=0)
    swa_v_p = jnp.concatenate([kv[meta_rows, D_KV:], kv[last_rows, D_KV:]], axis=0)
    swa_k_p = swa_k_p.reshape(1, 1, N_META + WINDOW, N_KV_HEADS, HEAD_DIM)
    swa_v_p = swa_v_p.reshape(1, 1, N_META + WINDOW, N_KV_HEADS, HEAD_DIM)
    pool_p = p[t_rows - POOL_BUF:t_rows].reshape(1, 1, POOL_BUF, D_POOL)
    swa_k_s = jnp.concatenate([ck[:, :N_META], ck[:, N_META + s_new:], kn], axis=1)
    swa_v_s = jnp.concatenate([cv[:, :N_META], cv[:, N_META + s_new:], vn], axis=1)
    swa_k_s = swa_k_s.reshape(1, b, N_META + WINDOW, N_KV_HEADS, HEAD_DIM)
    swa_v_s = swa_v_s.reshape(1, b, N_META + WINDOW, N_KV_HEADS, HEAD_DIM)
    pool_s = jnp.concatenate([cache_pool[0], p_s.transpose(1, 0, 2)], axis=1)[:, -POOL_BUF:][None]

    d_in_odd = rec_w_in.shape[2]
    w_in1 = jnp.pad(rec_w_in[0], ((0, 0), (0, LANES - SSD_HEADS))).astype(BF16)
    assert d_in_odd == 2 * D_LRU + D_SSD + D_XBC + SSD_HEADS
    cx, cg, z, xbc, dt = _norm_proj((x,), norm_mix[1], w_in1, (D_LRU, D_LRU, D_SSD, D_XBC, LANES))
    wgate = jnp.concatenate([_block_diag(lru_w_rg[0]), _block_diag(lru_w_ig[0])], axis=1).astype(BF16)
    bgate = jnp.concatenate([lru_b_rg[0], lru_b_ig[0]]).reshape(1, 2 * D_LRU)
    pad8 = (0, LANES - SSD_HEADS)
    dtb = jnp.pad(ssd_dt_bias[0], pad8).reshape(1, LANES)
    alog = jnp.pad(ssd_a_log[0], pad8).reshape(1, LANES)
    dskip = jnp.repeat(ssd_d[0], SSD_HEAD_DIM).reshape(1, D_SSD)
    lru_w = (lru_conv_w[0], lru_conv_b[0].reshape(1, D_LRU), wgate, bgate, lru_lambda[0].reshape(1, D_LRU))
    ssd_w = (ssd_conv_w[0], ssd_conv_b[0].reshape(1, D_XBC), dtb, alog, dskip, ssd_norm_w[0].reshape(1, D_SSD))
    mix, lru_h_p, ssd_h_p = _mix1_prompt(cx, cg, z, xbc, dt, lru_w + ssd_w, t_rows)
    cx_s, cg_s, z_s, xbc_s, dt_s = (sample_tm(a) for a in (cx, cg, z, xbc, dt))
    yc_s, lru_h_s = _lru_sample(cx_s, cg_s, state_lru_conv[0].transpose(1, 0, 2), state_lru_h[0], lru_w)
    h0 = state_ssd_h[0].reshape(b, SSD_HEADS * SSD_HEAD_DIM, SSD_STATE)
    yd_s, ssd_h_s = _ssd_sample(xbc_s, dt_s, z_s, state_ssd_conv[0].transpose(1, 0, 2), h0, ssd_w)
    mix_s = jnp.concatenate([yc_s, yd_s], axis=-1).reshape(ns, D_MIX_ODD)
    wr_hi, wr_lo, br = _router_weights(moe_w_rg[1], moe_b_rg[1], moe_w_re[1], moe_b_re[1])
    x1, xn, info, infot, cnt = _outproj_route((x,), mix, mix_s, rec_w_out[0].astype(BF16), norm_ffn[1],
                                              wr_hi, wr_lo, br, t_rows, valid_lo, valid_mid)
    y_p, y_small = _moe(x1, xn, info, infot, cnt, moe_w_gate, moe_w_up, moe_w_down, 1, norm_final,
                        t_rows, ns, True)

    tail = CONV_WIDTH - 1
    lru_conv_p = cx[t_rows - tail:t_rows].reshape(1, 1, tail, D_LRU)
    ssd_conv_p = xbc[t_rows - tail:t_rows].reshape(1, 1, tail, D_XBC)
    lru_conv_s = jnp.concatenate([state_lru_conv[0], cx_s.transpose(1, 0, 2)], axis=1)[:, -tail:][None]
    ssd_conv_s = jnp.concatenate([state_ssd_conv[0], xbc_s.transpose(1, 0, 2)], axis=1)[:, -tail:][None]

    y_prompt = y_p[None]
    y_sample = y_small[:ns].reshape(s_new, b, D_MODEL).transpose(1, 0, 2)
    return (y_prompt, y_sample, swa_k_p, swa_v_p, pool_p, lru_conv_p,
            lru_h_p.reshape(1, 1, D_LRU), ssd_conv_p,
            ssd_h_p.reshape(1, 1, SSD_HEADS, SSD_HEAD_DIM, SSD_STATE),
            swa_k_s, swa_v_s, pool_s, lru_conv_s, lru_h_s[None], ssd_conv_s,
            ssd_h_s.reshape(1, b, SSD_HEADS, SSD_HEAD_DIM, SSD_STATE))
```

```python
import functools

import jax
import jax.numpy as jnp
from jax import lax
from jax.experimental import pallas as pl
from jax.experimental.pallas import tpu as pltpu

F32 = jnp.float32
BF16 = jnp.bfloat16
I32 = jnp.int32

D_MODEL = 1024
N_META = 16
EPS = 1e-6
PAST_LEN = 16384
N_Q_HEADS = 16
N_KV_HEADS = 2
GQA = N_Q_HEADS // N_KV_HEADS
HEAD_DIM = 64
WINDOW = 128
D_ATTN = N_Q_HEADS * HEAD_DIM
D_KV = N_KV_HEADS * HEAD_DIM
POOL_WINDOWS = (2, 4, 8, 16)
POOL_GROUP_DIM = 96
D_POOL = len(POOL_WINDOWS) * POOL_GROUP_DIM
POOL_BUF = max(POOL_WINDOWS) - 1
D_MIX_EVEN = D_ATTN + D_POOL
D_LRU = 512
LRU_HEADS = 8
LRU_BLOCK = D_LRU // LRU_HEADS
LRU_C = 8.0
CONV_WIDTH = 4
SSD_HEADS = 8
SSD_HEAD_DIM = 64
D_SSD = SSD_HEADS * SSD_HEAD_DIM
SSD_GROUPS = 2
SSD_STATE = 128
D_XBC = D_SSD + 2 * SSD_GROUPS * SSD_STATE
D_MIX_ODD = D_LRU + D_SSD
MOE_GROUPS = 4
EXPERTS_PER_GROUP = 8
N_EXPERTS = MOE_GROUPS * EXPERTS_PER_GROUP
D_EXPERT = 256

BLK = 128
META_PAD = BLK - N_META
LANES = 128
MOE_TM = 512
FFN_ROWS = 256
MIX0_BLKS = 4
MIX1_BLKS = 4
NEG_INF = float("-inf")
VMEM_LIMIT = 56 * 1024 * 1024


def _cparams(*sem):
    return pltpu.CompilerParams(dimension_semantics=sem, vmem_limit_bytes=VMEM_LIMIT)


TT = 512
SUB = 8


def _small_tile_ranges(t_rows, ns, n):
    out = {}
    for k in range(t_rows // TT, n // TT):
        lo = k * TT
        ranges = []
        a0, a1 = max(lo, t_rows), min(lo + TT, t_rows + ns)
        if a1 > a0:
            ranges.append((a0 - lo, a1 - lo))
        b0 = max(lo, n - N_META)
        if lo + TT > b0:
            ranges.append((b0 - lo, TT))
        out[k] = tuple(ranges)
    return out


def _for_tile_rows(i, n_full, ranges, fn):
    @pl.when(i < n_full)
    def _():
        fn(0, TT)

    for k, rs in ranges.items():
        if rs:
            @pl.when(i == k)
            def _(rs=rs):
                for lo, hi in rs:
                    fn(lo, hi)


def _slab(c, rows, base=0):
    return pl.ds(base + c, rows, stride=SUB)


def _tile_of(r):
    return pl.ds(pl.multiple_of(r * SUB, SUB), SUB)


def _sigmoid(x):
    return 1.0 / (1.0 + jnp.exp(-x))


def _silu(x):
    return x * _sigmoid(x)


def _softplus(x):
    return jnp.maximum(x, 0.0) + jnp.log1p(jnp.exp(-jnp.abs(x)))


def _gelu_tanh(x):
    return 0.5 * x * (1.0 + jnp.tanh(0.7978845608028654 * (x + 0.044715 * x * x * x)))


def _dot(a, b):
    return jnp.dot(a.astype(BF16), b.astype(BF16), preferred_element_type=F32)


def _dot_nt(a, b):
    return lax.dot_general(a.astype(BF16), b.astype(BF16), (((1,), (1,)), ((), ())),
                           preferred_element_type=F32)


def _rms(x, g):
    ms = jnp.mean(x * x, axis=-1, keepdims=True)
    return x * lax.rsqrt(ms + EPS) * g


def _row_specs(xs, width):
    if len(xs) == 1:
        return [pl.BlockSpec((TT, width), lambda i: (i, 0))]
    na = xs[0].shape[0] // TT
    return [pl.BlockSpec((TT, width), lambda i: (jnp.minimum(i, na - 1), 0)),
            pl.BlockSpec((TT, width), lambda i: (jnp.maximum(i - na, 0), 0))]


def _pick_rows(refs, n_first):
    if len(refs) == 1:
        return refs[0][...]
    return jnp.where(pl.program_id(0) < n_first, refs[0][...], refs[1][...])


def _norm_proj_kernel(*refs, splits, n_x, n_first):
    x_refs, (g_ref, w_ref), out_refs = refs[:n_x], refs[n_x:n_x + 2], refs[n_x + 2:]
    y = _rms(_pick_rows(x_refs, n_first), g_ref[...]).astype(BF16)
    off = 0
    for o_ref, n in zip(out_refs, splits):
        o_ref[...] = jnp.dot(y, w_ref[:, off:off + n], preferred_element_type=F32).astype(o_ref.dtype)
        off += n


def _norm_proj(xs, g, w_bf16, splits, dtypes=None):
    n = sum(x.shape[0] for x in xs)
    dtypes = dtypes or (F32,) * len(splits)
    return pl.pallas_call(
        functools.partial(_norm_proj_kernel, splits=splits, n_x=len(xs), n_first=xs[0].shape[0] // TT),
        grid=(n // TT,),
        in_specs=_row_specs(xs, D_MODEL) + [pl.BlockSpec((1, D_MODEL), lambda i: (0, 0)),
                                             pl.BlockSpec(w_bf16.shape, lambda i: (0, 0))],
        out_specs=[pl.BlockSpec((TT, s), lambda i: (i, 0)) for s in splits],
        out_shape=[jax.ShapeDtypeStruct((n, s), dt) for s, dt in zip(splits, dtypes)],
        compiler_params=_cparams("parallel"),
        name="norm_proj",
    )(*xs, g.reshape(1, D_MODEL), w_bf16)


def _mix0_prompt_kernel(*refs, nblk):
    j = pl.program_id(0)
    o_ref = refs[8]

    @pl.when(j <= nblk)
    def _():
        _mix0_prompt_step(*refs)

    @pl.when(j > nblk)
    def _():
        o_ref[...] = jnp.zeros(o_ref.shape, o_ref.dtype)


def _mix0_prompt_step(sink_ref, q_ref, kvc_ref, kvp_ref, kvm_ref, p_ref, wp_ref, ps_ref,
                      o_ref, seq_ref, s_ref, pr_ref, rd_ref):
    j = pl.program_id(0)
    kv_meta = jnp.concatenate([kvm_ref[...], jnp.zeros((META_PAD, 2 * D_KV), F32)], axis=0)
    kv_blocks = [_kv_halves(kvp_ref[...])]
    kv_blocks += [_kv_halves(kvc_ref[sub * BLK:(sub + 1) * BLK, :]) for sub in range(MIX0_BLKS)]
    kv_meta = _kv_halves(kv_meta)
    r = lax.broadcasted_iota(I32, (BLK, BLK), 0)
    c = lax.broadcasted_iota(I32, (BLK, BLK), 1)
    own = r >= c
    dist_loc = jnp.where(own, r - c, r - c + BLK).astype(F32)
    for sub in range(MIX0_BLKS):
        blk = (j - 1) * MIX0_BLKS + sub
        j_prev = 2 if sub == 0 else 1
        c_min_meta = META_PAD if sub == MIX0_BLKS - 1 else BLK
        keys = [kv_blocks[sub], kv_blocks[sub + 1], kv_meta]
        _attn_block(sink_ref, q_ref, keys, o_ref, s_ref, pr_ref, rd_ref,
                    sub * BLK, j, blk, j_prev, c_min_meta, r, c, own, dist_loc)
    _pool_rows(j, p_ref, wp_ref, ps_ref, o_ref, seq_ref)


def _kv_halves(kv):
    k_all, v_all = kv[:, :D_KV], kv[:, D_KV:]
    k_rot = pltpu.roll(k_all, HEAD_DIM, 1)
    v_rot = pltpu.roll(v_all, HEAD_DIM, 1)
    lo_half = lax.broadcasted_iota(I32, k_all.shape, 1) < HEAD_DIM

    def halves(own_lanes, other_lanes, g):
        src_lo, src_hi = (own_lanes, other_lanes) if g == 0 else (other_lanes, own_lanes)
        return (jnp.where(lo_half, src_lo, 0.0).astype(BF16), jnp.where(lo_half, 0.0, src_hi).astype(BF16))

    return [halves(k_all, k_rot, g) + halves(v_all, v_rot, g) for g in range(N_KV_HEADS)]


def _attn_block(sink_ref, q_ref, keys, o_ref, s_ref, pr_ref, rd_ref,
                ro, j, blk, j_prev, c_min_meta, r, c, own, dist_loc):
    d_min = jnp.where(j >= j_prev, -BLK, 0)
    c_min = jnp.where(j == 0, c_min_meta, 0)
    mask_loc = jnp.where((r - c >= d_min) & (c >= c_min), 0.0, NEG_INF)
    meta_hi = jnp.where(j >= 1, N_META, 0)
    dist_meta = jnp.where(c < meta_hi, (N_META + blk * BLK + r - c).astype(F32), jnp.inf)
    slabs = GQA // 2
    nkeys = 3 * BLK
    for g in range(N_KV_HEADS):
        k_even, k_odd, v_even, v_odd = (jnp.concatenate([blk_halves[g][part] for blk_halves in keys], axis=0)
                                        for part in range(4))
        qg = jnp.concatenate([q_ref[ro:ro + BLK, (g * slabs + i) * LANES:(g * slabs + i + 1) * LANES]
                              for i in range(slabs)], axis=0) * (HEAD_DIM ** -0.5)
        s_ref[...] = _dot_nt(qg, jnp.concatenate([k_even, k_odd], axis=0))
        for i in range(slabs):
            rows = slice(i * BLK, (i + 1) * BLK)
            for par in range(2):
                h = g * GQA + 2 * i + par
                slope = 2.0 ** (-8.0 * (h + 1) / N_Q_HEADS)
                sink = sink_ref[h]
                k0 = par * nkeys
                s_loc = (jnp.where(own, s_ref[rows, k0 + BLK:k0 + 2 * BLK], s_ref[rows, k0:k0 + BLK])
                         - slope * dist_loc + mask_loc)
                s_met = s_ref[rows, k0 + 2 * BLK:k0 + 3 * BLK] - slope * dist_meta
                m = jnp.maximum(jnp.max(jnp.maximum(s_loc, s_met), axis=-1, keepdims=True), sink)
                p_loc = jnp.exp(s_loc - m)
                p_met = jnp.exp(s_met - m)
                den = jnp.sum(p_loc + p_met, axis=-1, keepdims=True) + jnp.exp(sink - m)
                pr_ref[par, rows, 0:BLK] = jnp.where(own, 0.0, p_loc).astype(BF16)
                pr_ref[par, rows, BLK:2 * BLK] = jnp.where(own, p_loc, 0.0).astype(BF16)
                pr_ref[par, rows, 2 * BLK:] = p_met.astype(BF16)
                rd_ref[par, rows, :] = jnp.broadcast_to(1.0 / den, (BLK, LANES))
        o = (jnp.dot(pr_ref[0], v_even, preferred_element_type=F32) * rd_ref[0]
             + jnp.dot(pr_ref[1], v_odd, preferred_element_type=F32) * rd_ref[1])
        for i in range(slabs):
            o_ref[ro:ro + BLK, (g * slabs + i) * LANES:(g * slabs + i + 1) * LANES] = (
                o[i * BLK:(i + 1) * BLK, :].astype(o_ref.dtype))


def _pool_rows(j, p_ref, wp_ref, ps_ref, o_ref, seq_ref):
    n_rows = p_ref.shape[0]
    carry = jnp.where(j == 0, 0.0, seq_ref[...])
    s0 = jnp.concatenate([carry, p_ref[...]], axis=0)
    s1 = s0 + pltpu.roll(s0, 1, 0)
    s2 = s1 + pltpu.roll(s1, 2, 0)
    s3 = s2 + pltpu.roll(s2, 4, 0)
    s4 = s3 + pltpu.roll(s3, 8, 0)
    rows = N_META + n_rows
    col = lax.broadcasted_iota(I32, (rows, D_POOL), 1)
    row = lax.broadcasted_iota(I32, (rows, D_POOL), 0) - N_META
    wsum = jnp.where(col < POOL_GROUP_DIM, s1,
                     jnp.where(col < 2 * POOL_GROUP_DIM, s2,
                               jnp.where(col < 3 * POOL_GROUP_DIM, s3, s4)))
    wlen = jnp.where(col < POOL_GROUP_DIM, 2,
                     jnp.where(col < 2 * POOL_GROUP_DIM, 4,
                               jnp.where(col < 3 * POOL_GROUP_DIM, 8, 16)))
    pos = row + jnp.where(j == 0, N_META - n_rows, N_META + (j - 1) * n_rows)
    cnt = jnp.clip(jnp.minimum(pos + 1, wlen), 1, 16).astype(F32)
    d = (wsum / cnt - s0)[N_META:, :]
    o_ref[:, D_ATTN:] = (_dot(d, wp_ref[...]) * ps_ref[...]).astype(o_ref.dtype)
    seq_ref[...] = p_ref[n_rows - N_META:, :]


def _mix0_prompt(q, kv, p, sinks, wpool_bd, pool_scale, t_rows, meta_blk):
    n = q.shape[0]
    step = MIX0_BLKS * BLK
    assert t_rows % step == 0 and n % step == 0 and step <= TT
    nstep = t_rows // step
    last = n // step - 1

    def cur(j):
        return jnp.where(j == 0, last, j - 1)

    meta16 = (meta_blk * BLK + META_PAD) // N_META
    return pl.pallas_call(
        functools.partial(_mix0_prompt_kernel, nblk=nstep),
        grid=(last + 1,),
        in_specs=[pl.BlockSpec(memory_space=pltpu.SMEM),
                  pl.BlockSpec((step, D_ATTN), lambda j: (cur(j), 0)),
                  pl.BlockSpec((step, 2 * D_KV), lambda j: (cur(j), 0)),
                  pl.BlockSpec((BLK, 2 * D_KV), lambda j: (jnp.maximum((j - 1) * MIX0_BLKS - 1, 0), 0)),
                  pl.BlockSpec((N_META, 2 * D_KV), lambda j: (meta16, 0)),
                  pl.BlockSpec((step, D_POOL), lambda j: (cur(j), 0)),
                  pl.BlockSpec((D_POOL, D_POOL), lambda j: (0, 0)),
                  pl.BlockSpec((1, D_POOL), lambda j: (0, 0))],
        out_specs=pl.BlockSpec((step, D_MIX_EVEN), lambda j: (cur(j), 0)),
        out_shape=jax.ShapeDtypeStruct((n, D_MIX_EVEN), BF16),
        scratch_shapes=[pltpu.VMEM((N_META, D_POOL), F32),
                        pltpu.VMEM((GQA // 2 * BLK, 2 * 3 * BLK), F32),
                        pltpu.VMEM((2, GQA // 2 * BLK, 3 * BLK), BF16),
                        pltpu.VMEM((2, GQA // 2 * BLK, LANES), F32)],
        compiler_params=_cparams("arbitrary"),
        name="mix0_prompt",
    )(sinks, q, kv, kv, kv, p, wpool_bd, pool_scale.reshape(1, D_POOL))


def _attn_sample_kernel(sink_ref, q_ref, ck_ref, cv_ref, kn_ref, vn_ref, o_ref, *, n_new):
    nk = N_META + WINDOW + n_new
    nq = n_new * GQA
    k_all = jnp.concatenate([ck_ref[...], kn_ref[...]], axis=1).astype(BF16)
    v_all = jnp.concatenate([cv_ref[...], vn_ref[...]], axis=1).astype(BF16)
    r = lax.broadcasted_iota(I32, (nq, nk), 0)
    slot = lax.broadcasted_iota(I32, (nq, nk), 1)
    t = r // GQA
    hh = r - t * GQA
    kpos = jnp.where(slot < N_META, slot,
                     jnp.where(slot < N_META + WINDOW, PAST_LEN - WINDOW - N_META + slot,
                               PAST_LEN - N_META - WINDOW + slot))
    dist = PAST_LEN + t - kpos
    allowed = (dist >= 0) & ((slot < N_META) | (dist < WINDOW))
    distf = dist.astype(F32)
    hh1 = hh[:, :1]
    for g in range(N_KV_HEADS):
        slope = jnp.exp2(-8.0 * (g * GQA + hh + 1).astype(F32) / N_Q_HEADS)
        sink = jnp.zeros((nq, 1), F32)
        for i in range(GQA):
            sink = jnp.where(hh1 == i, sink_ref[g * GQA + i], sink)
        kg = k_all[:, :, g * HEAD_DIM:(g + 1) * HEAD_DIM]
        vg = v_all[:, :, g * HEAD_DIM:(g + 1) * HEAD_DIM]
        s = jnp.einsum("bqd,bkd->bqk", q_ref[:, g].astype(BF16), kg,
                       preferred_element_type=F32) * (HEAD_DIM ** -0.5)
        s = jnp.where(allowed[None], s - (slope * distf)[None], NEG_INF)
        sink = sink[None]
        m = jnp.maximum(jnp.max(s, axis=-1, keepdims=True), sink)
        pr = jnp.exp(s - m)
        den = jnp.sum(pr, axis=-1, keepdims=True) + jnp.exp(sink - m)
        o = jnp.einsum("bqk,bkd->bqd", pr.astype(BF16), vg, preferred_element_type=F32)
        o_ref[:, g] = o / den


def _attn_sample(qs, ck, cv, kn, vn, sinks):
    b, _, nq, _ = qs.shape
    n_new = kn.shape[1]
    bs = 16 if b % 16 == 0 else b
    nc = N_META + WINDOW
    return pl.pallas_call(
        functools.partial(_attn_sample_kernel, n_new=n_new),
        grid=(b // bs,),
        in_specs=[pl.BlockSpec(memory_space=pltpu.SMEM),
                  pl.BlockSpec((bs, N_KV_HEADS, nq, HEAD_DIM), lambda i: (i, 0, 0, 0)),
                  pl.BlockSpec((bs, nc, D_KV), lambda i: (i, 0, 0)),
                  pl.BlockSpec((bs, nc, D_KV), lambda i: (i, 0, 0)),
                  pl.BlockSpec((bs, n_new, D_KV), lambda i: (i, 0, 0)),
                  pl.BlockSpec((bs, n_new, D_KV), lambda i: (i, 0, 0))],
        out_specs=pl.BlockSpec((bs, N_KV_HEADS, nq, HEAD_DIM), lambda i: (i, 0, 0, 0)),
        out_shape=jax.ShapeDtypeStruct(qs.shape, F32),
        compiler_params=_cparams("parallel"),
        name="attn_sample",
    )(sinks, qs, ck, cv, kn, vn)


def _pool_sample_kernel(buf_ref, p_ref, wp_ref, ps_ref, o_ref, *, n_new):
    seq = [buf_ref[i] for i in range(POOL_BUF)] + [p_ref[i] for i in range(n_new)]
    b = p_ref.shape[1]
    col = lax.broadcasted_iota(I32, (b, D_POOL), 1)
    for t in range(n_new):
        cur = seq[POOL_BUF + t]
        acc = cur
        sums = {}
        for back in range(1, max(POOL_WINDOWS)):
            acc = acc + seq[POOL_BUF + t - back]
            if back + 1 in POOL_WINDOWS:
                sums[back + 1] = acc
        mean = jnp.where(col < POOL_GROUP_DIM, sums[2] / 2.0,
                         jnp.where(col < 2 * POOL_GROUP_DIM, sums[4] / 4.0,
                                   jnp.where(col < 3 * POOL_GROUP_DIM, sums[8] / 8.0,
                                             sums[16] / 16.0)))
        o_ref[t] = _dot(mean - cur, wp_ref[...]) * ps_ref[...]


def _pool_sample(buf_tm, p_tm, wpool_bd, pool_scale):
    n_new = p_tm.shape[0]
    return pl.pallas_call(
        functools.partial(_pool_sample_kernel, n_new=n_new),
        out_shape=jax.ShapeDtypeStruct(p_tm.shape, F32),
        compiler_params=pltpu.CompilerParams(vmem_limit_bytes=VMEM_LIMIT),
        name="pool_sample",
    )(buf_tm, p_tm, wpool_bd, pool_scale.reshape(1, D_POOL))


def _outproj_route_kernel(*refs, valid_lo, valid_mid, n_x, n_first, n_tiles):
    tt = TT
    x_refs = refs[:n_x]
    (mix_ref, mixs_ref, w_ref, g_ref, wrh_ref, wrl_ref, br_ref,
     x1_ref, xn_ref, info_ref, infot_ref, cnt_ref, acc_ref, lg_ref) = refs[n_x:]
    i = pl.program_id(0)

    @pl.when(i == 0)
    def _():
        lg_ref[...] = jnp.zeros(lg_ref.shape, F32)

    tile = jnp.minimum(i, n_tiles - 1)
    is_sample_tile = tile == n_first
    acc = jnp.where(i == 0, 0.0, acc_ref[...])
    mix = mix_ref[...] + jnp.where(is_sample_tile, mixs_ref[...], 0).astype(mix_ref.dtype)
    x = x_refs[0][...] if n_x == 1 else jnp.where(tile < n_first, x_refs[0][...], x_refs[1][...])
    x1 = x + _dot(mix, w_ref[...])
    x1_ref[...] = x1
    xn = _rms(x1, g_ref[...])
    for cc_ in range(D_MODEL // LANES):
        xn_ref[_slab(cc_, tt), :] = xn[:, cc_ * LANES:(cc_ + 1) * LANES]
    hi = xn.astype(BF16)
    lo = (xn - hi.astype(F32)).astype(BF16)
    hi_out = jnp.dot(hi, jnp.concatenate([wrh_ref[...], wrl_ref[...]], axis=1), preferred_element_type=F32)
    logits_new = (hi_out[:, :LANES] + jnp.dot(lo, wrh_ref[...], preferred_element_type=F32)
                  + hi_out[:, LANES:]) + br_ref[...]
    logits = lg_ref[...]
    lg_ref[...] = logits_new
    lane = lax.broadcasted_iota(I32, (tt, LANES), 1)
    lanef = lane.astype(F32)
    lg = jnp.where(lane < MOE_GROUPS, logits, NEG_INF)
    mg = jnp.max(lg, axis=-1, keepdims=True)
    gstar = jnp.min(jnp.where(lg == mg, lanef, 1e9), axis=-1, keepdims=True)
    pgroup = 1.0 / jnp.sum(jnp.exp(lg - mg), axis=-1, keepdims=True)
    lo_lane = MOE_GROUPS + gstar * EXPERTS_PER_GROUP
    le = jnp.where((lanef >= lo_lane) & (lanef < lo_lane + EXPERTS_PER_GROUP), logits, NEG_INF)
    v1 = jnp.max(le, axis=-1, keepdims=True)
    i1 = jnp.min(jnp.where(le == v1, lanef, 1e9), axis=-1, keepdims=True)
    le2 = jnp.where(lanef == i1, NEG_INF, le)
    v2 = jnp.max(le2, axis=-1, keepdims=True)
    i2 = jnp.min(jnp.where(le2 == v2, lanef, 1e9), axis=-1, keepdims=True)
    tq = jnp.exp(v2 - v1)
    w1 = pgroup / (1.0 + tq)
    w2 = pgroup * tq / (1.0 + tq)
    e1 = i1 - MOE_GROUPS
    e2 = i2 - MOE_GROUPS
    rowg = (i - 1) * tt + lax.broadcasted_iota(I32, (tt, LANES), 0)
    valid = (rowg >= 0) & ((rowg < valid_lo) | (rowg >= valid_mid))
    oh1 = jnp.where(valid & (lanef == e1), 1.0, 0.0)
    oh2 = jnp.where(valid & (lanef == e2), 1.0, 0.0)
    both = oh1 + oh2
    rr = lax.broadcasted_iota(I32, (tt, tt), 0)
    cc = lax.broadcasted_iota(I32, (tt, tt), 1)
    tril = jnp.where(cc < rr, 1.0, 0.0).astype(BF16)
    base = acc + jnp.dot(tril, both.astype(BF16), preferred_element_type=F32)
    rank1 = jnp.sum(oh1 * base, axis=-1, keepdims=True)
    rank2 = jnp.sum(oh2 * base, axis=-1, keepdims=True)
    info = jnp.where(lane == 0, e1, jnp.where(lane == 1, e2, jnp.where(lane == 2, w1, jnp.where(
        lane == 3, w2, jnp.where(lane == 4, rank1, jnp.where(lane == 5, rank2, 0.0))))))
    info_ref[...] = info
    infot_ref[0] = info.T[:SUB, :]
    total = acc + jnp.sum(both, axis=0, keepdims=True)
    acc_ref[...] = total
    cnt_ref[...] = jnp.broadcast_to(total, (8, LANES))


def _outproj_route(xs, mix, mix_s, w_out_bf16, g_ffn, wr_hi, wr_lo, br, t_rows, valid_lo, valid_mid):
    n = mix.shape[0]
    ns, dm = mix_s.shape
    mix_s = jnp.pad(mix_s, ((0, TT - ns), (0, 0))).astype(mix.dtype)
    nt = n // TT

    def const2(i):
        return (0, 0)

    n_first = t_rows // TT

    def cur(i):
        return jnp.minimum(i, nt - 1)

    def prev(i):
        return jnp.maximum(i - 1, 0)

    if len(xs) == 1:
        x_specs = [pl.BlockSpec((TT, D_MODEL), lambda i: (cur(i), 0))]
    else:
        x_specs = [pl.BlockSpec((TT, D_MODEL), lambda i: (jnp.minimum(cur(i), n_first - 1), 0)),
                   pl.BlockSpec((TT, D_MODEL), lambda i: (jnp.maximum(cur(i) - n_first, 0), 0))]
    return pl.pallas_call(
        functools.partial(_outproj_route_kernel, valid_lo=valid_lo, valid_mid=valid_mid,
                          n_x=len(xs), n_first=n_first, n_tiles=nt),
        grid=(nt + 1,),
        in_specs=x_specs + [
            pl.BlockSpec((TT, dm), lambda i: (cur(i), 0)),
            pl.BlockSpec((TT, dm), const2),
            pl.BlockSpec((dm, D_MODEL), const2),
            pl.BlockSpec((1, D_MODEL), const2),
            pl.BlockSpec((D_MODEL, LANES), const2),
            pl.BlockSpec((D_MODEL, LANES), const2),
            pl.BlockSpec((1, LANES), const2)],
        out_specs=[pl.BlockSpec((TT, D_MODEL), lambda i: (cur(i), 0)),
                   pl.BlockSpec((TT * SUB, LANES), lambda i: (cur(i), 0)),
                   pl.BlockSpec((TT, LANES), lambda i: (prev(i), 0)),
                   pl.BlockSpec((1, SUB, TT), lambda i: (prev(i), 0, 0)),
                   pl.BlockSpec((8, LANES), const2)],
        out_shape=[jax.ShapeDtypeStruct((n, D_MODEL), F32),
                   jax.ShapeDtypeStruct((n * SUB, LANES), F32),
                   jax.ShapeDtypeStruct((n, LANES), F32),
                   jax.ShapeDtypeStruct((nt, SUB, TT), F32),
                   jax.ShapeDtypeStruct((8, LANES), F32)],
        scratch_shapes=[pltpu.VMEM((1, LANES), F32),
                        pltpu.VMEM((TT, LANES), F32)],
        compiler_params=_cparams("arbitrary"),
        name="outproj_route",
    )(*xs, mix, mix_s, w_out_bf16, g_ffn.reshape(1, D_MODEL), wr_hi, wr_lo, br)


ROWS_PER_ISSUE = 8


def _issue_rows(lo, hi, row_copy):
    assert (hi - lo) % ROWS_PER_ISSUE == 0

    def body(it, c):
        for u in range(ROWS_PER_ISSUE):
            r = lo + it * ROWS_PER_ISSUE + u
            row_copy(r, 0).start(priority=0)
            row_copy(r, 1).start(priority=1)
        return c

    lax.fori_loop(0, (hi - lo) // ROWS_PER_ISSUE, body, 0)


def _dispatch_kernel(pos_ref, zt_ref, xn_ref, xs_ref, zbuf, sem, zsem, *, n_full, ranges, n_tiles):
    i = pl.program_id(0)

    @pl.when(i == 0)
    def _():
        zbuf[...] = jnp.zeros(zbuf.shape, F32)

        def zero_copy(t):
            rows = MOE_TM * SUB
            return pltpu.make_async_copy(zbuf, xs_ref.at[pl.ds(pl.multiple_of(t * rows, rows), rows)], zsem)

        def zstart(t, c):
            @pl.when(zt_ref[t] != 0)
            def _():
                zero_copy(t).start()
            return c

        def zwait(t, c):
            @pl.when(zt_ref[t] != 0)
            def _():
                zero_copy(t).wait()
            return c

        lax.fori_loop(0, n_tiles, zstart, 0)
        lax.fori_loop(0, n_tiles, zwait, 0)

    def row_copy(r, k):
        return pltpu.make_async_copy(xn_ref.at[_tile_of(r)], xs_ref.at[_tile_of(pos_ref[0, 0, k * TT + r])], sem)

    def scatter(lo, hi):
        _issue_rows(lo, hi, row_copy)
        nrow = (hi - lo) * SUB
        for _ in range(2):
            pltpu.make_async_copy(xn_ref.at[pl.ds(lo * SUB, nrow)], xs_ref.at[pl.ds(0, nrow)], sem).wait()

    _for_tile_rows(i, n_full, ranges, scatter)


def _dispatch(xn, pos, zero_tiles, n_sorted, t_rows, ranges):
    n = xn.shape[0] // SUB
    return pl.pallas_call(
        functools.partial(_dispatch_kernel, n_full=t_rows // TT, ranges=ranges,
                          n_tiles=n_sorted // MOE_TM),
        grid=(n // TT,),
        in_specs=[pl.BlockSpec((1, 1, 2 * TT), lambda i: (i, 0, 0), memory_space=pltpu.SMEM),
                  pl.BlockSpec(memory_space=pltpu.SMEM),
                  pl.BlockSpec((TT * SUB, LANES), lambda i: (i, 0))],
        out_specs=pl.BlockSpec(memory_space=pl.ANY),
        out_shape=jax.ShapeDtypeStruct((n_sorted * SUB, LANES), F32),
        scratch_shapes=[pltpu.VMEM((MOE_TM * SUB, LANES), F32),
                        pltpu.SemaphoreType.DMA(()),
                        pltpu.SemaphoreType.DMA(())],
        compiler_params=_cparams("arbitrary"),
        name="moe_dispatch",
    )(pos, zero_tiles, xn)


def _expert_ffn_kernel(te_ref, nu_ref, xs_ref, wg_ref, wu_ref, wd_ref, ys_ref, wgb, wub, wdb):
    j = pl.program_id(0)
    changed = (j == 0) | (te_ref[j] != te_ref[jnp.maximum(j - 1, 0)])

    @pl.when((j < nu_ref[0]) & changed)
    def _():
        wgb[...] = wg_ref[0, 0].astype(BF16)
        wub[...] = wu_ref[0, 0].astype(BF16)
        wdb[...] = wd_ref[0, 0].astype(BF16)

    n_slab = D_MODEL // LANES

    @pl.when(j < nu_ref[0])
    def _():
        for part in range(MOE_TM // FFN_ROWS):
            base = part * FFN_ROWS * SUB
            x = jnp.concatenate([xs_ref[_slab(c, FFN_ROWS, base), :] for c in range(n_slab)],
                                axis=1).astype(BF16)
            hg = jnp.dot(x, wgb[...], preferred_element_type=F32)
            hu = jnp.dot(x, wub[...], preferred_element_type=F32)
            h = (_silu(hg) * hu).astype(BF16)
            y = jnp.dot(h, wdb[...], preferred_element_type=F32)
            for c in range(n_slab):
                ys_ref[_slab(c, FFN_ROWS, base), :] = y[:, c * LANES:(c + 1) * LANES]

    @pl.when(j >= nu_ref[0])
    def _():
        ys_ref[...] = jnp.zeros(ys_ref.shape, F32)


def _expert_ffn(xs, tile_expert, n_used, w_gate, w_up, w_down, layer):
    n_sorted = xs.shape[0] // SUB
    nt = n_sorted // MOE_TM

    def row_map(j, te, nu):
        return (jnp.minimum(j, nu[0] - 1), 0)

    def w_map(j, te, nu):
        return (layer, te[j], 0, 0)

    return pl.pallas_call(
        _expert_ffn_kernel,
        grid_spec=pltpu.PrefetchScalarGridSpec(
            num_scalar_prefetch=2,
            grid=(nt,),
            in_specs=[pl.BlockSpec((MOE_TM * SUB, LANES), row_map),
                      pl.BlockSpec((1, 1, D_MODEL, D_EXPERT), w_map),
                      pl.BlockSpec((1, 1, D_MODEL, D_EXPERT), w_map),
                      pl.BlockSpec((1, 1, D_EXPERT, D_MODEL), w_map)],
            out_specs=pl.BlockSpec((MOE_TM * SUB, LANES), lambda j, te, nu: (j, 0)),
            scratch_shapes=[pltpu.VMEM((D_MODEL, D_EXPERT), BF16),
                            pltpu.VMEM((D_MODEL, D_EXPERT), BF16),
                            pltpu.VMEM((D_EXPERT, D_MODEL), BF16)]),
        out_shape=jax.ShapeDtypeStruct((n_sorted * SUB, LANES), F32),
        compiler_params=_cparams("arbitrary"),
        name="expert_ffn",
    )(tile_expert, n_used, xs, w_gate, w_up, w_down)


def _combine_kernel(posc_ref, posn_ref, x1_ref, info_ref, g_ref, ys_ref, *rest,
                    n_full, n_tiles, ranges, valid_lo, valid_mid, final_norm):
    out_refs, (ybuf, sem) = rest[:-2], rest[-2:]
    i = pl.program_id(0)
    slot = lax.rem(i, 2)

    def buf_base(s, k):
        return (s * 2 + k) * (TT * SUB)

    def gather(pos_ref, dst_slot):
        def row_copy(r, k):
            dst = pl.ds(pl.multiple_of(buf_base(dst_slot, k) + r * SUB, SUB), SUB)
            return pltpu.make_async_copy(ys_ref.at[_tile_of(pos_ref[0, 0, k * TT + r])],
                                         ybuf.at[dst], sem.at[dst_slot])
        return lambda lo, hi: _issue_rows(lo, hi, row_copy)

    @pl.when(i == 0)
    def _():
        ybuf[...] = jnp.zeros(ybuf.shape, F32)
        _for_tile_rows(i, n_full, ranges, gather(posc_ref, 0))

    @pl.when(i + 1 < n_tiles)
    def _():
        _for_tile_rows(i + 1, n_full, ranges, gather(posn_ref, 1 - slot))

    def drain(lo, hi):
        nrow = (hi - lo) * SUB
        for k in range(2):
            dst = pl.ds(pl.multiple_of(buf_base(slot, k) + lo * SUB, SUB), nrow)
            pltpu.make_async_copy(ys_ref.at[pl.ds(0, nrow)], ybuf.at[dst], sem.at[slot]).wait()

    _for_tile_rows(i, n_full, ranges, drain)

    info = info_ref[...]
    w1 = info[:, 2:3]
    w2 = info[:, 3:4]
    rowg = i * TT + lax.broadcasted_iota(I32, (TT, 1), 0)
    valid = (rowg < valid_lo) | (rowg >= valid_mid)
    outs = []
    for c in range(D_MODEL // LANES):
        y1 = ybuf[_slab(c, TT, buf_base(slot, 0)), :]
        y2 = ybuf[_slab(c, TT, buf_base(slot, 1)), :]
        outs.append(x1_ref[:, c * LANES:(c + 1) * LANES] + jnp.where(valid, w1 * y1 + w2 * y2, 0.0))
    out = jnp.concatenate(outs, axis=1)
    if not final_norm:
        out_refs[0][...] = out
    else:
        out = _rms(out, g_ref[...])

        @pl.when(i < n_full)
        def _():
            out_refs[0][...] = out

        @pl.when(i >= n_full)
        def _():
            out_refs[1][...] = out


def _combine(x1, info, pos, ys, g_final, t_rows, ranges, valid_lo, valid_mid, final_norm):
    n = x1.shape[0]
    nt = n // TT
    n_full = t_rows // TT
    if final_norm:
        out_specs = [pl.BlockSpec((TT, D_MODEL), lambda i: (jnp.minimum(i, n_full - 1), 0)),
                     pl.BlockSpec((TT, D_MODEL), lambda i: (jnp.maximum(i - n_full, 0), 0))]
        out_shape = [jax.ShapeDtypeStruct((t_rows, D_MODEL), F32),
                     jax.ShapeDtypeStruct((n - t_rows, D_MODEL), F32)]
    else:
        out_specs = [pl.BlockSpec((TT, D_MODEL), lambda i: (i, 0))]
        out_shape = [jax.ShapeDtypeStruct((n, D_MODEL), F32)]
    return pl.pallas_call(
        functools.partial(_combine_kernel, n_full=n_full, n_tiles=nt, ranges=ranges,
                          valid_lo=valid_lo, valid_mid=valid_mid, final_norm=final_norm),
        grid=(nt,),
        in_specs=[pl.BlockSpec((1, 1, 2 * TT), lambda i: (i, 0, 0), memory_space=pltpu.SMEM),
                  pl.BlockSpec((1, 1, 2 * TT), lambda i: (jnp.minimum(i + 1, nt - 1), 0, 0),
                               memory_space=pltpu.SMEM),
                  pl.BlockSpec((TT, D_MODEL), lambda i: (i, 0)),
                  pl.BlockSpec((TT, LANES), lambda i: (i, 0)),
                  pl.BlockSpec((1, D_MODEL), lambda i: (0, 0)),
                  pl.BlockSpec(memory_space=pl.ANY)],
        out_specs=out_specs,
        out_shape=out_shape,
        scratch_shapes=[pltpu.VMEM((2 * 2 * TT * SUB, LANES), F32),
                        pltpu.SemaphoreType.DMA((2,))],
        compiler_params=_cparams("arbitrary"),
        name="moe_combine",
    )(pos, pos, x1, info, g_final.reshape(1, D_MODEL), ys)


def _moe(x1, xn, info, infot, cnt, w_gate, w_up, w_down, layer, g_final, t_rows, ns, final_norm):
    n = x1.shape[0]
    valid_lo, valid_mid = t_rows + ns, n - N_META
    ranges = _small_tile_ranges(t_rows, ns, n)
    n_valid = valid_lo + N_META
    nt = (2 * n_valid + N_EXPERTS * (MOE_TM - 1)) // MOE_TM
    counts = cnt[0, :N_EXPERTS].astype(I32)
    tiles = (counts + MOE_TM - 1) // MOE_TM
    tile_end = jnp.cumsum(tiles)
    row_start = (tile_end - tiles) * MOE_TM
    n_used = tile_end[-1:]
    experts = jnp.arange(N_EXPERTS, dtype=I32)

    def slot_rows(e, rank):
        start = jnp.sum(jnp.where(e[..., None] == experts, row_start, 0), axis=-1)
        return start + rank

    pos = jnp.concatenate([slot_rows(infot[:, 0].astype(I32), infot[:, 4].astype(I32)),
                           slot_rows(infot[:, 1].astype(I32), infot[:, 5].astype(I32))], axis=-1)
    rows = jnp.arange(n).reshape(n // TT, TT)
    valid = (rows < valid_lo) | (rows >= valid_mid)
    pos = jnp.where(jnp.concatenate([valid, valid], axis=-1), pos, 0)[:, None, :]
    tile_ids = jnp.arange(nt, dtype=I32)
    tile_expert = jnp.sum(tile_end[None, :] <= jnp.minimum(tile_ids, n_used[0] - 1)[:, None],
                          axis=1).astype(I32)
    is_last = jnp.any((tile_ids[:, None] == tile_end[None, :] - 1) & (tiles[None, :] > 0), axis=1)
    zero_tiles = (is_last | (tile_ids >= n_used[0])).astype(I32)
    xs = _dispatch(xn, pos, zero_tiles, nt * MOE_TM, t_rows, ranges)
    ys = _expert_ffn(xs, tile_expert, n_used.astype(I32), w_gate, w_up, w_down, layer)
    return _combine(x1, info, pos, ys, g_final, t_rows, ranges, valid_lo, valid_mid, final_norm)


def _shift_rows(x, k, fill):
    if k % SUB == 0:
        return jnp.concatenate([jnp.full((k, x.shape[1]), fill, x.dtype), x[:x.shape[0] - k]], axis=0)
    row = lax.broadcasted_iota(I32, x.shape, 0)
    return jnp.where(row >= k, pltpu.roll(x, k, 0), fill)


def _lru_gates(xc, wgate_ref, bgate_ref, lam_sp):
    gates = _dot(xc, wgate_ref[...]) + bgate_ref[...]
    r = _sigmoid(gates[:, :D_LRU])
    ig = _sigmoid(gates[:, D_LRU:])
    log_a = -LRU_C * r * lam_sp
    a = jnp.exp(log_a)
    bx = jnp.sqrt(-jnp.tanh(log_a) * (a * a + 1.0)) * (ig * xc)
    return a, bx


def _conv_block(seq_ref, u, w_ref, b_ref, fresh):
    seq = jnp.concatenate([jnp.where(fresh, 0.0, seq_ref[...]), u], axis=0)
    out = b_ref[...] + u * w_ref[CONV_WIDTH - 1:CONV_WIDTH, :]
    for back in range(1, CONV_WIDTH):
        tap = CONV_WIDTH - 1 - back
        out = out + pltpu.roll(seq, back, 0)[8:, :] * w_ref[tap:tap + 1, :]
    seq_ref[...] = u[BLK - 8:, :]
    return out


def _mix1_prompt_kernel(*refs, nblk):
    j = pl.program_id(0)
    o_ref = refs[16]

    @pl.when(j <= nblk)
    def _():
        _mix1_prompt_step(*refs)

    @pl.when(j > nblk)
    def _():
        o_ref[...] = jnp.zeros(o_ref.shape, o_ref.dtype)


def _mix1_prompt_step(cx_ref, cg_ref, z_ref, xbc_ref, dt_ref,
                      cwc_ref, cbc_ref, wgate_ref, bgate_ref, lam_ref,
                      cwd_ref, cbd_ref, dtb_ref, alog_ref, dskip_ref, nw_ref,
                      o_ref, lruh_ref, ssdh_ref,
                      seqc_ref, seqd_ref, hl_ref, hs_ref):
    refs = (cx_ref, cg_ref, z_ref, xbc_ref, dt_ref, cwc_ref, cbc_ref, wgate_ref, bgate_ref, lam_ref,
            cwd_ref, cbd_ref, dtb_ref, alog_ref, dskip_ref, nw_ref, o_ref, lruh_ref, ssdh_ref,
            seqc_ref, seqd_ref, hl_ref, hs_ref)
    for sub in range(MIX1_BLKS):
        _mix1_block(*refs, sub=sub)


def _mix1_block(cx_ref, cg_ref, z_ref, xbc_ref, dt_ref,
                cwc_ref, cbc_ref, wgate_ref, bgate_ref, lam_ref,
                cwd_ref, cbd_ref, dtb_ref, alog_ref, dskip_ref, nw_ref,
                o_ref, lruh_ref, ssdh_ref,
                seqc_ref, seqd_ref, hl_ref, hs_ref, *, sub):
    j = pl.program_id(0)
    rows = slice(sub * BLK, (sub + 1) * BLK)
    fresh = (j == 0) if sub == 0 else False
    rowc = lax.broadcasted_iota(I32, (BLK, 1), 0)
    live_min_meta = META_PAD if sub == MIX1_BLKS - 1 else BLK
    live = rowc >= jnp.where(j >= 1, 0, live_min_meta)

    xc = _conv_block(seqc_ref, cx_ref[rows, :], cwc_ref, cbc_ref, fresh)
    a, bx = _lru_gates(xc, wgate_ref, bgate_ref, _softplus(-lam_ref[...]))
    a = jnp.where(live, a, 1.0)
    bx = jnp.where(live, bx, 0.0)
    k = 1
    while k < BLK:
        bx = a * _shift_rows(bx, k, 0.0) + bx
        a = a * _shift_rows(a, k, 1.0)
        k *= 2
    h = a * jnp.where(fresh, 0.0, hl_ref[...]) + bx
    hl_ref[...] = h[BLK - 1:BLK, :]
    lruh_ref[...] = h[BLK - 1:BLK, :]
    o_ref[rows, :D_LRU] = (h * _gelu_tanh(cg_ref[rows, :])).astype(o_ref.dtype)

    xbc = _silu(_conv_block(seqd_ref, xbc_ref[rows, :], cwd_ref, cbd_ref, fresh))
    xs = xbc[:, :D_SSD]
    dt = jnp.where(live, _softplus(dt_ref[rows, :] + dtb_ref[...]), 0.0)
    a_neg = -jnp.exp(alog_ref[...])
    acum = dt * a_neg
    k = 1
    while k < BLK:
        acum = acum + _shift_rows(acum, k, 0.0)
        k *= 2
    acum_t = acum.T
    dt_t = dt.T
    xs_t = xs.T
    tri = (lax.broadcasted_iota(I32, (BLK, BLK), 0) >= lax.broadcasted_iota(I32, (BLK, BLK), 1))
    rep = SSD_HEADS // SSD_GROUPS
    ys = []
    for g in range(SSD_GROUPS):
        bg = xbc[:, D_SSD + g * SSD_STATE:D_SSD + (g + 1) * SSD_STATE]
        cgm = xbc[:, D_SSD + (SSD_GROUPS + g) * SSD_STATE:D_SSD + (SSD_GROUPS + g + 1) * SSD_STATE]
        cb = _dot_nt(cgm, bg)
        for hh in range(rep):
            hd = g * rep + hh
            col = acum[:, hd:hd + 1]
            rowv = acum_t[hd:hd + 1, :]
            decay = jnp.exp(jnp.where(tri, col - rowv, NEG_INF))
            scores = cb * decay * dt_t[hd:hd + 1, :]
            xh = xs[:, hd * SSD_HEAD_DIM:(hd + 1) * SSD_HEAD_DIM]
            y = _dot(scores, xh)
            hprev = jnp.where(fresh, 0.0, hs_ref[hd])
            y = y + jnp.exp(col) * _dot_nt(cgm, hprev)
            last = acum_t[hd:hd + 1, BLK - 1:BLK]
            w_end = jnp.exp(last - rowv) * dt_t[hd:hd + 1, :]
            xw_t = xs_t[hd * SSD_HEAD_DIM:(hd + 1) * SSD_HEAD_DIM, :] * w_end
            hnew = jnp.exp(last) * hprev + _dot(xw_t, bg)
            hs_ref[hd] = hnew
            ssdh_ref[hd] = hnew
            ys.append(y + dskip_ref[:, hd * SSD_HEAD_DIM:(hd + 1) * SSD_HEAD_DIM] * xh)
    y = jnp.concatenate(ys, axis=1) * _silu(z_ref[rows, :])
    o_ref[rows, D_LRU:] = _rms(y, nw_ref[...]).astype(o_ref.dtype)


def _mix1_prompt(cx, cg, z, xbc, dt, wts, t_rows):
    n = cx.shape[0]
    step = MIX1_BLKS * BLK
    assert t_rows % step == 0 and n % step == 0 and step <= TT
    nstep = t_rows // step
    last = n // step - 1

    def cur(j):
        return (jnp.where(j == 0, last, j - 1), 0)

    def const2(j):
        return (0, 0)

    in_specs = [pl.BlockSpec((step, D_LRU), cur), pl.BlockSpec((step, D_LRU), cur),
                pl.BlockSpec((step, D_SSD), cur), pl.BlockSpec((step, D_XBC), cur),
                pl.BlockSpec((step, LANES), cur)]
    in_specs += [pl.BlockSpec(w.shape, const2) for w in wts]
    return pl.pallas_call(
        functools.partial(_mix1_prompt_kernel, nblk=nstep),
        grid=(last + 1,),
        in_specs=in_specs,
        out_specs=[pl.BlockSpec((step, D_MIX_ODD), cur),
                   pl.BlockSpec((1, D_LRU), const2),
                   pl.BlockSpec((SSD_HEADS, SSD_HEAD_DIM, SSD_STATE), lambda j: (0, 0, 0))],
        out_shape=[jax.ShapeDtypeStruct((n, D_MIX_ODD), BF16),
                   jax.ShapeDtypeStruct((1, D_LRU), F32),
                   jax.ShapeDtypeStruct((SSD_HEADS, SSD_HEAD_DIM, SSD_STATE), F32)],
        scratch_shapes=[pltpu.VMEM((8, D_LRU), F32),
                        pltpu.VMEM((8, D_XBC), F32),
                        pltpu.VMEM((1, D_LRU), F32),
                        pltpu.VMEM((SSD_HEADS, SSD_HEAD_DIM, SSD_STATE), F32)],
        compiler_params=_cparams("arbitrary"),
        name="mix1_prompt",
    )(cx, cg, z, xbc, dt, *wts)


def _conv_step(seq, t, w_ref, b_ref):
    out = b_ref[...]
    for tap in range(CONV_WIDTH):
        out = out + seq[t + tap] * w_ref[tap:tap + 1, :]
    return out


def _lru_sample_kernel(cx_ref, cg_ref, buf_ref, h0_ref, cwc_ref, cbc_ref, wgate_ref, bgate_ref,
                       lam_ref, o_ref, hout_ref, *, n_new):
    seq = [buf_ref[i] for i in range(CONV_WIDTH - 1)] + [cx_ref[i] for i in range(n_new)]
    lam_sp = _softplus(-lam_ref[...])
    h = h0_ref[...]
    for t in range(n_new):
        xc = _conv_step(seq, t, cwc_ref, cbc_ref)
        a, bx = _lru_gates(xc, wgate_ref, bgate_ref, lam_sp)
        h = a * h + bx
        o_ref[t] = h * _gelu_tanh(cg_ref[t])
    hout_ref[...] = h


def _lru_sample(cx_tm, cg_tm, buf_tm, h0, wts):
    n_new = cx_tm.shape[0]
    return pl.pallas_call(
        functools.partial(_lru_sample_kernel, n_new=n_new),
        out_shape=[jax.ShapeDtypeStruct(cx_tm.shape, F32), jax.ShapeDtypeStruct(h0.shape, F32)],
        compiler_params=pltpu.CompilerParams(vmem_limit_bytes=VMEM_LIMIT),
        name="lru_sample",
    )(cx_tm, cg_tm, buf_tm, h0, *wts)


def _split3(x):
    hi = x.astype(BF16)
    r = x - hi.astype(F32)
    mid = r.astype(BF16)
    lo = (r - mid.astype(F32)).astype(BF16)
    return hi, mid, lo


def _ssd_sample_kernel(xbc_ref, dt_ref, z_ref, buf_ref, h0_ref, cwd_ref, cbd_ref, dtb_ref, alog_ref,
                       dskip_ref, nw_ref, o_ref, hout_ref, xc_ref, dec_ref, dtx_ref, y_ref, *, n_new, bs):
    seq = [buf_ref[i] for i in range(CONV_WIDTH - 1)] + [xbc_ref[i] for i in range(n_new)]
    a_neg = -jnp.exp(alog_ref[...])
    hcol = lax.broadcasted_iota(I32, (LANES, D_SSD), 1) // SSD_HEAD_DIM
    expand = jnp.where(lax.broadcasted_iota(I32, (LANES, D_SSD), 0) == hcol, 1.0, 0.0).astype(BF16)

    def widen(v):
        parts = _split3(v)
        return sum(jnp.dot(p, expand, preferred_element_type=F32) for p in parts)

    for t in range(n_new):
        xc = _silu(_conv_step(seq, t, cwd_ref, cbd_ref))
        xc_ref[t] = xc
        dt = _softplus(dt_ref[t] + dtb_ref[...])
        dtw = widen(dt)
        dec_ref[t] = jnp.exp(widen(dt * a_neg))
        dtx_ref[t] = dtw * xc[:, :D_SSD]

    zero_rows = jnp.zeros((LANES - 2 * n_new * SSD_GROUPS, SSD_STATE), F32)
    half = D_SSD // SSD_GROUPS

    def per_seq(b, c):
        rows = [xc_ref[t, pl.ds(b, 1), :] for t in range(n_new)]
        bc = [r[:, D_SSD + k * SSD_STATE:D_SSD + (k + 1) * SSD_STATE]
              for r in rows for k in range(2 * SSD_GROUPS)]
        bc_t = jnp.concatenate(bc + [zero_rows], axis=0).T
        ht = h0_ref[b].T
        for t in range(n_new):
            bcol = [jnp.broadcast_to(bc_t[:, t * 4 + g:t * 4 + g + 1], (SSD_STATE, half))
                    for g in range(SSD_GROUPS)]
            ccol = [jnp.broadcast_to(bc_t[:, t * 4 + SSD_GROUPS + g:t * 4 + SSD_GROUPS + g + 1],
                                     (SSD_STATE, half)) for g in range(SSD_GROUPS)]
            bw = jnp.concatenate(bcol, axis=1)
            cw = jnp.concatenate(ccol, axis=1)
            ht = ht * dec_ref[t, pl.ds(b, 1), :] + bw * dtx_ref[t, pl.ds(b, 1), :]
            y = jnp.sum(cw * ht, axis=0, keepdims=True)
            y_ref[t, pl.ds(b, 1), :] = y + dskip_ref[...] * rows[t][:, :D_SSD]
        hout_ref[b] = ht.T
        return c

    lax.fori_loop(0, bs, per_seq, 0)
    for t in range(n_new):
        o_ref[t] = _rms(y_ref[t] * _silu(z_ref[t]), nw_ref[...])


def _ssd_sample(xbc_tm, dt_tm, z_tm, buf_tm, h0, wts):
    n_new, b, _ = xbc_tm.shape
    bs = 16 if b % 16 == 0 else b
    hp = SSD_HEADS * SSD_HEAD_DIM

    def tm(c):
        return pl.BlockSpec((n_new, bs, c), lambda i: (0, i, 0))

    in_specs = [tm(D_XBC), tm(LANES), tm(D_SSD),
                pl.BlockSpec((CONV_WIDTH - 1, bs, D_XBC), lambda i: (0, i, 0)),
                pl.BlockSpec((bs, hp, SSD_STATE), lambda i: (i, 0, 0))]
    in_specs += [pl.BlockSpec(w.shape, lambda i: (0, 0)) for w in wts]
    return pl.pallas_call(
        functools.partial(_ssd_sample_kernel, n_new=n_new, bs=bs),
        grid=(b // bs,),
        in_specs=in_specs,
        out_specs=[tm(D_SSD), pl.BlockSpec((bs, hp, SSD_STATE), lambda i: (i, 0, 0))],
        out_shape=[jax.ShapeDtypeStruct((n_new, b, D_SSD), F32),
                   jax.ShapeDtypeStruct((b, hp, SSD_STATE), F32)],
        scratch_shapes=[pltpu.VMEM((n_new, bs, D_XBC), F32),
                        pltpu.VMEM((n_new, bs, D_SSD), F32),
                        pltpu.VMEM((n_new, bs, D_SSD), F32),
                        pltpu.VMEM((n_new, bs, D_SSD), F32)],
        compiler_params=_cparams("parallel"),
        name="ssd_sample",
    )(xbc_tm, dt_tm, z_tm, buf_tm, h0, *wts)


def _block_diag(w):
    g, c, _ = w.shape
    eye = jnp.eye(g, dtype=w.dtype)
    return (eye[:, None, :, None] * w[:, :, None, :]).reshape(g * c, g * c)


def _router_weights(w_rg, b_rg, w_re, b_re):
    w = jnp.concatenate([w_rg, w_re], axis=1)
    w = jnp.pad(w, ((0, 0), (0, LANES - w.shape[1])))
    hi = w.astype(BF16)
    lo = (w - hi.astype(F32)).astype(BF16)
    b = jnp.pad(jnp.concatenate([b_rg, b_re]), (0, LANES - MOE_GROUPS - N_EXPERTS)).reshape(1, LANES)
    return hi, lo, b


def kernel(x_prompt, x_sample, cache_swa_k, cache_swa_v, cache_pool, state_lru_conv, state_lru_h, state_ssd_conv, state_ssd_h, meta_tokens, norm_mix, norm_ffn, norm_final, attn_w_in, attn_w_out, attn_sinks, pool_w, pool_scale, rec_w_in, rec_w_out, lru_conv_w, lru_conv_b, lru_w_rg, lru_b_rg, lru_w_ig, lru_b_ig, lru_lambda, ssd_conv_w, ssd_conv_b, ssd_dt_bias, ssd_a_log, ssd_d, ssd_norm_w, moe_w_rg, moe_b_rg, moe_w_re, moe_b_re, moe_w_gate, moe_w_up, moe_w_down):
    t_rows = x_prompt.shape[1]
    b, s_new = x_sample.shape[:2]
    ns = b * s_new
    small = -(-ns // TT) * TT + TT
    n = t_rows + small
    meta_blk = n // BLK - 1
    valid_lo = t_rows + ns
    valid_mid = n - N_META
    assert x_prompt.shape[0] == 1 and t_rows % TT == 0 and ns % BLK == 0 and ns <= TT

    x_small = jnp.concatenate([
        x_sample.transpose(1, 0, 2).reshape(ns, D_MODEL),
        jnp.zeros((small - ns - N_META, D_MODEL), F32),
        meta_tokens.astype(F32)], axis=0)
    xs0 = (x_prompt[0], x_small)

    def sample_tm(a):
        return a[t_rows:t_rows + ns].reshape(s_new, b, a.shape[1])

    q, kv, p = _norm_proj(xs0, norm_mix[0], attn_w_in[0].astype(BF16), (D_ATTN, 2 * D_KV, D_POOL),
                          (BF16, F32, F32))
    wpool_bd = _block_diag(pool_w[0]).astype(BF16)
    mix = _mix0_prompt(q, kv, p, attn_sinks[0], wpool_bd, pool_scale[0], t_rows, meta_blk)
    nq = s_new * GQA
    qs = sample_tm(q).reshape(s_new, b, N_KV_HEADS, GQA, HEAD_DIM).transpose(1, 2, 0, 3, 4)
    qs = qs.reshape(b, N_KV_HEADS, nq, HEAD_DIM)
    kv_s = sample_tm(kv).transpose(1, 0, 2)
    kn, vn = kv_s[:, :, :D_KV], kv_s[:, :, D_KV:]
    ck = cache_swa_k[0].reshape(b, N_META + WINDOW, D_KV)
    cv = cache_swa_v[0].reshape(b, N_META + WINDOW, D_KV)
    o_s = _attn_sample(qs, ck, cv, kn, vn, attn_sinks[0])
    o_s = o_s.reshape(b, N_KV_HEADS, s_new, GQA, HEAD_DIM).transpose(2, 0, 1, 3, 4).reshape(s_new, b, D_ATTN)
    p_s = sample_tm(p)
    pool_buf_tm = cache_pool[0].transpose(1, 0, 2)
    yp_s = _pool_sample(pool_buf_tm, p_s, wpool_bd, pool_scale[0])
    mix_s = jnp.concatenate([o_s, yp_s], axis=-1).reshape(ns, D_MIX_EVEN)
    wr_hi, wr_lo, br = _router_weights(moe_w_rg[0], moe_b_rg[0], moe_w_re[0], moe_b_re[0])
    x1, xn, info, infot, cnt = _outproj_route(xs0, mix, mix_s, attn_w_out[0].astype(BF16), norm_ffn[0],
                                              wr_hi, wr_lo, br, t_rows, valid_lo, valid_mid)
    x, = _moe(x1, xn, info, infot, cnt, moe_w_gate, moe_w_up, moe_w_down, 0, norm_final,
              t_rows, ns, False)

    meta_rows = slice(n - N_META, n)
    last_rows = slice(t_rows - WINDOW, t_rows)
    swa_k_p = jnp.concatenate([kv[meta_rows, :D_KV], kv[last_rows, :D_KV]], axis=0)
    swa_v_p = jnp.concatenate([kv[meta_rows, D_KV:], kv[last_rows, D_KV:]], axis=0)
    swa_k_p = swa_k_p.reshape(1, 1, N_META + WINDOW, N_KV_HEADS, HEAD_DIM)
    swa_v_p = swa_v_p.reshape(1, 1, N_META + WINDOW, N_KV_HEADS, HEAD_DIM)
    pool_p = p[t_rows - POOL_BUF:t_rows].reshape(1, 1, POOL_BUF, D_POOL)
    swa_k_s = jnp.concatenate([ck[:, :N_META], ck[:, N_META + s_new:], kn], axis=1)
    swa_v_s = jnp.concatenate([cv[:, :N_META], cv[:, N_META + s_new:], vn], axis=1)
    swa_k_s = swa_k_s.reshape(1, b, N_META + WINDOW, N_KV_HEADS, HEAD_DIM)
    swa_v_s = swa_v_s.reshape(1, b, N_META + WINDOW, N_KV_HEADS, HEAD_DIM)
    pool_s = jnp.concatenate([cache_pool[0], p_s.transpose(1, 0, 2)], axis=1)[:, -POOL_BUF:][None]

    d_in_odd = rec_w_in.shape[2]
    w_in1 = jnp.pad(rec_w_in[0], ((0, 0), (0, LANES - SSD_HEADS))).astype(BF16)
    assert d_in_odd == 2 * D_LRU + D_SSD + D_XBC + SSD_HEADS
    cx, cg, z, xbc, dt = _norm_proj((x,), norm_mix[1], w_in1, (D_LRU, D_LRU, D_SSD, D_XBC, LANES))
    wgate = jnp.concatenate([_block_diag(lru_w_rg[0]), _block_diag(lru_w_ig[0])], axis=1).astype(BF16)
    bgate = jnp.concatenate([lru_b_rg[0], lru_b_ig[0]]).reshape(1, 2 * D_LRU)
    pad8 = (0, LANES - SSD_HEADS)
    dtb = jnp.pad(ssd_dt_bias[0], pad8).reshape(1, LANES)
    alog = jnp.pad(ssd_a_log[0], pad8).reshape(1, LANES)
    dskip = jnp.repeat(ssd_d[0], SSD_HEAD_DIM).reshape(1, D_SSD)
    lru_w = (lru_conv_w[0], lru_conv_b[0].reshape(1, D_LRU), wgate, bgate, lru_lambda[0].reshape(1, D_LRU))
    ssd_w = (ssd_conv_w[0], ssd_conv_b[0].reshape(1, D_XBC), dtb, alog, dskip, ssd_norm_w[0].reshape(1, D_SSD))
    mix, lru_h_p, ssd_h_p = _mix1_prompt(cx, cg, z, xbc, dt, lru_w + ssd_w, t_rows)
    cx_s, cg_s, z_s, xbc_s, dt_s = (sample_tm(a) for a in (cx, cg, z, xbc, dt))
    yc_s, lru_h_s = _lru_sample(cx_s, cg_s, state_lru_conv[0].transpose(1, 0, 2), state_lru_h[0], lru_w)
    h0 = state_ssd_h[0].reshape(b, SSD_HEADS * SSD_HEAD_DIM, SSD_STATE)
    yd_s, ssd_h_s = _ssd_sample(xbc_s, dt_s, z_s, state_ssd_conv[0].transpose(1, 0, 2), h0, ssd_w)
    mix_s = jnp.concatenate([yc_s, yd_s], axis=-1).reshape(ns, D_MIX_ODD)
    wr_hi, wr_lo, br = _router_weights(moe_w_rg[1], moe_b_rg[1], moe_w_re[1], moe_b_re[1])
    x1, xn, info, infot, cnt = _outproj_route((x,), mix, mix_s, rec_w_out[0].astype(BF16), norm_ffn[1],
                                              wr_hi, wr_lo, br, t_rows, valid_lo, valid_mid)
    y_p, y_small = _moe(x1, xn, info, infot, cnt, moe_w_gate, moe_w_up, moe_w_down, 1, norm_final,
                        t_rows, ns, True)

    tail = CONV_WIDTH - 1
    lru_conv_p = cx[t_rows - tail:t_rows].reshape(1, 1, tail, D_LRU)
    ssd_conv_p = xbc[t_rows - tail:t_rows].reshape(1, 1, tail, D_XBC)
    lru_conv_s = jnp.concatenate([state_lru_conv[0], cx_s.transpose(1, 0, 2)], axis=1)[:, -tail:][None]
    ssd_conv_s = jnp.concatenate([state_ssd_conv[0], xbc_s.transpose(1, 0, 2)], axis=1)[:, -tail:][None]

    y_prompt = y_p[None]
    y_sample = y_small[:ns].reshape(s_new, b, D_MODEL).transpose(1, 0, 2)
    return (y_prompt, y_sample, swa_k_p, swa_v_p, pool_p, lru_conv_p,
            lru_h_p.reshape(1, 1, D_LRU), ssd_conv_p,
            ssd_h_p.reshape(1, 1, SSD_HEADS, SSD_HEAD_DIM, SSD_STATE),
            swa_k_s, swa_v_s, pool_s, lru_conv_s, lru_h_s[None], ssd_conv_s,
            ssd_h_s.reshape(1, b, SSD_HEADS, SSD_HEAD_DIM, SSD_STATE))
```
